```python
import math
import jax, jax.numpy as jnp
from jax import lax
import numpy as np

D_MODEL = 1024
BATCH = 1
SEQ = 16384
DEPTH = 2

D_MIX = D_MODEL
S5_WIDTH = D_MIX // 2
S5_GROUP = 16
S5_GROUPS = S5_WIDTH // S5_GROUP
S5_STATE = 64
S5_MAX_RE = -1e-4
DN_HEADS = 4
DN_HEAD_DIM = 128
DN_WIDTH = DN_HEADS * DN_HEAD_DIM
DN_CONV = 5
DN_CHUNK = 64
IN_COLS = S5_WIDTH + 4 * DN_WIDTH + 4 * DN_HEADS
N_EXPERT_GROUPS = 4
EXPERTS_PER_GROUP = 8
N_EXPERTS = N_EXPERT_GROUPS * EXPERTS_PER_GROUP
TOP_K = 2
D_EXPERT = 512
MOE_BLOCK = 128
NORM_EPS = 1e-6

kernel_name = 'hybrid_s5_gdn_hier_moe_encoder'


def _rmsnorm(x, w):
    xf = x.astype(jnp.float32)
    return xf * lax.rsqrt(jnp.mean(xf * xf, axis=-1, keepdims=True) + NORM_EPS) * w.astype(jnp.float32)


def _l2norm(x):
    return x * lax.rsqrt(jnp.sum(x * x, axis=-1, keepdims=True) + NORM_EPS)


def _ssm_combine(left, right):
    a_l, b_l = left
    a_r, b_r = right
    return a_r * a_l, a_r * b_l + b_r


def _s5_scan(u, lam_re, lam_im, log_dt, b_re, b_im, c_re, c_im, reverse):
    f32 = jnp.float32
    lam = lax.complex(jnp.minimum(lam_re.astype(f32), S5_MAX_RE), lam_im.astype(f32))
    dt = jnp.exp(log_dt.astype(f32))[:, None]
    lam_bar = jnp.exp(lam * dt)
    b_bar = ((lam_bar - 1.0) / lam)[:, :, None] * lax.complex(b_re.astype(f32), b_im.astype(f32))
    bu = jnp.einsum('blgp,gnp->blgn', u.astype(jnp.complex64), b_bar)
    a = jnp.broadcast_to(lam_bar, bu.shape)
    _, state = lax.associative_scan(_ssm_combine, (a, bu), reverse=reverse, axis=1)
    c = lax.complex(c_re.astype(f32), c_im.astype(f32))
    return jnp.real(jnp.einsum('blgn,gpn->blgp', state, c))


def _s5_mixer(u, lam_re, lam_im, log_dt, b_re, b_im, c_re, c_im, d, w_glu, b_glu):
    bsz, seq, _ = u.shape
    u = u.astype(jnp.float32)
    ug = u.reshape(bsz, seq, S5_GROUPS, S5_GROUP)
    y_f = _s5_scan(ug, lam_re[0], lam_im[0], log_dt[0], b_re[0], b_im[0], c_re[0], c_im[0], False)
    y_b = _s5_scan(ug, lam_re[1], lam_im[1], log_dt[1], b_re[1], b_im[1], c_re[1], c_im[1], True)
    y = (y_f + y_b).reshape(bsz, seq, S5_WIDTH) + d * u
    y = jax.nn.gelu(y, approximate=False)
    return y * jax.nn.sigmoid(y @ w_glu + b_glu)


def _centred_dwconv(x, w):
    ch = x.shape[-1]
    pad = DN_CONV // 2
    return lax.conv_general_dilated(x, w[:, None, :], window_strides=(1,), padding=((pad, pad),),
                                    dimension_numbers=('NWC', 'WIO', 'NWC'), feature_group_count=ch)


def _gated_delta_chunked(q, k, v, g, beta):
    bsz, nh, seq, dk = q.shape
    dv = v.shape[-1]
    n_chunks = seq // DN_CHUNK
    q, k, v = [t.reshape(bsz, nh, n_chunks, DN_CHUNK, -1) for t in (q, k, v)]
    g = jnp.cumsum(g.reshape(bsz, nh, n_chunks, DN_CHUNK), axis=-1)
    beta = beta.reshape(bsz, nh, n_chunks, DN_CHUNK)
    tril = jnp.tril(jnp.ones((DN_CHUNK, DN_CHUNK), bool))
    strict = jnp.tril(jnp.ones((DN_CHUNK, DN_CHUNK), bool), -1)
    diff = g[..., :, None] - g[..., None, :]
    decay = jnp.where(tril, jnp.exp(jnp.where(tril, diff, 0.0)), 0.0)
    kb = k * beta[..., None]
    a_mat = jnp.where(strict, jnp.einsum('bhnid,bhnjd->bhnij', kb, k) * decay, 0.0)
    sys = jnp.eye(DN_CHUNK, dtype=q.dtype) + a_mat
    rhs = jnp.concatenate([v * beta[..., None], kb * jnp.exp(g)[..., None]], axis=-1)
    sol = lax.linalg.triangular_solve(sys, rhs, left_side=True, lower=True, unit_diagonal=True)
    u_c, w_c = sol[..., :dv], sol[..., dv:]
    attn = jnp.where(tril, jnp.einsum('bhnid,bhnjd->bhnij', q, k) * decay, 0.0)
    g_last = g[..., -1]
    k_dec = k * jnp.exp(g_last[..., None] - g)[..., None]
    q_dec = q * jnp.exp(g)[..., None]

    def step(state, inp):
        q_i, k_i, u_i, w_i, attn_i, gl_i = inp
        v_new = u_i - jnp.einsum('bhck,bhkv->bhcv', w_i, state)
        o_i = jnp.einsum('bhck,bhkv->bhcv', q_i, state) + jnp.einsum('bhij,bhjv->bhiv', attn_i, v_new)
        state = state * jnp.exp(gl_i)[..., None, None] + jnp.einsum('bhck,bhcv->bhkv', k_i, v_new)
        return state, o_i

    xs = tuple(jnp.moveaxis(t, 2, 0) for t in (q_dec, k_dec, u_c, w_c, attn, g_last))
    s0 = jnp.zeros((bsz, nh, dk, dv), q.dtype)
    _, o = lax.scan(step, s0, xs)
    return jnp.moveaxis(o, 0, 2).reshape(bsz, nh, seq, dv)


def _gdn_mixer(qkv, z, a_raw, b_raw, conv_w, a_log, dt_bias, norm_w):
    f32 = jnp.float32
    bsz, seq, _ = qkv.shape
    qkv = jax.nn.silu(_centred_dwconv(qkv.astype(f32), conv_w.astype(f32)))
    q, k, v = jnp.split(qkv, 3, axis=-1)

    def heads(t):
        return t.reshape(bsz, seq, DN_HEADS, DN_HEAD_DIM).transpose(0, 2, 1, 3)

    q = _l2norm(heads(q)) * DN_HEAD_DIM ** -0.5
    k = _l2norm(heads(k))
    v = heads(v)
    g = -jnp.exp(a_log.astype(f32)) * jax.nn.softplus(a_raw.astype(f32) + dt_bias.astype(f32))
    beta = jax.nn.sigmoid(b_raw.astype(f32))
    g = g.transpose(0, 2, 3, 1)
    beta = beta.transpose(0, 2, 3, 1)

    def flip(t):
        return jnp.flip(t, axis=2)

    o_f = _gated_delta_chunked(q, k, v, g[:, 0], beta[:, 0])
    o_b = flip(_gated_delta_chunked(flip(q), flip(k), flip(v), flip(g[:, 1]), flip(beta[:, 1])))
    o = (o_f + o_b).transpose(0, 2, 1, 3)
    zh = z.astype(f32).reshape(bsz, seq, DN_HEADS, DN_HEAD_DIM)
    o = o * lax.rsqrt(jnp.mean(o * o, axis=-1, keepdims=True) + NORM_EPS) * norm_w.astype(f32) * jax.nn.silu(zh)
    return o.reshape(bsz, seq, DN_WIDTH)


def _hier_moe(h, w_rg, b_rg, w_re, b_re, w_gate, w_up, w_down):
    n_tok, d = h.shape
    g_logits = h @ w_rg + b_rg
    g_prob = jax.nn.softmax(g_logits.astype(jnp.float32), axis=-1)
    _, g_sel = lax.top_k(g_logits, 1)
    p_group = jnp.take_along_axis(g_prob, g_sel, axis=-1)
    e_logits_all = jnp.einsum('td,gde->tge', h, w_re) + b_re
    e_logits = jnp.take_along_axis(e_logits_all, g_sel[:, :, None], axis=1)[:, 0]
    top_val, top_idx = lax.top_k(e_logits, TOP_K)
    gate = jax.nn.softmax(top_val.astype(jnp.float32), axis=-1) * p_group
    expert_idx = (g_sel * EXPERTS_PER_GROUP + top_idx).astype(jnp.int32)

    n_assign = n_tok * TOP_K
    n_blocks = -(-n_assign // MOE_BLOCK) + N_EXPERTS
    n_slots = n_blocks * MOE_BLOCK
    flat_e = expert_idx.reshape(-1)
    flat_tok = jnp.repeat(jnp.arange(n_tok, dtype=jnp.int32), TOP_K)
    flat_w = gate.reshape(-1)
    order = jnp.argsort(flat_e)
    e_sorted, tok_sorted, w_sorted = flat_e[order], flat_tok[order], flat_w[order]
    counts = jnp.bincount(flat_e, length=N_EXPERTS)
    padded = ((counts + MOE_BLOCK - 1) // MOE_BLOCK) * MOE_BLOCK
    pad_end = jnp.cumsum(padded)
    pad_start = pad_end - padded
    start = jnp.cumsum(counts) - counts
    dest = pad_start[e_sorted] + (jnp.arange(n_assign, dtype=jnp.int32) - start[e_sorted])
    slot_tok = jnp.zeros((n_slots,), jnp.int32).at[dest].set(tok_sorted)
    slot_w = jnp.zeros((n_slots,), jnp.float32).at[dest].set(w_sorted)
    block_start = jnp.arange(n_blocks, dtype=jnp.int32) * MOE_BLOCK
    block_expert = jnp.minimum(jnp.searchsorted(pad_end, block_start, side='right'), N_EXPERTS - 1)
    x_blocks = h[slot_tok].reshape(n_blocks, MOE_BLOCK, d)

    def expert_block(args):
        xb, e = args
        hid = jax.nn.silu(xb @ w_gate[e]) * (xb @ w_up[e])
        return hid @ w_down[e]

    y = lax.map(expert_block, (x_blocks, block_expert)).reshape(n_slots, d)
    return jnp.zeros((n_tok, d), y.dtype).at[slot_tok].add(y * slot_w[:, None])


def setup_inputs(seed: int = 0) -> dict:
    key = jax.random.key(seed)
    ks = jax.random.split(key, 32)
    f32 = jnp.float32

    def nrm(k, shape, s):
        return jax.random.normal(k, shape, f32) * s

    st = (DEPTH, 2, S5_GROUPS, S5_STATE)
    dt_gdn = jnp.exp(jax.random.uniform(ks[15], (DEPTH, 2, DN_HEADS), f32, math.log(1e-3), math.log(1e-1)))
    return {
        'x': nrm(ks[0], (BATCH, SEQ, D_MODEL), 1.0),
        'norm_mix': 1.0 + nrm(ks[1], (DEPTH, D_MODEL), 0.02),
        'w_in': nrm(ks[2], (DEPTH, D_MODEL, IN_COLS), D_MODEL ** -0.5),
        's5_lam_re': -0.5 + nrm(ks[3], st, 0.01),
        's5_lam_im': math.pi * jnp.arange(S5_STATE, dtype=f32) + nrm(ks[4], st, 0.01),
        's5_log_dt': jax.random.uniform(ks[5], (DEPTH, 2, S5_GROUPS), f32, math.log(1e-3), math.log(1e-1)),
        's5_b_re': nrm(ks[6], (DEPTH, 2, S5_GROUPS, S5_STATE, S5_GROUP), (2 * S5_GROUP) ** -0.5),
        's5_b_im': nrm(ks[7], (DEPTH, 2, S5_GROUPS, S5_STATE, S5_GROUP), (2 * S5_GROUP) ** -0.5),
        's5_c_re': nrm(ks[8], (DEPTH, 2, S5_GROUPS, S5_GROUP, S5_STATE), S5_STATE ** -0.5),
        's5_c_im': nrm(ks[9], (DEPTH, 2, S5_GROUPS, S5_GROUP, S5_STATE), S5_STATE ** -0.5),
        's5_d': nrm(ks[10], (DEPTH, S5_WIDTH), 1.0),
        's5_w_glu': nrm(ks[11], (DEPTH, S5_WIDTH, S5_WIDTH), S5_WIDTH ** -0.5),
        's5_b_glu': nrm(ks[12], (DEPTH, S5_WIDTH), 0.01),
        'gdn_conv_w': nrm(ks[13], (DEPTH, DN_CONV, 3 * DN_WIDTH), DN_CONV ** -0.5),
        'gdn_a_log': jnp.log(jax.random.uniform(ks[14], (DEPTH, 2, DN_HEADS), f32, 1.0, 16.0)),
        'gdn_dt_bias': dt_gdn + jnp.log(-jnp.expm1(-dt_gdn)),
        'gdn_norm_w': 1.0 + nrm(ks[16], (DEPTH, DN_HEAD_DIM), 0.02),
        'w_out': nrm(ks[17], (DEPTH, D_MIX, D_MODEL), D_MIX ** -0.5),
        'norm_ffn': 1.0 + nrm(ks[18], (DEPTH, D_MODEL), 0.02),
        'router_w_group': nrm(ks[19], (DEPTH, D_MODEL, N_EXPERT_GROUPS), D_MODEL ** -0.5),
        'router_b_group': nrm(ks[20], (DEPTH, N_EXPERT_GROUPS), 0.01),
        'router_w_expert': nrm(ks[21], (DEPTH, N_EXPERT_GROUPS, D_MODEL, EXPERTS_PER_GROUP), D_MODEL ** -0.5),
        'router_b_expert': nrm(ks[22], (DEPTH, N_EXPERT_GROUPS, EXPERTS_PER_GROUP), 0.01),
        'expert_w_gate': nrm(ks[23], (DEPTH, N_EXPERTS, D_MODEL, D_EXPERT), D_MODEL ** -0.5),
        'expert_w_up': nrm(ks[24], (DEPTH, N_EXPERTS, D_MODEL, D_EXPERT), D_MODEL ** -0.5),
        'expert_w_down': nrm(ks[25], (DEPTH, N_EXPERTS, D_EXPERT, D_MODEL), D_EXPERT ** -0.5),
        'norm_final': 1.0 + nrm(ks[26], (D_MODEL,), 0.02),
    }


def reference(x, norm_mix, w_in, s5_lam_re, s5_lam_im, s5_log_dt, s5_b_re, s5_b_im, s5_c_re, s5_c_im,
              s5_d, s5_w_glu, s5_b_glu, gdn_conv_w, gdn_a_log, gdn_dt_bias, gdn_norm_w, w_out, norm_ffn,
              router_w_group, router_b_group, router_w_expert, router_b_expert,
              expert_w_gate, expert_w_up, expert_w_down, norm_final):
    x = x.astype(jnp.float32)
    bsz, seq, d = x.shape
    cuts = [S5_WIDTH, S5_WIDTH + 3 * DN_WIDTH, S5_WIDTH + 4 * DN_WIDTH, S5_WIDTH + 4 * DN_WIDTH + 2 * DN_HEADS]
    for i in range(DEPTH):
        h = _rmsnorm(x, norm_mix[i])
        proj = h @ w_in[i]
        u_s5, qkv, z, a_raw, b_raw = jnp.split(proj, cuts, axis=-1)
        y_s5 = _s5_mixer(u_s5, s5_lam_re[i], s5_lam_im[i], s5_log_dt[i], s5_b_re[i], s5_b_im[i],
                         s5_c_re[i], s5_c_im[i], s5_d[i], s5_w_glu[i], s5_b_glu[i])
        y_dn = _gdn_mixer(qkv, z, a_raw.reshape(bsz, seq, 2, DN_HEADS), b_raw.reshape(bsz, seq, 2, DN_HEADS),
                          gdn_conv_w[i], gdn_a_log[i], gdn_dt_bias[i], gdn_norm_w[i])
        x = x + jnp.concatenate([y_s5, y_dn], axis=-1) @ w_out[i]
        h = _rmsnorm(x, norm_ffn[i]).reshape(bsz * seq, d)
        y = _hier_moe(h, router_w_group[i], router_b_group[i], router_w_expert[i], router_b_expert[i],
                      expert_w_gate[i], expert_w_up[i], expert_w_down[i])
        x = x + y.reshape(bsz, seq, d)
    return _rmsnorm(x, norm_final)
```

```python
import functools
import math

import jax
import jax.numpy as jnp
from jax import lax
from jax.experimental import pallas as pl
from jax.experimental.pallas import tpu as pltpu

F32 = jnp.float32
BF16 = jnp.bfloat16
HIGHEST = lax.Precision.HIGHEST

D_MODEL = 1024
S5_WIDTH = 512
S5_GROUP = 16
S5_GROUPS = 32
S5_STATE = 64
S5_MAX_RE = -1e-4
DN_HEADS = 4
DN_HEAD_DIM = 128
DN_WIDTH = 512
DN_CONV = 5
DN_CHUNK = 64
N_EXPERT_GROUPS = 4
EXPERTS_PER_GROUP = 8
N_EXPERTS = 32
D_EXPERT = 512
NORM_EPS = 1e-6

LANES = 128
SUBLANES = 8
VMEM_LIMIT = 56 * 1024 * 1024

S5_CHUNK = 16
S5_TILE = 128
ROW_TILE = 512
DN_STEP_CHUNKS = 8
MOE_BLOCK = 256
MOE_TOK_TILE = 256


def _params(*sem):
    return pltpu.CompilerParams(dimension_semantics=sem, vmem_limit_bytes=VMEM_LIMIT)


def _inproj_kernel(x_ref, nw_ref, w_ref, wab_ref, u_ref, qkv_ref, z_ref, ab_ref):
    x = x_ref[...]
    h = x * lax.rsqrt(jnp.mean(x * x, axis=-1, keepdims=True) + NORM_EPS) * nw_ref[...]
    hb = h.astype(BF16)
    u_ref[...] = jnp.dot(hb, w_ref[:, 0:S5_WIDTH], preferred_element_type=F32)
    qkv_ref[...] = jnp.dot(hb, w_ref[:, S5_WIDTH:S5_WIDTH + 3 * DN_WIDTH], preferred_element_type=F32)
    z_ref[...] = jnp.dot(hb, w_ref[:, S5_WIDTH + 3 * DN_WIDTH:S5_WIDTH + 4 * DN_WIDTH],
                         preferred_element_type=F32)
    ab_ref[...] = jnp.dot(h, wab_ref[...], precision=HIGHEST, preferred_element_type=F32)


def _inproj(x, norm_w, w_in):
    t = x.shape[0]
    n_main = S5_WIDTH + 4 * DN_WIDTH
    w_main = w_in[:, :n_main].astype(BF16)
    w_ab = jnp.pad(w_in[:, n_main:], ((0, 0), (0, LANES - 4 * DN_HEADS)))
    tm = min(ROW_TILE, t)
    row = lambda i: (i, 0)
    const = lambda i: (0, 0)
    return pl.pallas_call(
        _inproj_kernel,
        grid=(t // tm,),
        in_specs=[
            pl.BlockSpec((tm, D_MODEL), row),
            pl.BlockSpec((1, D_MODEL), const),
            pl.BlockSpec((D_MODEL, n_main), const),
            pl.BlockSpec((D_MODEL, LANES), const),
        ],
        out_specs=[
            pl.BlockSpec((tm, S5_WIDTH), row),
            pl.BlockSpec((tm, 3 * DN_WIDTH), row),
            pl.BlockSpec((tm, DN_WIDTH), row),
            pl.BlockSpec((tm, LANES), row),
        ],
        out_shape=[
            jax.ShapeDtypeStruct((t, S5_WIDTH), F32),
            jax.ShapeDtypeStruct((t, 3 * DN_WIDTH), F32),
            jax.ShapeDtypeStruct((t, DN_WIDTH), F32),
            jax.ShapeDtypeStruct((t, LANES), F32),
        ],
        compiler_params=_params("parallel"),
        name="inproj",
    )(x, norm_w.reshape(1, D_MODEL), w_main, w_ab)


def _s5_tables(lam_re, lam_im, log_dt, b_re, b_im, c_re, c_im):
    c_len = S5_CHUNK
    lr = jnp.minimum(lam_re.astype(F32), S5_MAX_RE)
    li = lam_im.astype(F32)
    dt = jnp.exp(log_dt.astype(F32))[..., None]
    zr, zi = lr * dt, li * dt
    e1 = jnp.exp(zr)
    ar, ai = e1 * jnp.cos(zi), e1 * jnp.sin(zi)
    den = lr * lr + li * li
    nr, ni = ar - 1.0, ai
    fr = (nr * lr + ni * li) / den
    fi = (ni * lr - nr * li) / den
    bbr = fr[..., None] * b_re - fi[..., None] * b_im
    bbi = fr[..., None] * b_im + fi[..., None] * b_re
    tau = jnp.arange(c_len + 1, dtype=F32)[:, None, None, None]
    mag = jnp.exp(tau * zr)
    pr, pi = mag * jnp.cos(tau * zi), mag * jnp.sin(tau * zi)
    m_r = pr[..., None] * bbr - pi[..., None] * bbi
    m_i = pr[..., None] * bbi + pi[..., None] * bbr
    kern = (jnp.einsum('dgpn,tdgnq->tdgpq', c_re, m_r[:c_len], precision=HIGHEST)
            - jnp.einsum('dgpn,tdgnq->tdgpq', c_im, m_i[:c_len], precision=HIGHEST))
    beta_r = c_re[None] * pr[:, :, :, None, :] - c_im[None] * pi[:, :, :, None, :]
    beta_i = c_re[None] * pi[:, :, :, None, :] + c_im[None] * pr[:, :, :, None, :]

    j = jnp.arange(c_len)
    tables = []
    for d in range(2):
        if d == 0:
            lag = j[None, :] - j[:, None]
            dist_in = c_len - 1 - j
            dist_out = j + 1
        else:
            lag = j[:, None] - j[None, :]
            dist_in = j
            dist_out = c_len - j
        mask = (lag >= 0).astype(F32)
        kd = kern[:, d]
        wt = kd[jnp.clip(lag, 0, c_len - 1)] * mask[:, :, None, None, None]
        wt = wt.transpose(2, 0, 4, 1, 3).reshape(S5_GROUPS, 256, 256)
        er = m_r[dist_in, d].transpose(1, 0, 3, 2).reshape(S5_GROUPS, 256, S5_STATE)
        ei = m_i[dist_in, d].transpose(1, 0, 3, 2).reshape(S5_GROUPS, 256, S5_STATE)
        we = jnp.concatenate([er, ei, ei, er], axis=-1)
        sr = beta_r[dist_out, d].transpose(1, 3, 0, 2).reshape(S5_GROUPS, S5_STATE, 256)
        si = beta_i[dist_out, d].transpose(1, 3, 0, 2).reshape(S5_GROUPS, S5_STATE, 256)
        ws = jnp.concatenate([sr, -si], axis=1)
        a_r, a_i = pr[c_len, d], pi[c_len, d]
        coef = jnp.stack([jnp.concatenate([a_r, a_r], -1),
                          jnp.concatenate([-a_i, a_i], -1),
                          jnp.concatenate([a_i, -a_i], -1)])
        tables.append((wt.astype(BF16), we.astype(BF16), ws.astype(BF16), coef))
    return tables


def _s5_scan_kernel(u_ref, wt_ref, we_ref, ws_ref, coef_ref, *rest, reverse, add_prev):
    if add_prev:
        prev_ref, y_ref, e_scr, es_scr, s_scr, carry = rest
    else:
        y_ref, e_scr, es_scr, s_scr, carry = rest
    n_rows = u_ref.shape[1]

    @pl.when(pl.program_id(0) == 0)
    def _():
        carry[...] = jnp.zeros_like(carry)

    def contrib(g, _):
        grp = pl.ds(pl.multiple_of(g * n_rows, n_rows), n_rows)
        e = jnp.dot(u_ref[g], we_ref[g], preferred_element_type=F32)
        e_scr[grp, :] = e[:, :LANES]
        es_scr[grp, :] = e[:, LANES:]
        return 0

    lax.fori_loop(0, S5_GROUPS, contrib, 0)

    c1, c2, c3 = coef_ref[0], coef_ref[1], coef_ref[2]

    def step(i, vs):
        v, vp = vs
        r = (n_rows - 1 - i) if reverse else i
        rows = pl.ds(r, S5_GROUPS, stride=n_rows)
        s_scr[rows, :] = v
        v_new = v * c1 + vp * c2 + e_scr[rows, :]
        vp_new = vp * c1 + v * c3 + es_scr[rows, :]
        return v_new, vp_new

    v, vp = lax.fori_loop(0, n_rows, step, (carry[0], carry[1]))
    carry[0] = v
    carry[1] = vp

    def emit(g, _):
        s_in = s_scr[pl.ds(pl.multiple_of(g * n_rows, n_rows), n_rows), :].astype(BF16)
        y = (jnp.dot(u_ref[g], wt_ref[g], preferred_element_type=F32)
             + jnp.dot(s_in, ws_ref[g], preferred_element_type=F32))
        if add_prev:
            y = y + prev_ref[g]
        y_ref[g] = y
        return 0

    lax.fori_loop(0, S5_GROUPS, emit, 0)


def _s5_direction(ug, tables, prev, reverse):
    n_chunks = ug.shape[1]
    rows = min(S5_TILE, n_chunks)
    n_tiles = n_chunks // rows
    wt, we, ws, coef = tables
    tile = (lambda i: (0, n_tiles - 1 - i, 0)) if reverse else (lambda i: (0, i, 0))
    const3 = lambda i: (0, 0, 0)
    in_specs = [
        pl.BlockSpec((S5_GROUPS, rows, 256), tile),
        pl.BlockSpec((S5_GROUPS, 256, 256), const3),
        pl.BlockSpec((S5_GROUPS, 256, 256), const3),
        pl.BlockSpec((S5_GROUPS, LANES, 256), const3),
        pl.BlockSpec((3, S5_GROUPS, LANES), const3),
    ]
    args = [ug, wt, we, ws, coef]
    if prev is not None:
        in_specs.append(pl.BlockSpec((S5_GROUPS, rows, 256), tile))
        args.append(prev)
    return pl.pallas_call(
        functools.partial(_s5_scan_kernel, reverse=reverse, add_prev=prev is not None),
        grid=(n_tiles,),
        in_specs=in_specs,
        out_specs=pl.BlockSpec((S5_GROUPS, rows, 256), tile),
        out_shape=jax.ShapeDtypeStruct((S5_GROUPS, n_chunks, 256), F32),
        scratch_shapes=[
            pltpu.VMEM((S5_GROUPS * rows, LANES), F32),
            pltpu.VMEM((S5_GROUPS * rows, LANES), F32),
            pltpu.VMEM((S5_GROUPS * rows, LANES), F32),
            pltpu.VMEM((2, S5_GROUPS, LANES), F32),
        ],
        compiler_params=_params("arbitrary"),
        name="s5_bwd" if reverse else "s5_fwd",
    )(*args)


def _s5_mix(u, tables):
    t = u.shape[0]
    n_chunks = t // S5_CHUNK
    ug = (u.reshape(n_chunks, S5_CHUNK, S5_GROUPS, S5_GROUP).transpose(2, 0, 1, 3)
          .reshape(S5_GROUPS, n_chunks, 256).astype(BF16))
    y = _s5_direction(ug, tables[0], None, False)
    y = _s5_direction(ug, tables[1], y, True)
    return (y.reshape(S5_GROUPS, n_chunks, S5_CHUNK, S5_GROUP).transpose(1, 2, 0, 3)
            .reshape(t, S5_WIDTH))


def _dn_prep_kernel(cur_ref, prev_ref, next_ref, ab_ref, cw_ref, gp_ref,
                    q_ref, k_ref, v_ref, gate_ref, ext):
    i = pl.program_id(0)
    tm = cur_ref.shape[0]
    pad = DN_CONV // 2
    ext[0:SUBLANES, :] = jnp.where(i > 0, prev_ref[...], 0.0)
    ext[SUBLANES:SUBLANES + tm, :] = cur_ref[...]
    ext[SUBLANES + tm:, :] = jnp.where(i < pl.num_programs(0) - 1, next_ref[...], 0.0)
    outs = (q_ref, k_ref, v_ref)
    for part in range(3):
        cols = slice(part * DN_WIDTH, (part + 1) * DN_WIDTH)
        acc = ext[pl.ds(SUBLANES - pad, tm), cols] * cw_ref[0:1, cols]
        for tap in range(1, DN_CONV):
            acc = acc + ext[pl.ds(SUBLANES - pad + tap, tm), cols] * cw_ref[tap:tap + 1, cols]
        act = acc * jax.nn.sigmoid(acc)
        if part == 2:
            v_ref[...] = act
            continue
        scale = DN_HEAD_DIM ** -0.5 if part == 0 else 1.0
        for h in range(DN_HEADS):
            hs = slice(h * DN_HEAD_DIM, (h + 1) * DN_HEAD_DIM)
            xh = act[:, hs]
            inv = lax.rsqrt(jnp.sum(xh * xh, axis=-1, keepdims=True) + NORM_EPS)
            outs[part][:, hs] = xh * inv * scale
    ab = ab_ref[...]
    lane = lax.broadcasted_iota(jnp.int32, ab.shape, 1)
    pre = ab + gp_ref[1:2, :]
    softplus = jnp.maximum(pre, 0.0) + jnp.log1p(jnp.exp(-jnp.abs(pre)))
    gate_ref[...] = jnp.where(lane < 2 * DN_HEADS, gp_ref[0:1, :] * softplus, jax.nn.sigmoid(ab))


def _dn_prep(qkv, ab, conv_w, a_log, dt_bias):
    t = qkv.shape[0]
    tm = min(ROW_TILE, t)
    nb = tm // SUBLANES
    last = t // SUBLANES - 1
    gp = jnp.zeros((SUBLANES, LANES), F32)
    gp = gp.at[0, :2 * DN_HEADS].set(-jnp.exp(a_log.astype(F32)).reshape(-1))
    gp = gp.at[1, :2 * DN_HEADS].set(dt_bias.astype(F32).reshape(-1))
    cw = jnp.pad(conv_w.astype(F32), ((0, SUBLANES - DN_CONV), (0, 0)))
    row = lambda i: (i, 0)
    const = lambda i: (0, 0)
    return pl.pallas_call(
        _dn_prep_kernel,
        grid=(t // tm,),
        in_specs=[
            pl.BlockSpec((tm, 3 * DN_WIDTH), row),
            pl.BlockSpec((SUBLANES, 3 * DN_WIDTH), lambda i: (jnp.maximum(i * nb - 1, 0), 0)),
            pl.BlockSpec((SUBLANES, 3 * DN_WIDTH), lambda i: (jnp.minimum((i + 1) * nb, last), 0)),
            pl.BlockSpec((tm, LANES), row),
            pl.BlockSpec((SUBLANES, 3 * DN_WIDTH), const),
            pl.BlockSpec((SUBLANES, LANES), const),
        ],
        out_specs=[
            pl.BlockSpec((tm, DN_WIDTH), row),
            pl.BlockSpec((tm, DN_WIDTH), row),
            pl.BlockSpec((tm, DN_WIDTH), row),
            pl.BlockSpec((tm, LANES), row),
        ],
        out_shape=[
            jax.ShapeDtypeStruct((t, DN_WIDTH), F32),
            jax.ShapeDtypeStruct((t, DN_WIDTH), F32),
            jax.ShapeDtypeStruct((t, DN_WIDTH), F32),
            jax.ShapeDtypeStruct((t, LANES), F32),
        ],
        scratch_shapes=[pltpu.VMEM((tm + 2 * SUBLANES, 3 * DN_WIDTH), F32)],
        compiler_params=_params("parallel"),
        name="dn_prep",
    )(qkv, qkv, qkv, ab, cw, gp)


def _dn_chunk_kernel(q_ref, k_ref, v_ref, gate_ref, gate_t_ref, o_ref, state, *, reverse):
    c_len = DN_CHUNK
    n_ch = gate_t_ref.shape[0]
    d = 1 if reverse else 0

    @pl.when(pl.program_id(0) == 0)
    def _():
        state[...] = jnp.zeros_like(state)

    ri = lax.broadcasted_iota(jnp.int32, (c_len, c_len), 0)
    ci = lax.broadcasted_iota(jnp.int32, (c_len, c_len), 1)
    if reverse:
        incl, strict = ri <= ci, ri < ci
    else:
        incl, strict = ri >= ci, ri > ci
    tri = jnp.where(incl, 1.0, 0.0).astype(F32)
    tri_t = jnp.where((ri >= ci) if reverse else (ri <= ci), 1.0, 0.0).astype(F32)
    eye = jnp.where(ri == ci, 1.0, 0.0).astype(F32)
    last = 0 if reverse else c_len - 1

    def chunk(cidx, _):
        cc = (n_ch - 1 - cidx) if reverse else cidx
        rows = pl.ds(pl.multiple_of(cc * c_len, c_len), c_len)
        gates = gate_ref[rows, :]
        gc_col = jnp.dot(tri, gates, precision=HIGHEST, preferred_element_type=F32)
        gc_row = jnp.dot(gate_t_ref[cc], tri_t, precision=HIGHEST, preferred_element_type=F32)
        for h in range(DN_HEADS):
            idx = d * DN_HEADS + h
            hs = slice(h * DN_HEAD_DIM, (h + 1) * DN_HEAD_DIM)
            gcol = gc_col[:, idx:idx + 1]
            grow = gc_row[idx:idx + 1, :]
            bcol = gates[:, 2 * DN_HEADS + idx:2 * DN_HEADS + idx + 1]
            glast = grow[:, last:last + 1]
            qh, kh, vh = q_ref[rows, hs], k_ref[rows, hs], v_ref[rows, hs]
            kb = kh.astype(BF16)
            qk_kk = lax.dot_general(jnp.concatenate([qh.astype(BF16), kb], axis=0), kb,
                                    (((1,), (1,)), ((), ())), preferred_element_type=F32)
            qk, kk = qk_kk[:c_len], qk_kk[c_len:]
            decay = jnp.where(incl, jnp.exp(jnp.where(incl, gcol - grow, 0.0)), 0.0)
            a_mat = jnp.where(strict, bcol * kk * decay, 0.0)
            pw = -a_mat
            inv = eye + pw
            for _ in range(5):
                pwb = pw.astype(BF16)
                pw = jnp.dot(pwb, pwb, preferred_element_type=F32)
                inv = inv + jnp.dot(inv.astype(BF16), pw.astype(BF16), preferred_element_type=F32)
            egc = jnp.exp(gcol)
            rhs = jnp.concatenate([vh * bcol, kh * (bcol * egc)], axis=1).astype(BF16)
            uw = jnp.dot(inv.astype(BF16), rhs, preferred_element_type=F32)
            u_c, w_c = uw[:, :DN_HEAD_DIM], uw[:, DN_HEAD_DIM:]
            attn = jnp.where(incl, qk * decay, 0.0)
            s_old = state[h]
            wq = jnp.concatenate([w_c, qh * egc], axis=0).astype(BF16)
            wq_s = jnp.dot(wq, s_old.astype(BF16), preferred_element_type=F32)
            v_new = u_c - wq_s[:c_len]
            v_nb = v_new.astype(BF16)
            o_ref[rows, hs] = wq_s[c_len:] + jnp.dot(attn.astype(BF16), v_nb, preferred_element_type=F32)
            k_dec = (kh * jnp.exp(glast - gcol)).astype(BF16)
            state[h] = s_old * jnp.exp(glast) + lax.dot_general(
                k_dec, v_nb, (((0,), (0,)), ((), ())), preferred_element_type=F32)
        return 0

    lax.fori_loop(0, n_ch, chunk, 0)


def _dn_direction(q, k, v, gates, gates_t, reverse):
    t = q.shape[0]
    n_chunks = t // DN_CHUNK
    n_ch = min(DN_STEP_CHUNKS, n_chunks)
    n_steps = n_chunks // n_ch
    rows = n_ch * DN_CHUNK
    row = (lambda i: (n_steps - 1 - i, 0)) if reverse else (lambda i: (i, 0))
    row3 = (lambda i: (n_steps - 1 - i, 0, 0)) if reverse else (lambda i: (i, 0, 0))
    return pl.pallas_call(
        functools.partial(_dn_chunk_kernel, reverse=reverse),
        grid=(n_steps,),
        in_specs=[
            pl.BlockSpec((rows, DN_WIDTH), row),
            pl.BlockSpec((rows, DN_WIDTH), row),
            pl.BlockSpec((rows, DN_WIDTH), row),
            pl.BlockSpec((rows, LANES), row),
            pl.BlockSpec((n_ch, 4 * DN_HEADS, DN_CHUNK), row3),
        ],
        out_specs=pl.BlockSpec((rows, DN_WIDTH), row),
        out_shape=jax.ShapeDtypeStruct((t, DN_WIDTH), F32),
        scratch_shapes=[pltpu.VMEM((DN_HEADS, DN_HEAD_DIM, DN_HEAD_DIM), F32)],
        compiler_params=_params("arbitrary"),
        name="dn_bwd" if reverse else "dn_fwd",
    )(q, k, v, gates, gates_t)


def _post_kernel(x_ref, u_ref, ys_ref, of_ref, ob_ref, z_ref, d_ref, wglu_ref, bglu_ref, nw_ref,
                 wout_ref, nffn_ref, wr_ref, br_ref, x1_ref, h_ref, route_ref, cnt_ref, base):
    i = pl.program_id(0)
    tm = x_ref.shape[0]

    @pl.when(i == 0)
    def _():
        base[...] = jnp.zeros_like(base)

    y = ys_ref[...] + d_ref[...] * u_ref[...]
    y = 0.5 * y * (1.0 + lax.erf(y * (2.0 ** -0.5)))
    gate = jnp.dot(y.astype(BF16), wglu_ref[...], preferred_element_type=F32) + bglu_ref[...]
    y_s5 = y * jax.nn.sigmoid(gate)
    acc = x_ref[...] + jnp.dot(y_s5.astype(BF16), wout_ref[0:S5_WIDTH, :], preferred_element_type=F32)
    for h in range(DN_HEADS):
        hs = slice(h * DN_HEAD_DIM, (h + 1) * DN_HEAD_DIM)
        o = of_ref[:, hs] + ob_ref[:, hs]
        zh = z_ref[:, hs]
        o = o * lax.rsqrt(jnp.mean(o * o, axis=-1, keepdims=True) + NORM_EPS) * nw_ref[...]
        y_dn = o * (zh * jax.nn.sigmoid(zh))
        acc = acc + jnp.dot(y_dn.astype(BF16),
                            wout_ref[S5_WIDTH + h * DN_HEAD_DIM:S5_WIDTH + (h + 1) * DN_HEAD_DIM, :],
                            preferred_element_type=F32)
    x1_ref[...] = acc
    hn = acc * lax.rsqrt(jnp.mean(acc * acc, axis=-1, keepdims=True) + NORM_EPS) * nffn_ref[...]
    h_ref[...] = hn

    logits = jnp.dot(hn, wr_ref[...], precision=HIGHEST, preferred_element_type=F32) + br_ref[...]
    lane_i = lax.broadcasted_iota(jnp.int32, logits.shape, 1)
    lane = lane_i.astype(F32)
    neg = jnp.float32(-jnp.inf)
    big = jnp.float32(LANES)
    gl = jnp.where(lane_i < N_EXPERT_GROUPS, logits, neg)
    gmax = jnp.max(gl, axis=-1, keepdims=True)
    g_sel = jnp.min(jnp.where(gl == gmax, lane, big), axis=-1, keepdims=True)
    p_group = 1.0 / jnp.sum(jnp.exp(gl - gmax), axis=-1, keepdims=True)
    lo = N_EXPERT_GROUPS + g_sel * EXPERTS_PER_GROUP
    el = jnp.where((lane >= lo) & (lane < lo + EXPERTS_PER_GROUP), logits, neg)
    top1 = jnp.max(el, axis=-1, keepdims=True)
    idx1 = jnp.min(jnp.where(el == top1, lane, big), axis=-1, keepdims=True)
    el2 = jnp.where(lane == idx1, neg, el)
    top2 = jnp.max(el2, axis=-1, keepdims=True)
    idx2 = jnp.min(jnp.where(el2 == top2, lane, big), axis=-1, keepdims=True)
    e21 = jnp.exp(top2 - top1)
    w1 = p_group / (1.0 + e21)
    w2 = w1 * e21
    oh1 = jnp.where(lane == idx1, 1.0, 0.0).astype(F32)
    oh2 = jnp.where(lane == idx2, 1.0, 0.0).astype(F32)
    ri = lax.broadcasted_iota(jnp.int32, (tm, tm), 0)
    ci = lax.broadcasted_iota(jnp.int32, (tm, tm), 1)
    before = jnp.where(ri > ci, 1.0, 0.0).astype(BF16)
    ohs = oh1 + oh2
    prior = jnp.dot(before, ohs.astype(BF16), preferred_element_type=F32) + base[0:1, :]
    rank1 = jnp.sum(oh1 * prior, axis=-1, keepdims=True)
    rank2 = jnp.sum(oh2 * prior, axis=-1, keepdims=True)
    base[0:1, :] = base[0:1, :] + jnp.sum(ohs, axis=0, keepdims=True)
    e1 = idx1 - N_EXPERT_GROUPS
    e2 = idx2 - N_EXPERT_GROUPS
    route = jnp.where(lane_i == 0, e1, jnp.where(lane_i == 1, e2, jnp.where(lane_i == 2, w1, jnp.where(
        lane_i == 3, w2, jnp.where(lane_i == 4, rank1, jnp.where(lane_i == 5, rank2, 0.0))))))
    route_ref[...] = route
    cnt_ref[...] = base[...]


def _post(x, u, ys, o_f, o_b, z, s5_d, w_glu, b_glu, dn_norm_w, w_out, norm_ffn, w_rg, b_rg, w_re, b_re):
    t = x.shape[0]
    tm = min(ROW_TILE, t)
    wr = jnp.concatenate([w_rg, w_re.transpose(1, 0, 2).reshape(D_MODEL, N_EXPERTS)], axis=1)
    wr = jnp.pad(wr.astype(F32), ((0, 0), (0, LANES - N_EXPERT_GROUPS - N_EXPERTS)))
    br = jnp.pad(jnp.concatenate([b_rg, b_re.reshape(-1)]).astype(F32),
                 (0, LANES - N_EXPERT_GROUPS - N_EXPERTS)).reshape(1, LANES)
    row = lambda i: (i, 0)
    const = lambda i: (0, 0)
    return pl.pallas_call(
        _post_kernel,
        grid=(t // tm,),
        in_specs=[
            pl.BlockSpec((tm, D_MODEL), row),
            pl.BlockSpec((tm, S5_WIDTH), row),
            pl.BlockSpec((tm, S5_WIDTH), row),
            pl.BlockSpec((tm, DN_WIDTH), row),
            pl.BlockSpec((tm, DN_WIDTH), row),
            pl.BlockSpec((tm, DN_WIDTH), row),
            pl.BlockSpec((1, S5_WIDTH), const),
            pl.BlockSpec((S5_WIDTH, S5_WIDTH), const),
            pl.BlockSpec((1, S5_WIDTH), const),
            pl.BlockSpec((1, DN_HEAD_DIM), const),
            pl.BlockSpec((D_MODEL, D_MODEL), const),
            pl.BlockSpec((1, D_MODEL), const),
            pl.BlockSpec((D_MODEL, LANES), const),
            pl.BlockSpec((1, LANES), const),
        ],
        out_specs=[
            pl.BlockSpec((tm, D_MODEL), row),
            pl.BlockSpec((tm, D_MODEL), row),
            pl.BlockSpec((tm, LANES), row),
            pl.BlockSpec((SUBLANES, LANES), const),
        ],
        out_shape=[
            jax.ShapeDtypeStruct((t, D_MODEL), F32),
            jax.ShapeDtypeStruct((t, D_MODEL), F32),
            jax.ShapeDtypeStruct((t, LANES), F32),
            jax.ShapeDtypeStruct((SUBLANES, LANES), F32),
        ],
        scratch_shapes=[pltpu.VMEM((SUBLANES, LANES), F32)],
        compiler_params=_params("arbitrary"),
        name="mixer_post",
    )(x, u, ys, o_f, o_b, z, s5_d.reshape(1, -1).astype(F32), w_glu.astype(BF16),
      b_glu.reshape(1, -1).astype(F32), dn_norm_w.reshape(1, -1).astype(F32), w_out.astype(BF16),
      norm_ffn.reshape(1, -1).astype(F32), wr, br)


def _dispatch_kernel(dest_ref, h_ref, zeros_ref, xs_ref, sem):
    del zeros_ref
    i = pl.program_id(0)
    tm = h_ref.shape[0]

    def copy(r, k):
        slot = dest_ref[2 * (i * tm + r) + k]
        return pltpu.make_async_copy(h_ref.at[pl.ds(r, 1), :], xs_ref.at[pl.ds(slot, 1), :], sem)

    def start(r, _):
        copy(r, 0).start()
        copy(r, 1).start()
        return 0

    def wait(r, _):
        copy(r, 0).wait()
        copy(r, 1).wait()
        return 0

    lax.fori_loop(0, tm, start, 0)
    lax.fori_loop(0, tm, wait, 0)


def _dispatch(h, dest, n_slots):
    t = h.shape[0]
    tm = min(MOE_TOK_TILE, t)
    grid_spec = pltpu.PrefetchScalarGridSpec(
        num_scalar_prefetch=1,
        grid=(t // tm,),
        in_specs=[
            pl.BlockSpec((tm, D_MODEL), lambda i, dest: (i, 0)),
            pl.BlockSpec(memory_space=pl.ANY),
        ],
        out_specs=pl.BlockSpec(memory_space=pl.ANY),
        scratch_shapes=[pltpu.SemaphoreType.DMA(())],
    )
    return pl.pallas_call(
        _dispatch_kernel,
        grid_spec=grid_spec,
        out_shape=jax.ShapeDtypeStruct((n_slots, D_MODEL), F32),
        input_output_aliases={2: 0},
        compiler_params=_params("arbitrary"),
        name="moe_dispatch",
    )(dest, h, jnp.zeros((n_slots, D_MODEL), F32))


def _expert_kernel(be_ref, used_ref, xs_ref, wg_ref, wu_ref, wd_ref, ys_ref):
    del be_ref
    i = pl.program_id(0)

    @pl.when(i < used_ref[0])
    def _():
        xb = xs_ref[...].astype(BF16)
        g = jnp.dot(xb, wg_ref[...].astype(BF16), preferred_element_type=F32)
        u = jnp.dot(xb, wu_ref[...].astype(BF16), preferred_element_type=F32)
        hid = (g * jax.nn.sigmoid(g) * u).astype(BF16)
        ys_ref[...] = jnp.dot(hid, wd_ref[...].astype(BF16), preferred_element_type=F32)

    @pl.when(i >= used_ref[0])
    def _():
        ys_ref[...] = jnp.zeros_like(ys_ref)


def _experts(xs, block_expert, used, w_gate, w_up, w_down, layer):
    n_slots = xs.shape[0]
    n_blocks = n_slots // MOE_BLOCK
    grid_spec = pltpu.PrefetchScalarGridSpec(
        num_scalar_prefetch=2,
        grid=(n_blocks,),
        in_specs=[
            pl.BlockSpec((MOE_BLOCK, D_MODEL), lambda i, be, used: (i, 0)),
            pl.BlockSpec((None, None, D_MODEL, D_EXPERT), lambda i, be, used: (layer, be[i], 0, 0)),
            pl.BlockSpec((None, None, D_MODEL, D_EXPERT), lambda i, be, used: (layer, be[i], 0, 0)),
            pl.BlockSpec((None, None, D_EXPERT, D_MODEL), lambda i, be, used: (layer, be[i], 0, 0)),
        ],
        out_specs=pl.BlockSpec((MOE_BLOCK, D_MODEL), lambda i, be, used: (i, 0)),
    )
    return pl.pallas_call(
        _expert_kernel,
        grid_spec=grid_spec,
        out_shape=jax.ShapeDtypeStruct((n_slots, D_MODEL), F32),
        compiler_params=_params("arbitrary"),
        name="moe_experts",
    )(block_expert, used, xs, w_gate, w_up, w_down)


def _combine_kernel(dest_ref, x1_ref, route_ref, ys_ref, nw_ref, out_ref, buf, sem, *, final_norm):
    i = pl.program_id(0)
    tm = x1_ref.shape[0]

    def copy(r, k):
        slot = dest_ref[2 * (i * tm + r) + k]
        return pltpu.make_async_copy(ys_ref.at[pl.ds(slot, 1), :], buf.at[k, pl.ds(r, 1), :], sem)

    def start(r, _):
        copy(r, 0).start()
        copy(r, 1).start()
        return 0

    def wait(r, _):
        copy(r, 0).wait()
        copy(r, 1).wait()
        return 0

    lax.fori_loop(0, tm, start, 0)
    lax.fori_loop(0, tm, wait, 0)
    route = route_ref[...]
    out = x1_ref[...] + route[:, 2:3] * buf[0] + route[:, 3:4] * buf[1]
    if final_norm:
        out = out * lax.rsqrt(jnp.mean(out * out, axis=-1, keepdims=True) + NORM_EPS) * nw_ref[...]
    out_ref[...] = out


def _combine(x1, route, ys, dest, norm_w, final_norm):
    t = x1.shape[0]
    tm = min(MOE_TOK_TILE, t)
    grid_spec = pltpu.PrefetchScalarGridSpec(
        num_scalar_prefetch=1,
        grid=(t // tm,),
        in_specs=[
            pl.BlockSpec((tm, D_MODEL), lambda i, dest: (i, 0)),
            pl.BlockSpec((tm, LANES), lambda i, dest: (i, 0)),
            pl.BlockSpec(memory_space=pl.ANY),
            pl.BlockSpec((1, D_MODEL), lambda i, dest: (0, 0)),
        ],
        out_specs=pl.BlockSpec((tm, D_MODEL), lambda i, dest: (i, 0)),
        scratch_shapes=[pltpu.VMEM((2, tm, D_MODEL), F32), pltpu.SemaphoreType.DMA(())],
    )
    return pl.pallas_call(
        functools.partial(_combine_kernel, final_norm=final_norm),
        grid_spec=grid_spec,
        out_shape=jax.ShapeDtypeStruct((t, D_MODEL), F32),
        compiler_params=_params("arbitrary"),
        name="moe_combine",
    )(dest, x1, route, ys, norm_w.reshape(1, D_MODEL).astype(F32))


def _moe(x1, h, route, counts, w_gate, w_up, w_down, layer, norm_w, final_norm):
    t = x1.shape[0]
    n_blocks = -(-(2 * t) // MOE_BLOCK) + N_EXPERTS
    n_slots = n_blocks * MOE_BLOCK
    cnt = counts[0, N_EXPERT_GROUPS:N_EXPERT_GROUPS + N_EXPERTS].astype(jnp.int32)
    padded = ((cnt + MOE_BLOCK - 1) // MOE_BLOCK) * MOE_BLOCK
    pad_end = jnp.cumsum(padded)
    pad_start = pad_end - padded
    expert = route[:, 0:2].astype(jnp.int32)
    rank = route[:, 4:6].astype(jnp.int32)
    dest = (pad_start[expert] + rank).reshape(-1)
    block_start = jnp.arange(n_blocks, dtype=jnp.int32) * MOE_BLOCK
    block_expert = jnp.minimum(jnp.searchsorted(pad_end, block_start, side='right'),
                               N_EXPERTS - 1).astype(jnp.int32)
    used = (pad_end[-1:] // MOE_BLOCK).astype(jnp.int32)
    xs = _dispatch(h, dest, n_slots)
    ys = _experts(xs, block_expert, used, w_gate, w_up, w_down, layer)
    return _combine(x1, route, ys, dest, norm_w, final_norm)


def kernel(x, norm_mix, w_in, s5_lam_re, s5_lam_im, s5_log_dt, s5_b_re, s5_b_im, s5_c_re, s5_c_im,
           s5_d, s5_w_glu, s5_b_glu, gdn_conv_w, gdn_a_log, gdn_dt_bias, gdn_norm_w, w_out, norm_ffn,
           router_w_group, router_b_group, router_w_expert, router_b_expert,
           expert_w_gate, expert_w_up, expert_w_down, norm_final):
    bsz, seq, d = x.shape
    depth = norm_mix.shape[0]
    xt = x.astype(F32).reshape(bsz * seq, d)
    for i in range(depth):
        u, qkv, z, ab = _inproj(xt, norm_mix[i], w_in[i])
        tables = _s5_tables(s5_lam_re[i], s5_lam_im[i], s5_log_dt[i], s5_b_re[i], s5_b_im[i],
                            s5_c_re[i], s5_c_im[i])
        ys = _s5_mix(u, tables)
        q, k, v, gates = _dn_prep(qkv, ab, gdn_conv_w[i], gdn_a_log[i], gdn_dt_bias[i])
        gates_t = (gates[:, :4 * DN_HEADS].reshape(-1, DN_CHUNK, 4 * DN_HEADS).transpose(0, 2, 1))
        o_f = _dn_direction(q, k, v, gates, gates_t, False)
        o_b = _dn_direction(q, k, v, gates, gates_t, True)
        x1, h, route, counts = _post(xt, u, ys, o_f, o_b, z, s5_d[i], s5_w_glu[i], s5_b_glu[i],
                                     gdn_norm_w[i], w_out[i], norm_ffn[i], router_w_group[i],
                                     router_b_group[i], router_w_expert[i], router_b_expert[i])
        xt = _moe(x1, h, route, counts, expert_w_gate, expert_w_up, expert_w_down, i,
                  norm_final, i == depth - 1)
    return xt.reshape(bsz, seq, d)
```

```python
import functools
import math

import jax
import jax.numpy as jnp
from jax import lax
from jax.experimental import pallas as pl
from jax.experimental.pallas import tpu as pltpu

F32 = jnp.float32
BF16 = jnp.bfloat16
HIGHEST = lax.Precision.HIGHEST

D_MODEL = 1024
S5_WIDTH = 512
S5_GROUP = 16
S5_GROUPS = 32
S5_STATE = 64
S5_MAX_RE = -1e-4
DN_HEADS = 4
DN_HEAD_DIM = 128
DN_WIDTH = 512
DN_CONV = 5
DN_CHUNK = 64
N_EXPERT_GROUPS = 4
EXPERTS_PER_GROUP = 8
N_EXPERTS = 32
D_EXPERT = 512
NORM_EPS = 1e-6

LANES = 128
SUBLANES = 8
VMEM_LIMIT = 56 * 1024 * 1024

S5_CHUNK = 16
S5_TILE = 128
ROW_TILE = 512
DN_STEP_CHUNKS = 8
MOE_BLOCK = 256
MOE_TOK_TILE = 256


def _params(*sem):
    return pltpu.CompilerParams(dimension_semantics=sem, vmem_limit_bytes=VMEM_LIMIT)


def _inproj_kernel(x_ref, nw_ref, w_ref, wab_ref, u_ref, qkv_ref, z_ref, ab_ref):
    x = x_ref[...]
    h = x * lax.rsqrt(jnp.mean(x * x, axis=-1, keepdims=True) + NORM_EPS) * nw_ref[...]
    hb = h.astype(BF16)
    for blk in range(S5_WIDTH // LANES):
        u_ref[blk] = jnp.dot(hb, w_ref[:, blk * LANES:(blk + 1) * LANES], preferred_element_type=F32)
    qkv_ref[...] = jnp.dot(hb, w_ref[:, S5_WIDTH:S5_WIDTH + 3 * DN_WIDTH], preferred_element_type=F32)
    z_ref[...] = jnp.dot(hb, w_ref[:, S5_WIDTH + 3 * DN_WIDTH:S5_WIDTH + 4 * DN_WIDTH],
                         preferred_element_type=F32)
    ab_ref[...] = jnp.dot(h, wab_ref[...], precision=HIGHEST, preferred_element_type=F32)


def _inproj(x, norm_w, w_in):
    t = x.shape[0]
    n_main = S5_WIDTH + 4 * DN_WIDTH
    w_main = w_in[:, :n_main].astype(BF16)
    w_ab = jnp.pad(w_in[:, n_main:], ((0, 0), (0, LANES - 4 * DN_HEADS)))
    tm = min(ROW_TILE, t)
    row = lambda i: (i, 0)
    const = lambda i: (0, 0)
    return pl.pallas_call(
        _inproj_kernel,
        grid=(t // tm,),
        in_specs=[
            pl.BlockSpec((tm, D_MODEL), row),
            pl.BlockSpec((1, D_MODEL), const),
            pl.BlockSpec((D_MODEL, n_main), const),
            pl.BlockSpec((D_MODEL, LANES), const),
        ],
        out_specs=[
            pl.BlockSpec((S5_WIDTH // LANES, tm, LANES), lambda i: (0, i, 0)),
            pl.BlockSpec((tm, 3 * DN_WIDTH), row),
            pl.BlockSpec((tm, DN_WIDTH), row),
            pl.BlockSpec((tm, LANES), row),
        ],
        out_shape=[
            jax.ShapeDtypeStruct((S5_WIDTH // LANES, t, LANES), F32),
            jax.ShapeDtypeStruct((t, 3 * DN_WIDTH), F32),
            jax.ShapeDtypeStruct((t, DN_WIDTH), F32),
            jax.ShapeDtypeStruct((t, LANES), F32),
        ],
        compiler_params=_params("parallel"),
        name="inproj",
    )(x, norm_w.reshape(1, D_MODEL), w_main, w_ab)


def _s5_tables(lam_re, lam_im, log_dt, b_re, b_im, c_re, c_im):
    c_len = S5_CHUNK
    lr = jnp.minimum(lam_re.astype(F32), S5_MAX_RE)
    li = lam_im.astype(F32)
    dt = jnp.exp(log_dt.astype(F32))[..., None]
    zr, zi = lr * dt, li * dt
    e1 = jnp.exp(zr)
    ar, ai = e1 * jnp.cos(zi), e1 * jnp.sin(zi)
    den = lr * lr + li * li
    nr, ni = ar - 1.0, ai
    fr = (nr * lr + ni * li) / den
    fi = (ni * lr - nr * li) / den
    bbr = fr[..., None] * b_re - fi[..., None] * b_im
    bbi = fr[..., None] * b_im + fi[..., None] * b_re
    tau = jnp.arange(c_len + 1, dtype=F32)[:, None, None, None]
    mag = jnp.exp(tau * zr)
    pr, pi = mag * jnp.cos(tau * zi), mag * jnp.sin(tau * zi)
    m_r = pr[..., None] * bbr - pi[..., None] * bbi
    m_i = pr[..., None] * bbi + pi[..., None] * bbr
    kern = (jnp.einsum('dgpn,tdgnq->tdgpq', c_re, m_r[:c_len], precision=HIGHEST)
            - jnp.einsum('dgpn,tdgnq->tdgpq', c_im, m_i[:c_len], precision=HIGHEST))
    beta_r = c_re[None] * pr[:, :, :, None, :] - c_im[None] * pi[:, :, :, None, :]
    beta_i = c_re[None] * pi[:, :, :, None, :] + c_im[None] * pr[:, :, :, None, :]

    j = jnp.arange(c_len)
    tables = []
    for d in range(2):
        if d == 0:
            lag = j[None, :] - j[:, None]
            dist_in = c_len - 1 - j
            dist_out = j + 1
        else:
            lag = j[:, None] - j[None, :]
            dist_in = j
            dist_out = c_len - j
        mask = (lag >= 0).astype(F32)
        kd = kern[:, d]
        wt = kd[jnp.clip(lag, 0, c_len - 1)] * mask[:, :, None, None, None]
        wt = wt.transpose(2, 0, 4, 1, 3).reshape(S5_GROUPS, 256, 256)
        er = m_r[dist_in, d].transpose(1, 0, 3, 2).reshape(S5_GROUPS, 256, S5_STATE)
        ei = m_i[dist_in, d].transpose(1, 0, 3, 2).reshape(S5_GROUPS, 256, S5_STATE)
        we = jnp.concatenate([er, ei, ei, er], axis=-1)
        sr = beta_r[dist_out, d].transpose(1, 3, 0, 2).reshape(S5_GROUPS, S5_STATE, 256)
        si = beta_i[dist_out, d].transpose(1, 3, 0, 2).reshape(S5_GROUPS, S5_STATE, 256)
        ws = jnp.concatenate([sr, -si], axis=1)
        a_r, a_i = pr[c_len, d], pi[c_len, d]
        coef = jnp.stack([jnp.concatenate([a_r, a_r], -1),
                          jnp.concatenate([-a_i, a_i], -1),
                          jnp.concatenate([a_i, -a_i], -1)])
        tables.append((wt.astype(BF16), we.astype(BF16), ws.astype(BF16), coef))
    return tables


def _block_transpose8(xs, lane):
    for k in (2, 1, 0):
        shift = S5_GROUP << k
        bit = jnp.bitwise_and(lax.shift_right_logical(lane, 4 + k), 1)
        new = list(xs)
        for a in range(8):
            if (a >> k) & 1:
                continue
            b = a + (1 << k)
            new[a] = jnp.where(bit == 0, xs[a], pltpu.roll(xs[b], shift, 1))
            new[b] = jnp.where(bit == 1, xs[b], pltpu.roll(xs[a], LANES - shift, 1))
        xs = new
    return xs


def _s5_scan_kernel(u_ref, wt_ref, we_ref, ws_ref, coef_ref, *rest, reverse, add_prev):
    if add_prev:
        prev_ref, y_ref, ug_scr, yg_scr, e_scr, es_scr, s_scr, carry = rest
    else:
        y_ref, ug_scr, yg_scr, e_scr, es_scr, s_scr, carry = rest
    n_rows = ug_scr.shape[1]
    n_blk = S5_WIDTH // LANES
    per_blk = LANES // S5_GROUP
    halves = S5_CHUNK // per_blk

    @pl.when(pl.program_id(0) == 0)
    def _():
        carry[...] = jnp.zeros_like(carry)

    lane = lax.broadcasted_iota(jnp.int32, (n_rows, LANES), 1)

    for blk in range(n_blk):
        for half in range(halves):
            xs = [u_ref[blk, pl.ds(half * per_blk + jl, n_rows, stride=S5_CHUNK), :] for jl in range(per_blk)]
            ys = _block_transpose8(xs, lane)
            for gl in range(per_blk):
                ug_scr[blk * per_blk + gl, :, half * LANES:(half + 1) * LANES] = ys[gl].astype(BF16)

    for g in range(S5_GROUPS):
        e = jnp.dot(ug_scr[g], we_ref[g], preferred_element_type=F32)
        e_scr[g * n_rows:(g + 1) * n_rows, :] = e[:, :LANES]
        es_scr[g * n_rows:(g + 1) * n_rows, :] = e[:, LANES:]

    c1, c2, c3 = coef_ref[0], coef_ref[1], coef_ref[2]

    def step(i, vs):
        v, vp = vs
        r = (n_rows - 1 - i) if reverse else i
        rows = pl.ds(r, S5_GROUPS, stride=n_rows)
        s_scr[rows, :] = v
        v_new = v * c1 + vp * c2 + e_scr[rows, :]
        vp_new = vp * c1 + v * c3 + es_scr[rows, :]
        return v_new, vp_new

    v, vp = lax.fori_loop(0, n_rows, step, (carry[0], carry[1]), unroll=8)
    carry[0] = v
    carry[1] = vp

    for blk in range(n_blk):
        for gl in range(per_blk):
            g = blk * per_blk + gl
            s_in = s_scr[g * n_rows:(g + 1) * n_rows, :].astype(BF16)
            yg_scr[gl] = (jnp.dot(ug_scr[g], wt_ref[g], preferred_element_type=F32)
                          + jnp.dot(s_in, ws_ref[g], preferred_element_type=F32))
        for half in range(halves):
            zs = [yg_scr[gl, :, half * LANES:(half + 1) * LANES] for gl in range(per_blk)]
            ws = _block_transpose8(zs, lane)
            for tl in range(per_blk):
                rows = pl.ds(half * per_blk + tl, n_rows, stride=S5_CHUNK)
                out = ws[tl]
                if add_prev:
                    out = out + prev_ref[blk, rows, :]
                y_ref[blk, rows, :] = out


def _s5_direction(u4, tables, prev, reverse):
    n_blk, t, _ = u4.shape
    n_chunks = t // S5_CHUNK
    rows = min(S5_TILE, n_chunks)
    n_tiles = n_chunks // rows
    wt, we, ws, coef = tables
    tile = (lambda i: (0, n_tiles - 1 - i, 0)) if reverse else (lambda i: (0, i, 0))
    const3 = lambda i: (0, 0, 0)
    once = pl.Buffered(1)
    in_specs = [
        pl.BlockSpec((n_blk, rows * S5_CHUNK, LANES), tile),
        pl.BlockSpec((S5_GROUPS, 256, 256), const3, pipeline_mode=once),
        pl.BlockSpec((S5_GROUPS, 256, 256), const3, pipeline_mode=once),
        pl.BlockSpec((S5_GROUPS, LANES, 256), const3, pipeline_mode=once),
        pl.BlockSpec((3, S5_GROUPS, LANES), const3, pipeline_mode=once),
    ]
    args = [u4, wt, we, ws, coef]
    if prev is not None:
        in_specs.append(pl.BlockSpec((n_blk, rows * S5_CHUNK, LANES), tile))
        args.append(prev)
    return pl.pallas_call(
        functools.partial(_s5_scan_kernel, reverse=reverse, add_prev=prev is not None),
        grid=(n_tiles,),
        in_specs=in_specs,
        out_specs=pl.BlockSpec((n_blk, rows * S5_CHUNK, LANES), tile),
        out_shape=jax.ShapeDtypeStruct((n_blk, t, LANES), F32),
        scratch_shapes=[
            pltpu.VMEM((S5_GROUPS, rows, 256), BF16),
            pltpu.VMEM((LANES // S5_GROUP, rows, 256), F32),
            pltpu.VMEM((S5_GROUPS * rows, LANES), F32),
            pltpu.VMEM((S5_GROUPS * rows, LANES), F32),
            pltpu.VMEM((S5_GROUPS * rows, LANES), F32),
            pltpu.VMEM((2, S5_GROUPS, LANES), F32),
        ],
        compiler_params=_params("arbitrary"),
        name="s5_bwd" if reverse else "s5_fwd",
    )(*args)


def _s5_mix(u4, tables):
    y = _s5_direction(u4, tables[0], None, False)
    return _s5_direction(u4, tables[1], y, True)


def _dn_prep_kernel(cur_ref, prev_ref, next_ref, ab_ref, cw_ref, gp_ref,
                    q_ref, k_ref, v_ref, gate_ref, ext):
    i = pl.program_id(0)
    tm = cur_ref.shape[0]
    pad = DN_CONV // 2
    ext[0:SUBLANES, :] = jnp.where(i > 0, prev_ref[...], 0.0)
    ext[SUBLANES:SUBLANES + tm, :] = cur_ref[...]
    ext[SUBLANES + tm:, :] = jnp.where(i < pl.num_programs(0) - 1, next_ref[...], 0.0)
    outs = (q_ref, k_ref, v_ref)
    for part in range(3):
        cols = slice(part * DN_WIDTH, (part + 1) * DN_WIDTH)
        acc = ext[pl.ds(SUBLANES - pad, tm), cols] * cw_ref[0:1, cols]
        for tap in range(1, DN_CONV):
            acc = acc + ext[pl.ds(SUBLANES - pad + tap, tm), cols] * cw_ref[tap:tap + 1, cols]
        act = acc * jax.nn.sigmoid(acc)
        if part == 2:
            v_ref[...] = act
            continue
        scale = DN_HEAD_DIM ** -0.5 if part == 0 else 1.0
        for h in range(DN_HEADS):
            hs = slice(h * DN_HEAD_DIM, (h + 1) * DN_HEAD_DIM)
            xh = act[:, hs]
            inv = lax.rsqrt(jnp.sum(xh * xh, axis=-1, keepdims=True) + NORM_EPS)
            outs[part][:, hs] = xh * inv * scale
    ab = ab_ref[...]
    lane = lax.broadcasted_iota(jnp.int32, ab.shape, 1)
    pre = ab + gp_ref[1:2, :]
    softplus = jnp.maximum(pre, 0.0) + jnp.log1p(jnp.exp(-jnp.abs(pre)))
    n_gate = 2 * DN_HEADS
    g = jnp.where(lane < n_gate, gp_ref[0:1, :] * softplus, 0.0)
    hi = g.astype(BF16).astype(F32)
    r1 = g - hi
    mid = r1.astype(BF16).astype(F32)
    lo = (r1 - mid).astype(BF16).astype(F32)
    pieces = (hi + pltpu.roll(mid, n_gate, 1) + pltpu.roll(lo, 2 * n_gate, 1)).astype(BF16)
    ri = lax.broadcasted_iota(jnp.int32, (tm, tm), 0)
    ci = lax.broadcasted_iota(jnp.int32, (tm, tm), 1)
    chunk_bits = DN_CHUNK.bit_length() - 1
    same = lax.shift_right_logical(ri, chunk_bits) == lax.shift_right_logical(ci, chunk_bits)
    tri_lo = jnp.where(same & (ri >= ci), 1.0, 0.0).astype(BF16)
    tri_up = jnp.where(same & (ri <= ci), 1.0, 0.0).astype(BF16)
    pref = jnp.dot(tri_lo, pieces, preferred_element_type=F32)
    suff = jnp.dot(tri_up, pieces, preferred_element_type=F32)
    part = jnp.where(jnp.bitwise_and(lane, n_gate - 1) < DN_HEADS, pref, suff)
    gsum = part + pltpu.roll(part, LANES - n_gate, 1) + pltpu.roll(part, LANES - 2 * n_gate, 1)
    gate_ref[...] = jnp.where(lane < n_gate, g, jnp.where(
        lane < 2 * n_gate, jax.nn.sigmoid(ab), jnp.where(
            lane < 3 * n_gate, pltpu.roll(gsum, 2 * n_gate, 1), 0.0)))


def _dn_prep(qkv, ab, conv_w, a_log, dt_bias):
    t = qkv.shape[0]
    tm = min(ROW_TILE, t)
    nb = tm // SUBLANES
    last = t // SUBLANES - 1
    gp = jnp.zeros((SUBLANES, LANES), F32)
    gp = gp.at[0, :2 * DN_HEADS].set(-jnp.exp(a_log.astype(F32)).reshape(-1))
    gp = gp.at[1, :2 * DN_HEADS].set(dt_bias.astype(F32).reshape(-1))
    cw = jnp.pad(conv_w.astype(F32), ((0, SUBLANES - DN_CONV), (0, 0)))
    row = lambda i: (i, 0)
    const = lambda i: (0, 0)
    return pl.pallas_call(
        _dn_prep_kernel,
        grid=(t // tm,),
        in_specs=[
            pl.BlockSpec((tm, 3 * DN_WIDTH), row),
            pl.BlockSpec((SUBLANES, 3 * DN_WIDTH), lambda i: (jnp.maximum(i * nb - 1, 0), 0)),
            pl.BlockSpec((SUBLANES, 3 * DN_WIDTH), lambda i: (jnp.minimum((i + 1) * nb, last), 0)),
            pl.BlockSpec((tm, LANES), row),
            pl.BlockSpec((SUBLANES, 3 * DN_WIDTH), const),
            pl.BlockSpec((SUBLANES, LANES), const),
        ],
        out_specs=[
            pl.BlockSpec((tm, DN_WIDTH), row),
            pl.BlockSpec((tm, DN_WIDTH), row),
            pl.BlockSpec((tm, DN_WIDTH), row),
            pl.BlockSpec((tm, LANES), row),
        ],
        out_shape=[
            jax.ShapeDtypeStruct((t, DN_WIDTH), F32),
            jax.ShapeDtypeStruct((t, DN_WIDTH), F32),
            jax.ShapeDtypeStruct((t, DN_WIDTH), F32),
            jax.ShapeDtypeStruct((t, LANES), F32),
        ],
        scratch_shapes=[pltpu.VMEM((tm + 2 * SUBLANES, 3 * DN_WIDTH), F32)],
        compiler_params=_params("parallel"),
        name="dn_prep",
    )(qkv, qkv, qkv, ab, cw, gp)


def _bmm(a, b):
    return lax.dot_general(a, b, (((2,), (1,)), ((0,), (0,))), preferred_element_type=F32)


def _dn_chunk_kernel(q_ref, k_ref, v_ref, gate_ref, gate_t_ref, o_ref,
                     state, wq_scr, kd_scr, at_scr, u_scr, *, reverse):
    c_len = DN_CHUNK
    n_ch = gate_t_ref.shape[0]
    d = 1 if reverse else 0
    n_gate = 2 * DN_HEADS

    @pl.when(pl.program_id(0) == 0)
    def _():
        state[...] = jnp.zeros_like(state)

    ri = lax.broadcasted_iota(jnp.int32, (c_len, c_len), 0)
    ci = lax.broadcasted_iota(jnp.int32, (c_len, c_len), 1)
    if reverse:
        incl, strict = ri <= ci, ri < ci
    else:
        incl, strict = ri >= ci, ri > ci
    eye = jnp.where(ri == ci, 1.0, 0.0).astype(F32)
    last = 0 if reverse else c_len - 1

    gates = gate_ref[...].reshape(n_ch, c_len, LANES)
    gates_t = gate_t_ref[...]
    gammas = []
    for h in range(DN_HEADS):
        idx = d * DN_HEADS + h
        hs = slice(h * DN_HEAD_DIM, (h + 1) * DN_HEAD_DIM)
        gcol = gates[:, :, 2 * n_gate + idx:2 * n_gate + idx + 1]
        bcol = gates[:, :, n_gate + idx:n_gate + idx + 1]
        grow = gates_t[:, 2 * n_gate + idx:2 * n_gate + idx + 1, :]
        glast = grow[:, :, last:last + 1]
        qh = q_ref[:, hs].reshape(n_ch, c_len, DN_HEAD_DIM)
        kh = k_ref[:, hs].reshape(n_ch, c_len, DN_HEAD_DIM)
        vh = v_ref[:, hs].reshape(n_ch, c_len, DN_HEAD_DIM)
        kb = kh.astype(BF16)
        qk_kk = lax.dot_general(jnp.concatenate([qh.astype(BF16), kb], axis=1), kb,
                                (((2,), (2,)), ((0,), (0,))), preferred_element_type=F32)
        qk, kk = qk_kk[:, :c_len], qk_kk[:, c_len:]
        decay = jnp.where(incl, jnp.exp(jnp.where(incl, gcol - grow, 0.0)), 0.0)
        a_mat = jnp.where(strict, bcol * kk * decay, 0.0)
        pw = -a_mat
        inv = eye + pw
        for _ in range(5):
            pwb = pw.astype(BF16)
            pw = _bmm(pwb, pwb)
            inv = inv + _bmm(inv.astype(BF16), pw.astype(BF16))
        egc = jnp.exp(gcol)
        rhs = jnp.concatenate([vh * bcol, kh * (bcol * egc)], axis=2).astype(BF16)
        uw = _bmm(inv.astype(BF16), rhs)
        wq = jnp.concatenate([uw[:, :, DN_HEAD_DIM:], qh * egc], axis=1).astype(BF16)
        attn = jnp.where(incl, qk * decay, 0.0).astype(BF16)
        k_dec = (kh * jnp.exp(glast - gcol)).astype(BF16)
        for c in range(n_ch):
            inst = c * DN_HEADS + h
            wq_scr[inst] = wq[c]
            kd_scr[inst] = k_dec[c]
            at_scr[inst] = attn[c]
            u_scr[inst] = uw[c, :, :DN_HEAD_DIM]
        gammas.append(jnp.exp(glast))

    s = [state[h] for h in range(DN_HEADS)]
    for step in range(n_ch):
        c = (n_ch - 1 - step) if reverse else step
        for h in range(DN_HEADS):
            inst = c * DN_HEADS + h
            hs = slice(h * DN_HEAD_DIM, (h + 1) * DN_HEAD_DIM)
            wq_s = jnp.dot(wq_scr[inst], s[h].astype(BF16), preferred_element_type=F32)
            v_nb = (u_scr[inst] - wq_s[:c_len]).astype(BF16)
            o_ref[c * c_len:(c + 1) * c_len, hs] = wq_s[c_len:] + jnp.dot(
                at_scr[inst], v_nb, preferred_element_type=F32)
            s[h] = s[h] * gammas[h][c] + lax.dot_general(
                kd_scr[inst], v_nb, (((0,), (0,)), ((), ())), preferred_element_type=F32)
    for h in range(DN_HEADS):
        state[h] = s[h]


def _dn_direction(q, k, v, gates, gates_t, reverse):
    t = q.shape[0]
    n_chunks = t // DN_CHUNK
    n_ch = min(DN_STEP_CHUNKS, n_chunks)
    n_steps = n_chunks // n_ch
    rows = n_ch * DN_CHUNK
    row = (lambda i: (n_steps - 1 - i, 0)) if reverse else (lambda i: (i, 0))
    row3 = (lambda i: (n_steps - 1 - i, 0, 0)) if reverse else (lambda i: (i, 0, 0))
    return pl.pallas_call(
        functools.partial(_dn_chunk_kernel, reverse=reverse),
        grid=(n_steps,),
        in_specs=[
            pl.BlockSpec((rows, DN_WIDTH), row),
            pl.BlockSpec((rows, DN_WIDTH), row),
            pl.BlockSpec((rows, DN_WIDTH), row),
            pl.BlockSpec((rows, LANES), row),
            pl.BlockSpec((n_ch, 6 * DN_HEADS, DN_CHUNK), row3),
        ],
        out_specs=pl.BlockSpec((rows, DN_WIDTH), row),
        out_shape=jax.ShapeDtypeStruct((t, DN_WIDTH), F32),
        scratch_shapes=[
            pltpu.VMEM((DN_HEADS, DN_HEAD_DIM, DN_HEAD_DIM), F32),
            pltpu.VMEM((n_ch * DN_HEADS, 2 * DN_CHUNK, DN_HEAD_DIM), BF16),
            pltpu.VMEM((n_ch * DN_HEADS, DN_CHUNK, DN_HEAD_DIM), BF16),
            pltpu.VMEM((n_ch * DN_HEADS, DN_CHUNK, DN_CHUNK), BF16),
            pltpu.VMEM((n_ch * DN_HEADS, DN_CHUNK, DN_HEAD_DIM), F32),
        ],
        compiler_params=_params("arbitrary"),
        name="dn_bwd" if reverse else "dn_fwd",
    )(q, k, v, gates, gates_t)


def _post_kernel(x_ref, u_ref, ys_ref, of_ref, ob_ref, z_ref, d_ref, wglu_ref, bglu_ref, nw_ref,
                 wout_ref, nffn_ref, wr_ref, br_ref, x1_ref, h_ref, route_ref, cnt_ref, base):
    i = pl.program_id(0)
    tm = x_ref.shape[0]

    @pl.when(i == 0)
    def _():
        base[...] = jnp.zeros_like(base)

    y = jnp.concatenate([ys_ref[b] + d_ref[:, b * LANES:(b + 1) * LANES] * u_ref[b]
                         for b in range(S5_WIDTH // LANES)], axis=1)
    y = 0.5 * y * (1.0 + lax.erf(y * (2.0 ** -0.5)))
    gate = jnp.dot(y.astype(BF16), wglu_ref[...], preferred_element_type=F32) + bglu_ref[...]
    y_s5 = y * jax.nn.sigmoid(gate)
    acc = x_ref[...] + jnp.dot(y_s5.astype(BF16), wout_ref[0:S5_WIDTH, :], preferred_element_type=F32)
    for h in range(DN_HEADS):
        hs = slice(h * DN_HEAD_DIM, (h + 1) * DN_HEAD_DIM)
        o = of_ref[:, hs] + ob_ref[:, hs]
        zh = z_ref[:, hs]
        o = o * lax.rsqrt(jnp.mean(o * o, axis=-1, keepdims=True) + NORM_EPS) * nw_ref[...]
        y_dn = o * (zh * jax.nn.sigmoid(zh))
        acc = acc + jnp.dot(y_dn.astype(BF16),
                            wout_ref[S5_WIDTH + h * DN_HEAD_DIM:S5_WIDTH + (h + 1) * DN_HEAD_DIM, :],
                            preferred_element_type=F32)
    x1_ref[...] = acc
    hn = acc * lax.rsqrt(jnp.mean(acc * acc, axis=-1, keepdims=True) + NORM_EPS) * nffn_ref[...]
    h_ref[...] = hn

    logits = jnp.dot(hn, wr_ref[...], precision=HIGHEST, preferred_element_type=F32) + br_ref[...]
    lane_i = lax.broadcasted_iota(jnp.int32, logits.shape, 1)
    lane = lane_i.astype(F32)
    neg = jnp.float32(-jnp.inf)
    big = jnp.float32(LANES)
    gl = jnp.where(lane_i < N_EXPERT_GROUPS, logits, neg)
    gmax = jnp.max(gl, axis=-1, keepdims=True)
    g_sel = jnp.min(jnp.where(gl == gmax, lane, big), axis=-1, keepdims=True)
    p_group = 1.0 / jnp.sum(jnp.exp(gl - gmax), axis=-1, keepdims=True)
    lo = N_EXPERT_GROUPS + g_sel * EXPERTS_PER_GROUP
    el = jnp.where((lane >= lo) & (lane < lo + EXPERTS_PER_GROUP), logits, neg)
    top1 = jnp.max(el, axis=-1, keepdims=True)
    idx1 = jnp.min(jnp.where(el == top1, lane, big), axis=-1, keepdims=True)
    el2 = jnp.where(lane == idx1, neg, el)
    top2 = jnp.max(el2, axis=-1, keepdims=True)
    idx2 = jnp.min(jnp.where(el2 == top2, lane, big), axis=-1, keepdims=True)
    e21 = jnp.exp(top2 - top1)
    w1 = p_group / (1.0 + e21)
    w2 = w1 * e21
    oh1 = jnp.where(lane == idx1, 1.0, 0.0).astype(F32)
    oh2 = jnp.where(lane == idx2, 1.0, 0.0).astype(F32)
    ri = lax.broadcasted_iota(jnp.int32, (tm, tm), 0)
    ci = lax.broadcasted_iota(jnp.int32, (tm, tm), 1)
    before = jnp.where(ri > ci, 1.0, 0.0).astype(BF16)
    ohs = oh1 + oh2
    prior = jnp.dot(before, ohs.astype(BF16), preferred_element_type=F32) + base[0:1, :]
    rank1 = jnp.sum(oh1 * prior, axis=-1, keepdims=True)
    rank2 = jnp.sum(oh2 * prior, axis=-1, keepdims=True)
    base[0:1, :] = base[0:1, :] + jnp.sum(ohs, axis=0, keepdims=True)
    e1 = idx1 - N_EXPERT_GROUPS
    e2 = idx2 - N_EXPERT_GROUPS
    route = jnp.where(lane_i == 0, e1, jnp.where(lane_i == 1, e2, jnp.where(lane_i == 2, w1, jnp.where(
        lane_i == 3, w2, jnp.where(lane_i == 4, rank1, jnp.where(lane_i == 5, rank2, 0.0))))))
    route_ref[...] = route
    cnt_ref[...] = base[...]


def _post(x, u, ys, o_f, o_b, z, s5_d, w_glu, b_glu, dn_norm_w, w_out, norm_ffn, w_rg, b_rg, w_re, b_re):
    t = x.shape[0]
    tm = min(ROW_TILE, t)
    wr = jnp.concatenate([w_rg, w_re.transpose(1, 0, 2).reshape(D_MODEL, N_EXPERTS)], axis=1)
    wr = jnp.pad(wr.astype(F32), ((0, 0), (0, LANES - N_EXPERT_GROUPS - N_EXPERTS)))
    br = jnp.pad(jnp.concatenate([b_rg, b_re.reshape(-1)]).astype(F32),
                 (0, LANES - N_EXPERT_GROUPS - N_EXPERTS)).reshape(1, LANES)
    row = lambda i: (i, 0)
    const = lambda i: (0, 0)
    return pl.pallas_call(
        _post_kernel,
        grid=(t // tm,),
        in_specs=[
            pl.BlockSpec((tm, D_MODEL), row),
            pl.BlockSpec((S5_WIDTH // LANES, tm, LANES), lambda i: (0, i, 0)),
            pl.BlockSpec((S5_WIDTH // LANES, tm, LANES), lambda i: (0, i, 0)),
            pl.BlockSpec((tm, DN_WIDTH), row),
            pl.BlockSpec((tm, DN_WIDTH), row),
            pl.BlockSpec((tm, DN_WIDTH), row),
            pl.BlockSpec((1, S5_WIDTH), const),
            pl.BlockSpec((S5_WIDTH, S5_WIDTH), const),
            pl.BlockSpec((1, S5_WIDTH), const),
            pl.BlockSpec((1, DN_HEAD_DIM), const),
            pl.BlockSpec((D_MODEL, D_MODEL), const),
            pl.BlockSpec((1, D_MODEL), const),
            pl.BlockSpec((D_MODEL, LANES), const),
            pl.BlockSpec((1, LANES), const),
        ],
        out_specs=[
            pl.BlockSpec((tm, D_MODEL), row),
            pl.BlockSpec((tm, D_MODEL), row),
            pl.BlockSpec((tm, LANES), row),
            pl.BlockSpec((SUBLANES, LANES), const),
        ],
        out_shape=[
            jax.ShapeDtypeStruct((t, D_MODEL), F32),
            jax.ShapeDtypeStruct((t, D_MODEL), F32),
            jax.ShapeDtypeStruct((t, LANES), F32),
            jax.ShapeDtypeStruct((SUBLANES, LANES), F32),
        ],
        scratch_shapes=[pltpu.VMEM((SUBLANES, LANES), F32)],
        compiler_params=_params("arbitrary"),
        name="mixer_post",
    )(x, u, ys, o_f, o_b, z, s5_d.reshape(1, -1).astype(F32), w_glu.astype(BF16),
      b_glu.reshape(1, -1).astype(F32), dn_norm_w.reshape(1, -1).astype(F32), w_out.astype(BF16),
      norm_ffn.reshape(1, -1).astype(F32), wr, br)


def _dispatch_kernel(dest_ref, h_ref, zeros_ref, xs_ref, sem):
    del zeros_ref
    i = pl.program_id(0)
    tm = h_ref.shape[0]

    def copy(r, k):
        slot = dest_ref[2 * (i * tm + r) + k]
        return pltpu.make_async_copy(h_ref.at[pl.ds(r, 1), :], xs_ref.at[pl.ds(slot, 1), :], sem)

    def start(r, _):
        copy(r, 0).start()
        copy(r, 1).start()
        return 0

    def wait(r, _):
        copy(r, 0).wait()
        copy(r, 1).wait()
        return 0

    lax.fori_loop(0, tm, start, 0)
    lax.fori_loop(0, tm, wait, 0)


def _dispatch(h, dest, n_slots):
    t = h.shape[0]
    tm = min(MOE_TOK_TILE, t)
    grid_spec = pltpu.PrefetchScalarGridSpec(
        num_scalar_prefetch=1,
        grid=(t // tm,),
        in_specs=[
            pl.BlockSpec((tm, D_MODEL), lambda i, dest: (i, 0)),
            pl.BlockSpec(memory_space=pl.ANY),
        ],
        out_specs=pl.BlockSpec(memory_space=pl.ANY),
        scratch_shapes=[pltpu.SemaphoreType.DMA(())],
    )
    return pl.pallas_call(
        _dispatch_kernel,
        grid_spec=grid_spec,
        out_shape=jax.ShapeDtypeStruct((n_slots, D_MODEL), F32),
        input_output_aliases={2: 0},
        compiler_params=_params("arbitrary"),
        name="moe_dispatch",
    )(dest, h, jnp.zeros((n_slots, D_MODEL), F32))


def _expert_kernel(be_ref, used_ref, xs_ref, wg_ref, wu_ref, wd_ref, ys_ref):
    del be_ref
    i = pl.program_id(0)

    @pl.when(i < used_ref[0])
    def _():
        xb = xs_ref[...].astype(BF16)
        g = jnp.dot(xb, wg_ref[...].astype(BF16), preferred_element_type=F32)
        u = jnp.dot(xb, wu_ref[...].astype(BF16), preferred_element_type=F32)
        hid = (g * jax.nn.sigmoid(g) * u).astype(BF16)
        ys_ref[...] = jnp.dot(hid, wd_ref[...].astype(BF16), preferred_element_type=F32)

    @pl.when(i >= used_ref[0])
    def _():
        ys_ref[...] = jnp.zeros_like(ys_ref)


def _experts(xs, block_expert, used, w_gate, w_up, w_down, layer):
    n_slots = xs.shape[0]
    n_blocks = n_slots // MOE_BLOCK
    grid_spec = pltpu.PrefetchScalarGridSpec(
        num_scalar_prefetch=2,
        grid=(n_blocks,),
        in_specs=[
            pl.BlockSpec((MOE_BLOCK, D_MODEL), lambda i, be, used: (i, 0)),
            pl.BlockSpec((None, None, D_MODEL, D_EXPERT), lambda i, be, used: (layer, be[i], 0, 0)),
            pl.BlockSpec((None, None, D_MODEL, D_EXPERT), lambda i, be, used: (layer, be[i], 0, 0)),
            pl.BlockSpec((None, None, D_EXPERT, D_MODEL), lambda i, be, used: (layer, be[i], 0, 0)),
        ],
        out_specs=pl.BlockSpec((MOE_BLOCK, D_MODEL), lambda i, be, used: (i, 0)),
    )
    return pl.pallas_call(
        _expert_kernel,
        grid_spec=grid_spec,
        out_shape=jax.ShapeDtypeStruct((n_slots, D_MODEL), F32),
        compiler_params=_params("arbitrary"),
        name="moe_experts",
    )(block_expert, used, xs, w_gate, w_up, w_down)


def _combine_kernel(dest_ref, x1_ref, route_ref, ys_ref, nw_ref, out_ref, buf, sem, *, final_norm):
    i = pl.program_id(0)
    tm = x1_ref.shape[0]

    def copy(r, k):
        slot = dest_ref[2 * (i * tm + r) + k]
        return pltpu.make_async_copy(ys_ref.at[pl.ds(slot, 1), :], buf.at[k, pl.ds(r, 1), :], sem)

    def start(r, _):
        copy(r, 0).start()
        copy(r, 1).start()
        return 0

    def wait(r, _):
        copy(r, 0).wait()
        copy(r, 1).wait()
        return 0

    lax.fori_loop(0, tm, start, 0)
    lax.fori_loop(0, tm, wait, 0)
    route = route_ref[...]
    out = x1_ref[...] + route[:, 2:3] * buf[0] + route[:, 3:4] * buf[1]
    if final_norm:
        out = out * lax.rsqrt(jnp.mean(out * out, axis=-1, keepdims=True) + NORM_EPS) * nw_ref[...]
    out_ref[...] = out


def _combine(x1, route, ys, dest, norm_w, final_norm):
    t = x1.shape[0]
    tm = min(MOE_TOK_TILE, t)
    grid_spec = pltpu.PrefetchScalarGridSpec(
        num_scalar_prefetch=1,
        grid=(t // tm,),
        in_specs=[
            pl.BlockSpec((tm, D_MODEL), lambda i, dest: (i, 0)),
            pl.BlockSpec((tm, LANES), lambda i, dest: (i, 0)),
            pl.BlockSpec(memory_space=pl.ANY),
            pl.BlockSpec((1, D_MODEL), lambda i, dest: (0, 0)),
        ],
        out_specs=pl.BlockSpec((tm, D_MODEL), lambda i, dest: (i, 0)),
        scratch_shapes=[pltpu.VMEM((2, tm, D_MODEL), F32), pltpu.SemaphoreType.DMA(())],
    )
    return pl.pallas_call(
        functools.partial(_combine_kernel, final_norm=final_norm),
        grid_spec=grid_spec,
        out_shape=jax.ShapeDtypeStruct((t, D_MODEL), F32),
        compiler_params=_params("arbitrary"),
        name="moe_combine",
    )(dest, x1, route, ys, norm_w.reshape(1, D_MODEL).astype(F32))


def _moe(x1, h, route, counts, w_gate, w_up, w_down, layer, norm_w, final_norm):
    t = x1.shape[0]
    n_blocks = -(-(2 * t) // MOE_BLOCK) + N_EXPERTS
    n_slots = n_blocks * MOE_BLOCK
    cnt = counts[0, N_EXPERT_GROUPS:N_EXPERT_GROUPS + N_EXPERTS].astype(jnp.int32)
    padded = ((cnt + MOE_BLOCK - 1) // MOE_BLOCK) * MOE_BLOCK
    pad_end = jnp.cumsum(padded)
    pad_start = pad_end - padded
    expert = route[:, 0:2].astype(jnp.int32)
    rank = route[:, 4:6].astype(jnp.int32)
    dest = (pad_start[expert] + rank).reshape(-1)
    block_start = jnp.arange(n_blocks, dtype=jnp.int32) * MOE_BLOCK
    block_expert = jnp.minimum(jnp.sum(pad_end[None, :] <= block_start[:, None], axis=1),
                               N_EXPERTS - 1).astype(jnp.int32)
    used = (pad_end[-1:] // MOE_BLOCK).astype(jnp.int32)
    xs = _dispatch(h, dest, n_slots)
    ys = _experts(xs, block_expert, used, w_gate, w_up, w_down, layer)
    return _combine(x1, route, ys, dest, norm_w, final_norm)


def kernel(x, norm_mix, w_in, s5_lam_re, s5_lam_im, s5_log_dt, s5_b_re, s5_b_im, s5_c_re, s5_c_im,
           s5_d, s5_w_glu, s5_b_glu, gdn_conv_w, gdn_a_log, gdn_dt_bias, gdn_norm_w, w_out, norm_ffn,
           router_w_group, router_b_group, router_w_expert, router_b_expert,
           expert_w_gate, expert_w_up, expert_w_down, norm_final):
    bsz, seq, d = x.shape
    depth = norm_mix.shape[0]
    xt = x.astype(F32).reshape(bsz * seq, d)
    for i in range(depth):
        u, qkv, z, ab = _inproj(xt, norm_mix[i], w_in[i])
        tables = _s5_tables(s5_lam_re[i], s5_lam_im[i], s5_log_dt[i], s5_b_re[i], s5_b_im[i],
                            s5_c_re[i], s5_c_im[i])
        ys = _s5_mix(u, tables)
        q, k, v, gates = _dn_prep(qkv, ab, gdn_conv_w[i], gdn_a_log[i], gdn_dt_bias[i])
        gates_t = (gates[:, :6 * DN_HEADS].reshape(-1, DN_CHUNK, 6 * DN_HEADS).transpose(0, 2, 1))
        o_f = _dn_direction(q, k, v, gates, gates_t, False)
        o_b = _dn_direction(q, k, v, gates, gates_t, True)
        x1, h, route, counts = _post(xt, u, ys, o_f, o_b, z, s5_d[i], s5_w_glu[i], s5_b_glu[i],
                                     gdn_norm_w[i], w_out[i], norm_ffn[i], router_w_group[i],
                                     router_b_group[i], router_w_expert[i], router_b_expert[i])
        xt = _moe(x1, h, route, counts, expert_w_gate, expert_w_up, expert_w_down, i,
                  norm_final, i == depth - 1)
    return xt.reshape(bsz, seq, d)
```

```python
import functools
import math

import jax
import jax.numpy as jnp
import numpy as np
from jax import lax
from jax.experimental import pallas as pl
from jax.experimental.pallas import tpu as pltpu

F32 = jnp.float32
BF16 = jnp.bfloat16
HIGHEST = lax.Precision.HIGHEST

D_MODEL = 1024
S5_WIDTH = 512
S5_GROUP = 16
S5_GROUPS = 32
S5_STATE = 64
S5_MAX_RE = -1e-4
DN_HEADS = 4
DN_HEAD_DIM = 128
DN_WIDTH = 512
DN_CONV = 5
DN_CHUNK = 64
N_EXPERT_GROUPS = 4
EXPERTS_PER_GROUP = 8
N_EXPERTS = 32
D_EXPERT = 512
NORM_EPS = 1e-6

LANES = 128
SUBLANES = 8
VMEM_LIMIT = 56 * 1024 * 1024

S5_CHUNK = 16
S5_TILE = 128
ROW_TILE = 512
DN_STEP_CHUNKS = 8
MOE_BLOCK = 256
MOE_TOK_TILE = 256


def _params(*sem):
    return pltpu.CompilerParams(dimension_semantics=sem, vmem_limit_bytes=VMEM_LIMIT)


def _inproj_kernel(x_ref, nw_ref, w_ref, wab_ref, u_ref, qkv_ref, z_ref, ab_ref):
    x = x_ref[...]
    h = x * lax.rsqrt(jnp.mean(x * x, axis=-1, keepdims=True) + NORM_EPS) * nw_ref[...]
    hb = h.astype(BF16)
    for blk in range(S5_WIDTH // LANES):
        u_ref[blk] = jnp.dot(hb, w_ref[:, blk * LANES:(blk + 1) * LANES], preferred_element_type=F32)
    qkv_ref[...] = jnp.dot(hb, w_ref[:, S5_WIDTH:S5_WIDTH + 3 * DN_WIDTH], preferred_element_type=F32)
    z_ref[...] = jnp.dot(hb, w_ref[:, S5_WIDTH + 3 * DN_WIDTH:S5_WIDTH + 4 * DN_WIDTH],
                         preferred_element_type=F32)
    ab_ref[...] = jnp.dot(h, wab_ref[...], precision=HIGHEST, preferred_element_type=F32)


def _inproj(x, norm_w, w_in):
    t = x.shape[0]
    n_main = S5_WIDTH + 4 * DN_WIDTH
    w_main = w_in[:, :n_main].astype(BF16)
    w_ab = jnp.pad(w_in[:, n_main:], ((0, 0), (0, LANES - 4 * DN_HEADS)))
    tm = min(ROW_TILE, t)
    row = lambda i: (i, 0)
    const = lambda i: (0, 0)
    return pl.pallas_call(
        _inproj_kernel,
        grid=(t // tm,),
        in_specs=[
            pl.BlockSpec((tm, D_MODEL), row),
            pl.BlockSpec((1, D_MODEL), const),
            pl.BlockSpec((D_MODEL, n_main), const),
            pl.BlockSpec((D_MODEL, LANES), const),
        ],
        out_specs=[
            pl.BlockSpec((S5_WIDTH // LANES, tm, LANES), lambda i: (0, i, 0)),
            pl.BlockSpec((tm, 3 * DN_WIDTH), row),
            pl.BlockSpec((tm, DN_WIDTH), row),
            pl.BlockSpec((tm, LANES), row),
        ],
        out_shape=[
            jax.ShapeDtypeStruct((S5_WIDTH // LANES, t, LANES), F32),
            jax.ShapeDtypeStruct((t, 3 * DN_WIDTH), F32),
            jax.ShapeDtypeStruct((t, DN_WIDTH), F32),
            jax.ShapeDtypeStruct((t, LANES), F32),
        ],
        compiler_params=_params("parallel"),
        name="inproj",
    )(x, norm_w.reshape(1, D_MODEL), w_main, w_ab)


def _s5_tables(lam_re, lam_im, log_dt, b_re, b_im, c_re, c_im):
    c_len = S5_CHUNK
    lr = jnp.minimum(lam_re.astype(F32), S5_MAX_RE)
    li = lam_im.astype(F32)
    dt = jnp.exp(log_dt.astype(F32))[..., None]
    zr, zi = lr * dt, li * dt
    e1 = jnp.exp(zr)
    ar, ai = e1 * jnp.cos(zi), e1 * jnp.sin(zi)
    den = lr * lr + li * li
    nr, ni = ar - 1.0, ai
    fr = (nr * lr + ni * li) / den
    fi = (ni * lr - nr * li) / den
    bbr = fr[..., None] * b_re - fi[..., None] * b_im
    bbi = fr[..., None] * b_im + fi[..., None] * b_re
    tau = jnp.arange(c_len + 1, dtype=F32)[:, None, None, None]
    mag = jnp.exp(tau * zr)
    pr, pi = mag * jnp.cos(tau * zi), mag * jnp.sin(tau * zi)
    m_r = pr[..., None] * bbr - pi[..., None] * bbi
    m_i = pr[..., None] * bbi + pi[..., None] * bbr
    kern = (jnp.einsum('dgpn,tdgnq->tdgpq', c_re, m_r[:c_len], precision=HIGHEST)
            - jnp.einsum('dgpn,tdgnq->tdgpq', c_im, m_i[:c_len], precision=HIGHEST))
    beta_r = c_re[None] * pr[:, :, :, None, :] - c_im[None] * pi[:, :, :, None, :]
    beta_i = c_re[None] * pi[:, :, :, None, :] + c_im[None] * pr[:, :, :, None, :]

    j = np.arange(c_len)
    tables = []
    for d in range(2):
        lag = (j[None, :] - j[:, None]) if d == 0 else (j[:, None] - j[None, :])
        select = (lag[None] == j[:, None, None]).astype(np.float32)
        flip = (lambda a: a[::-1]) if d == 0 else (lambda a: a)
        unflip = (lambda a: a) if d == 0 else (lambda a: a[::-1])
        kd = kern[:, d]
        wt = jnp.einsum('ajt,agpq->gjqtp', select, kd, precision=HIGHEST)
        wt = wt.reshape(S5_GROUPS, 256, 256)
        er = flip(m_r[:c_len, d]).transpose(1, 0, 3, 2).reshape(S5_GROUPS, 256, S5_STATE)
        ei = flip(m_i[:c_len, d]).transpose(1, 0, 3, 2).reshape(S5_GROUPS, 256, S5_STATE)
        we = jnp.concatenate([er, ei, ei, er], axis=-1)
        sr = unflip(beta_r[1:, d]).transpose(1, 3, 0, 2).reshape(S5_GROUPS, S5_STATE, 256)
        si = unflip(beta_i[1:, d]).transpose(1, 3, 0, 2).reshape(S5_GROUPS, S5_STATE, 256)
        ws = jnp.concatenate([sr, -si], axis=1)
        a_r, a_i = pr[c_len, d], pi[c_len, d]
        coef = jnp.stack([jnp.concatenate([a_r, a_r], -1),
                          jnp.concatenate([-a_i, a_i], -1),
                          jnp.concatenate([a_i, -a_i], -1)])
        tables.append((wt.astype(BF16), we.astype(BF16), ws.astype(BF16), coef))
    return tables


def _block_transpose8(xs, lane):
    for k in (2, 1, 0):
        shift = S5_GROUP << k
        bit = jnp.bitwise_and(lax.shift_right_logical(lane, 4 + k), 1)
        new = list(xs)
        for a in range(8):
            if (a >> k) & 1:
                continue
            b = a + (1 << k)
            new[a] = jnp.where(bit == 0, xs[a], pltpu.roll(xs[b], shift, 1))
            new[b] = jnp.where(bit == 1, xs[b], pltpu.roll(xs[a], LANES - shift, 1))
        xs = new
    return xs


def _s5_scan_kernel(u_ref, wt_ref, we_ref, ws_ref, coef_ref, *rest, reverse, add_prev):
    if add_prev:
        prev_ref, y_ref, ug_scr, yg_scr, e_scr, es_scr, s_scr, carry = rest
    else:
        y_ref, ug_scr, yg_scr, e_scr, es_scr, s_scr, carry = rest
    n_rows = ug_scr.shape[1]
    n_blk = S5_WIDTH // LANES
    per_blk = LANES // S5_GROUP
    halves = S5_CHUNK // per_blk

    @pl.when(pl.program_id(0) == 0)
    def _():
        carry[...] = jnp.zeros_like(carry)

    lane = lax.broadcasted_iota(jnp.int32, (n_rows, LANES), 1)

    for blk in range(n_blk):
        for half in range(halves):
            xs = [u_ref[blk, pl.ds(half * per_blk + jl, n_rows, stride=S5_CHUNK), :] for jl in range(per_blk)]
            ys = _block_transpose8(xs, lane)
            for gl in range(per_blk):
                ug_scr[blk * per_blk + gl, :, half * LANES:(half + 1) * LANES] = ys[gl].astype(BF16)

    for g in range(S5_GROUPS):
        e = jnp.dot(ug_scr[g], we_ref[g], preferred_element_type=F32)
        e_scr[g * n_rows:(g + 1) * n_rows, :] = e[:, :LANES]
        es_scr[g * n_rows:(g + 1) * n_rows, :] = e[:, LANES:]

    c1, c2, c3 = coef_ref[0], coef_ref[1], coef_ref[2]

    def step(i, vs):
        v, vp = vs
        r = (n_rows - 1 - i) if reverse else i
        rows = pl.ds(r, S5_GROUPS, stride=n_rows)
        s_scr[rows, :] = v
        v_new = v * c1 + vp * c2 + e_scr[rows, :]
        vp_new = vp * c1 + v * c3 + es_scr[rows, :]
        return v_new, vp_new

    v, vp = lax.fori_loop(0, n_rows, step, (carry[0], carry[1]), unroll=8)
    carry[0] = v
    carry[1] = vp

    for blk in range(n_blk):
        for gl in range(per_blk):
            g = blk * per_blk + gl
            s_in = s_scr[g * n_rows:(g + 1) * n_rows, :].astype(BF16)
            yg_scr[gl] = (jnp.dot(ug_scr[g], wt_ref[g], preferred_element_type=F32)
                          + jnp.dot(s_in, ws_ref[g], preferred_element_type=F32))
        for half in range(halves):
            zs = [yg_scr[gl, :, half * LANES:(half + 1) * LANES] for gl in range(per_blk)]
            ws = _block_transpose8(zs, lane)
            for tl in range(per_blk):
                rows = pl.ds(half * per_blk + tl, n_rows, stride=S5_CHUNK)
                out = ws[tl]
                if add_prev:
                    out = out + prev_ref[blk, rows, :]
                y_ref[blk, rows, :] = out


def _s5_direction(u4, tables, prev, reverse):
    n_blk, t, _ = u4.shape
    n_chunks = t // S5_CHUNK
    rows = min(S5_TILE, n_chunks)
    n_tiles = n_chunks // rows
    wt, we, ws, coef = tables
    tile = (lambda i: (0, n_tiles - 1 - i, 0)) if reverse else (lambda i: (0, i, 0))
    const3 = lambda i: (0, 0, 0)
    once = pl.Buffered(1)
    in_specs = [
        pl.BlockSpec((n_blk, rows * S5_CHUNK, LANES), tile),
        pl.BlockSpec((S5_GROUPS, 256, 256), const3, pipeline_mode=once),
        pl.BlockSpec((S5_GROUPS, 256, 256), const3, pipeline_mode=once),
        pl.BlockSpec((S5_GROUPS, LANES, 256), const3, pipeline_mode=once),
        pl.BlockSpec((3, S5_GROUPS, LANES), const3, pipeline_mode=once),
    ]
    args = [u4, wt, we, ws, coef]
    if prev is not None:
        in_specs.append(pl.BlockSpec((n_blk, rows * S5_CHUNK, LANES), tile))
        args.append(prev)
    return pl.pallas_call(
        functools.partial(_s5_scan_kernel, reverse=reverse, add_prev=prev is not None),
        grid=(n_tiles,),
        in_specs=in_specs,
        out_specs=pl.BlockSpec((n_blk, rows * S5_CHUNK, LANES), tile),
        out_shape=jax.ShapeDtypeStruct((n_blk, t, LANES), F32),
        scratch_shapes=[
            pltpu.VMEM((S5_GROUPS, rows, 256), BF16),
            pltpu.VMEM((LANES // S5_GROUP, rows, 256), F32),
            pltpu.VMEM((S5_GROUPS * rows, LANES), F32),
            pltpu.VMEM((S5_GROUPS * rows, LANES), F32),
            pltpu.VMEM((S5_GROUPS * rows, LANES), F32),
            pltpu.VMEM((2, S5_GROUPS, LANES), F32),
        ],
        compiler_params=_params("arbitrary"),
        name="s5_bwd" if reverse else "s5_fwd",
    )(*args)


def _s5_mix(u4, tables):
    y = _s5_direction(u4, tables[0], None, False)
    return _s5_direction(u4, tables[1], y, True)


def _dn_prep_kernel(cur_ref, prev_ref, next_ref, ab_ref, cw_ref, gp_ref,
                    q_ref, k_ref, v_ref, gate_ref, ext):
    i = pl.program_id(0)
    tm = cur_ref.shape[0]
    pad = DN_CONV // 2
    ext[0:SUBLANES, :] = jnp.where(i > 0, prev_ref[...], 0.0)
    ext[SUBLANES:SUBLANES + tm, :] = cur_ref[...]
    ext[SUBLANES + tm:, :] = jnp.where(i < pl.num_programs(0) - 1, next_ref[...], 0.0)
    outs = (q_ref, k_ref, v_ref)
    for part in range(3):
        cols = slice(part * DN_WIDTH, (part + 1) * DN_WIDTH)
        acc = ext[pl.ds(SUBLANES - pad, tm), cols] * cw_ref[0:1, cols]
        for tap in range(1, DN_CONV):
            acc = acc + ext[pl.ds(SUBLANES - pad + tap, tm), cols] * cw_ref[tap:tap + 1, cols]
        act = acc * jax.nn.sigmoid(acc)
        if part == 2:
            v_ref[...] = act
            continue
        scale = DN_HEAD_DIM ** -0.5 if part == 0 else 1.0
        for h in range(DN_HEADS):
            hs = slice(h * DN_HEAD_DIM, (h + 1) * DN_HEAD_DIM)
            xh = act[:, hs]
            inv = lax.rsqrt(jnp.sum(xh * xh, axis=-1, keepdims=True) + NORM_EPS)
            outs[part][:, hs] = xh * inv * scale
    ab = ab_ref[...]
    lane = lax.broadcasted_iota(jnp.int32, ab.shape, 1)
    pre = ab + gp_ref[1:2, :]
    softplus = jnp.maximum(pre, 0.0) + jnp.log1p(jnp.exp(-jnp.abs(pre)))
    n_gate = 2 * DN_HEADS
    g = jnp.where(lane < n_gate, gp_ref[0:1, :] * softplus, 0.0)
    hi = g.astype(BF16).astype(F32)
    r1 = g - hi
    mid = r1.astype(BF16).astype(F32)
    lo = (r1 - mid).astype(BF16).astype(F32)
    pieces = (hi + pltpu.roll(mid, n_gate, 1) + pltpu.roll(lo, 2 * n_gate, 1)).astype(BF16)
    ri = lax.broadcasted_iota(jnp.int32, (tm, tm), 0)
    ci = lax.broadcasted_iota(jnp.int32, (tm, tm), 1)
    chunk_bits = DN_CHUNK.bit_length() - 1
    same = lax.shift_right_logical(ri, chunk_bits) == lax.shift_right_logical(ci, chunk_bits)
    tri_lo = jnp.where(same & (ri >= ci), 1.0, 0.0).astype(BF16)
    tri_up = jnp.where(same & (ri <= ci), 1.0, 0.0).astype(BF16)
    pref = jnp.dot(tri_lo, pieces, preferred_element_type=F32)
    suff = jnp.dot(tri_up, pieces, preferred_element_type=F32)
    part = jnp.where(jnp.bitwise_and(lane, n_gate - 1) < DN_HEADS, pref, suff)
    gsum = part + pltpu.roll(part, LANES - n_gate, 1) + pltpu.roll(part, LANES - 2 * n_gate, 1)
    gate_ref[...] = jnp.where(lane < n_gate, g, jnp.where(
        lane < 2 * n_gate, jax.nn.sigmoid(ab), jnp.where(
            lane < 3 * n_gate, pltpu.roll(gsum, 2 * n_gate, 1), 0.0)))


def _dn_prep(qkv, ab, conv_w, a_log, dt_bias):
    t = qkv.shape[0]
    tm = min(ROW_TILE, t)
    nb = tm // SUBLANES
    last = t // SUBLANES - 1
    gp = jnp.zeros((SUBLANES, LANES), F32)
    gp = gp.at[0, :2 * DN_HEADS].set(-jnp.exp(a_log.astype(F32)).reshape(-1))
    gp = gp.at[1, :2 * DN_HEADS].set(dt_bias.astype(F32).reshape(-1))
    cw = jnp.pad(conv_w.astype(F32), ((0, SUBLANES - DN_CONV), (0, 0)))
    row = lambda i: (i, 0)
    const = lambda i: (0, 0)
    return pl.pallas_call(
        _dn_prep_kernel,
        grid=(t // tm,),
        in_specs=[
            pl.BlockSpec((tm, 3 * DN_WIDTH), row),
            pl.BlockSpec((SUBLANES, 3 * DN_WIDTH), lambda i: (jnp.maximum(i * nb - 1, 0), 0)),
            pl.BlockSpec((SUBLANES, 3 * DN_WIDTH), lambda i: (jnp.minimum((i + 1) * nb, last), 0)),
            pl.BlockSpec((tm, LANES), row),
            pl.BlockSpec((SUBLANES, 3 * DN_WIDTH), const),
            pl.BlockSpec((SUBLANES, LANES), const),
        ],
        out_specs=[
            pl.BlockSpec((tm, DN_WIDTH), row),
            pl.BlockSpec((tm, DN_WIDTH), row),
            pl.BlockSpec((tm, DN_WIDTH), row),
            pl.BlockSpec((tm, LANES), row),
        ],
        out_shape=[
            jax.ShapeDtypeStruct((t, DN_WIDTH), F32),
            jax.ShapeDtypeStruct((t, DN_WIDTH), F32),
            jax.ShapeDtypeStruct((t, DN_WIDTH), F32),
            jax.ShapeDtypeStruct((t, LANES), F32),
        ],
        scratch_shapes=[pltpu.VMEM((tm + 2 * SUBLANES, 3 * DN_WIDTH), F32)],
        compiler_params=_params("parallel"),
        name="dn_prep",
    )(qkv, qkv, qkv, ab, cw, gp)


def _bmm(a, b):
    return lax.dot_general(a, b, (((2,), (1,)), ((0,), (0,))), preferred_element_type=F32)


def _dn_chunk_kernel(q_ref, k_ref, v_ref, gate_ref, gate_t_ref, o_ref,
                     state, wq_scr, kd_scr, at_scr, u_scr, *, reverse):
    c_len = DN_CHUNK
    n_ch = gate_t_ref.shape[0]
    d = 1 if reverse else 0
    n_gate = 2 * DN_HEADS

    @pl.when(pl.program_id(0) == 0)
    def _():
        state[...] = jnp.zeros_like(state)

    ri = lax.broadcasted_iota(jnp.int32, (c_len, c_len), 0)
    ci = lax.broadcasted_iota(jnp.int32, (c_len, c_len), 1)
    if reverse:
        incl, strict = ri <= ci, ri < ci
    else:
        incl, strict = ri >= ci, ri > ci
    eye = jnp.where(ri == ci, 1.0, 0.0).astype(F32)
    last = 0 if reverse else c_len - 1

    gates = gate_ref[...].reshape(n_ch, c_len, LANES)
    gates_t = gate_t_ref[...]
    gammas = []
    for h in range(DN_HEADS):
        idx = d * DN_HEADS + h
        hs = slice(h * DN_HEAD_DIM, (h + 1) * DN_HEAD_DIM)
        gcol = gates[:, :, 2 * n_gate + idx:2 * n_gate + idx + 1]
        bcol = gates[:, :, n_gate + idx:n_gate + idx + 1]
        grow = gates_t[:, 2 * n_gate + idx:2 * n_gate + idx + 1, :]
        glast = grow[:, :, last:last + 1]
        qh = q_ref[:, hs].reshape(n_ch, c_len, DN_HEAD_DIM)
        kh = k_ref[:, hs].reshape(n_ch, c_len, DN_HEAD_DIM)
        vh = v_ref[:, hs].reshape(n_ch, c_len, DN_HEAD_DIM)
        kb = kh.astype(BF16)
        qk_kk = lax.dot_general(jnp.concatenate([qh.astype(BF16), kb], axis=1), kb,
                                (((2,), (2,)), ((0,), (0,))), preferred_element_type=F32)
        qk, kk = qk_kk[:, :c_len], qk_kk[:, c_len:]
        decay = jnp.where(incl, jnp.exp(jnp.where(incl, gcol - grow, 0.0)), 0.0)
        a_mat = jnp.where(strict, bcol * kk * decay, 0.0)
        pw = -a_mat
        inv = eye + pw
        for _ in range(5):
            pwb = pw.astype(BF16)
            pw = _bmm(pwb, pwb)
            inv = inv + _bmm(inv.astype(BF16), pw.astype(BF16))
        egc = jnp.exp(gcol)
        rhs = jnp.concatenate([vh * bcol, kh * (bcol * egc)], axis=2).astype(BF16)
        uw = _bmm(inv.astype(BF16), rhs)
        wq = jnp.concatenate([uw[:, :, DN_HEAD_DIM:], qh * egc], axis=1).astype(BF16)
        attn = jnp.where(incl, qk * decay, 0.0).astype(BF16)
        k_dec = (kh * jnp.exp(glast - gcol)).astype(BF16)
        for c in range(n_ch):
            inst = c * DN_HEADS + h
            wq_scr[inst] = wq[c]
            kd_scr[inst] = k_dec[c]
            at_scr[inst] = attn[c]
            u_scr[inst] = uw[c, :, :DN_HEAD_DIM]
        gammas.append(jnp.exp(glast))

    s = [state[h] for h in range(DN_HEADS)]
    for step in range(n_ch):
        c = (n_ch - 1 - step) if reverse else step
        for h in range(DN_HEADS):
            inst = c * DN_HEADS + h
            hs = slice(h * DN_HEAD_DIM, (h + 1) * DN_HEAD_DIM)
            wq_s = jnp.dot(wq_scr[inst], s[h].astype(BF16), preferred_element_type=F32)
            v_nb = (u_scr[inst] - wq_s[:c_len]).astype(BF16)
            o_ref[c * c_len:(c + 1) * c_len, hs] = wq_s[c_len:] + jnp.dot(
                at_scr[inst], v_nb, preferred_element_type=F32)
            s[h] = s[h] * gammas[h][c] + lax.dot_general(
                kd_scr[inst], v_nb, (((0,), (0,)), ((), ())), preferred_element_type=F32)
    for h in range(DN_HEADS):
        state[h] = s[h]


def _dn_direction(q, k, v, gates, gates_t, reverse):
    t = q.shape[0]
    n_chunks = t // DN_CHUNK
    n_ch = min(DN_STEP_CHUNKS, n_chunks)
    n_steps = n_chunks // n_ch
    rows = n_ch * DN_CHUNK
    row = (lambda i: (n_steps - 1 - i, 0)) if reverse else (lambda i: (i, 0))
    row3 = (lambda i: (n_steps - 1 - i, 0, 0)) if reverse else (lambda i: (i, 0, 0))
    return pl.pallas_call(
        functools.partial(_dn_chunk_kernel, reverse=reverse),
        grid=(n_steps,),
        in_specs=[
            pl.BlockSpec((rows, DN_WIDTH), row),
            pl.BlockSpec((rows, DN_WIDTH), row),
            pl.BlockSpec((rows, DN_WIDTH), row),
            pl.BlockSpec((rows, LANES), row),
            pl.BlockSpec((n_ch, 6 * DN_HEADS, DN_CHUNK), row3),
        ],
        out_specs=pl.BlockSpec((rows, DN_WIDTH), row),
        out_shape=jax.ShapeDtypeStruct((t, DN_WIDTH), F32),
        scratch_shapes=[
            pltpu.VMEM((DN_HEADS, DN_HEAD_DIM, DN_HEAD_DIM), F32),
            pltpu.VMEM((n_ch * DN_HEADS, 2 * DN_CHUNK, DN_HEAD_DIM), BF16),
            pltpu.VMEM((n_ch * DN_HEADS, DN_CHUNK, DN_HEAD_DIM), BF16),
            pltpu.VMEM((n_ch * DN_HEADS, DN_CHUNK, DN_CHUNK), BF16),
            pltpu.VMEM((n_ch * DN_HEADS, DN_CHUNK, DN_HEAD_DIM), F32),
        ],
        compiler_params=_params("arbitrary"),
        name="dn_bwd" if reverse else "dn_fwd",
    )(q, k, v, gates, gates_t)


def _post_kernel(x_ref, u_ref, ys_ref, of_ref, ob_ref, z_ref, d_ref, wglu_ref, bglu_ref, nw_ref,
                 wout_ref, nffn_ref, wr_ref, br_ref, x1_ref, h_ref, route_ref, cnt_ref, base):
    i = pl.program_id(0)
    tm = x_ref.shape[0]

    @pl.when(i == 0)
    def _():
        base[...] = jnp.zeros_like(base)

    y = jnp.concatenate([ys_ref[b] + d_ref[:, b * LANES:(b + 1) * LANES] * u_ref[b]
                         for b in range(S5_WIDTH // LANES)], axis=1)
    y = 0.5 * y * (1.0 + lax.erf(y * (2.0 ** -0.5)))
    gate = jnp.dot(y.astype(BF16), wglu_ref[...], preferred_element_type=F32) + bglu_ref[...]
    y_s5 = y * jax.nn.sigmoid(gate)
    acc = x_ref[...] + jnp.dot(y_s5.astype(BF16), wout_ref[0:S5_WIDTH, :], preferred_element_type=F32)
    for h in range(DN_HEADS):
        hs = slice(h * DN_HEAD_DIM, (h + 1) * DN_HEAD_DIM)
        o = of_ref[:, hs] + ob_ref[:, hs]
        zh = z_ref[:, hs]
        o = o * lax.rsqrt(jnp.mean(o * o, axis=-1, keepdims=True) + NORM_EPS) * nw_ref[...]
        y_dn = o * (zh * jax.nn.sigmoid(zh))
        acc = acc + jnp.dot(y_dn.astype(BF16),
                            wout_ref[S5_WIDTH + h * DN_HEAD_DIM:S5_WIDTH + (h + 1) * DN_HEAD_DIM, :],
                            preferred_element_type=F32)
    x1_ref[...] = acc
    hn = acc * lax.rsqrt(jnp.mean(acc * acc, axis=-1, keepdims=True) + NORM_EPS) * nffn_ref[...]
    _matrix_to_rows(h_ref, hn)

    logits = jnp.dot(hn, wr_ref[...], precision=HIGHEST, preferred_element_type=F32) + br_ref[...]
    lane_i = lax.broadcasted_iota(jnp.int32, logits.shape, 1)
    lane = lane_i.astype(F32)
    neg = jnp.float32(-jnp.inf)
    big = jnp.float32(LANES)
    gl = jnp.where(lane_i < N_EXPERT_GROUPS, logits, neg)
    gmax = jnp.max(gl, axis=-1, keepdims=True)
    g_sel = jnp.min(jnp.where(gl == gmax, lane, big), axis=-1, keepdims=True)
    p_group = 1.0 / jnp.sum(jnp.exp(gl - gmax), axis=-1, keepdims=True)
    lo = N_EXPERT_GROUPS + g_sel * EXPERTS_PER_GROUP
    el = jnp.where((lane >= lo) & (lane < lo + EXPERTS_PER_GROUP), logits, neg)
    top1 = jnp.max(el, axis=-1, keepdims=True)
    idx1 = jnp.min(jnp.where(el == top1, lane, big), axis=-1, keepdims=True)
    el2 = jnp.where(lane == idx1, neg, el)
    top2 = jnp.max(el2, axis=-1, keepdims=True)
    idx2 = jnp.min(jnp.where(el2 == top2, lane, big), axis=-1, keepdims=True)
    e21 = jnp.exp(top2 - top1)
    w1 = p_group / (1.0 + e21)
    w2 = w1 * e21
    oh1 = jnp.where(lane == idx1, 1.0, 0.0).astype(F32)
    oh2 = jnp.where(lane == idx2, 1.0, 0.0).astype(F32)
    ri = lax.broadcasted_iota(jnp.int32, (tm, tm), 0)
    ci = lax.broadcasted_iota(jnp.int32, (tm, tm), 1)
    before = jnp.where(ri > ci, 1.0, 0.0).astype(BF16)
    ohs = oh1 + oh2
    prior = jnp.dot(before, ohs.astype(BF16), preferred_element_type=F32) + base[0:1, :]
    rank1 = jnp.sum(oh1 * prior, axis=-1, keepdims=True)
    rank2 = jnp.sum(oh2 * prior, axis=-1, keepdims=True)
    base[0:1, :] = base[0:1, :] + jnp.sum(ohs, axis=0, keepdims=True)
    e1 = idx1 - N_EXPERT_GROUPS
    e2 = idx2 - N_EXPERT_GROUPS
    route = jnp.where(lane_i == 0, e1, jnp.where(lane_i == 1, e2, jnp.where(lane_i == 2, w1, jnp.where(
        lane_i == 3, w2, jnp.where(lane_i == 4, rank1, jnp.where(lane_i == 5, rank2, 0.0))))))
    route_ref[...] = route
    cnt_ref[...] = base[...]


def _post(x, u, ys, o_f, o_b, z, s5_d, w_glu, b_glu, dn_norm_w, w_out, norm_ffn, w_rg, b_rg, w_re, b_re):
    t = x.shape[0]
    tm = min(ROW_TILE, t)
    wr = jnp.concatenate([w_rg, w_re.transpose(1, 0, 2).reshape(D_MODEL, N_EXPERTS)], axis=1)
    wr = jnp.pad(wr.astype(F32), ((0, 0), (0, LANES - N_EXPERT_GROUPS - N_EXPERTS)))
    br = jnp.pad(jnp.concatenate([b_rg, b_re.reshape(-1)]).astype(F32),
                 (0, LANES - N_EXPERT_GROUPS - N_EXPERTS)).reshape(1, LANES)
    row = lambda i: (i, 0)
    const = lambda i: (0, 0)
    return pl.pallas_call(
        _post_kernel,
        grid=(t // tm,),
        in_specs=[
            pl.BlockSpec((tm, D_MODEL), row),
            pl.BlockSpec((S5_WIDTH // LANES, tm, LANES), lambda i: (0, i, 0)),
            pl.BlockSpec((S5_WIDTH // LANES, tm, LANES), lambda i: (0, i, 0)),
            pl.BlockSpec((tm, DN_WIDTH), row),
            pl.BlockSpec((tm, DN_WIDTH), row),
            pl.BlockSpec((tm, DN_WIDTH), row),
            pl.BlockSpec((1, S5_WIDTH), const),
            pl.BlockSpec((S5_WIDTH, S5_WIDTH), const),
            pl.BlockSpec((1, S5_WIDTH), const),
            pl.BlockSpec((1, DN_HEAD_DIM), const),
            pl.BlockSpec((D_MODEL, D_MODEL), const),
            pl.BlockSpec((1, D_MODEL), const),
            pl.BlockSpec((D_MODEL, LANES), const),
            pl.BlockSpec((1, LANES), const),
        ],
        out_specs=[
            pl.BlockSpec((tm, D_MODEL), row),
            pl.BlockSpec((tm * (D_MODEL // LANES), LANES), row),
            pl.BlockSpec((tm, LANES), row),
            pl.BlockSpec((SUBLANES, LANES), const),
        ],
        out_shape=[
            jax.ShapeDtypeStruct((t, D_MODEL), F32),
            jax.ShapeDtypeStruct((t * (D_MODEL // LANES), LANES), F32),
            jax.ShapeDtypeStruct((t, LANES), F32),
            jax.ShapeDtypeStruct((SUBLANES, LANES), F32),
        ],
        scratch_shapes=[pltpu.VMEM((SUBLANES, LANES), F32)],
        compiler_params=_params("arbitrary"),
        name="mixer_post",
    )(x, u, ys, o_f, o_b, z, s5_d.reshape(1, -1).astype(F32), w_glu.astype(BF16),
      b_glu.reshape(1, -1).astype(F32), dn_norm_w.reshape(1, -1).astype(F32), w_out.astype(BF16),
      norm_ffn.reshape(1, -1).astype(F32), wr, br)


ROW_SPLIT = D_MODEL // LANES


def _rows_to_matrix(ref, n_rows, lead=None):
    parts = []
    for s in range(ROW_SPLIT):
        idx = pl.ds(s, n_rows, stride=ROW_SPLIT)
        parts.append(ref[idx, :] if lead is None else ref[lead, idx, :])
    return jnp.concatenate(parts, axis=1)


def _matrix_to_rows(ref, val):
    n_rows = val.shape[0]
    for s in range(ROW_SPLIT):
        ref[pl.ds(s, n_rows, stride=ROW_SPLIT), :] = val[:, s * LANES:(s + 1) * LANES]


def _row(ref, r):
    return ref.at[pl.ds(pl.multiple_of(r * ROW_SPLIT, ROW_SPLIT), ROW_SPLIT), :]


def _dispatch_kernel(dest_ref, h_ref, zeros_ref, xs_ref, sem):
    del zeros_ref
    i = pl.program_id(0)
    tm = h_ref.shape[0] // ROW_SPLIT

    def copy(r, k):
        slot = dest_ref[2 * (i * tm + r) + k]
        return pltpu.make_async_copy(_row(h_ref, r), _row(xs_ref, slot), sem)

    def start(r, _):
        copy(r, 0).start(priority=0)
        copy(r, 1).start(priority=1)
        return 0

    def wait(r, _):
        copy(r, 0).wait()
        copy(r, 1).wait()
        return 0

    lax.fori_loop(0, tm, start, 0, unroll=8)
    lax.fori_loop(0, tm, wait, 0, unroll=8)


def _dispatch(h, dest, n_slots):
    t = h.shape[0] // ROW_SPLIT
    tm = min(MOE_TOK_TILE, t)
    grid_spec = pltpu.PrefetchScalarGridSpec(
        num_scalar_prefetch=1,
        grid=(t // tm,),
        in_specs=[
            pl.BlockSpec((tm * ROW_SPLIT, LANES), lambda i, dest: (i, 0)),
            pl.BlockSpec(memory_space=pl.ANY),
        ],
        out_specs=pl.BlockSpec(memory_space=pl.ANY),
        scratch_shapes=[pltpu.SemaphoreType.DMA(())],
    )
    return pl.pallas_call(
        _dispatch_kernel,
        grid_spec=grid_spec,
        out_shape=jax.ShapeDtypeStruct((n_slots * ROW_SPLIT, LANES), F32),
        input_output_aliases={2: 0},
        compiler_params=_params("arbitrary"),
        name="moe_dispatch",
    )(dest, h, jnp.zeros((n_slots * ROW_SPLIT, LANES), F32))


def _expert_kernel(be_ref, used_ref, xs_ref, wg_ref, wu_ref, wd_ref, ys_ref, wgu_b, wd_b):
    i = pl.program_id(0)
    blk = ys_ref.shape[0] // ROW_SPLIT

    @pl.when((i == 0) | (be_ref[i] != be_ref[jnp.maximum(i - 1, 0)]))
    def _():
        wgu_b[:, :D_EXPERT] = wg_ref[...].astype(BF16)
        wgu_b[:, D_EXPERT:] = wu_ref[...].astype(BF16)
        wd_b[...] = wd_ref[...].astype(BF16)

    @pl.when(i < used_ref[0])
    def _():
        xb = _rows_to_matrix(xs_ref, blk).astype(BF16)
        gu = jnp.dot(xb, wgu_b[...], preferred_element_type=F32)
        g, u = gu[:, :D_EXPERT], gu[:, D_EXPERT:]
        hid = (g * jax.nn.sigmoid(g) * u).astype(BF16)
        _matrix_to_rows(ys_ref, jnp.dot(hid, wd_b[...], preferred_element_type=F32))

    @pl.when(i >= used_ref[0])
    def _():
        ys_ref[...] = jnp.zeros_like(ys_ref)


def _experts(xs, block_expert, used, w_gate, w_up, w_down, layer):
    n_slots = xs.shape[0] // ROW_SPLIT
    n_blocks = n_slots // MOE_BLOCK
    grid_spec = pltpu.PrefetchScalarGridSpec(
        num_scalar_prefetch=2,
        grid=(n_blocks,),
        in_specs=[
            pl.BlockSpec((MOE_BLOCK * ROW_SPLIT, LANES), lambda i, be, used: (i, 0)),
            pl.BlockSpec((None, None, D_MODEL, D_EXPERT), lambda i, be, used: (layer, be[i], 0, 0)),
            pl.BlockSpec((None, None, D_MODEL, D_EXPERT), lambda i, be, used: (layer, be[i], 0, 0)),
            pl.BlockSpec((None, None, D_EXPERT, D_MODEL), lambda i, be, used: (layer, be[i], 0, 0)),
        ],
        out_specs=pl.BlockSpec((MOE_BLOCK * ROW_SPLIT, LANES), lambda i, be, used: (i, 0)),
        scratch_shapes=[pltpu.VMEM((D_MODEL, 2 * D_EXPERT), BF16), pltpu.VMEM((D_EXPERT, D_MODEL), BF16)],
    )
    return pl.pallas_call(
        _expert_kernel,
        grid_spec=grid_spec,
        out_shape=jax.ShapeDtypeStruct((n_slots * ROW_SPLIT, LANES), F32),
        compiler_params=_params("arbitrary"),
        name="moe_experts",
    )(block_expert, used, xs, w_gate, w_up, w_down)


def _combine_kernel(dest_ref, x1_ref, route_ref, ys_ref, nw_ref, out_ref, buf, sem, *, final_norm):
    i = pl.program_id(0)
    tm = x1_ref.shape[0]

    def copy(r, k):
        slot = dest_ref[2 * (i * tm + r) + k]
        dst = buf.at[k, pl.ds(pl.multiple_of(r * ROW_SPLIT, ROW_SPLIT), ROW_SPLIT), :]
        return pltpu.make_async_copy(_row(ys_ref, slot), dst, sem)

    def start(r, _):
        copy(r, 0).start(priority=0)
        copy(r, 1).start(priority=1)
        return 0

    def wait(r, _):
        copy(r, 0).wait()
        copy(r, 1).wait()
        return 0

    lax.fori_loop(0, tm, start, 0, unroll=8)
    lax.fori_loop(0, tm, wait, 0, unroll=8)
    route = route_ref[...]
    out = (x1_ref[...] + route[:, 2:3] * _rows_to_matrix(buf, tm, 0)
           + route[:, 3:4] * _rows_to_matrix(buf, tm, 1))
    if final_norm:
        out = out * lax.rsqrt(jnp.mean(out * out, axis=-1, keepdims=True) + NORM_EPS) * nw_ref[...]
    out_ref[...] = out


def _combine(x1, route, ys, dest, norm_w, final_norm):
    t = x1.shape[0]
    tm = min(MOE_TOK_TILE, t)
    grid_spec = pltpu.PrefetchScalarGridSpec(
        num_scalar_prefetch=1,
        grid=(t // tm,),
        in_specs=[
            pl.BlockSpec((tm, D_MODEL), lambda i, dest: (i, 0)),
            pl.BlockSpec((tm, LANES), lambda i, dest: (i, 0)),
            pl.BlockSpec(memory_space=pl.ANY),
            pl.BlockSpec((1, D_MODEL), lambda i, dest: (0, 0)),
        ],
        out_specs=pl.BlockSpec((tm, D_MODEL), lambda i, dest: (i, 0)),
        scratch_shapes=[pltpu.VMEM((2, tm * ROW_SPLIT, LANES), F32), pltpu.SemaphoreType.DMA(())],
    )
    return pl.pallas_call(
        functools.partial(_combine_kernel, final_norm=final_norm),
        grid_spec=grid_spec,
        out_shape=jax.ShapeDtypeStruct((t, D_MODEL), F32),
        compiler_params=_params("arbitrary"),
        name="moe_combine",
    )(dest, x1, route, ys, norm_w.reshape(1, D_MODEL).astype(F32))


def _moe(x1, h, route, counts, w_gate, w_up, w_down, layer, norm_w, final_norm):
    t = x1.shape[0]
    n_blocks = -(-(2 * t) // MOE_BLOCK) + N_EXPERTS
    n_slots = n_blocks * MOE_BLOCK
    cnt = counts[0, N_EXPERT_GROUPS:N_EXPERT_GROUPS + N_EXPERTS].astype(jnp.int32)
    padded = ((cnt + MOE_BLOCK - 1) // MOE_BLOCK) * MOE_BLOCK
    pad_end = jnp.cumsum(padded)
    pad_start = pad_end - padded
    expert = route[:, 0:2].astype(jnp.int32)
    rank = route[:, 4:6].astype(jnp.int32)
    dest = (pad_start[expert] + rank).reshape(-1)
    block_start = jnp.arange(n_blocks, dtype=jnp.int32) * MOE_BLOCK
    block_expert = jnp.minimum(jnp.sum(pad_end[None, :] <= block_start[:, None], axis=1),
                               N_EXPERTS - 1).astype(jnp.int32)
    used = (pad_end[-1:] // MOE_BLOCK).astype(jnp.int32)
    xs = _dispatch(h, dest, n_slots)
    ys = _experts(xs, block_expert, used, w_gate, w_up, w_down, layer)
    return _combine(x1, route, ys, dest, norm_w, final_norm)


def kernel(x, norm_mix, w_in, s5_lam_re, s5_lam_im, s5_log_dt, s5_b_re, s5_b_im, s5_c_re, s5_c_im,
           s5_d, s5_w_glu, s5_b_glu, gdn_conv_w, gdn_a_log, gdn_dt_bias, gdn_norm_w, w_out, norm_ffn,
           router_w_group, router_b_group, router_w_expert, router_b_expert,
           expert_w_gate, expert_w_up, expert_w_down, norm_final):
    bsz, seq, d = x.shape
    depth = norm_mix.shape[0]
    xt = x.astype(F32).reshape(bsz * seq, d)
    for i in range(depth):
        u, qkv, z, ab = _inproj(xt, norm_mix[i], w_in[i])
        tables = _s5_tables(s5_lam_re[i], s5_lam_im[i], s5_log_dt[i], s5_b_re[i], s5_b_im[i],
                            s5_c_re[i], s5_c_im[i])
        ys = _s5_mix(u, tables)
        q, k, v, gates = _dn_prep(qkv, ab, gdn_conv_w[i], gdn_a_log[i], gdn_dt_bias[i])
        gates_t = (gates[:, :6 * DN_HEADS].reshape(-1, DN_CHUNK, 6 * DN_HEADS).transpose(0, 2, 1))
        o_f = _dn_direction(q, k, v, gates, gates_t, False)
        o_b = _dn_direction(q, k, v, gates, gates_t, True)
        x1, h, route, counts = _post(xt, u, ys, o_f, o_b, z, s5_d[i], s5_w_glu[i], s5_b_glu[i],
                                     gdn_norm_w[i], w_out[i], norm_ffn[i], router_w_group[i],
                                     router_b_group[i], router_w_expert[i], router_b_expert[i])
        xt = _moe(x1, h, route, counts, expert_w_gate, expert_w_up, expert_w_down, i,
                  norm_final, i == depth - 1)
    return xt.reshape(bsz, seq, d)
```

```python
import functools
import math

import jax
import jax.numpy as jnp
import numpy as np
from jax import lax
from jax.experimental import pallas as pl
from jax.experimental.pallas import tpu as pltpu

F32 = jnp.float32
BF16 = jnp.bfloat16
HIGHEST = lax.Precision.HIGHEST

D_MODEL = 1024
S5_WIDTH = 512
S5_GROUP = 16
S5_GROUPS = 32
S5_STATE = 64
S5_MAX_RE = -1e-4
DN_HEADS = 4
DN_HEAD_DIM = 128
DN_WIDTH = 512
DN_CONV = 5
DN_CHUNK = 64
N_EXPERT_GROUPS = 4
EXPERTS_PER_GROUP = 8
N_EXPERTS = 32
D_EXPERT = 512
NORM_EPS = 1e-6

LANES = 128
SUBLANES = 8
VMEM_LIMIT = 56 * 1024 * 1024

S5_CHUNK = 16
S5_TILE = 128
ROW_TILE = 512
DN_STEP_CHUNKS = 8
MOE_BLOCK = 256
MOE_TOK_TILE = 256


def _params(*sem):
    return pltpu.CompilerParams(dimension_semantics=sem, vmem_limit_bytes=VMEM_LIMIT)


def _split_weight(w):
    hi = w.astype(BF16)
    lo = (w - hi.astype(F32)).astype(BF16)
    return jnp.concatenate([hi, lo], axis=1)


def _dot_split(a, w_split):
    a_hi = a.astype(BF16)
    a_lo = (a - a_hi.astype(F32)).astype(BF16)
    p = jnp.dot(a_hi, w_split, preferred_element_type=F32)
    q = jnp.dot(a_lo, w_split[:, :LANES], preferred_element_type=F32)
    return p[:, :LANES] + p[:, LANES:] + q


def _inproj_kernel(x_ref, nw_ref, w_ref, wab_ref, u_ref, qkv_ref, z_ref, ab_ref):
    x = x_ref[...]
    h = x * lax.rsqrt(jnp.mean(x * x, axis=-1, keepdims=True) + NORM_EPS) * nw_ref[...]
    hb = h.astype(BF16)
    for blk in range(S5_WIDTH // LANES):
        u_ref[blk] = jnp.dot(hb, w_ref[:, blk * LANES:(blk + 1) * LANES], preferred_element_type=F32)
    qkv_ref[...] = jnp.dot(hb, w_ref[:, S5_WIDTH:S5_WIDTH + 3 * DN_WIDTH], preferred_element_type=F32)
    z_ref[...] = jnp.dot(hb, w_ref[:, S5_WIDTH + 3 * DN_WIDTH:S5_WIDTH + 4 * DN_WIDTH],
                         preferred_element_type=F32)
    ab_ref[...] = _dot_split(h, wab_ref[...])


def _inproj(x, norm_w, w_in):
    t = x.shape[0]
    n_main = S5_WIDTH + 4 * DN_WIDTH
    w_main = w_in[:, :n_main].astype(BF16)
    w_ab = _split_weight(jnp.pad(w_in[:, n_main:].astype(F32), ((0, 0), (0, LANES - 4 * DN_HEADS))))
    tm = min(ROW_TILE, t)
    row = lambda i: (i, 0)
    const = lambda i: (0, 0)
    return pl.pallas_call(
        _inproj_kernel,
        grid=(t // tm,),
        in_specs=[
            pl.BlockSpec((tm, D_MODEL), row),
            pl.BlockSpec((1, D_MODEL), const),
            pl.BlockSpec((D_MODEL, n_main), const),
            pl.BlockSpec((D_MODEL, 2 * LANES), const),
        ],
        out_specs=[
            pl.BlockSpec((S5_WIDTH // LANES, tm, LANES), lambda i: (0, i, 0)),
            pl.BlockSpec((tm, 3 * DN_WIDTH), row),
            pl.BlockSpec((tm, DN_WIDTH), row),
            pl.BlockSpec((tm, LANES), row),
        ],
        out_shape=[
            jax.ShapeDtypeStruct((S5_WIDTH // LANES, t, LANES), F32),
            jax.ShapeDtypeStruct((t, 3 * DN_WIDTH), F32),
            jax.ShapeDtypeStruct((t, DN_WIDTH), F32),
            jax.ShapeDtypeStruct((t, LANES), F32),
        ],
        compiler_params=_params("parallel"),
        name="inproj",
    )(x, norm_w.reshape(1, D_MODEL), w_main, w_ab)


def _s5_tables(lam_re, lam_im, log_dt, b_re, b_im, c_re, c_im):
    c_len = S5_CHUNK
    lr = jnp.minimum(lam_re.astype(F32), S5_MAX_RE)
    li = lam_im.astype(F32)
    dt = jnp.exp(log_dt.astype(F32))[..., None]
    zr, zi = lr * dt, li * dt
    e1 = jnp.exp(zr)
    ar, ai = e1 * jnp.cos(zi), e1 * jnp.sin(zi)
    den = lr * lr + li * li
    nr, ni = ar - 1.0, ai
    fr = (nr * lr + ni * li) / den
    fi = (ni * lr - nr * li) / den
    bbr = fr[..., None] * b_re - fi[..., None] * b_im
    bbi = fr[..., None] * b_im + fi[..., None] * b_re
    tau = jnp.arange(c_len + 1, dtype=F32)[:, None, None, None]
    mag = jnp.exp(tau * zr)
    pr, pi = mag * jnp.cos(tau * zi), mag * jnp.sin(tau * zi)
    m_r = pr[..., None] * bbr - pi[..., None] * bbi
    m_i = pr[..., None] * bbi + pi[..., None] * bbr
    kern = (jnp.einsum('dgpn,tdgnq->tdgpq', c_re, m_r[:c_len], precision=HIGHEST)
            - jnp.einsum('dgpn,tdgnq->tdgpq', c_im, m_i[:c_len], precision=HIGHEST))
    beta_r = c_re[None] * pr[:, :, :, None, :] - c_im[None] * pi[:, :, :, None, :]
    beta_i = c_re[None] * pi[:, :, :, None, :] + c_im[None] * pr[:, :, :, None, :]

    j = np.arange(c_len)
    tables = []
    for d in range(2):
        lag = (j[None, :] - j[:, None]) if d == 0 else (j[:, None] - j[None, :])
        select = (lag[None] == j[:, None, None]).astype(np.float32)
        flip = (lambda a: a[::-1]) if d == 0 else (lambda a: a)
        unflip = (lambda a: a) if d == 0 else (lambda a: a[::-1])
        kd = kern[:, d]
        wt = jnp.einsum('ajt,agpq->gjqtp', select, kd, precision=HIGHEST)
        wt = wt.reshape(S5_GROUPS, 256, 256)
        er = flip(m_r[:c_len, d]).transpose(1, 0, 3, 2).reshape(S5_GROUPS, 256, S5_STATE)
        ei = flip(m_i[:c_len, d]).transpose(1, 0, 3, 2).reshape(S5_GROUPS, 256, S5_STATE)
        we = jnp.concatenate([er, ei, ei, er], axis=-1)
        sr = unflip(beta_r[1:, d]).transpose(1, 3, 0, 2).reshape(S5_GROUPS, S5_STATE, 256)
        si = unflip(beta_i[1:, d]).transpose(1, 3, 0, 2).reshape(S5_GROUPS, S5_STATE, 256)
        ws = jnp.concatenate([sr, -si], axis=1)
        a_r, a_i = pr[c_len, d], pi[c_len, d]
        coef = jnp.stack([jnp.concatenate([a_r, a_r], -1),
                          jnp.concatenate([-a_i, a_i], -1),
                          jnp.concatenate([a_i, -a_i], -1)])
        tables.append((wt.astype(BF16), we.astype(BF16), ws.astype(BF16), coef))
    return tables


def _block_transpose8(xs, lane):
    for k in (2, 1, 0):
        shift = S5_GROUP << k
        bit = jnp.bitwise_and(lax.shift_right_logical(lane, 4 + k), 1)
        new = list(xs)
        for a in range(8):
            if (a >> k) & 1:
                continue
            b = a + (1 << k)
            new[a] = jnp.where(bit == 0, xs[a], pltpu.roll(xs[b], shift, 1))
            new[b] = jnp.where(bit == 1, xs[b], pltpu.roll(xs[a], LANES - shift, 1))
        xs = new
    return xs


def _s5_scan_kernel(u_ref, wt_ref, we_ref, ws_ref, coef_ref, *rest, reverse, add_prev):
    if add_prev:
        prev_ref, y_ref, ug_scr, yg_scr, e_scr, es_scr, s_scr, carry = rest
    else:
        y_ref, ug_scr, yg_scr, e_scr, es_scr, s_scr, carry = rest
    n_rows = ug_scr.shape[1]
    n_blk = S5_WIDTH // LANES
    per_blk = LANES // S5_GROUP
    halves = S5_CHUNK // per_blk

    @pl.when(pl.program_id(0) == 0)
    def _():
        carry[...] = jnp.zeros_like(carry)

    lane = lax.broadcasted_iota(jnp.int32, (n_rows, LANES), 1)

    for blk in range(n_blk):
        for half in range(halves):
            xs = [u_ref[blk, pl.ds(half * per_blk + jl, n_rows, stride=S5_CHUNK), :] for jl in range(per_blk)]
            ys = _block_transpose8(xs, lane)
            for gl in range(per_blk):
                ug_scr[blk * per_blk + gl, :, half * LANES:(half + 1) * LANES] = ys[gl].astype(BF16)

    for g in range(S5_GROUPS):
        e = jnp.dot(ug_scr[g], we_ref[g], preferred_element_type=F32)
        e_scr[g * n_rows:(g + 1) * n_rows, :] = e[:, :LANES]
        es_scr[g * n_rows:(g + 1) * n_rows, :] = e[:, LANES:]

    c1, c2, c3 = coef_ref[0], coef_ref[1], coef_ref[2]

    def step(i, vs):
        v, vp = vs
        r = (n_rows - 1 - i) if reverse else i
        rows = pl.ds(r, S5_GROUPS, stride=n_rows)
        s_scr[rows, :] = v
        v_new = v * c1 + vp * c2 + e_scr[rows, :]
        vp_new = vp * c1 + v * c3 + es_scr[rows, :]
        return v_new, vp_new

    v, vp = lax.fori_loop(0, n_rows, step, (carry[0], carry[1]), unroll=8)
    carry[0] = v
    carry[1] = vp

    for blk in range(n_blk):
        for gl in range(per_blk):
            g = blk * per_blk + gl
            s_in = s_scr[g * n_rows:(g + 1) * n_rows, :].astype(BF16)
            yg_scr[gl] = (jnp.dot(ug_scr[g], wt_ref[g], preferred_element_type=F32)
                          + jnp.dot(s_in, ws_ref[g], preferred_element_type=F32))
        for half in range(halves):
            zs = [yg_scr[gl, :, half * LANES:(half + 1) * LANES] for gl in range(per_blk)]
            ws = _block_transpose8(zs, lane)
            for tl in range(per_blk):
                rows = pl.ds(half * per_blk + tl, n_rows, stride=S5_CHUNK)
                out = ws[tl]
                if add_prev:
                    out = out + prev_ref[blk, rows, :]
                y_ref[blk, rows, :] = out


def _s5_direction(u4, tables, prev, reverse):
    n_blk, t, _ = u4.shape
    n_chunks = t // S5_CHUNK
    rows = min(S5_TILE, n_chunks)
    n_tiles = n_chunks // rows
    wt, we, ws, coef = tables
    tile = (lambda i: (0, n_tiles - 1 - i, 0)) if reverse else (lambda i: (0, i, 0))
    const3 = lambda i: (0, 0, 0)
    once = pl.Buffered(1)
    in_specs = [
        pl.BlockSpec((n_blk, rows * S5_CHUNK, LANES), tile),
        pl.BlockSpec((S5_GROUPS, 256, 256), const3, pipeline_mode=once),
        pl.BlockSpec((S5_GROUPS, 256, 256), const3, pipeline_mode=once),
        pl.BlockSpec((S5_GROUPS, LANES, 256), const3, pipeline_mode=once),
        pl.BlockSpec((3, S5_GROUPS, LANES), const3, pipeline_mode=once),
    ]
    args = [u4, wt, we, ws, coef]
    if prev is not None:
        in_specs.append(pl.BlockSpec((n_blk, rows * S5_CHUNK, LANES), tile))
        args.append(prev)
    return pl.pallas_call(
        functools.partial(_s5_scan_kernel, reverse=reverse, add_prev=prev is not None),
        grid=(n_tiles,),
        in_specs=in_specs,
        out_specs=pl.BlockSpec((n_blk, rows * S5_CHUNK, LANES), tile),
        out_shape=jax.ShapeDtypeStruct((n_blk, t, LANES), F32),
        scratch_shapes=[
            pltpu.VMEM((S5_GROUPS, rows, 256), BF16),
            pltpu.VMEM((LANES // S5_GROUP, rows, 256), F32),
            pltpu.VMEM((S5_GROUPS * rows, LANES), F32),
            pltpu.VMEM((S5_GROUPS * rows, LANES), F32),
            pltpu.VMEM((S5_GROUPS * rows, LANES), F32),
            pltpu.VMEM((2, S5_GROUPS, LANES), F32),
        ],
        compiler_params=_params("arbitrary"),
        name="s5_bwd" if reverse else "s5_fwd",
    )(*args)


def _s5_mix(u4, tables):
    y = _s5_direction(u4, tables[0], None, False)
    return _s5_direction(u4, tables[1], y, True)


def _dn_prep_kernel(cur_ref, prev_ref, next_ref, ab_ref, cw_ref, gp_ref,
                    q_ref, k_ref, v_ref, gate_ref, ext):
    i = pl.program_id(0)
    tm = cur_ref.shape[0]
    pad = DN_CONV // 2
    ext[0:SUBLANES, :] = jnp.where(i > 0, prev_ref[...], 0.0)
    ext[SUBLANES:SUBLANES + tm, :] = cur_ref[...]
    ext[SUBLANES + tm:, :] = jnp.where(i < pl.num_programs(0) - 1, next_ref[...], 0.0)
    outs = (q_ref, k_ref, v_ref)
    for part in range(3):
        cols = slice(part * DN_WIDTH, (part + 1) * DN_WIDTH)
        acc = ext[pl.ds(SUBLANES - pad, tm), cols] * cw_ref[0:1, cols]
        for tap in range(1, DN_CONV):
            acc = acc + ext[pl.ds(SUBLANES - pad + tap, tm), cols] * cw_ref[tap:tap + 1, cols]
        act = acc * jax.nn.sigmoid(acc)
        if part == 2:
            v_ref[...] = act
            continue
        scale = DN_HEAD_DIM ** -0.5 if part == 0 else 1.0
        for h in range(DN_HEADS):
            hs = slice(h * DN_HEAD_DIM, (h + 1) * DN_HEAD_DIM)
            xh = act[:, hs]
            inv = lax.rsqrt(jnp.sum(xh * xh, axis=-1, keepdims=True) + NORM_EPS)
            outs[part][:, hs] = xh * inv * scale
    ab = ab_ref[...]
    lane = lax.broadcasted_iota(jnp.int32, ab.shape, 1)
    pre = ab + gp_ref[1:2, :]
    softplus = jnp.maximum(pre, 0.0) + jnp.log1p(jnp.exp(-jnp.abs(pre)))
    n_gate = 2 * DN_HEADS
    g = jnp.where(lane < n_gate, gp_ref[0:1, :] * softplus, 0.0)
    hi = g.astype(BF16).astype(F32)
    r1 = g - hi
    mid = r1.astype(BF16).astype(F32)
    lo = (r1 - mid).astype(BF16).astype(F32)
    pieces = (hi + pltpu.roll(mid, n_gate, 1) + pltpu.roll(lo, 2 * n_gate, 1)).astype(BF16)
    ri = lax.broadcasted_iota(jnp.int32, (tm, tm), 0)
    ci = lax.broadcasted_iota(jnp.int32, (tm, tm), 1)
    chunk_bits = DN_CHUNK.bit_length() - 1
    same = lax.shift_right_logical(ri, chunk_bits) == lax.shift_right_logical(ci, chunk_bits)
    tri_lo = jnp.where(same & (ri >= ci), 1.0, 0.0).astype(BF16)
    tri_up = jnp.where(same & (ri <= ci), 1.0, 0.0).astype(BF16)
    pref = jnp.dot(tri_lo, pieces, preferred_element_type=F32)
    suff = jnp.dot(tri_up, pieces, preferred_element_type=F32)
    part = jnp.where(jnp.bitwise_and(lane, n_gate - 1) < DN_HEADS, pref, suff)
    gsum = part + pltpu.roll(part, LANES - n_gate, 1) + pltpu.roll(part, LANES - 2 * n_gate, 1)
    gate_ref[...] = jnp.where(lane < n_gate, g, jnp.where(
        lane < 2 * n_gate, jax.nn.sigmoid(ab), jnp.where(
            lane < 3 * n_gate, pltpu.roll(gsum, 2 * n_gate, 1), 0.0)))


def _dn_prep(qkv, ab, conv_w, a_log, dt_bias):
    t = qkv.shape[0]
    tm = min(ROW_TILE, t)
    nb = tm // SUBLANES
    last = t // SUBLANES - 1
    gp = jnp.zeros((SUBLANES, LANES), F32)
    gp = gp.at[0, :2 * DN_HEADS].set(-jnp.exp(a_log.astype(F32)).reshape(-1))
    gp = gp.at[1, :2 * DN_HEADS].set(dt_bias.astype(F32).reshape(-1))
    cw = jnp.pad(conv_w.astype(F32), ((0, SUBLANES - DN_CONV), (0, 0)))
    row = lambda i: (i, 0)
    const = lambda i: (0, 0)
    return pl.pallas_call(
        _dn_prep_kernel,
        grid=(t // tm,),
        in_specs=[
            pl.BlockSpec((tm, 3 * DN_WIDTH), row),
            pl.BlockSpec((SUBLANES, 3 * DN_WIDTH), lambda i: (jnp.maximum(i * nb - 1, 0), 0)),
            pl.BlockSpec((SUBLANES, 3 * DN_WIDTH), lambda i: (jnp.minimum((i + 1) * nb, last), 0)),
            pl.BlockSpec((tm, LANES), row),
            pl.BlockSpec((SUBLANES, 3 * DN_WIDTH), const),
            pl.BlockSpec((SUBLANES, LANES), const),
        ],
        out_specs=[
            pl.BlockSpec((tm, DN_WIDTH), row),
            pl.BlockSpec((tm, DN_WIDTH), row),
            pl.BlockSpec((tm, DN_WIDTH), row),
            pl.BlockSpec((tm, LANES), row),
        ],
        out_shape=[
            jax.ShapeDtypeStruct((t, DN_WIDTH), F32),
            jax.ShapeDtypeStruct((t, DN_WIDTH), F32),
            jax.ShapeDtypeStruct((t, DN_WIDTH), F32),
            jax.ShapeDtypeStruct((t, LANES), F32),
        ],
        scratch_shapes=[pltpu.VMEM((tm + 2 * SUBLANES, 3 * DN_WIDTH), F32)],
        compiler_params=_params("parallel"),
        name="dn_prep",
    )(qkv, qkv, qkv, ab, cw, gp)


def _bmm(a, b):
    return lax.dot_general(a, b, (((2,), (1,)), ((0,), (0,))), preferred_element_type=F32)


def _dn_kernel(qf, kf, vf, gf, gtf, qb, kb_, vb, gb, gtb, of_ref, ob_ref,
               state, wq_scr, kdt_scr, at_scr, u_scr, gam_scr):
    c_len = DN_CHUNK
    n_ch = gtf.shape[0]
    n_gate = 2 * DN_HEADS
    n_chain = 2 * DN_HEADS

    @pl.when(pl.program_id(0) == 0)
    def _():
        state[...] = jnp.zeros_like(state)

    ri = lax.broadcasted_iota(jnp.int32, (c_len, c_len), 0)
    ci = lax.broadcasted_iota(jnp.int32, (c_len, c_len), 1)
    eye = jnp.where(ri == ci, 1.0, 0.0).astype(F32)
    dirs = ((qf, kf, vf, gf, gtf, ri >= ci, ri > ci, c_len - 1),
            (qb, kb_, vb, gb, gtb, ri <= ci, ri < ci, 0))

    for d, (q_ref, k_ref, v_ref, gate_ref, gate_t_ref, incl, strict, last) in enumerate(dirs):
        gates = gate_ref[...].reshape(n_ch, c_len, LANES)
        gates_t = gate_t_ref[...]
        for h in range(DN_HEADS):
            idx = d * DN_HEADS + h
            hs = slice(h * DN_HEAD_DIM, (h + 1) * DN_HEAD_DIM)
            gcol = gates[:, :, 2 * n_gate + idx:2 * n_gate + idx + 1]
            bcol = gates[:, :, n_gate + idx:n_gate + idx + 1]
            grow = gates_t[:, 2 * n_gate + idx:2 * n_gate + idx + 1, :]
            glast = grow[:, :, last:last + 1]
            qh = q_ref[:, hs].reshape(n_ch, c_len, DN_HEAD_DIM)
            kh = k_ref[:, hs].reshape(n_ch, c_len, DN_HEAD_DIM)
            vh = v_ref[:, hs].reshape(n_ch, c_len, DN_HEAD_DIM)
            kb = kh.astype(BF16)
            qk_kk = lax.dot_general(jnp.concatenate([qh.astype(BF16), kb], axis=1), kb,
                                    (((2,), (2,)), ((0,), (0,))), preferred_element_type=F32)
            qk, kk = qk_kk[:, :c_len], qk_kk[:, c_len:]
            decay = jnp.where(incl, jnp.exp(jnp.where(incl, gcol - grow, 0.0)), 0.0)
            a_mat = jnp.where(strict, bcol * kk * decay, 0.0)
            pw = -a_mat
            inv = eye + pw
            for _ in range(5):
                pwb = pw.astype(BF16)
                pw = _bmm(pwb, pwb)
                inv = inv + _bmm(inv.astype(BF16), pw.astype(BF16))
            egc = jnp.exp(gcol)
            rhs = jnp.concatenate([vh * bcol, kh * (bcol * egc)], axis=2).astype(BF16)
            uw = _bmm(inv.astype(BF16), rhs)
            wq = jnp.concatenate([uw[:, :, DN_HEAD_DIM:], qh * egc], axis=1).astype(BF16)
            attn = jnp.where(incl, qk * decay, 0.0).astype(BF16)
            k_dec_t = jnp.swapaxes(kh * jnp.exp(glast - gcol), 1, 2).astype(BF16)
            gamma = jnp.broadcast_to(jnp.exp(glast), (n_ch, 1, LANES))
            for c in range(n_ch):
                slot = (c if d == 0 else n_ch - 1 - c) * n_chain + idx
                wq_scr[slot] = wq[c]
                kdt_scr[slot] = k_dec_t[c]
                at_scr[slot] = attn[c]
                u_scr[slot] = uw[c, :, :DN_HEAD_DIM]
                gam_scr[slot] = gamma[c]

    for step in range(n_ch):
        grp = slice(step * n_chain, (step + 1) * n_chain)
        s = state[...]
        wq_s = _bmm(wq_scr[grp], s.astype(BF16))
        v_nb = (u_scr[grp] - wq_s[:, :c_len]).astype(BF16)
        o = wq_s[:, c_len:] + _bmm(at_scr[grp], v_nb)
        state[...] = s * gam_scr[grp] + _bmm(kdt_scr[grp], v_nb)
        for d, o_ref in enumerate((of_ref, ob_ref)):
            c = step if d == 0 else n_ch - 1 - step
            for h in range(DN_HEADS):
                o_ref[c * c_len:(c + 1) * c_len, h * DN_HEAD_DIM:(h + 1) * DN_HEAD_DIM] = o[d * DN_HEADS + h]


def _dn_mix(q, k, v, gates, gates_t):
    t = q.shape[0]
    n_chunks = t // DN_CHUNK
    n_ch = min(DN_STEP_CHUNKS, n_chunks)
    n_steps = n_chunks // n_ch
    rows = n_ch * DN_CHUNK
    n_inst = n_ch * 2 * DN_HEADS
    specs = []
    for row, row3 in ((lambda i: (i, 0), lambda i: (i, 0, 0)),
                      (lambda i: (n_steps - 1 - i, 0), lambda i: (n_steps - 1 - i, 0, 0))):
        specs += [
            pl.BlockSpec((rows, DN_WIDTH), row),
            pl.BlockSpec((rows, DN_WIDTH), row),
            pl.BlockSpec((rows, DN_WIDTH), row),
            pl.BlockSpec((rows, LANES), row),
            pl.BlockSpec((n_ch, 6 * DN_HEADS, DN_CHUNK), row3),
        ]
    return pl.pallas_call(
        _dn_kernel,
        grid=(n_steps,),
        in_specs=specs,
        out_specs=[
            pl.BlockSpec((rows, DN_WIDTH), lambda i: (i, 0)),
            pl.BlockSpec((rows, DN_WIDTH), lambda i: (n_steps - 1 - i, 0)),
        ],
        out_shape=[jax.ShapeDtypeStruct((t, DN_WIDTH), F32), jax.ShapeDtypeStruct((t, DN_WIDTH), F32)],
        scratch_shapes=[
            pltpu.VMEM((2 * DN_HEADS, DN_HEAD_DIM, DN_HEAD_DIM), F32),
            pltpu.VMEM((n_inst, 2 * DN_CHUNK, DN_HEAD_DIM), BF16),
            pltpu.VMEM((n_inst, DN_HEAD_DIM, DN_CHUNK), BF16),
            pltpu.VMEM((n_inst, DN_CHUNK, DN_CHUNK), BF16),
            pltpu.VMEM((n_inst, DN_CHUNK, DN_HEAD_DIM), F32),
            pltpu.VMEM((n_inst, 1, LANES), F32),
        ],
        compiler_params=_params("arbitrary"),
        name="dn_mix",
    )(q, k, v, gates, gates_t, q, k, v, gates, gates_t)


def _post_kernel(x_ref, u_ref, ys_ref, of_ref, ob_ref, z_ref, d_ref, wglu_ref, bglu_ref, nw_ref,
                 wout_ref, nffn_ref, wr_ref, br_ref, x1_ref, h_ref, route_ref, cnt_ref, base):
    i = pl.program_id(0)
    tm = x_ref.shape[0]

    @pl.when(i == 0)
    def _():
        base[...] = jnp.zeros_like(base)

    y = jnp.concatenate([ys_ref[b] + d_ref[:, b * LANES:(b + 1) * LANES] * u_ref[b]
                         for b in range(S5_WIDTH // LANES)], axis=1)
    y = 0.5 * y * (1.0 + lax.erf(y * (2.0 ** -0.5)))
    gate = jnp.dot(y.astype(BF16), wglu_ref[...], preferred_element_type=F32) + bglu_ref[...]
    y_s5 = y * jax.nn.sigmoid(gate)
    acc = x_ref[...] + jnp.dot(y_s5.astype(BF16), wout_ref[0:S5_WIDTH, :], preferred_element_type=F32)
    for h in range(DN_HEADS):
        hs = slice(h * DN_HEAD_DIM, (h + 1) * DN_HEAD_DIM)
        o = of_ref[:, hs] + ob_ref[:, hs]
        zh = z_ref[:, hs]
        o = o * lax.rsqrt(jnp.mean(o * o, axis=-1, keepdims=True) + NORM_EPS) * nw_ref[...]
        y_dn = o * (zh * jax.nn.sigmoid(zh))
        acc = acc + jnp.dot(y_dn.astype(BF16),
                            wout_ref[S5_WIDTH + h * DN_HEAD_DIM:S5_WIDTH + (h + 1) * DN_HEAD_DIM, :],
                            preferred_element_type=F32)
    x1_ref[...] = acc
    hn = acc * lax.rsqrt(jnp.mean(acc * acc, axis=-1, keepdims=True) + NORM_EPS) * nffn_ref[...]
    _matrix_to_rows(h_ref, hn)

    logits = _dot_split(hn, wr_ref[...]) + br_ref[...]
    lane_i = lax.broadcasted_iota(jnp.int32, logits.shape, 1)
    lane = lane_i.astype(F32)
    neg = jnp.float32(-jnp.inf)
    big = jnp.float32(LANES)
    gl = jnp.where(lane_i < N_EXPERT_GROUPS, logits, neg)
    gmax = jnp.max(gl, axis=-1, keepdims=True)
    g_sel = jnp.min(jnp.where(gl == gmax, lane, big), axis=-1, keepdims=True)
    p_group = 1.0 / jnp.sum(jnp.exp(gl - gmax), axis=-1, keepdims=True)
    lo = N_EXPERT_GROUPS + g_sel * EXPERTS_PER_GROUP
    el = jnp.where((lane >= lo) & (lane < lo + EXPERTS_PER_GROUP), logits, neg)
    top1 = jnp.max(el, axis=-1, keepdims=True)
    idx1 = jnp.min(jnp.where(el == top1, lane, big), axis=-1, keepdims=True)
    el2 = jnp.where(lane == idx1, neg, el)
    top2 = jnp.max(el2, axis=-1, keepdims=True)
    idx2 = jnp.min(jnp.where(el2 == top2, lane, big), axis=-1, keepdims=True)
    e21 = jnp.exp(top2 - top1)
    w1 = p_group / (1.0 + e21)
    w2 = w1 * e21
    oh1 = jnp.where(lane == idx1, 1.0, 0.0).astype(F32)
    oh2 = jnp.where(lane == idx2, 1.0, 0.0).astype(F32)
    ri = lax.broadcasted_iota(jnp.int32, (tm, tm), 0)
    ci = lax.broadcasted_iota(jnp.int32, (tm, tm), 1)
    before = jnp.where(ri > ci, 1.0, 0.0).astype(BF16)
    ohs = oh1 + oh2
    prior = jnp.dot(before, ohs.astype(BF16), preferred_element_type=F32) + base[0:1, :]
    rank1 = jnp.sum(oh1 * prior, axis=-1, keepdims=True)
    rank2 = jnp.sum(oh2 * prior, axis=-1, keepdims=True)
    base[0:1, :] = base[0:1, :] + jnp.sum(ohs, axis=0, keepdims=True)
    e1 = idx1 - N_EXPERT_GROUPS
    e2 = idx2 - N_EXPERT_GROUPS
    route = jnp.where(lane_i == 0, e1, jnp.where(lane_i == 1, e2, jnp.where(lane_i == 2, w1, jnp.where(
        lane_i == 3, w2, jnp.where(lane_i == 4, rank1, jnp.where(lane_i == 5, rank2, 0.0))))))
    route_ref[...] = route
    cnt_ref[...] = base[...]


def _post(x, u, ys, o_f, o_b, z, s5_d, w_glu, b_glu, dn_norm_w, w_out, norm_ffn, w_rg, b_rg, w_re, b_re):
    t = x.shape[0]
    tm = min(ROW_TILE, t)
    wr = jnp.concatenate([w_rg, w_re.transpose(1, 0, 2).reshape(D_MODEL, N_EXPERTS)], axis=1)
    wr = _split_weight(jnp.pad(wr.astype(F32), ((0, 0), (0, LANES - N_EXPERT_GROUPS - N_EXPERTS))))
    br = jnp.pad(jnp.concatenate([b_rg, b_re.reshape(-1)]).astype(F32),
                 (0, LANES - N_EXPERT_GROUPS - N_EXPERTS)).reshape(1, LANES)
    row = lambda i: (i, 0)
    const = lambda i: (0, 0)
    return pl.pallas_call(
        _post_kernel,
        grid=(t // tm,),
        in_specs=[
            pl.BlockSpec((tm, D_MODEL), row),
            pl.BlockSpec((S5_WIDTH // LANES, tm, LANES), lambda i: (0, i, 0)),
            pl.BlockSpec((S5_WIDTH // LANES, tm, LANES), lambda i: (0, i, 0)),
            pl.BlockSpec((tm, DN_WIDTH), row),
            pl.BlockSpec((tm, DN_WIDTH), row),
            pl.BlockSpec((tm, DN_WIDTH), row),
            pl.BlockSpec((1, S5_WIDTH), const),
            pl.BlockSpec((S5_WIDTH, S5_WIDTH), const),
            pl.BlockSpec((1, S5_WIDTH), const),
            pl.BlockSpec((1, DN_HEAD_DIM), const),
            pl.BlockSpec((D_MODEL, D_MODEL), const),
            pl.BlockSpec((1, D_MODEL), const),
            pl.BlockSpec((D_MODEL, 2 * LANES), const),
            pl.BlockSpec((1, LANES), const),
        ],
        out_specs=[
            pl.BlockSpec((tm, D_MODEL), row),
            pl.BlockSpec((tm * (D_MODEL // LANES), LANES), row),
            pl.BlockSpec((tm, LANES), row),
            pl.BlockSpec((SUBLANES, LANES), const),
        ],
        out_shape=[
            jax.ShapeDtypeStruct((t, D_MODEL), F32),
            jax.ShapeDtypeStruct((t * (D_MODEL // LANES), LANES), F32),
            jax.ShapeDtypeStruct((t, LANES), F32),
            jax.ShapeDtypeStruct((SUBLANES, LANES), F32),
        ],
        scratch_shapes=[pltpu.VMEM((SUBLANES, LANES), F32)],
        compiler_params=_params("arbitrary"),
        name="mixer_post",
    )(x, u, ys, o_f, o_b, z, s5_d.reshape(1, -1).astype(F32), w_glu.astype(BF16),
      b_glu.reshape(1, -1).astype(F32), dn_norm_w.reshape(1, -1).astype(F32), w_out.astype(BF16),
      norm_ffn.reshape(1, -1).astype(F32), wr, br)


ROW_SPLIT = D_MODEL // LANES


def _rows_to_matrix(ref, n_rows, lead=None):
    parts = []
    for s in range(ROW_SPLIT):
        idx = pl.ds(s, n_rows, stride=ROW_SPLIT)
        parts.append(ref[idx, :] if lead is None else ref[lead, idx, :])
    return jnp.concatenate(parts, axis=1)


def _matrix_to_rows(ref, val):
    n_rows = val.shape[0]
    for s in range(ROW_SPLIT):
        ref[pl.ds(s, n_rows, stride=ROW_SPLIT), :] = val[:, s * LANES:(s + 1) * LANES]


def _row(ref, r):
    return ref.at[pl.ds(pl.multiple_of(r * ROW_SPLIT, ROW_SPLIT), ROW_SPLIT), :]


def _dispatch_kernel(dest_ref, h_ref, zeros_ref, xs_ref, sem):
    del zeros_ref
    i = pl.program_id(0)
    tm = h_ref.shape[0] // ROW_SPLIT

    def copy(r, k):
        slot = dest_ref[2 * (i * tm + r) + k]
        return pltpu.make_async_copy(_row(h_ref, r), _row(xs_ref, slot), sem)

    def start(r, _):
        copy(r, 0).start(priority=0)
        copy(r, 1).start(priority=1)
        return 0

    def wait(r, _):
        copy(r, 0).wait()
        copy(r, 1).wait()
        return 0

    lax.fori_loop(0, tm, start, 0, unroll=8)
    lax.fori_loop(0, tm, wait, 0, unroll=8)


def _dispatch(h, dest, n_slots):
    t = h.shape[0] // ROW_SPLIT
    tm = min(MOE_TOK_TILE, t)
    grid_spec = pltpu.PrefetchScalarGridSpec(
        num_scalar_prefetch=1,
        grid=(t // tm,),
        in_specs=[
            pl.BlockSpec((tm * ROW_SPLIT, LANES), lambda i, dest: (i, 0)),
            pl.BlockSpec(memory_space=pl.ANY),
        ],
        out_specs=pl.BlockSpec(memory_space=pl.ANY),
        scratch_shapes=[pltpu.SemaphoreType.DMA(())],
    )
    return pl.pallas_call(
        _dispatch_kernel,
        grid_spec=grid_spec,
        out_shape=jax.ShapeDtypeStruct((n_slots * ROW_SPLIT, LANES), F32),
        input_output_aliases={2: 0},
        compiler_params=_params("arbitrary"),
        name="moe_dispatch",
    )(dest, h, jnp.zeros((n_slots * ROW_SPLIT, LANES), F32))


def _expert_kernel(be_ref, used_ref, xs_ref, wg_ref, wu_ref, wd_ref, ys_ref, wgu_b, wd_b):
    i = pl.program_id(0)
    blk = ys_ref.shape[0] // ROW_SPLIT

    @pl.when((i == 0) | (be_ref[i] != be_ref[jnp.maximum(i - 1, 0)]))
    def _():
        wgu_b[:, :D_EXPERT] = wg_ref[...].astype(BF16)
        wgu_b[:, D_EXPERT:] = wu_ref[...].astype(BF16)
        wd_b[...] = wd_ref[...].astype(BF16)

    @pl.when(i < used_ref[0])
    def _():
        xb = _rows_to_matrix(xs_ref, blk).astype(BF16)
        gu = jnp.dot(xb, wgu_b[...], preferred_element_type=F32)
        g, u = gu[:, :D_EXPERT], gu[:, D_EXPERT:]
        hid = (g * jax.nn.sigmoid(g) * u).astype(BF16)
        _matrix_to_rows(ys_ref, jnp.dot(hid, wd_b[...], preferred_element_type=F32))

    @pl.when(i >= used_ref[0])
    def _():
        ys_ref[...] = jnp.zeros_like(ys_ref)


def _experts(xs, block_expert, used, w_gate, w_up, w_down, layer):
    n_slots = xs.shape[0] // ROW_SPLIT
    n_blocks = n_slots // MOE_BLOCK
    grid_spec = pltpu.PrefetchScalarGridSpec(
        num_scalar_prefetch=2,
        grid=(n_blocks,),
        in_specs=[
            pl.BlockSpec((MOE_BLOCK * ROW_SPLIT, LANES), lambda i, be, used: (i, 0)),
            pl.BlockSpec((None, None, D_MODEL, D_EXPERT), lambda i, be, used: (layer, be[i], 0, 0)),
            pl.BlockSpec((None, None, D_MODEL, D_EXPERT), lambda i, be, used: (layer, be[i], 0, 0)),
            pl.BlockSpec((None, None, D_EXPERT, D_MODEL), lambda i, be, used: (layer, be[i], 0, 0)),
        ],
        out_specs=pl.BlockSpec((MOE_BLOCK * ROW_SPLIT, LANES), lambda i, be, used: (i, 0)),
        scratch_shapes=[pltpu.VMEM((D_MODEL, 2 * D_EXPERT), BF16), pltpu.VMEM((D_EXPERT, D_MODEL), BF16)],
    )
    return pl.pallas_call(
        _expert_kernel,
        grid_spec=grid_spec,
        out_shape=jax.ShapeDtypeStruct((n_slots * ROW_SPLIT, LANES), F32),
        compiler_params=_params("arbitrary"),
        name="moe_experts",
    )(block_expert, used, xs, w_gate, w_up, w_down)


def _combine_kernel(dest_ref, x1_ref, route_ref, ys_ref, nw_ref, out_ref, buf, sem, *, final_norm):
    i = pl.program_id(0)
    tm = x1_ref.shape[0]

    def copy(r, k):
        slot = dest_ref[2 * (i * tm + r) + k]
        dst = buf.at[k, pl.ds(pl.multiple_of(r * ROW_SPLIT, ROW_SPLIT), ROW_SPLIT), :]
        return pltpu.make_async_copy(_row(ys_ref, slot), dst, sem)

    def start(r, _):
        copy(r, 0).start(priority=0)
        copy(r, 1).start(priority=1)
        return 0

    def wait(r, _):
        copy(r, 0).wait()
        copy(r, 1).wait()
        return 0

    lax.fori_loop(0, tm, start, 0, unroll=8)
    lax.fori_loop(0, tm, wait, 0, unroll=8)
    route = route_ref[...]
    out = (x1_ref[...] + route[:, 2:3] * _rows_to_matrix(buf, tm, 0)
           + route[:, 3:4] * _rows_to_matrix(buf, tm, 1))
    if final_norm:
        out = out * lax.rsqrt(jnp.mean(out * out, axis=-1, keepdims=True) + NORM_EPS) * nw_ref[...]
    out_ref[...] = out


def _combine(x1, route, ys, dest, norm_w, final_norm):
    t = x1.shape[0]
    tm = min(MOE_TOK_TILE, t)
    grid_spec = pltpu.PrefetchScalarGridSpec(
        num_scalar_prefetch=1,
        grid=(t // tm,),
        in_specs=[
            pl.BlockSpec((tm, D_MODEL), lambda i, dest: (i, 0)),
            pl.BlockSpec((tm, LANES), lambda i, dest: (i, 0)),
            pl.BlockSpec(memory_space=pl.ANY),
            pl.BlockSpec((1, D_MODEL), lambda i, dest: (0, 0)),
        ],
        out_specs=pl.BlockSpec((tm, D_MODEL), lambda i, dest: (i, 0)),
        scratch_shapes=[pltpu.VMEM((2, tm * ROW_SPLIT, LANES), F32), pltpu.SemaphoreType.DMA(())],
    )
    return pl.pallas_call(
        functools.partial(_combine_kernel, final_norm=final_norm),
        grid_spec=grid_spec,
        out_shape=jax.ShapeDtypeStruct((t, D_MODEL), F32),
        compiler_params=_params("arbitrary"),
        name="moe_combine",
    )(dest, x1, route, ys, norm_w.reshape(1, D_MODEL).astype(F32))


def _moe(x1, h, route, counts, w_gate, w_up, w_down, layer, norm_w, final_norm):
    t = x1.shape[0]
    n_blocks = -(-(2 * t) // MOE_BLOCK) + N_EXPERTS
    n_slots = n_blocks * MOE_BLOCK
    cnt = counts[0, N_EXPERT_GROUPS:N_EXPERT_GROUPS + N_EXPERTS].astype(jnp.int32)
    padded = ((cnt + MOE_BLOCK - 1) // MOE_BLOCK) * MOE_BLOCK
    pad_end = jnp.cumsum(padded)
    pad_start = pad_end - padded
    expert = route[:, 0:2].astype(jnp.int32)
    rank = route[:, 4:6].astype(jnp.int32)
    dest = (pad_start[expert] + rank).reshape(-1)
    block_start = jnp.arange(n_blocks, dtype=jnp.int32) * MOE_BLOCK
    block_expert = jnp.minimum(jnp.sum(pad_end[None, :] <= block_start[:, None], axis=1),
                               N_EXPERTS - 1).astype(jnp.int32)
    used = (pad_end[-1:] // MOE_BLOCK).astype(jnp.int32)
    xs = _dispatch(h, dest, n_slots)
    ys = _experts(xs, block_expert, used, w_gate, w_up, w_down, layer)
    return _combine(x1, route, ys, dest, norm_w, final_norm)


def kernel(x, norm_mix, w_in, s5_lam_re, s5_lam_im, s5_log_dt, s5_b_re, s5_b_im, s5_c_re, s5_c_im,
           s5_d, s5_w_glu, s5_b_glu, gdn_conv_w, gdn_a_log, gdn_dt_bias, gdn_norm_w, w_out, norm_ffn,
           router_w_group, router_b_group, router_w_expert, router_b_expert,
           expert_w_gate, expert_w_up, expert_w_down, norm_final):
    bsz, seq, d = x.shape
    depth = norm_mix.shape[0]
    xt = x.astype(F32).reshape(bsz * seq, d)
    for i in range(depth):
        u, qkv, z, ab = _inproj(xt, norm_mix[i], w_in[i])
        tables = _s5_tables(s5_lam_re[i], s5_lam_im[i], s5_log_dt[i], s5_b_re[i], s5_b_im[i],
                            s5_c_re[i], s5_c_im[i])
        ys = _s5_mix(u, tables)
        q, k, v, gates = _dn_prep(qkv, ab, gdn_conv_w[i], gdn_a_log[i], gdn_dt_bias[i])
        gates_t = (gates[:, :6 * DN_HEADS].reshape(-1, DN_CHUNK, 6 * DN_HEADS).transpose(0, 2, 1))
        o_f, o_b = _dn_mix(q, k, v, gates, gates_t)
        x1, h, route, counts = _post(xt, u, ys, o_f, o_b, z, s5_d[i], s5_w_glu[i], s5_b_glu[i],
                                     gdn_norm_w[i], w_out[i], norm_ffn[i], router_w_group[i],
                                     router_b_group[i], router_w_expert[i], router_b_expert[i])
        xt = _moe(x1, h, route, counts, expert_w_gate, expert_w_up, expert_w_down, i,
                  norm_final, i == depth - 1)
    return xt.reshape(bsz, seq, d)
```

```python
import functools
import math

import jax
import jax.numpy as jnp
import numpy as np
from jax import lax
from jax.experimental import pallas as pl
from jax.experimental.pallas import tpu as pltpu

F32 = jnp.float32
BF16 = jnp.bfloat16
HIGHEST = lax.Precision.HIGHEST

D_MODEL = 1024
S5_WIDTH = 512
S5_GROUP = 16
S5_GROUPS = 32
S5_STATE = 64
S5_MAX_RE = -1e-4
DN_HEADS = 4
DN_HEAD_DIM = 128
DN_WIDTH = 512
DN_CONV = 5
DN_CHUNK = 64
N_EXPERT_GROUPS = 4
EXPERTS_PER_GROUP = 8
N_EXPERTS = 32
D_EXPERT = 512
NORM_EPS = 1e-6

LANES = 128
SUBLANES = 8
VMEM_LIMIT = 56 * 1024 * 1024

S5_CHUNK = 16
S5_TILE = 128
ROW_TILE = 512
DN_STEP_CHUNKS = 8
MOE_BLOCK = 256
MOE_TOK_TILE = 256


def _params(*sem):
    return pltpu.CompilerParams(dimension_semantics=sem, vmem_limit_bytes=VMEM_LIMIT)


def _split_weight(w):
    hi = w.astype(BF16)
    lo = (w - hi.astype(F32)).astype(BF16)
    return jnp.concatenate([hi, lo], axis=1)


def _dot_split(a, w_split):
    a_hi = a.astype(BF16)
    a_lo = (a - a_hi.astype(F32)).astype(BF16)
    p = jnp.dot(a_hi, w_split, preferred_element_type=F32)
    q = jnp.dot(a_lo, w_split[:, :LANES], preferred_element_type=F32)
    return p[:, :LANES] + p[:, LANES:] + q


def _inproj_kernel(x_ref, nw_ref, w_ref, wab_ref, u_ref, qkv_ref, z_ref, ab_ref):
    x = x_ref[...]
    h = x * lax.rsqrt(jnp.mean(x * x, axis=-1, keepdims=True) + NORM_EPS) * nw_ref[...]
    hb = h.astype(BF16)
    for blk in range(S5_WIDTH // LANES):
        u_ref[blk] = jnp.dot(hb, w_ref[:, blk * LANES:(blk + 1) * LANES], preferred_element_type=F32)
    qkv_ref[...] = jnp.dot(hb, w_ref[:, S5_WIDTH:S5_WIDTH + 3 * DN_WIDTH], preferred_element_type=F32)
    z_ref[...] = jnp.dot(hb, w_ref[:, S5_WIDTH + 3 * DN_WIDTH:S5_WIDTH + 4 * DN_WIDTH],
                         preferred_element_type=F32)
    ab_ref[...] = _dot_split(h, wab_ref[...])


def _inproj(x, norm_w, w_in):
    t = x.shape[0]
    n_main = S5_WIDTH + 4 * DN_WIDTH
    w_main = w_in[:, :n_main].astype(BF16)
    w_ab = _split_weight(jnp.pad(w_in[:, n_main:].astype(F32), ((0, 0), (0, LANES - 4 * DN_HEADS))))
    tm = min(ROW_TILE, t)
    row = lambda i: (i, 0)
    const = lambda i: (0, 0)
    return pl.pallas_call(
        _inproj_kernel,
        grid=(t // tm,),
        in_specs=[
            pl.BlockSpec((tm, D_MODEL), row),
            pl.BlockSpec((1, D_MODEL), const),
            pl.BlockSpec((D_MODEL, n_main), const),
            pl.BlockSpec((D_MODEL, 2 * LANES), const),
        ],
        out_specs=[
            pl.BlockSpec((S5_WIDTH // LANES, tm, LANES), lambda i: (0, i, 0)),
            pl.BlockSpec((tm, 3 * DN_WIDTH), row),
            pl.BlockSpec((tm, DN_WIDTH), row),
            pl.BlockSpec((tm, LANES), row),
        ],
        out_shape=[
            jax.ShapeDtypeStruct((S5_WIDTH // LANES, t, LANES), F32),
            jax.ShapeDtypeStruct((t, 3 * DN_WIDTH), F32),
            jax.ShapeDtypeStruct((t, DN_WIDTH), F32),
            jax.ShapeDtypeStruct((t, LANES), F32),
        ],
        compiler_params=_params("parallel"),
        name="inproj",
    )(x, norm_w.reshape(1, D_MODEL), w_main, w_ab)


def _s5_tables(lam_re, lam_im, log_dt, b_re, b_im, c_re, c_im):
    c_len = S5_CHUNK
    lr = jnp.minimum(lam_re.astype(F32), S5_MAX_RE)
    li = lam_im.astype(F32)
    dt = jnp.exp(log_dt.astype(F32))[..., None]
    zr, zi = lr * dt, li * dt
    e1 = jnp.exp(zr)
    ar, ai = e1 * jnp.cos(zi), e1 * jnp.sin(zi)
    den = lr * lr + li * li
    nr, ni = ar - 1.0, ai
    fr = (nr * lr + ni * li) / den
    fi = (ni * lr - nr * li) / den
    bbr = fr[..., None] * b_re - fi[..., None] * b_im
    bbi = fr[..., None] * b_im + fi[..., None] * b_re
    tau = jnp.arange(c_len + 1, dtype=F32)[:, None, None, None]
    mag = jnp.exp(tau * zr)
    pr, pi = mag * jnp.cos(tau * zi), mag * jnp.sin(tau * zi)
    m_r = pr[..., None] * bbr - pi[..., None] * bbi
    m_i = pr[..., None] * bbi + pi[..., None] * bbr
    kern = (jnp.einsum('dgpn,tdgnq->tdgpq', c_re, m_r[:c_len], precision=HIGHEST)
            - jnp.einsum('dgpn,tdgnq->tdgpq', c_im, m_i[:c_len], precision=HIGHEST))
    beta_r = c_re[None] * pr[:, :, :, None, :] - c_im[None] * pi[:, :, :, None, :]
    beta_i = c_re[None] * pi[:, :, :, None, :] + c_im[None] * pr[:, :, :, None, :]

    j = np.arange(c_len)
    tables = []
    for d in range(2):
        lag = (j[None, :] - j[:, None]) if d == 0 else (j[:, None] - j[None, :])
        select = (lag[None] == j[:, None, None]).astype(np.float32)
        flip = (lambda a: a[::-1]) if d == 0 else (lambda a: a)
        unflip = (lambda a: a) if d == 0 else (lambda a: a[::-1])
        kd = kern[:, d]
        wt = jnp.einsum('ajt,agpq->gjqtp', select, kd, precision=HIGHEST)
        wt = wt.reshape(S5_GROUPS, 256, 256)
        er = flip(m_r[:c_len, d]).transpose(1, 0, 3, 2).reshape(S5_GROUPS, 256, S5_STATE)
        ei = flip(m_i[:c_len, d]).transpose(1, 0, 3, 2).reshape(S5_GROUPS, 256, S5_STATE)
        we = jnp.concatenate([er, ei, ei, er], axis=-1)
        sr = unflip(beta_r[1:, d]).transpose(1, 3, 0, 2).reshape(S5_GROUPS, S5_STATE, 256)
        si = unflip(beta_i[1:, d]).transpose(1, 3, 0, 2).reshape(S5_GROUPS, S5_STATE, 256)
        ws = jnp.concatenate([sr, -si], axis=1)
        a_r, a_i = pr[c_len, d], pi[c_len, d]
        coef = jnp.stack([jnp.concatenate([a_r, a_r], -1),
                          jnp.concatenate([-a_i, a_i], -1),
                          jnp.concatenate([a_i, -a_i], -1)])
        tables.append((wt.astype(BF16), we.astype(BF16), ws.astype(BF16), coef))
    return tables


def _block_transpose8(xs, lane):
    for k in (2, 1, 0):
        shift = S5_GROUP << k
        bit = jnp.bitwise_and(lax.shift_right_logical(lane, 4 + k), 1)
        new = list(xs)
        for a in range(8):
            if (a >> k) & 1:
                continue
            b = a + (1 << k)
            new[a] = jnp.where(bit == 0, xs[a], pltpu.roll(xs[b], shift, 1))
            new[b] = jnp.where(bit == 1, xs[b], pltpu.roll(xs[a], LANES - shift, 1))
        xs = new
    return xs


def _s5_scan_kernel(u_ref, wt_ref, we_ref, ws_ref, coef_ref, *rest, reverse, add_prev):
    if add_prev:
        prev_ref, y_ref, ug_scr, yg_scr, e_scr, es_scr, s_scr, carry = rest
    else:
        y_ref, ug_scr, yg_scr, e_scr, es_scr, s_scr, carry = rest
    n_rows = ug_scr.shape[1]
    n_blk = S5_WIDTH // LANES
    per_blk = LANES // S5_GROUP
    halves = S5_CHUNK // per_blk

    @pl.when(pl.program_id(0) == 0)
    def _():
        carry[...] = jnp.zeros_like(carry)

    lane = lax.broadcasted_iota(jnp.int32, (n_rows, LANES), 1)

    for blk in range(n_blk):
        for half in range(halves):
            xs = [u_ref[blk, pl.ds(half * per_blk + jl, n_rows, stride=S5_CHUNK), :] for jl in range(per_blk)]
            ys = _block_transpose8(xs, lane)
            for gl in range(per_blk):
                ug_scr[blk * per_blk + gl, :, half * LANES:(half + 1) * LANES] = ys[gl].astype(BF16)

    for g in range(S5_GROUPS):
        e = jnp.dot(ug_scr[g], we_ref[g], preferred_element_type=F32)
        e_scr[g * n_rows:(g + 1) * n_rows, :] = e[:, :LANES]
        es_scr[g * n_rows:(g + 1) * n_rows, :] = e[:, LANES:]

    c1, c2, c3 = coef_ref[0], coef_ref[1], coef_ref[2]

    def step(i, vs):
        v, vp = vs
        r = (n_rows - 1 - i) if reverse else i
        rows = pl.ds(r, S5_GROUPS, stride=n_rows)
        s_scr[rows, :] = v
        v_new = v * c1 + vp * c2 + e_scr[rows, :]
        vp_new = vp * c1 + v * c3 + es_scr[rows, :]
        return v_new, vp_new

    v, vp = lax.fori_loop(0, n_rows, step, (carry[0], carry[1]), unroll=8)
    carry[0] = v
    carry[1] = vp

    for blk in range(n_blk):
        for gl in range(per_blk):
            g = blk * per_blk + gl
            s_in = s_scr[g * n_rows:(g + 1) * n_rows, :].astype(BF16)
            yg_scr[gl] = (jnp.dot(ug_scr[g], wt_ref[g], preferred_element_type=F32)
                          + jnp.dot(s_in, ws_ref[g], preferred_element_type=F32))
        for half in range(halves):
            zs = [yg_scr[gl, :, half * LANES:(half + 1) * LANES] for gl in range(per_blk)]
            ws = _block_transpose8(zs, lane)
            for tl in range(per_blk):
                rows = pl.ds(half * per_blk + tl, n_rows, stride=S5_CHUNK)
                out = ws[tl]
                if add_prev:
                    out = out + prev_ref[blk, rows, :]
                y_ref[blk, rows, :] = out


def _s5_direction(u4, tables, prev, reverse):
    n_blk, t, _ = u4.shape
    n_chunks = t // S5_CHUNK
    rows = min(S5_TILE, n_chunks)
    n_tiles = n_chunks // rows
    wt, we, ws, coef = tables
    tile = (lambda i: (0, n_tiles - 1 - i, 0)) if reverse else (lambda i: (0, i, 0))
    const3 = lambda i: (0, 0, 0)
    once = pl.Buffered(1)
    in_specs = [
        pl.BlockSpec((n_blk, rows * S5_CHUNK, LANES), tile),
        pl.BlockSpec((S5_GROUPS, 256, 256), const3, pipeline_mode=once),
        pl.BlockSpec((S5_GROUPS, 256, 256), const3, pipeline_mode=once),
        pl.BlockSpec((S5_GROUPS, LANES, 256), const3, pipeline_mode=once),
        pl.BlockSpec((3, S5_GROUPS, LANES), const3, pipeline_mode=once),
    ]
    args = [u4, wt, we, ws, coef]
    if prev is not None:
        in_specs.append(pl.BlockSpec((n_blk, rows * S5_CHUNK, LANES), tile))
        args.append(prev)
    return pl.pallas_call(
        functools.partial(_s5_scan_kernel, reverse=reverse, add_prev=prev is not None),
        grid=(n_tiles,),
        in_specs=in_specs,
        out_specs=pl.BlockSpec((n_blk, rows * S5_CHUNK, LANES), tile),
        out_shape=jax.ShapeDtypeStruct((n_blk, t, LANES), F32),
        scratch_shapes=[
            pltpu.VMEM((S5_GROUPS, rows, 256), BF16),
            pltpu.VMEM((LANES // S5_GROUP, rows, 256), F32),
            pltpu.VMEM((S5_GROUPS * rows, LANES), F32),
            pltpu.VMEM((S5_GROUPS * rows, LANES), F32),
            pltpu.VMEM((S5_GROUPS * rows, LANES), F32),
            pltpu.VMEM((2, S5_GROUPS, LANES), F32),
        ],
        compiler_params=_params("arbitrary"),
        name="s5_bwd" if reverse else "s5_fwd",
    )(*args)


def _s5_mix(u4, tables):
    y = _s5_direction(u4, tables[0], None, False)
    return _s5_direction(u4, tables[1], y, True)


def _dn_prep_kernel(cur_ref, prev_ref, next_ref, ab_ref, cw_ref, gp_ref,
                    q_ref, k_ref, v_ref, gate_ref, ext):
    i = pl.program_id(0)
    tm = cur_ref.shape[0]
    pad = DN_CONV // 2
    ext[0:SUBLANES, :] = jnp.where(i > 0, prev_ref[...], 0.0)
    ext[SUBLANES:SUBLANES + tm, :] = cur_ref[...]
    ext[SUBLANES + tm:, :] = jnp.where(i < pl.num_programs(0) - 1, next_ref[...], 0.0)
    outs = (q_ref, k_ref, v_ref)
    for part in range(3):
        cols = slice(part * DN_WIDTH, (part + 1) * DN_WIDTH)
        acc = ext[pl.ds(SUBLANES - pad, tm), cols] * cw_ref[0:1, cols]
        for tap in range(1, DN_CONV):
            acc = acc + ext[pl.ds(SUBLANES - pad + tap, tm), cols] * cw_ref[tap:tap + 1, cols]
        act = acc * jax.nn.sigmoid(acc)
        if part == 2:
            v_ref[...] = act
            continue
        scale = DN_HEAD_DIM ** -0.5 if part == 0 else 1.0
        for h in range(DN_HEADS):
            hs = slice(h * DN_HEAD_DIM, (h + 1) * DN_HEAD_DIM)
            xh = act[:, hs]
            inv = lax.rsqrt(jnp.sum(xh * xh, axis=-1, keepdims=True) + NORM_EPS)
            outs[part][:, hs] = xh * inv * scale
    ab = ab_ref[...]
    lane = lax.broadcasted_iota(jnp.int32, ab.shape, 1)
    pre = ab + gp_ref[1:2, :]
    softplus = jnp.maximum(pre, 0.0) + jnp.log1p(jnp.exp(-jnp.abs(pre)))
    n_gate = 2 * DN_HEADS
    g = jnp.where(lane < n_gate, gp_ref[0:1, :] * softplus, 0.0)
    hi = g.astype(BF16).astype(F32)
    r1 = g - hi
    mid = r1.astype(BF16).astype(F32)
    lo = (r1 - mid).astype(BF16).astype(F32)
    pieces = (hi + pltpu.roll(mid, n_gate, 1) + pltpu.roll(lo, 2 * n_gate, 1)).astype(BF16)
    ri = lax.broadcasted_iota(jnp.int32, (tm, tm), 0)
    ci = lax.broadcasted_iota(jnp.int32, (tm, tm), 1)
    chunk_bits = DN_CHUNK.bit_length() - 1
    same = lax.shift_right_logical(ri, chunk_bits) == lax.shift_right_logical(ci, chunk_bits)
    tri_lo = jnp.where(same & (ri >= ci), 1.0, 0.0).astype(BF16)
    tri_up = jnp.where(same & (ri <= ci), 1.0, 0.0).astype(BF16)
    pref = jnp.dot(tri_lo, pieces, preferred_element_type=F32)
    suff = jnp.dot(tri_up, pieces, preferred_element_type=F32)
    part = jnp.where(jnp.bitwise_and(lane, n_gate - 1) < DN_HEADS, pref, suff)
    gsum = part + pltpu.roll(part, LANES - n_gate, 1) + pltpu.roll(part, LANES - 2 * n_gate, 1)
    gate_ref[...] = jnp.where(lane < n_gate, g, jnp.where(
        lane < 2 * n_gate, jax.nn.sigmoid(ab), jnp.where(
            lane < 3 * n_gate, pltpu.roll(gsum, 2 * n_gate, 1), 0.0)))


def _dn_prep(qkv, ab, conv_w, a_log, dt_bias):
    t = qkv.shape[0]
    tm = min(ROW_TILE, t)
    nb = tm // SUBLANES
    last = t // SUBLANES - 1
    gp = jnp.zeros((SUBLANES, LANES), F32)
    gp = gp.at[0, :2 * DN_HEADS].set(-jnp.exp(a_log.astype(F32)).reshape(-1))
    gp = gp.at[1, :2 * DN_HEADS].set(dt_bias.astype(F32).reshape(-1))
    cw = jnp.pad(conv_w.astype(F32), ((0, SUBLANES - DN_CONV), (0, 0)))
    row = lambda i: (i, 0)
    const = lambda i: (0, 0)
    return pl.pallas_call(
        _dn_prep_kernel,
        grid=(t // tm,),
        in_specs=[
            pl.BlockSpec((tm, 3 * DN_WIDTH), row),
            pl.BlockSpec((SUBLANES, 3 * DN_WIDTH), lambda i: (jnp.maximum(i * nb - 1, 0), 0)),
            pl.BlockSpec((SUBLANES, 3 * DN_WIDTH), lambda i: (jnp.minimum((i + 1) * nb, last), 0)),
            pl.BlockSpec((tm, LANES), row),
            pl.BlockSpec((SUBLANES, 3 * DN_WIDTH), const),
            pl.BlockSpec((SUBLANES, LANES), const),
        ],
        out_specs=[
            pl.BlockSpec((tm, DN_WIDTH), row),
            pl.BlockSpec((tm, DN_WIDTH), row),
            pl.BlockSpec((tm, DN_WIDTH), row),
            pl.BlockSpec((tm, LANES), row),
        ],
        out_shape=[
            jax.ShapeDtypeStruct((t, DN_WIDTH), F32),
            jax.ShapeDtypeStruct((t, DN_WIDTH), F32),
            jax.ShapeDtypeStruct((t, DN_WIDTH), F32),
            jax.ShapeDtypeStruct((t, LANES), F32),
        ],
        scratch_shapes=[pltpu.VMEM((tm + 2 * SUBLANES, 3 * DN_WIDTH), F32)],
        compiler_params=_params("parallel"),
        name="dn_prep",
    )(qkv, qkv, qkv, ab, cw, gp)


def _bmm(a, b):
    return lax.dot_general(a, b, (((2,), (1,)), ((0,), (0,))), preferred_element_type=F32)


def _dn_kernel(qf, kf, vf, gf, gtf, qb, kb_, vb, gb, gtb, of_ref, ob_ref,
               state, wq_scr, kdt_scr, at_scr, u_scr, gam_scr):
    c_len = DN_CHUNK
    n_ch = gtf.shape[0]
    n_gate = 2 * DN_HEADS
    n_chain = 2 * DN_HEADS

    @pl.when(pl.program_id(0) == 0)
    def _():
        state[...] = jnp.zeros_like(state)

    ri = lax.broadcasted_iota(jnp.int32, (c_len, c_len), 0)
    ci = lax.broadcasted_iota(jnp.int32, (c_len, c_len), 1)
    eye = jnp.where(ri == ci, 1.0, 0.0).astype(F32)
    dirs = ((qf, kf, vf, gf, gtf, ri >= ci, ri > ci, c_len - 1),
            (qb, kb_, vb, gb, gtb, ri <= ci, ri < ci, 0))

    for d, (q_ref, k_ref, v_ref, gate_ref, gate_t_ref, incl, strict, last) in enumerate(dirs):
        gates = gate_ref[...].reshape(n_ch, c_len, LANES)
        gates_t = gate_t_ref[...]
        for h in range(DN_HEADS):
            idx = d * DN_HEADS + h
            hs = slice(h * DN_HEAD_DIM, (h + 1) * DN_HEAD_DIM)
            gcol = gates[:, :, 2 * n_gate + idx:2 * n_gate + idx + 1]
            bcol = gates[:, :, n_gate + idx:n_gate + idx + 1]
            grow = gates_t[:, 2 * n_gate + idx:2 * n_gate + idx + 1, :]
            glast = grow[:, :, last:last + 1]
            qh = q_ref[:, hs].reshape(n_ch, c_len, DN_HEAD_DIM)
            kh = k_ref[:, hs].reshape(n_ch, c_len, DN_HEAD_DIM)
            vh = v_ref[:, hs].reshape(n_ch, c_len, DN_HEAD_DIM)
            kb = kh.astype(BF16)
            qk_kk = lax.dot_general(jnp.concatenate([qh.astype(BF16), kb], axis=1), kb,
                                    (((2,), (2,)), ((0,), (0,))), preferred_element_type=F32)
            qk, kk = qk_kk[:, :c_len], qk_kk[:, c_len:]
            decay = jnp.where(incl, jnp.exp(jnp.where(incl, gcol - grow, 0.0)), 0.0)
            a_mat = jnp.where(strict, bcol * kk * decay, 0.0)
            pw = -a_mat
            inv = eye + pw
            for _ in range(5):
                pwb = pw.astype(BF16)
                pw = _bmm(pwb, pwb)
                inv = inv + _bmm(inv.astype(BF16), pw.astype(BF16))
            egc = jnp.exp(gcol)
            rhs = jnp.concatenate([vh * bcol, kh * (bcol * egc)], axis=2).astype(BF16)
            uw = _bmm(inv.astype(BF16), rhs)
            wq = jnp.concatenate([uw[:, :, DN_HEAD_DIM:], qh * egc], axis=1).astype(BF16)
            attn = jnp.where(incl, qk * decay, 0.0).astype(BF16)
            k_dec_t = jnp.swapaxes(kh * jnp.exp(glast - gcol), 1, 2).astype(BF16)
            gamma = jnp.broadcast_to(jnp.exp(glast), (n_ch, 1, LANES))
            for c in range(n_ch):
                slot = (c if d == 0 else n_ch - 1 - c) * n_chain + idx
                wq_scr[slot] = wq[c]
                kdt_scr[slot] = k_dec_t[c]
                at_scr[slot] = attn[c]
                u_scr[slot] = uw[c, :, :DN_HEAD_DIM]
                gam_scr[slot] = gamma[c]

    for step in range(n_ch):
        grp = slice(step * n_chain, (step + 1) * n_chain)
        s = state[...]
        wq_s = _bmm(wq_scr[grp], s.astype(BF16))
        v_nb = (u_scr[grp] - wq_s[:, :c_len]).astype(BF16)
        o = wq_s[:, c_len:] + _bmm(at_scr[grp], v_nb)
        state[...] = s * gam_scr[grp] + _bmm(kdt_scr[grp], v_nb)
        for d, o_ref in enumerate((of_ref, ob_ref)):
            c = step if d == 0 else n_ch - 1 - step
            for h in range(DN_HEADS):
                o_ref[c * c_len:(c + 1) * c_len, h * DN_HEAD_DIM:(h + 1) * DN_HEAD_DIM] = o[d * DN_HEADS + h]


def _dn_mix(q, k, v, gates, gates_t):
    t = q.shape[0]
    n_chunks = t // DN_CHUNK
    n_ch = min(DN_STEP_CHUNKS, n_chunks)
    n_steps = n_chunks // n_ch
    rows = n_ch * DN_CHUNK
    n_inst = n_ch * 2 * DN_HEADS
    specs = []
    for row, row3 in ((lambda i: (i, 0), lambda i: (i, 0, 0)),
                      (lambda i: (n_steps - 1 - i, 0), lambda i: (n_steps - 1 - i, 0, 0))):
        specs += [
            pl.BlockSpec((rows, DN_WIDTH), row),
            pl.BlockSpec((rows, DN_WIDTH), row),
            pl.BlockSpec((rows, DN_WIDTH), row),
            pl.BlockSpec((rows, LANES), row),
            pl.BlockSpec((n_ch, 6 * DN_HEADS, DN_CHUNK), row3),
        ]
    return pl.pallas_call(
        _dn_kernel,
        grid=(n_steps,),
        in_specs=specs,
        out_specs=[
            pl.BlockSpec((rows, DN_WIDTH), lambda i: (i, 0)),
            pl.BlockSpec((rows, DN_WIDTH), lambda i: (n_steps - 1 - i, 0)),
        ],
        out_shape=[jax.ShapeDtypeStruct((t, DN_WIDTH), F32), jax.ShapeDtypeStruct((t, DN_WIDTH), F32)],
        scratch_shapes=[
            pltpu.VMEM((2 * DN_HEADS, DN_HEAD_DIM, DN_HEAD_DIM), F32),
            pltpu.VMEM((n_inst, 2 * DN_CHUNK, DN_HEAD_DIM), BF16),
            pltpu.VMEM((n_inst, DN_HEAD_DIM, DN_CHUNK), BF16),
            pltpu.VMEM((n_inst, DN_CHUNK, DN_CHUNK), BF16),
            pltpu.VMEM((n_inst, DN_CHUNK, DN_HEAD_DIM), F32),
            pltpu.VMEM((n_inst, 1, LANES), F32),
        ],
        compiler_params=_params("arbitrary"),
        name="dn_mix",
    )(q, k, v, gates, gates_t, q, k, v, gates, gates_t)


def _post_kernel(x_ref, u_ref, ys_ref, of_ref, ob_ref, z_ref, d_ref, wglu_ref, bglu_ref, nw_ref,
                 wout_ref, nffn_ref, wr_ref, br_ref, x1_ref, h_ref, route_ref, cnt_ref, base):
    i = pl.program_id(0)
    tm = x_ref.shape[0]

    @pl.when(i == 0)
    def _():
        base[...] = jnp.zeros_like(base)

    y = jnp.concatenate([ys_ref[b] + d_ref[:, b * LANES:(b + 1) * LANES] * u_ref[b]
                         for b in range(S5_WIDTH // LANES)], axis=1)
    y = 0.5 * y * (1.0 + lax.erf(y * (2.0 ** -0.5)))
    gate = jnp.dot(y.astype(BF16), wglu_ref[...], preferred_element_type=F32) + bglu_ref[...]
    y_s5 = y * jax.nn.sigmoid(gate)
    acc = x_ref[...] + jnp.dot(y_s5.astype(BF16), wout_ref[0:S5_WIDTH, :], preferred_element_type=F32)
    for h in range(DN_HEADS):
        hs = slice(h * DN_HEAD_DIM, (h + 1) * DN_HEAD_DIM)
        o = of_ref[:, hs] + ob_ref[:, hs]
        zh = z_ref[:, hs]
        o = o * lax.rsqrt(jnp.mean(o * o, axis=-1, keepdims=True) + NORM_EPS) * nw_ref[...]
        y_dn = o * (zh * jax.nn.sigmoid(zh))
        acc = acc + jnp.dot(y_dn.astype(BF16),
                            wout_ref[S5_WIDTH + h * DN_HEAD_DIM:S5_WIDTH + (h + 1) * DN_HEAD_DIM, :],
                            preferred_element_type=F32)
    x1_ref[...] = acc
    hn = acc * lax.rsqrt(jnp.mean(acc * acc, axis=-1, keepdims=True) + NORM_EPS) * nffn_ref[...]
    _matrix_to_rows(h_ref, hn)

    logits = _dot_split(hn, wr_ref[...]) + br_ref[...]
    lane_i = lax.broadcasted_iota(jnp.int32, logits.shape, 1)
    lane = lane_i.astype(F32)
    neg = jnp.float32(-jnp.inf)
    big = jnp.float32(LANES)
    gl = jnp.where(lane_i < N_EXPERT_GROUPS, logits, neg)
    gmax = jnp.max(gl, axis=-1, keepdims=True)
    g_sel = jnp.min(jnp.where(gl == gmax, lane, big), axis=-1, keepdims=True)
    p_group = 1.0 / jnp.sum(jnp.exp(gl - gmax), axis=-1, keepdims=True)
    lo = N_EXPERT_GROUPS + g_sel * EXPERTS_PER_GROUP
    el = jnp.where((lane >= lo) & (lane < lo + EXPERTS_PER_GROUP), logits, neg)
    top1 = jnp.max(el, axis=-1, keepdims=True)
    idx1 = jnp.min(jnp.where(el == top1, lane, big), axis=-1, keepdims=True)
    el2 = jnp.where(lane == idx1, neg, el)
    top2 = jnp.max(el2, axis=-1, keepdims=True)
    idx2 = jnp.min(jnp.where(el2 == top2, lane, big), axis=-1, keepdims=True)
    e21 = jnp.exp(top2 - top1)
    w1 = p_group / (1.0 + e21)
    w2 = w1 * e21
    oh1 = jnp.where(lane == idx1, 1.0, 0.0).astype(F32)
    oh2 = jnp.where(lane == idx2, 1.0, 0.0).astype(F32)
    ri = lax.broadcasted_iota(jnp.int32, (tm, tm), 0)
    ci = lax.broadcasted_iota(jnp.int32, (tm, tm), 1)
    before = jnp.where(ri > ci, 1.0, 0.0).astype(BF16)
    ohs = oh1 + oh2
    prior = jnp.dot(before, ohs.astype(BF16), preferred_element_type=F32) + base[0:1, :]
    rank1 = jnp.sum(oh1 * prior, axis=-1, keepdims=True)
    rank2 = jnp.sum(oh2 * prior, axis=-1, keepdims=True)
    base[0:1, :] = base[0:1, :] + jnp.sum(ohs, axis=0, keepdims=True)
    e1 = idx1 - N_EXPERT_GROUPS
    e2 = idx2 - N_EXPERT_GROUPS
    route = jnp.where(lane_i == 0, e1, jnp.where(lane_i == 1, e2, jnp.where(lane_i == 2, w1, jnp.where(
        lane_i == 3, w2, jnp.where(lane_i == 4, rank1, jnp.where(lane_i == 5, rank2, 0.0))))))
    route_ref[...] = route
    cnt_ref[...] = base[...]


def _post(x, u, ys, o_f, o_b, z, s5_d, w_glu, b_glu, dn_norm_w, w_out, norm_ffn, w_rg, b_rg, w_re, b_re):
    t = x.shape[0]
    tm = min(ROW_TILE, t)
    wr = jnp.concatenate([w_rg, w_re.transpose(1, 0, 2).reshape(D_MODEL, N_EXPERTS)], axis=1)
    wr = _split_weight(jnp.pad(wr.astype(F32), ((0, 0), (0, LANES - N_EXPERT_GROUPS - N_EXPERTS))))
    br = jnp.pad(jnp.concatenate([b_rg, b_re.reshape(-1)]).astype(F32),
                 (0, LANES - N_EXPERT_GROUPS - N_EXPERTS)).reshape(1, LANES)
    row = lambda i: (i, 0)
    const = lambda i: (0, 0)
    return pl.pallas_call(
        _post_kernel,
        grid=(t // tm,),
        in_specs=[
            pl.BlockSpec((tm, D_MODEL), row),
            pl.BlockSpec((S5_WIDTH // LANES, tm, LANES), lambda i: (0, i, 0)),
            pl.BlockSpec((S5_WIDTH // LANES, tm, LANES), lambda i: (0, i, 0)),
            pl.BlockSpec((tm, DN_WIDTH), row),
            pl.BlockSpec((tm, DN_WIDTH), row),
            pl.BlockSpec((tm, DN_WIDTH), row),
            pl.BlockSpec((1, S5_WIDTH), const),
            pl.BlockSpec((S5_WIDTH, S5_WIDTH), const),
            pl.BlockSpec((1, S5_WIDTH), const),
            pl.BlockSpec((1, DN_HEAD_DIM), const),
            pl.BlockSpec((D_MODEL, D_MODEL), const),
            pl.BlockSpec((1, D_MODEL), const),
            pl.BlockSpec((D_MODEL, 2 * LANES), const),
            pl.BlockSpec((1, LANES), const),
        ],
        out_specs=[
            pl.BlockSpec((tm, D_MODEL), row),
            pl.BlockSpec((tm * (D_MODEL // LANES), LANES), row),
            pl.BlockSpec((tm, LANES), row),
            pl.BlockSpec((SUBLANES, LANES), const),
        ],
        out_shape=[
            jax.ShapeDtypeStruct((t, D_MODEL), F32),
            jax.ShapeDtypeStruct((t * (D_MODEL // LANES), LANES), F32),
            jax.ShapeDtypeStruct((t, LANES), F32),
            jax.ShapeDtypeStruct((SUBLANES, LANES), F32),
        ],
        scratch_shapes=[pltpu.VMEM((SUBLANES, LANES), F32)],
        compiler_params=_params("arbitrary"),
        name="mixer_post",
    )(x, u, ys, o_f, o_b, z, s5_d.reshape(1, -1).astype(F32), w_glu.astype(BF16),
      b_glu.reshape(1, -1).astype(F32), dn_norm_w.reshape(1, -1).astype(F32), w_out.astype(BF16),
      norm_ffn.reshape(1, -1).astype(F32), wr, br)


ROW_SPLIT = D_MODEL // LANES


def _rows_to_matrix(ref, n_rows, lead=None):
    parts = []
    for s in range(ROW_SPLIT):
        idx = pl.ds(s, n_rows, stride=ROW_SPLIT)
        parts.append(ref[idx, :] if lead is None else ref[lead, idx, :])
    return jnp.concatenate(parts, axis=1)


def _matrix_to_rows(ref, val):
    n_rows = val.shape[0]
    for s in range(ROW_SPLIT):
        ref[pl.ds(s, n_rows, stride=ROW_SPLIT), :] = val[:, s * LANES:(s + 1) * LANES]


def _row(ref, r):
    return ref.at[pl.ds(pl.multiple_of(r * ROW_SPLIT, ROW_SPLIT), ROW_SPLIT), :]


def _expert_kernel(be_ref, run_ref, nxt_ref, used_ref, tok_ref, h_ref, wg_ref, wu_ref, wd_ref, ys_ref,
                   xbuf, wg_buf, wu_buf, wd_buf, wgu_b, wd_b, xsem, wsem, *, layer):
    i = pl.program_id(0)
    blk = ys_ref.shape[0] // ROW_SPLIT
    used = used_ref[0]

    def gather(b, slot):
        def copy(r):
            tok = tok_ref[b * blk + r]
            dst = xbuf.at[slot, pl.ds(pl.multiple_of(r * ROW_SPLIT, ROW_SPLIT), ROW_SPLIT), :]
            return pltpu.make_async_copy(_row(h_ref, tok), dst, xsem.at[slot])
        return copy

    def start_rows(b, slot):
        copy = gather(b, slot)

        def body(r, _):
            copy(2 * r).start(priority=0)
            copy(2 * r + 1).start(priority=1)
            return 0
        lax.fori_loop(0, blk // 2, body, 0, unroll=4)

    def wait_rows(b, slot):
        copy = gather(b, slot)

        def body(r, _):
            copy(r).wait()
            return 0
        lax.fori_loop(0, blk, body, 0, unroll=8)

    def weight_copies(expert, slot):
        return (pltpu.make_async_copy(wg_ref.at[layer, expert], wg_buf.at[slot], wsem.at[slot]),
                pltpu.make_async_copy(wu_ref.at[layer, expert], wu_buf.at[slot], wsem.at[slot]),
                pltpu.make_async_copy(wd_ref.at[layer, expert], wd_buf.at[slot], wsem.at[slot]))

    run = run_ref[i]
    new_run = (i == 0) | (run != run_ref[jnp.maximum(i - 1, 0)])
    wslot = run % 2

    @pl.when(i == 0)
    def _():
        start_rows(0, 0)
        for c in weight_copies(be_ref[0], 0):
            c.start()

    @pl.when(i + 1 < used)
    def _():
        start_rows(i + 1, (i + 1) % 2)

    @pl.when(new_run)
    def _():
        for c in weight_copies(be_ref[i], wslot):
            c.wait()
        wgu_b[:, :D_EXPERT] = wg_buf[wslot].astype(BF16)
        wgu_b[:, D_EXPERT:] = wu_buf[wslot].astype(BF16)
        wd_b[...] = wd_buf[wslot].astype(BF16)

        @pl.when(nxt_ref[i] >= 0)
        def _():
            for c in weight_copies(nxt_ref[i], 1 - wslot):
                c.start()

    @pl.when(i < used)
    def _():
        wait_rows(i, i % 2)
        xb = _rows_to_matrix(xbuf, blk, i % 2).astype(BF16)
        gu = jnp.dot(xb, wgu_b[...], preferred_element_type=F32)
        g, u = gu[:, :D_EXPERT], gu[:, D_EXPERT:]
        hid = (g * jax.nn.sigmoid(g) * u).astype(BF16)
        _matrix_to_rows(ys_ref, jnp.dot(hid, wd_b[...], preferred_element_type=F32))

    @pl.when(i >= used)
    def _():
        ys_ref[...] = jnp.zeros_like(ys_ref)


def _experts(h, slot_tok, block_expert, block_run, next_expert, used, w_gate, w_up, w_down, layer):
    n_slots = slot_tok.shape[0]
    n_blocks = n_slots // MOE_BLOCK
    grid_spec = pltpu.PrefetchScalarGridSpec(
        num_scalar_prefetch=5,
        grid=(n_blocks,),
        in_specs=[pl.BlockSpec(memory_space=pl.ANY)] * 4,
        out_specs=pl.BlockSpec((MOE_BLOCK * ROW_SPLIT, LANES), lambda i, *_: (i, 0)),
        scratch_shapes=[
            pltpu.VMEM((2, MOE_BLOCK * ROW_SPLIT, LANES), F32),
            pltpu.VMEM((2, D_MODEL, D_EXPERT), F32),
            pltpu.VMEM((2, D_MODEL, D_EXPERT), F32),
            pltpu.VMEM((2, D_EXPERT, D_MODEL), F32),
            pltpu.VMEM((D_MODEL, 2 * D_EXPERT), BF16),
            pltpu.VMEM((D_EXPERT, D_MODEL), BF16),
            pltpu.SemaphoreType.DMA((2,)),
            pltpu.SemaphoreType.DMA((2,)),
        ],
    )
    return pl.pallas_call(
        functools.partial(_expert_kernel, layer=layer),
        grid_spec=grid_spec,
        out_shape=jax.ShapeDtypeStruct((n_slots * ROW_SPLIT, LANES), F32),
        compiler_params=_params("arbitrary"),
        name="moe_experts",
    )(block_expert, block_run, next_expert, used, slot_tok, h, w_gate, w_up, w_down)


def _combine_kernel(dest_ref, x1_ref, route_ref, ys_ref, nw_ref, out_ref, buf, sem, *, final_norm):
    i = pl.program_id(0)
    tm = x1_ref.shape[0]

    def copy(r, k):
        slot = dest_ref[2 * (i * tm + r) + k]
        dst = buf.at[k, pl.ds(pl.multiple_of(r * ROW_SPLIT, ROW_SPLIT), ROW_SPLIT), :]
        return pltpu.make_async_copy(_row(ys_ref, slot), dst, sem)

    def start(r, _):
        copy(r, 0).start(priority=0)
        copy(r, 1).start(priority=1)
        return 0

    def wait(r, _):
        copy(r, 0).wait()
        copy(r, 1).wait()
        return 0

    lax.fori_loop(0, tm, start, 0, unroll=8)
    lax.fori_loop(0, tm, wait, 0, unroll=8)
    route = route_ref[...]
    out = (x1_ref[...] + route[:, 2:3] * _rows_to_matrix(buf, tm, 0)
           + route[:, 3:4] * _rows_to_matrix(buf, tm, 1))
    if final_norm:
        out = out * lax.rsqrt(jnp.mean(out * out, axis=-1, keepdims=True) + NORM_EPS) * nw_ref[...]
    out_ref[...] = out


def _combine(x1, route, ys, dest, norm_w, final_norm):
    t = x1.shape[0]
    tm = min(MOE_TOK_TILE, t)
    grid_spec = pltpu.PrefetchScalarGridSpec(
        num_scalar_prefetch=1,
        grid=(t // tm,),
        in_specs=[
            pl.BlockSpec((tm, D_MODEL), lambda i, dest: (i, 0)),
            pl.BlockSpec((tm, LANES), lambda i, dest: (i, 0)),
            pl.BlockSpec(memory_space=pl.ANY),
            pl.BlockSpec((1, D_MODEL), lambda i, dest: (0, 0)),
        ],
        out_specs=pl.BlockSpec((tm, D_MODEL), lambda i, dest: (i, 0)),
        scratch_shapes=[pltpu.VMEM((2, tm * ROW_SPLIT, LANES), F32), pltpu.SemaphoreType.DMA(())],
    )
    return pl.pallas_call(
        functools.partial(_combine_kernel, final_norm=final_norm),
        grid_spec=grid_spec,
        out_shape=jax.ShapeDtypeStruct((t, D_MODEL), F32),
        compiler_params=_params("arbitrary"),
        name="moe_combine",
    )(dest, x1, route, ys, norm_w.reshape(1, D_MODEL).astype(F32))


def _moe(x1, h, route, counts, w_gate, w_up, w_down, layer, norm_w, final_norm):
    t = x1.shape[0]
    n_blocks = -(-(2 * t) // MOE_BLOCK) + N_EXPERTS
    n_slots = n_blocks * MOE_BLOCK
    cnt = counts[0, N_EXPERT_GROUPS:N_EXPERT_GROUPS + N_EXPERTS].astype(jnp.int32)
    padded = ((cnt + MOE_BLOCK - 1) // MOE_BLOCK) * MOE_BLOCK
    pad_end = jnp.cumsum(padded)
    pad_start = pad_end - padded
    expert = route[:, 0:2].astype(jnp.int32)
    rank = route[:, 4:6].astype(jnp.int32)
    experts = jnp.arange(N_EXPERTS, dtype=jnp.int32)
    dest = (jnp.sum(jnp.where(expert[..., None] == experts, pad_start, 0), axis=-1) + rank).reshape(-1)
    token = jnp.repeat(jnp.arange(t, dtype=jnp.int32), 2)
    slot_tok = jnp.zeros((n_slots,), jnp.int32).at[dest].set(token)
    blocks = jnp.arange(n_blocks, dtype=jnp.int32)
    block_expert = jnp.minimum(jnp.sum(pad_end[None, :] <= (blocks * MOE_BLOCK)[:, None], axis=1),
                               N_EXPERTS - 1).astype(jnp.int32)
    starts_run = jnp.concatenate([jnp.ones((1,), bool), block_expert[1:] != block_expert[:-1]])
    block_run = (jnp.cumsum(starts_run) - 1).astype(jnp.int32)
    later_start = starts_run[None, :] & (blocks[None, :] > blocks[:, None])
    next_expert = jnp.where(jnp.any(later_start, axis=1),
                            block_expert[jnp.argmax(later_start, axis=1)], -1).astype(jnp.int32)
    used = (pad_end[-1:] // MOE_BLOCK).astype(jnp.int32)
    ys = _experts(h, slot_tok, block_expert, block_run, next_expert, used, w_gate, w_up, w_down, layer)
    return _combine(x1, route, ys, dest, norm_w, final_norm)


def kernel(x, norm_mix, w_in, s5_lam_re, s5_lam_im, s5_log_dt, s5_b_re, s5_b_im, s5_c_re, s5_c_im,
           s5_d, s5_w_glu, s5_b_glu, gdn_conv_w, gdn_a_log, gdn_dt_bias, gdn_norm_w, w_out, norm_ffn,
           router_w_group, router_b_group, router_w_expert, router_b_expert,
           expert_w_gate, expert_w_up, expert_w_down, norm_final):
    bsz, seq, d = x.shape
    depth = norm_mix.shape[0]
    xt = x.astype(F32).reshape(bsz * seq, d)
    for i in range(depth):
        u, qkv, z, ab = _inproj(xt, norm_mix[i], w_in[i])
        tables = _s5_tables(s5_lam_re[i], s5_lam_im[i], s5_log_dt[i], s5_b_re[i], s5_b_im[i],
                            s5_c_re[i], s5_c_im[i])
        ys = _s5_mix(u, tables)
        q, k, v, gates = _dn_prep(qkv, ab, gdn_conv_w[i], gdn_a_log[i], gdn_dt_bias[i])
        gates_t = (gates[:, :6 * DN_HEADS].reshape(-1, DN_CHUNK, 6 * DN_HEADS).transpose(0, 2, 1))
        o_f, o_b = _dn_mix(q, k, v, gates, gates_t)
        x1, h, route, counts = _post(xt, u, ys, o_f, o_b, z, s5_d[i], s5_w_glu[i], s5_b_glu[i],
                                     gdn_norm_w[i], w_out[i], norm_ffn[i], router_w_group[i],
                                     router_b_group[i], router_w_expert[i], router_b_expert[i])
        xt = _moe(x1, h, route, counts, expert_w_gate, expert_w_up, expert_w_down, i,
                  norm_final, i == depth - 1)
    return xt.reshape(bsz, seq, d)
```

```python
import functools
import math

import jax
import jax.numpy as jnp
import numpy as np
from jax import lax
from jax.experimental import pallas as pl
from jax.experimental.pallas import tpu as pltpu

F32 = jnp.float32
BF16 = jnp.bfloat16
HIGHEST = lax.Precision.HIGHEST

D_MODEL = 1024
S5_WIDTH = 512
S5_GROUP = 16
S5_GROUPS = 32
S5_STATE = 64
S5_MAX_RE = -1e-4
DN_HEADS = 4
DN_HEAD_DIM = 128
DN_WIDTH = 512
DN_CONV = 5
DN_CHUNK = 64
N_EXPERT_GROUPS = 4
EXPERTS_PER_GROUP = 8
N_EXPERTS = 32
D_EXPERT = 512
NORM_EPS = 1e-6

LANES = 128
SUBLANES = 8
VMEM_LIMIT = 56 * 1024 * 1024

S5_CHUNK = 16
S5_TILE = 128
ROW_TILE = 512
DN_STEP_CHUNKS = 8
MOE_BLOCK = 256
MOE_TOK_TILE = 256


def _params(*sem):
    return pltpu.CompilerParams(dimension_semantics=sem, vmem_limit_bytes=VMEM_LIMIT)


def _split_weight(w):
    hi = w.astype(BF16)
    lo = (w - hi.astype(F32)).astype(BF16)
    return jnp.concatenate([hi, lo], axis=1)


def _dot_split(a, w_split):
    a_hi = a.astype(BF16)
    a_lo = (a - a_hi.astype(F32)).astype(BF16)
    p = jnp.dot(a_hi, w_split, preferred_element_type=F32)
    q = jnp.dot(a_lo, w_split[:, :LANES], preferred_element_type=F32)
    return p[:, :LANES] + p[:, LANES:] + q


def _inproj_kernel(x_ref, nw_ref, w_ref, wab_ref, u_ref, qkv_ref, z_ref, ab_ref):
    x = x_ref[...]
    h = x * lax.rsqrt(jnp.mean(x * x, axis=-1, keepdims=True) + NORM_EPS) * nw_ref[...]
    hb = h.astype(BF16)
    for blk in range(S5_WIDTH // LANES):
        u_ref[blk] = jnp.dot(hb, w_ref[:, blk * LANES:(blk + 1) * LANES], preferred_element_type=F32)
    qkv_ref[...] = jnp.dot(hb, w_ref[:, S5_WIDTH:S5_WIDTH + 3 * DN_WIDTH], preferred_element_type=F32)
    z_ref[...] = jnp.dot(hb, w_ref[:, S5_WIDTH + 3 * DN_WIDTH:S5_WIDTH + 4 * DN_WIDTH],
                         preferred_element_type=F32)
    ab_ref[...] = _dot_split(h, wab_ref[...])


def _inproj(x, norm_w, w_in):
    t = x.shape[0]
    n_main = S5_WIDTH + 4 * DN_WIDTH
    w_main = w_in[:, :n_main].astype(BF16)
    w_ab = _split_weight(jnp.pad(w_in[:, n_main:].astype(F32), ((0, 0), (0, LANES - 4 * DN_HEADS))))
    tm = min(ROW_TILE, t)
    row = lambda i: (i, 0)
    const = lambda i: (0, 0)
    return pl.pallas_call(
        _inproj_kernel,
        grid=(t // tm,),
        in_specs=[
            pl.BlockSpec((tm, D_MODEL), row),
            pl.BlockSpec((1, D_MODEL), const),
            pl.BlockSpec((D_MODEL, n_main), const),
            pl.BlockSpec((D_MODEL, 2 * LANES), const),
        ],
        out_specs=[
            pl.BlockSpec((S5_WIDTH // LANES, tm, LANES), lambda i: (0, i, 0)),
            pl.BlockSpec((tm, 3 * DN_WIDTH), row),
            pl.BlockSpec((tm, DN_WIDTH), row),
            pl.BlockSpec((tm, LANES), row),
        ],
        out_shape=[
            jax.ShapeDtypeStruct((S5_WIDTH // LANES, t, LANES), F32),
            jax.ShapeDtypeStruct((t, 3 * DN_WIDTH), F32),
            jax.ShapeDtypeStruct((t, DN_WIDTH), F32),
            jax.ShapeDtypeStruct((t, LANES), F32),
        ],
        compiler_params=_params("parallel"),
        name="inproj",
    )(x, norm_w.reshape(1, D_MODEL), w_main, w_ab)


def _s5_tables(lam_re, lam_im, log_dt, b_re, b_im, c_re, c_im):
    c_len = S5_CHUNK
    lr = jnp.minimum(lam_re.astype(F32), S5_MAX_RE)
    li = lam_im.astype(F32)
    dt = jnp.exp(log_dt.astype(F32))[..., None]
    zr, zi = lr * dt, li * dt
    e1 = jnp.exp(zr)
    ar, ai = e1 * jnp.cos(zi), e1 * jnp.sin(zi)
    den = lr * lr + li * li
    nr, ni = ar - 1.0, ai
    fr = (nr * lr + ni * li) / den
    fi = (ni * lr - nr * li) / den
    bbr = (fr[..., None] * b_re - fi[..., None] * b_im).swapaxes(-1, -2)
    bbi = (fr[..., None] * b_im + fi[..., None] * b_re).swapaxes(-1, -2)
    tau = jnp.arange(c_len + 1, dtype=F32)[:, None]
    mag = jnp.exp(tau * zr[..., None, :])
    pr = mag * jnp.cos(tau * zi[..., None, :])
    pi = mag * jnp.sin(tau * zi[..., None, :])
    prq, piq = pr[..., :, None, :], pi[..., :, None, :]
    m_r = prq * bbr[..., None, :, :] - piq * bbi[..., None, :, :]
    m_i = prq * bbi[..., None, :, :] + piq * bbr[..., None, :, :]
    kern = (jnp.einsum('ldgpn,ldgtqn->ldgqtp', c_re, m_r[..., :c_len, :, :], precision=HIGHEST)
            - jnp.einsum('ldgpn,ldgtqn->ldgqtp', c_im, m_i[..., :c_len, :, :], precision=HIGHEST))

    def per_direction(x, axis, fwd_flipped):
        f, b = x[:, 0], x[:, 1]
        f, b = (jnp.flip(f, axis), b) if fwd_flipped else (f, jnp.flip(b, axis))
        return jnp.stack([f, b], axis=1)

    padded = jnp.pad(kern, ((0, 0),) * 4 + ((c_len - 1, 0), (0, 0)))
    padded = per_direction(padded, 3, False)
    wt = jnp.stack([padded[..., c_len - 1 - j:2 * c_len - 1 - j, :] for j in range(c_len)], axis=3)
    wt = wt.reshape(wt.shape[:3] + (256, 256))
    er = per_direction(m_r[..., :c_len, :, :], 2, True).reshape(wt.shape[:3] + (256, S5_STATE))
    ei = per_direction(m_i[..., :c_len, :, :], 2, True).reshape(wt.shape[:3] + (256, S5_STATE))
    we = jnp.concatenate([er, ei, ei, er], axis=-1)
    c_rt, c_it = c_re.swapaxes(-1, -2)[..., None, :], c_im.swapaxes(-1, -2)[..., None, :]
    p_rt, p_it = pr.swapaxes(-1, -2)[..., 1:, None], pi.swapaxes(-1, -2)[..., 1:, None]
    sr = per_direction(c_rt * p_rt - c_it * p_it, 3, False).reshape(wt.shape[:3] + (S5_STATE, 256))
    si = per_direction(c_rt * p_it + c_it * p_rt, 3, False).reshape(wt.shape[:3] + (S5_STATE, 256))
    ws = jnp.concatenate([sr, -si], axis=3)
    a_r, a_i = pr[..., c_len, :], pi[..., c_len, :]
    coef = jnp.stack([jnp.concatenate([a_r, a_r], -1),
                      jnp.concatenate([-a_i, a_i], -1),
                      jnp.concatenate([a_i, -a_i], -1)], axis=2)
    return wt.astype(BF16), we.astype(BF16), ws.astype(BF16), coef


def _block_transpose8(xs, lane):
    for k in (2, 1, 0):
        shift = S5_GROUP << k
        bit = jnp.bitwise_and(lax.shift_right_logical(lane, 4 + k), 1)
        new = list(xs)
        for a in range(8):
            if (a >> k) & 1:
                continue
            b = a + (1 << k)
            new[a] = jnp.where(bit == 0, xs[a], pltpu.roll(xs[b], shift, 1))
            new[b] = jnp.where(bit == 1, xs[b], pltpu.roll(xs[a], LANES - shift, 1))
        xs = new
    return xs


def _s5_scan_kernel(u_ref, wt_ref, we_ref, ws_ref, coef_ref, *rest, reverse, add_prev):
    if add_prev:
        prev_ref, y_ref, ug_scr, yg_scr, e_scr, es_scr, s_scr, carry = rest
    else:
        y_ref, ug_scr, yg_scr, e_scr, es_scr, s_scr, carry = rest
    n_rows = ug_scr.shape[1]
    n_blk = S5_WIDTH // LANES
    per_blk = LANES // S5_GROUP
    halves = S5_CHUNK // per_blk

    @pl.when(pl.program_id(0) == 0)
    def _():
        carry[...] = jnp.zeros_like(carry)

    lane = lax.broadcasted_iota(jnp.int32, (n_rows, LANES), 1)

    for blk in range(n_blk):
        for half in range(halves):
            xs = [u_ref[blk, pl.ds(half * per_blk + jl, n_rows, stride=S5_CHUNK), :] for jl in range(per_blk)]
            ys = _block_transpose8(xs, lane)
            for gl in range(per_blk):
                ug_scr[blk * per_blk + gl, :, half * LANES:(half + 1) * LANES] = ys[gl].astype(BF16)

    for g in range(S5_GROUPS):
        e = jnp.dot(ug_scr[g], we_ref[g], preferred_element_type=F32)
        e_scr[g * n_rows:(g + 1) * n_rows, :] = e[:, :LANES]
        es_scr[g * n_rows:(g + 1) * n_rows, :] = e[:, LANES:]

    c1, c2, c3 = coef_ref[0], coef_ref[1], coef_ref[2]

    def step(i, vs):
        v, vp = vs
        r = (n_rows - 1 - i) if reverse else i
        rows = pl.ds(r, S5_GROUPS, stride=n_rows)
        s_scr[rows, :] = v
        v_new = v * c1 + vp * c2 + e_scr[rows, :]
        vp_new = vp * c1 + v * c3 + es_scr[rows, :]
        return v_new, vp_new

    v, vp = lax.fori_loop(0, n_rows, step, (carry[0], carry[1]), unroll=8)
    carry[0] = v
    carry[1] = vp

    for blk in range(n_blk):
        for gl in range(per_blk):
            g = blk * per_blk + gl
            s_in = s_scr[g * n_rows:(g + 1) * n_rows, :].astype(BF16)
            yg_scr[gl] = (jnp.dot(ug_scr[g], wt_ref[g], preferred_element_type=F32)
                          + jnp.dot(s_in, ws_ref[g], preferred_element_type=F32))
        for half in range(halves):
            zs = [yg_scr[gl, :, half * LANES:(half + 1) * LANES] for gl in range(per_blk)]
            ws = _block_transpose8(zs, lane)
            for tl in range(per_blk):
                rows = pl.ds(half * per_blk + tl, n_rows, stride=S5_CHUNK)
                out = ws[tl]
                if add_prev:
                    out = out + prev_ref[blk, rows, :]
                y_ref[blk, rows, :] = out


def _s5_direction(u4, tables, prev, reverse, layer):
    n_blk, t, _ = u4.shape
    n_chunks = t // S5_CHUNK
    rows = min(S5_TILE, n_chunks)
    n_tiles = n_chunks // rows
    wt, we, ws, coef = tables
    tile = (lambda i: (0, n_tiles - 1 - i, 0)) if reverse else (lambda i: (0, i, 0))
    table = lambda i: (layer, int(reverse), 0, 0, 0)
    once = pl.Buffered(1)
    in_specs = [
        pl.BlockSpec((n_blk, rows * S5_CHUNK, LANES), tile),
        pl.BlockSpec((None, None, S5_GROUPS, 256, 256), table, pipeline_mode=once),
        pl.BlockSpec((None, None, S5_GROUPS, 256, 256), table, pipeline_mode=once),
        pl.BlockSpec((None, None, S5_GROUPS, LANES, 256), table, pipeline_mode=once),
        pl.BlockSpec((None, None, 3, S5_GROUPS, LANES), table, pipeline_mode=once),
    ]
    args = [u4, wt, we, ws, coef]
    if prev is not None:
        in_specs.append(pl.BlockSpec((n_blk, rows * S5_CHUNK, LANES), tile))
        args.append(prev)
    return pl.pallas_call(
        functools.partial(_s5_scan_kernel, reverse=reverse, add_prev=prev is not None),
        grid=(n_tiles,),
        in_specs=in_specs,
        out_specs=pl.BlockSpec((n_blk, rows * S5_CHUNK, LANES), tile),
        out_shape=jax.ShapeDtypeStruct((n_blk, t, LANES), F32),
        scratch_shapes=[
            pltpu.VMEM((S5_GROUPS, rows, 256), BF16),
            pltpu.VMEM((LANES // S5_GROUP, rows, 256), F32),
            pltpu.VMEM((S5_GROUPS * rows, LANES), F32),
            pltpu.VMEM((S5_GROUPS * rows, LANES), F32),
            pltpu.VMEM((S5_GROUPS * rows, LANES), F32),
            pltpu.VMEM((2, S5_GROUPS, LANES), F32),
        ],
        compiler_params=_params("arbitrary"),
        name="s5_bwd" if reverse else "s5_fwd",
    )(*args)


def _s5_mix(u4, tables, layer):
    y = _s5_direction(u4, tables, None, False, layer)
    return _s5_direction(u4, tables, y, True, layer)


def _dn_prep_kernel(cur_ref, prev_ref, next_ref, ab_ref, cw_ref, gp_ref,
                    q_ref, k_ref, v_ref, gate_ref, ext):
    i = pl.program_id(0)
    tm = cur_ref.shape[0]
    pad = DN_CONV // 2
    ext[0:SUBLANES, :] = jnp.where(i > 0, prev_ref[...], 0.0)
    ext[SUBLANES:SUBLANES + tm, :] = cur_ref[...]
    ext[SUBLANES + tm:, :] = jnp.where(i < pl.num_programs(0) - 1, next_ref[...], 0.0)
    outs = (q_ref, k_ref, v_ref)
    for part in range(3):
        cols = slice(part * DN_WIDTH, (part + 1) * DN_WIDTH)
        acc = ext[pl.ds(SUBLANES - pad, tm), cols] * cw_ref[0:1, cols]
        for tap in range(1, DN_CONV):
            acc = acc + ext[pl.ds(SUBLANES - pad + tap, tm), cols] * cw_ref[tap:tap + 1, cols]
        act = acc * jax.nn.sigmoid(acc)
        if part == 2:
            v_ref[...] = act
            continue
        scale = DN_HEAD_DIM ** -0.5 if part == 0 else 1.0
        for h in range(DN_HEADS):
            hs = slice(h * DN_HEAD_DIM, (h + 1) * DN_HEAD_DIM)
            xh = act[:, hs]
            inv = lax.rsqrt(jnp.sum(xh * xh, axis=-1, keepdims=True) + NORM_EPS)
            outs[part][:, hs] = xh * inv * scale
    ab = ab_ref[...]
    lane = lax.broadcasted_iota(jnp.int32, ab.shape, 1)
    pre = ab + gp_ref[1:2, :]
    softplus = jnp.maximum(pre, 0.0) + jnp.log1p(jnp.exp(-jnp.abs(pre)))
    n_gate = 2 * DN_HEADS
    g = jnp.where(lane < n_gate, gp_ref[0:1, :] * softplus, 0.0)
    hi = g.astype(BF16).astype(F32)
    r1 = g - hi
    mid = r1.astype(BF16).astype(F32)
    lo = (r1 - mid).astype(BF16).astype(F32)
    pieces = (hi + pltpu.roll(mid, n_gate, 1) + pltpu.roll(lo, 2 * n_gate, 1)).astype(BF16)
    ri = lax.broadcasted_iota(jnp.int32, (tm, tm), 0)
    ci = lax.broadcasted_iota(jnp.int32, (tm, tm), 1)
    chunk_bits = DN_CHUNK.bit_length() - 1
    same = lax.shift_right_logical(ri, chunk_bits) == lax.shift_right_logical(ci, chunk_bits)
    tri_lo = jnp.where(same & (ri >= ci), 1.0, 0.0).astype(BF16)
    tri_up = jnp.where(same & (ri <= ci), 1.0, 0.0).astype(BF16)
    pref = jnp.dot(tri_lo, pieces, preferred_element_type=F32)
    suff = jnp.dot(tri_up, pieces, preferred_element_type=F32)
    part = jnp.where(jnp.bitwise_and(lane, n_gate - 1) < DN_HEADS, pref, suff)
    gsum = part + pltpu.roll(part, LANES - n_gate, 1) + pltpu.roll(part, LANES - 2 * n_gate, 1)
    gate_ref[...] = jnp.where(lane < n_gate, g, jnp.where(
        lane < 2 * n_gate, jax.nn.sigmoid(ab), jnp.where(
            lane < 3 * n_gate, pltpu.roll(gsum, 2 * n_gate, 1), 0.0)))


def _dn_prep(qkv, ab, conv_w, a_log, dt_bias):
    t = qkv.shape[0]
    tm = min(ROW_TILE, t)
    nb = tm // SUBLANES
    last = t // SUBLANES - 1
    gp = jnp.zeros((SUBLANES, LANES), F32)
    gp = gp.at[0, :2 * DN_HEADS].set(-jnp.exp(a_log.astype(F32)).reshape(-1))
    gp = gp.at[1, :2 * DN_HEADS].set(dt_bias.astype(F32).reshape(-1))
    cw = jnp.pad(conv_w.astype(F32), ((0, SUBLANES - DN_CONV), (0, 0)))
    row = lambda i: (i, 0)
    const = lambda i: (0, 0)
    return pl.pallas_call(
        _dn_prep_kernel,
        grid=(t // tm,),
        in_specs=[
            pl.BlockSpec((tm, 3 * DN_WIDTH), row),
            pl.BlockSpec((SUBLANES, 3 * DN_WIDTH), lambda i: (jnp.maximum(i * nb - 1, 0), 0)),
            pl.BlockSpec((SUBLANES, 3 * DN_WIDTH), lambda i: (jnp.minimum((i + 1) * nb, last), 0)),
            pl.BlockSpec((tm, LANES), row),
            pl.BlockSpec((SUBLANES, 3 * DN_WIDTH), const),
            pl.BlockSpec((SUBLANES, LANES), const),
        ],
        out_specs=[
            pl.BlockSpec((tm, DN_WIDTH), row),
            pl.BlockSpec((tm, DN_WIDTH), row),
            pl.BlockSpec((tm, DN_WIDTH), row),
            pl.BlockSpec((tm, LANES), row),
        ],
        out_shape=[
            jax.ShapeDtypeStruct((t, DN_WIDTH), F32),
            jax.ShapeDtypeStruct((t, DN_WIDTH), F32),
            jax.ShapeDtypeStruct((t, DN_WIDTH), F32),
            jax.ShapeDtypeStruct((t, LANES), F32),
        ],
        scratch_shapes=[pltpu.VMEM((tm + 2 * SUBLANES, 3 * DN_WIDTH), F32)],
        compiler_params=_params("parallel"),
        name="dn_prep",
    )(qkv, qkv, qkv, ab, cw, gp)


def _bmm(a, b):
    return lax.dot_general(a, b, (((2,), (1,)), ((0,), (0,))), preferred_element_type=F32)


def _dn_kernel(qf, kf, vf, gf, gtf, qb, kb_, vb, gb, gtb, of_ref, ob_ref,
               state, wq_scr, kdt_scr, at_scr, u_scr, gam_scr):
    c_len = DN_CHUNK
    n_ch = gtf.shape[0]
    n_gate = 2 * DN_HEADS
    n_chain = 2 * DN_HEADS

    @pl.when(pl.program_id(0) == 0)
    def _():
        state[...] = jnp.zeros_like(state)

    ri = lax.broadcasted_iota(jnp.int32, (c_len, c_len), 0)
    ci = lax.broadcasted_iota(jnp.int32, (c_len, c_len), 1)
    eye = jnp.where(ri == ci, 1.0, 0.0).astype(F32)
    dirs = ((qf, kf, vf, gf, gtf, ri >= ci, ri > ci, c_len - 1),
            (qb, kb_, vb, gb, gtb, ri <= ci, ri < ci, 0))

    for d, (q_ref, k_ref, v_ref, gate_ref, gate_t_ref, incl, strict, last) in enumerate(dirs):
        gates = gate_ref[...].reshape(n_ch, c_len, LANES)
        gates_t = gate_t_ref[...]
        for h in range(DN_HEADS):
            idx = d * DN_HEADS + h
            hs = slice(h * DN_HEAD_DIM, (h + 1) * DN_HEAD_DIM)
            gcol = gates[:, :, 2 * n_gate + idx:2 * n_gate + idx + 1]
            bcol = gates[:, :, n_gate + idx:n_gate + idx + 1]
            grow = gates_t[:, 2 * n_gate + idx:2 * n_gate + idx + 1, :]
            glast = grow[:, :, last:last + 1]
            qh = q_ref[:, hs].reshape(n_ch, c_len, DN_HEAD_DIM)
            kh = k_ref[:, hs].reshape(n_ch, c_len, DN_HEAD_DIM)
            vh = v_ref[:, hs].reshape(n_ch, c_len, DN_HEAD_DIM)
            kb = kh.astype(BF16)
            qk_kk = lax.dot_general(jnp.concatenate([qh.astype(BF16), kb], axis=1), kb,
                                    (((2,), (2,)), ((0,), (0,))), preferred_element_type=F32)
            qk, kk = qk_kk[:, :c_len], qk_kk[:, c_len:]
            decay = jnp.where(incl, jnp.exp(jnp.where(incl, gcol - grow, 0.0)), 0.0)
            a_mat = jnp.where(strict, bcol * kk * decay, 0.0)
            pw = -a_mat
            inv = eye + pw
            for _ in range(5):
                pwb = pw.astype(BF16)
                pw = _bmm(pwb, pwb)
                inv = inv + _bmm(inv.astype(BF16), pw.astype(BF16))
            egc = jnp.exp(gcol)
            rhs = jnp.concatenate([vh * bcol, kh * (bcol * egc)], axis=2).astype(BF16)
            uw = _bmm(inv.astype(BF16), rhs)
            wq = jnp.concatenate([uw[:, :, DN_HEAD_DIM:], qh * egc], axis=1).astype(BF16)
            attn = jnp.where(incl, qk * decay, 0.0).astype(BF16)
            k_dec_t = jnp.swapaxes(kh * jnp.exp(glast - gcol), 1, 2).astype(BF16)
            gamma = jnp.broadcast_to(jnp.exp(glast), (n_ch, 1, LANES))
            for c in range(n_ch):
                slot = (c if d == 0 else n_ch - 1 - c) * n_chain + idx
                wq_scr[slot] = wq[c]
                kdt_scr[slot] = k_dec_t[c]
                at_scr[slot] = attn[c]
                u_scr[slot] = uw[c, :, :DN_HEAD_DIM]
                gam_scr[slot] = gamma[c]

    for step in range(n_ch):
        grp = slice(step * n_chain, (step + 1) * n_chain)
        s = state[...]
        wq_s = _bmm(wq_scr[grp], s.astype(BF16))
        v_nb = (u_scr[grp] - wq_s[:, :c_len]).astype(BF16)
        o = wq_s[:, c_len:] + _bmm(at_scr[grp], v_nb)
        state[...] = s * gam_scr[grp] + _bmm(kdt_scr[grp], v_nb)
        for d, o_ref in enumerate((of_ref, ob_ref)):
            c = step if d == 0 else n_ch - 1 - step
            for h in range(DN_HEADS):
                o_ref[c * c_len:(c + 1) * c_len, h * DN_HEAD_DIM:(h + 1) * DN_HEAD_DIM] = o[d * DN_HEADS + h]


def _dn_mix(q, k, v, gates, gates_t):
    t = q.shape[0]
    n_chunks = t // DN_CHUNK
    n_ch = min(DN_STEP_CHUNKS, n_chunks)
    n_steps = n_chunks // n_ch
    rows = n_ch * DN_CHUNK
    n_inst = n_ch * 2 * DN_HEADS
    specs = []
    for row, row3 in ((lambda i: (i, 0), lambda i: (i, 0, 0)),
                      (lambda i: (n_steps - 1 - i, 0), lambda i: (n_steps - 1 - i, 0, 0))):
        specs += [
            pl.BlockSpec((rows, DN_WIDTH), row),
            pl.BlockSpec((rows, DN_WIDTH), row),
            pl.BlockSpec((rows, DN_WIDTH), row),
            pl.BlockSpec((rows, LANES), row),
            pl.BlockSpec((n_ch, 6 * DN_HEADS, DN_CHUNK), row3),
        ]
    return pl.pallas_call(
        _dn_kernel,
        grid=(n_steps,),
        in_specs=specs,
        out_specs=[
            pl.BlockSpec((rows, DN_WIDTH), lambda i: (i, 0)),
            pl.BlockSpec((rows, DN_WIDTH), lambda i: (n_steps - 1 - i, 0)),
        ],
        out_shape=[jax.ShapeDtypeStruct((t, DN_WIDTH), F32), jax.ShapeDtypeStruct((t, DN_WIDTH), F32)],
        scratch_shapes=[
            pltpu.VMEM((2 * DN_HEADS, DN_HEAD_DIM, DN_HEAD_DIM), F32),
            pltpu.VMEM((n_inst, 2 * DN_CHUNK, DN_HEAD_DIM), BF16),
            pltpu.VMEM((n_inst, DN_HEAD_DIM, DN_CHUNK), BF16),
            pltpu.VMEM((n_inst, DN_CHUNK, DN_CHUNK), BF16),
            pltpu.VMEM((n_inst, DN_CHUNK, DN_HEAD_DIM), F32),
            pltpu.VMEM((n_inst, 1, LANES), F32),
        ],
        compiler_params=_params("arbitrary"),
        name="dn_mix",
    )(q, k, v, gates, gates_t, q, k, v, gates, gates_t)


def _post_kernel(x_ref, u_ref, ys_ref, of_ref, ob_ref, z_ref, d_ref, wglu_ref, bglu_ref, nw_ref,
                 wout_ref, nffn_ref, wr_ref, br_ref, x1_ref, h_ref, route_ref, cnt_ref, base):
    i = pl.program_id(0)
    tm = x_ref.shape[0]

    @pl.when(i == 0)
    def _():
        base[...] = jnp.zeros_like(base)

    y = jnp.concatenate([ys_ref[b] + d_ref[:, b * LANES:(b + 1) * LANES] * u_ref[b]
                         for b in range(S5_WIDTH // LANES)], axis=1)
    y = 0.5 * y * (1.0 + lax.erf(y * (2.0 ** -0.5)))
    gate = jnp.dot(y.astype(BF16), wglu_ref[...], preferred_element_type=F32) + bglu_ref[...]
    y_s5 = y * jax.nn.sigmoid(gate)
    acc = x_ref[...] + jnp.dot(y_s5.astype(BF16), wout_ref[0:S5_WIDTH, :], preferred_element_type=F32)
    for h in range(DN_HEADS):
        hs = slice(h * DN_HEAD_DIM, (h + 1) * DN_HEAD_DIM)
        o = of_ref[:, hs] + ob_ref[:, hs]
        zh = z_ref[:, hs]
        o = o * lax.rsqrt(jnp.mean(o * o, axis=-1, keepdims=True) + NORM_EPS) * nw_ref[...]
        y_dn = o * (zh * jax.nn.sigmoid(zh))
        acc = acc + jnp.dot(y_dn.astype(BF16),
                            wout_ref[S5_WIDTH + h * DN_HEAD_DIM:S5_WIDTH + (h + 1) * DN_HEAD_DIM, :],
                            preferred_element_type=F32)
    x1_ref[...] = acc
    hn = acc * lax.rsqrt(jnp.mean(acc * acc, axis=-1, keepdims=True) + NORM_EPS) * nffn_ref[...]
    _matrix_to_rows(h_ref, hn)

    logits = _dot_split(hn, wr_ref[...]) + br_ref[...]
    lane_i = lax.broadcasted_iota(jnp.int32, logits.shape, 1)
    lane = lane_i.astype(F32)
    neg = jnp.float32(-jnp.inf)
    big = jnp.float32(LANES)
    gl = jnp.where(lane_i < N_EXPERT_GROUPS, logits, neg)
    gmax = jnp.max(gl, axis=-1, keepdims=True)
    g_sel = jnp.min(jnp.where(gl == gmax, lane, big), axis=-1, keepdims=True)
    p_group = 1.0 / jnp.sum(jnp.exp(gl - gmax), axis=-1, keepdims=True)
    lo = N_EXPERT_GROUPS + g_sel * EXPERTS_PER_GROUP
    el = jnp.where((lane >= lo) & (lane < lo + EXPERTS_PER_GROUP), logits, neg)
    top1 = jnp.max(el, axis=-1, keepdims=True)
    idx1 = jnp.min(jnp.where(el == top1, lane, big), axis=-1, keepdims=True)
    el2 = jnp.where(lane == idx1, neg, el)
    top2 = jnp.max(el2, axis=-1, keepdims=True)
    idx2 = jnp.min(jnp.where(el2 == top2, lane, big), axis=-1, keepdims=True)
    e21 = jnp.exp(top2 - top1)
    w1 = p_group / (1.0 + e21)
    w2 = w1 * e21
    oh1 = jnp.where(lane == idx1, 1.0, 0.0).astype(F32)
    oh2 = jnp.where(lane == idx2, 1.0, 0.0).astype(F32)
    ri = lax.broadcasted_iota(jnp.int32, (tm, tm), 0)
    ci = lax.broadcasted_iota(jnp.int32, (tm, tm), 1)
    before = jnp.where(ri > ci, 1.0, 0.0).astype(BF16)
    ohs = oh1 + oh2
    prior = jnp.dot(before, ohs.astype(BF16), preferred_element_type=F32) + base[0:1, :]
    rank1 = jnp.sum(oh1 * prior, axis=-1, keepdims=True)
    rank2 = jnp.sum(oh2 * prior, axis=-1, keepdims=True)
    base[0:1, :] = base[0:1, :] + jnp.sum(ohs, axis=0, keepdims=True)
    e1 = idx1 - N_EXPERT_GROUPS
    e2 = idx2 - N_EXPERT_GROUPS
    route = jnp.where(lane_i == 0, e1, jnp.where(lane_i == 1, e2, jnp.where(lane_i == 2, w1, jnp.where(
        lane_i == 3, w2, jnp.where(lane_i == 4, rank1, jnp.where(lane_i == 5, rank2, 0.0))))))
    route_ref[...] = route
    cnt_ref[...] = base[...]


def _post(x, u, ys, o_f, o_b, z, s5_d, w_glu, b_glu, dn_norm_w, w_out, norm_ffn, w_rg, b_rg, w_re, b_re):
    t = x.shape[0]
    tm = min(ROW_TILE, t)
    wr = jnp.concatenate([w_rg, w_re.transpose(1, 0, 2).reshape(D_MODEL, N_EXPERTS)], axis=1)
    wr = _split_weight(jnp.pad(wr.astype(F32), ((0, 0), (0, LANES - N_EXPERT_GROUPS - N_EXPERTS))))
    br = jnp.pad(jnp.concatenate([b_rg, b_re.reshape(-1)]).astype(F32),
                 (0, LANES - N_EXPERT_GROUPS - N_EXPERTS)).reshape(1, LANES)
    row = lambda i: (i, 0)
    const = lambda i: (0, 0)
    return pl.pallas_call(
        _post_kernel,
        grid=(t // tm,),
        in_specs=[
            pl.BlockSpec((tm, D_MODEL), row),
            pl.BlockSpec((S5_WIDTH // LANES, tm, LANES), lambda i: (0, i, 0)),
            pl.BlockSpec((S5_WIDTH // LANES, tm, LANES), lambda i: (0, i, 0)),
            pl.BlockSpec((tm, DN_WIDTH), row),
            pl.BlockSpec((tm, DN_WIDTH), row),
            pl.BlockSpec((tm, DN_WIDTH), row),
            pl.BlockSpec((1, S5_WIDTH), const),
            pl.BlockSpec((S5_WIDTH, S5_WIDTH), const),
            pl.BlockSpec((1, S5_WIDTH), const),
            pl.BlockSpec((1, DN_HEAD_DIM), const),
            pl.BlockSpec((D_MODEL, D_MODEL), const),
            pl.BlockSpec((1, D_MODEL), const),
            pl.BlockSpec((D_MODEL, 2 * LANES), const),
            pl.BlockSpec((1, LANES), const),
        ],
        out_specs=[
            pl.BlockSpec((tm, D_MODEL), row),
            pl.BlockSpec((tm * (D_MODEL // LANES), LANES), row),
            pl.BlockSpec((tm, LANES), row),
            pl.BlockSpec((SUBLANES, LANES), const),
        ],
        out_shape=[
            jax.ShapeDtypeStruct((t, D_MODEL), F32),
            jax.ShapeDtypeStruct((t * (D_MODEL // LANES), LANES), F32),
            jax.ShapeDtypeStruct((t, LANES), F32),
            jax.ShapeDtypeStruct((SUBLANES, LANES), F32),
        ],
        scratch_shapes=[pltpu.VMEM((SUBLANES, LANES), F32)],
        compiler_params=_params("arbitrary"),
        name="mixer_post",
    )(x, u, ys, o_f, o_b, z, s5_d.reshape(1, -1).astype(F32), w_glu.astype(BF16),
      b_glu.reshape(1, -1).astype(F32), dn_norm_w.reshape(1, -1).astype(F32), w_out.astype(BF16),
      norm_ffn.reshape(1, -1).astype(F32), wr, br)


ROW_SPLIT = D_MODEL // LANES


def _rows_to_matrix(ref, n_rows, lead=None):
    parts = []
    for s in range(ROW_SPLIT):
        idx = pl.ds(s, n_rows, stride=ROW_SPLIT)
        parts.append(ref[idx, :] if lead is None else ref[lead, idx, :])
    return jnp.concatenate(parts, axis=1)


def _matrix_to_rows(ref, val):
    n_rows = val.shape[0]
    for s in range(ROW_SPLIT):
        ref[pl.ds(s, n_rows, stride=ROW_SPLIT), :] = val[:, s * LANES:(s + 1) * LANES]


def _row(ref, r):
    return ref.at[pl.ds(pl.multiple_of(r * ROW_SPLIT, ROW_SPLIT), ROW_SPLIT), :]


def _dispatch_kernel(dest_ref, h_ref, zeros_ref, xs_ref, sem):
    del zeros_ref
    i = pl.program_id(0)
    tm = h_ref.shape[0] // ROW_SPLIT

    def copy(r, k):
        slot = dest_ref[2 * (i * tm + r) + k]
        return pltpu.make_async_copy(_row(h_ref, r), _row(xs_ref, slot), sem)

    def start(r, _):
        copy(r, 0).start(priority=0)
        copy(r, 1).start(priority=1)
        return 0

    def wait(r, _):
        copy(r, 0).wait()
        copy(r, 1).wait()
        return 0

    lax.fori_loop(0, tm, start, 0, unroll=8)
    lax.fori_loop(0, tm, wait, 0, unroll=8)


def _dispatch(h, dest, n_slots):
    t = h.shape[0] // ROW_SPLIT
    tm = min(MOE_TOK_TILE, t)
    grid_spec = pltpu.PrefetchScalarGridSpec(
        num_scalar_prefetch=1,
        grid=(t // tm,),
        in_specs=[
            pl.BlockSpec((tm * ROW_SPLIT, LANES), lambda i, dest: (i, 0)),
            pl.BlockSpec(memory_space=pl.ANY),
        ],
        out_specs=pl.BlockSpec(memory_space=pl.ANY),
        scratch_shapes=[pltpu.SemaphoreType.DMA(())],
    )
    return pl.pallas_call(
        _dispatch_kernel,
        grid_spec=grid_spec,
        out_shape=jax.ShapeDtypeStruct((n_slots * ROW_SPLIT, LANES), F32),
        input_output_aliases={2: 0},
        compiler_params=_params("arbitrary"),
        name="moe_dispatch",
    )(dest, h, jnp.zeros((n_slots * ROW_SPLIT, LANES), F32))


def _expert_kernel(be_ref, run_ref, nxt_ref, used_ref, xs_ref, wg_ref, wu_ref, wd_ref, ys_ref,
                   wg_buf, wu_buf, wd_buf, wgu_b, wd_b, wsem, *, layer):
    i = pl.program_id(0)
    blk = ys_ref.shape[0] // ROW_SPLIT

    def weight_copies(expert, slot):
        return (pltpu.make_async_copy(wg_ref.at[layer, expert], wg_buf.at[slot], wsem.at[slot]),
                pltpu.make_async_copy(wu_ref.at[layer, expert], wu_buf.at[slot], wsem.at[slot]),
                pltpu.make_async_copy(wd_ref.at[layer, expert], wd_buf.at[slot], wsem.at[slot]))

    run = run_ref[i]
    new_run = (i == 0) | (run != run_ref[jnp.maximum(i - 1, 0)])
    wslot = run % 2

    @pl.when(i == 0)
    def _():
        for c in weight_copies(be_ref[0], 0):
            c.start()

    @pl.when(new_run)
    def _():
        for c in weight_copies(be_ref[i], wslot):
            c.wait()

        @pl.when(nxt_ref[i] >= 0)
        def _():
            for c in weight_copies(nxt_ref[i], 1 - wslot):
                c.start()

        wgu_b[:, :D_EXPERT] = wg_buf[wslot].astype(BF16)
        wgu_b[:, D_EXPERT:] = wu_buf[wslot].astype(BF16)
        wd_b[...] = wd_buf[wslot].astype(BF16)

    @pl.when(i < used_ref[0])
    def _():
        xb = _rows_to_matrix(xs_ref, blk).astype(BF16)
        gu = jnp.dot(xb, wgu_b[...], preferred_element_type=F32)
        g, u = gu[:, :D_EXPERT], gu[:, D_EXPERT:]
        hid = (g * jax.nn.sigmoid(g) * u).astype(BF16)
        _matrix_to_rows(ys_ref, jnp.dot(hid, wd_b[...], preferred_element_type=F32))

    @pl.when(i >= used_ref[0])
    def _():
        ys_ref[...] = jnp.zeros_like(ys_ref)


def _experts(xs, block_expert, block_run, next_expert, used, w_gate, w_up, w_down, layer):
    n_slots = xs.shape[0] // ROW_SPLIT
    n_blocks = n_slots // MOE_BLOCK
    grid_spec = pltpu.PrefetchScalarGridSpec(
        num_scalar_prefetch=4,
        grid=(n_blocks,),
        in_specs=[pl.BlockSpec((MOE_BLOCK * ROW_SPLIT, LANES), lambda i, *_: (i, 0))]
        + [pl.BlockSpec(memory_space=pl.ANY)] * 3,
        out_specs=pl.BlockSpec((MOE_BLOCK * ROW_SPLIT, LANES), lambda i, *_: (i, 0)),
        scratch_shapes=[
            pltpu.VMEM((2, D_MODEL, D_EXPERT), F32),
            pltpu.VMEM((2, D_MODEL, D_EXPERT), F32),
            pltpu.VMEM((2, D_EXPERT, D_MODEL), F32),
            pltpu.VMEM((D_MODEL, 2 * D_EXPERT), BF16),
            pltpu.VMEM((D_EXPERT, D_MODEL), BF16),
            pltpu.SemaphoreType.DMA((2,)),
        ],
    )
    return pl.pallas_call(
        functools.partial(_expert_kernel, layer=layer),
        grid_spec=grid_spec,
        out_shape=jax.ShapeDtypeStruct((n_slots * ROW_SPLIT, LANES), F32),
        compiler_params=_params("arbitrary"),
        name="moe_experts",
    )(block_expert, block_run, next_expert, used, xs, w_gate, w_up, w_down)


def _combine_kernel(dest_ref, x1_ref, route_ref, ys_ref, nw_ref, out_ref, buf, sem, *, final_norm):
    i = pl.program_id(0)
    tm = x1_ref.shape[0]

    def copy(r, k):
        slot = dest_ref[2 * (i * tm + r) + k]
        dst = buf.at[k, pl.ds(pl.multiple_of(r * ROW_SPLIT, ROW_SPLIT), ROW_SPLIT), :]
        return pltpu.make_async_copy(_row(ys_ref, slot), dst, sem)

    def start(r, _):
        copy(r, 0).start(priority=0)
        copy(r, 1).start(priority=1)
        return 0

    def wait(r, _):
        copy(r, 0).wait()
        copy(r, 1).wait()
        return 0

    lax.fori_loop(0, tm, start, 0, unroll=8)
    lax.fori_loop(0, tm, wait, 0, unroll=8)
    route = route_ref[...]
    out = (x1_ref[...] + route[:, 2:3] * _rows_to_matrix(buf, tm, 0)
           + route[:, 3:4] * _rows_to_matrix(buf, tm, 1))
    if final_norm:
        out = out * lax.rsqrt(jnp.mean(out * out, axis=-1, keepdims=True) + NORM_EPS) * nw_ref[...]
    out_ref[...] = out


def _combine(x1, route, ys, dest, norm_w, final_norm):
    t = x1.shape[0]
    tm = min(MOE_TOK_TILE, t)
    grid_spec = pltpu.PrefetchScalarGridSpec(
        num_scalar_prefetch=1,
        grid=(t // tm,),
        in_specs=[
            pl.BlockSpec((tm, D_MODEL), lambda i, dest: (i, 0)),
            pl.BlockSpec((tm, LANES), lambda i, dest: (i, 0)),
            pl.BlockSpec(memory_space=pl.ANY),
            pl.BlockSpec((1, D_MODEL), lambda i, dest: (0, 0)),
        ],
        out_specs=pl.BlockSpec((tm, D_MODEL), lambda i, dest: (i, 0)),
        scratch_shapes=[pltpu.VMEM((2, tm * ROW_SPLIT, LANES), F32), pltpu.SemaphoreType.DMA(())],
    )
    return pl.pallas_call(
        functools.partial(_combine_kernel, final_norm=final_norm),
        grid_spec=grid_spec,
        out_shape=jax.ShapeDtypeStruct((t, D_MODEL), F32),
        compiler_params=_params("arbitrary"),
        name="moe_combine",
    )(dest, x1, route, ys, norm_w.reshape(1, D_MODEL).astype(F32))


def _moe(x1, h, route, counts, w_gate, w_up, w_down, layer, norm_w, final_norm):
    t = x1.shape[0]
    n_blocks = -(-(2 * t) // MOE_BLOCK) + N_EXPERTS
    n_slots = n_blocks * MOE_BLOCK
    cnt = counts[0, N_EXPERT_GROUPS:N_EXPERT_GROUPS + N_EXPERTS].astype(jnp.int32)
    padded = ((cnt + MOE_BLOCK - 1) // MOE_BLOCK) * MOE_BLOCK
    pad_end = jnp.cumsum(padded)
    pad_start = pad_end - padded
    expert = route[:, 0:2].astype(jnp.int32)
    rank = route[:, 4:6].astype(jnp.int32)
    experts = jnp.arange(N_EXPERTS, dtype=jnp.int32)
    dest = (jnp.sum(jnp.where(expert[..., None] == experts, pad_start, 0), axis=-1) + rank).reshape(-1)
    blocks = jnp.arange(n_blocks, dtype=jnp.int32)
    block_expert = jnp.minimum(jnp.sum(pad_end[None, :] <= (blocks * MOE_BLOCK)[:, None], axis=1),
                               N_EXPERTS - 1).astype(jnp.int32)
    starts_run = jnp.concatenate([jnp.ones((1,), bool), block_expert[1:] != block_expert[:-1]])
    block_run = (jnp.cumsum(starts_run) - 1).astype(jnp.int32)
    later_start = starts_run[None, :] & (blocks[None, :] > blocks[:, None])
    next_expert = jnp.where(jnp.any(later_start, axis=1),
                            block_expert[jnp.argmax(later_start, axis=1)], -1).astype(jnp.int32)
    used = (pad_end[-1:] // MOE_BLOCK).astype(jnp.int32)
    xs = _dispatch(h, dest, n_slots)
    ys = _experts(xs, block_expert, block_run, next_expert, used, w_gate, w_up, w_down, layer)
    return _combine(x1, route, ys, dest, norm_w, final_norm)


def kernel(x, norm_mix, w_in, s5_lam_re, s5_lam_im, s5_log_dt, s5_b_re, s5_b_im, s5_c_re, s5_c_im,
           s5_d, s5_w_glu, s5_b_glu, gdn_conv_w, gdn_a_log, gdn_dt_bias, gdn_norm_w, w_out, norm_ffn,
           router_w_group, router_b_group, router_w_expert, router_b_expert,
           expert_w_gate, expert_w_up, expert_w_down, norm_final):
    bsz, seq, d = x.shape
    depth = norm_mix.shape[0]
    xt = x.astype(F32).reshape(bsz * seq, d)
    tables = _s5_tables(s5_lam_re, s5_lam_im, s5_log_dt, s5_b_re, s5_b_im, s5_c_re, s5_c_im)
    for i in range(depth):
        u, qkv, z, ab = _inproj(xt, norm_mix[i], w_in[i])
        ys = _s5_mix(u, tables, i)
        q, k, v, gates = _dn_prep(qkv, ab, gdn_conv_w[i], gdn_a_log[i], gdn_dt_bias[i])
        gates_t = (gates[:, :6 * DN_HEADS].reshape(-1, DN_CHUNK, 6 * DN_HEADS).transpose(0, 2, 1))
        o_f, o_b = _dn_mix(q, k, v, gates, gates_t)
        x1, h, route, counts = _post(xt, u, ys, o_f, o_b, z, s5_d[i], s5_w_glu[i], s5_b_glu[i],
                                     gdn_norm_w[i], w_out[i], norm_ffn[i], router_w_group[i],
                                     router_b_group[i], router_w_expert[i], router_b_expert[i])
        xt = _moe(x1, h, route, counts, expert_w_gate, expert_w_up, expert_w_down, i,
                  norm_final, i == depth - 1)
    return xt.reshape(bsz, seq, d)
```

```python
import functools
import math

import jax
import jax.numpy as jnp
import numpy as np
from jax import lax
from jax.experimental import pallas as pl
from jax.experimental.pallas import tpu as pltpu

F32 = jnp.float32
BF16 = jnp.bfloat16
HIGHEST = lax.Precision.HIGHEST

D_MODEL = 1024
S5_WIDTH = 512
S5_GROUP = 16
S5_GROUPS = 32
S5_STATE = 64
S5_MAX_RE = -1e-4
DN_HEADS = 4
DN_HEAD_DIM = 128
DN_WIDTH = 512
DN_CONV = 5
DN_CHUNK = 64
N_EXPERT_GROUPS = 4
EXPERTS_PER_GROUP = 8
N_EXPERTS = 32
D_EXPERT = 512
NORM_EPS = 1e-6

LANES = 128
SUBLANES = 8
VMEM_LIMIT = 56 * 1024 * 1024

S5_CHUNK = 16
S5_TILE = 128
ROW_TILE = 512
DN_STEP_CHUNKS = 8
MOE_BLOCK = 256
MOE_TOK_TILE = 256


def _params(*sem):
    return pltpu.CompilerParams(dimension_semantics=sem, vmem_limit_bytes=VMEM_LIMIT)


def _split_weight(w):
    hi = w.astype(BF16)
    lo = (w - hi.astype(F32)).astype(BF16)
    return jnp.concatenate([hi, lo], axis=1)


def _dot_split(a, w_split):
    a_hi = a.astype(BF16)
    a_lo = (a - a_hi.astype(F32)).astype(BF16)
    p = jnp.dot(a_hi, w_split, preferred_element_type=F32)
    q = jnp.dot(a_lo, w_split[:, :LANES], preferred_element_type=F32)
    return p[:, :LANES] + p[:, LANES:] + q


def _inproj_kernel(x_ref, xp_ref, xn_ref, nw_ref, w_ref, wab_ref, cw_ref, gp_ref,
                   u_ref, z_ref, q_ref, k_ref, v_ref, gate_ref, ext):
    i = pl.program_id(0)
    tm = x_ref.shape[0]
    pad = DN_CONV // 2

    def norm(x):
        return x * lax.rsqrt(jnp.mean(x * x, axis=-1, keepdims=True) + NORM_EPS) * nw_ref[...]

    h = norm(x_ref[...])
    h_prev = norm(jnp.where(i > 0, xp_ref[...], 0.0))
    h_next = norm(jnp.where(i < pl.num_programs(0) - 1, xn_ref[...], 0.0))
    hb = h.astype(BF16)
    for blk in range(S5_WIDTH // LANES):
        u_ref[blk] = jnp.dot(hb, w_ref[:, blk * LANES:(blk + 1) * LANES], preferred_element_type=F32)
    z_ref[...] = jnp.dot(hb, w_ref[:, S5_WIDTH + 3 * DN_WIDTH:S5_WIDTH + 4 * DN_WIDTH],
                         preferred_element_type=F32)
    h_ext = jnp.concatenate([h_prev, h, h_next], axis=0).astype(BF16)
    ext[...] = jnp.dot(h_ext, w_ref[:, S5_WIDTH:S5_WIDTH + 3 * DN_WIDTH], preferred_element_type=F32)

    outs = (q_ref, k_ref, v_ref)
    for part in range(3):
        cols = slice(part * DN_WIDTH, (part + 1) * DN_WIDTH)
        acc = ext[pl.ds(SUBLANES - pad, tm), cols] * cw_ref[0:1, cols]
        for tap in range(1, DN_CONV):
            acc = acc + ext[pl.ds(SUBLANES - pad + tap, tm), cols] * cw_ref[tap:tap + 1, cols]
        act = acc * jax.nn.sigmoid(acc)
        if part == 2:
            v_ref[...] = act
            continue
        scale = DN_HEAD_DIM ** -0.5 if part == 0 else 1.0
        for hd in range(DN_HEADS):
            hs = slice(hd * DN_HEAD_DIM, (hd + 1) * DN_HEAD_DIM)
            xh = act[:, hs]
            inv = lax.rsqrt(jnp.sum(xh * xh, axis=-1, keepdims=True) + NORM_EPS)
            outs[part][:, hs] = xh * inv * scale

    ab = _dot_split(h, wab_ref[...])
    lane = lax.broadcasted_iota(jnp.int32, ab.shape, 1)
    pre = ab + gp_ref[1:2, :]
    softplus = jnp.maximum(pre, 0.0) + jnp.log1p(jnp.exp(-jnp.abs(pre)))
    n_gate = 2 * DN_HEADS
    g = jnp.where(lane < n_gate, gp_ref[0:1, :] * softplus, 0.0)
    hi = g.astype(BF16).astype(F32)
    r1 = g - hi
    mid = r1.astype(BF16).astype(F32)
    lo = (r1 - mid).astype(BF16).astype(F32)
    pieces = (hi + pltpu.roll(mid, n_gate, 1) + pltpu.roll(lo, 2 * n_gate, 1)).astype(BF16)
    ri = lax.broadcasted_iota(jnp.int32, (tm, tm), 0)
    ci = lax.broadcasted_iota(jnp.int32, (tm, tm), 1)
    chunk_bits = DN_CHUNK.bit_length() - 1
    same = lax.shift_right_logical(ri, chunk_bits) == lax.shift_right_logical(ci, chunk_bits)
    tri_lo = jnp.where(same & (ri >= ci), 1.0, 0.0).astype(BF16)
    tri_up = jnp.where(same & (ri <= ci), 1.0, 0.0).astype(BF16)
    pref = jnp.dot(tri_lo, pieces, preferred_element_type=F32)
    suff = jnp.dot(tri_up, pieces, preferred_element_type=F32)
    part = jnp.where(jnp.bitwise_and(lane, n_gate - 1) < DN_HEADS, pref, suff)
    gsum = part + pltpu.roll(part, LANES - n_gate, 1) + pltpu.roll(part, LANES - 2 * n_gate, 1)
    gate_ref[...] = jnp.where(lane < n_gate, g, jnp.where(
        lane < 2 * n_gate, jax.nn.sigmoid(ab), jnp.where(
            lane < 3 * n_gate, pltpu.roll(gsum, 2 * n_gate, 1), 0.0)))


def _inproj(x, norm_w, w_in, conv_w, a_log, dt_bias):
    t = x.shape[0]
    n_main = S5_WIDTH + 4 * DN_WIDTH
    w_main = w_in[:, :n_main].astype(BF16)
    w_ab = _split_weight(jnp.pad(w_in[:, n_main:].astype(F32), ((0, 0), (0, LANES - 4 * DN_HEADS))))
    gp = jnp.zeros((SUBLANES, LANES), F32)
    gp = gp.at[0, :2 * DN_HEADS].set(-jnp.exp(a_log.astype(F32)).reshape(-1))
    gp = gp.at[1, :2 * DN_HEADS].set(dt_bias.astype(F32).reshape(-1))
    cw = jnp.pad(conv_w.astype(F32), ((0, SUBLANES - DN_CONV), (0, 0)))
    tm = min(ROW_TILE, t)
    nb = tm // SUBLANES
    last = t // SUBLANES - 1
    row = lambda i: (i, 0)
    const = lambda i: (0, 0)
    return pl.pallas_call(
        _inproj_kernel,
        grid=(t // tm,),
        in_specs=[
            pl.BlockSpec((tm, D_MODEL), row),
            pl.BlockSpec((SUBLANES, D_MODEL), lambda i: (jnp.maximum(i * nb - 1, 0), 0)),
            pl.BlockSpec((SUBLANES, D_MODEL), lambda i: (jnp.minimum((i + 1) * nb, last), 0)),
            pl.BlockSpec((1, D_MODEL), const),
            pl.BlockSpec((D_MODEL, n_main), const),
            pl.BlockSpec((D_MODEL, 2 * LANES), const),
            pl.BlockSpec((SUBLANES, 3 * DN_WIDTH), const),
            pl.BlockSpec((SUBLANES, LANES), const),
        ],
        out_specs=[
            pl.BlockSpec((S5_WIDTH // LANES, tm, LANES), lambda i: (0, i, 0)),
            pl.BlockSpec((tm, DN_WIDTH), row),
            pl.BlockSpec((tm, DN_WIDTH), row),
            pl.BlockSpec((tm, DN_WIDTH), row),
            pl.BlockSpec((tm, DN_WIDTH), row),
            pl.BlockSpec((tm, LANES), row),
        ],
        out_shape=[
            jax.ShapeDtypeStruct((S5_WIDTH // LANES, t, LANES), F32),
            jax.ShapeDtypeStruct((t, DN_WIDTH), F32),
            jax.ShapeDtypeStruct((t, DN_WIDTH), F32),
            jax.ShapeDtypeStruct((t, DN_WIDTH), F32),
            jax.ShapeDtypeStruct((t, DN_WIDTH), F32),
            jax.ShapeDtypeStruct((t, LANES), F32),
        ],
        scratch_shapes=[pltpu.VMEM((tm + 2 * SUBLANES, 3 * DN_WIDTH), F32)],
        compiler_params=_params("parallel"),
        name="inproj",
    )(x, x, x, norm_w.reshape(1, D_MODEL), w_main, w_ab, cw, gp)


def _s5_tables(lam_re, lam_im, log_dt, b_re, b_im, c_re, c_im):
    c_len = S5_CHUNK
    lr = jnp.minimum(lam_re.astype(F32), S5_MAX_RE)
    li = lam_im.astype(F32)
    dt = jnp.exp(log_dt.astype(F32))[..., None]
    zr, zi = lr * dt, li * dt
    e1 = jnp.exp(zr)
    ar, ai = e1 * jnp.cos(zi), e1 * jnp.sin(zi)
    den = lr * lr + li * li
    nr, ni = ar - 1.0, ai
    fr = (nr * lr + ni * li) / den
    fi = (ni * lr - nr * li) / den
    bbr = (fr[..., None] * b_re - fi[..., None] * b_im).swapaxes(-1, -2)
    bbi = (fr[..., None] * b_im + fi[..., None] * b_re).swapaxes(-1, -2)
    tau = jnp.arange(c_len + 1, dtype=F32)[:, None]
    mag = jnp.exp(tau * zr[..., None, :])
    pr = mag * jnp.cos(tau * zi[..., None, :])
    pi = mag * jnp.sin(tau * zi[..., None, :])
    prq, piq = pr[..., :, None, :], pi[..., :, None, :]
    m_r = prq * bbr[..., None, :, :] - piq * bbi[..., None, :, :]
    m_i = prq * bbi[..., None, :, :] + piq * bbr[..., None, :, :]
    kern = (jnp.einsum('ldgpn,ldgtqn->ldgqtp', c_re, m_r[..., :c_len, :, :], precision=HIGHEST)
            - jnp.einsum('ldgpn,ldgtqn->ldgqtp', c_im, m_i[..., :c_len, :, :], precision=HIGHEST))

    def per_direction(x, axis, fwd_flipped):
        f, b = x[:, 0], x[:, 1]
        f, b = (jnp.flip(f, axis), b) if fwd_flipped else (f, jnp.flip(b, axis))
        return jnp.stack([f, b], axis=1)

    padded = jnp.pad(kern, ((0, 0),) * 4 + ((c_len - 1, 1), (0, 0)))
    padded = per_direction(padded[..., :2 * c_len - 1, :], 3, False)
    padded = jnp.pad(padded, ((0, 0),) * 4 + ((0, 1), (0, 0)))
    lead = padded.shape[:4]
    skew = jnp.broadcast_to(padded[..., None, :, :], lead + (c_len, 2 * c_len, S5_GROUP))
    skew = skew.reshape(lead + (2 * c_len * c_len, S5_GROUP))[..., :c_len * (2 * c_len - 1), :]
    skew = skew.reshape(lead + (c_len, 2 * c_len - 1, S5_GROUP))[..., c_len - 1:, :]
    wt = skew.transpose(0, 1, 2, 4, 3, 5, 6).reshape(lead[:3] + (256, 256))
    er = per_direction(m_r[..., :c_len, :, :], 2, True).reshape(wt.shape[:3] + (256, S5_STATE))
    ei = per_direction(m_i[..., :c_len, :, :], 2, True).reshape(wt.shape[:3] + (256, S5_STATE))
    we = jnp.concatenate([er, ei, ei, er], axis=-1)
    c_rt, c_it = c_re.swapaxes(-1, -2)[..., None, :], c_im.swapaxes(-1, -2)[..., None, :]
    p_rt, p_it = pr.swapaxes(-1, -2)[..., 1:, None], pi.swapaxes(-1, -2)[..., 1:, None]
    sr = per_direction(c_rt * p_rt - c_it * p_it, 3, False).reshape(wt.shape[:3] + (S5_STATE, 256))
    si = per_direction(c_rt * p_it + c_it * p_rt, 3, False).reshape(wt.shape[:3] + (S5_STATE, 256))
    ws = jnp.concatenate([sr, -si], axis=3)
    a_r, a_i = pr[..., c_len, :], pi[..., c_len, :]
    coef = jnp.stack([jnp.concatenate([a_r, a_r], -1),
                      jnp.concatenate([-a_i, a_i], -1),
                      jnp.concatenate([a_i, -a_i], -1)], axis=2)
    return wt.astype(BF16), we.astype(BF16), ws.astype(BF16), coef


def _block_transpose8(xs, lane):
    for k in (2, 1, 0):
        shift = S5_GROUP << k
        bit = jnp.bitwise_and(lax.shift_right_logical(lane, 4 + k), 1)
        new = list(xs)
        for a in range(8):
            if (a >> k) & 1:
                continue
            b = a + (1 << k)
            new[a] = jnp.where(bit == 0, xs[a], pltpu.roll(xs[b], shift, 1))
            new[b] = jnp.where(bit == 1, xs[b], pltpu.roll(xs[a], LANES - shift, 1))
        xs = new
    return xs


def _s5_scan_kernel(u_ref, wt_ref, we_ref, ws_ref, coef_ref, *rest, reverse, add_prev):
    if add_prev:
        prev_ref, y_ref, ug_scr, yg_scr, e_scr, es_scr, s_scr, carry = rest
    else:
        y_ref, ug_scr, yg_scr, e_scr, es_scr, s_scr, carry = rest
    n_rows = ug_scr.shape[1]
    n_blk = S5_WIDTH // LANES
    per_blk = LANES // S5_GROUP
    halves = S5_CHUNK // per_blk

    @pl.when(pl.program_id(0) == 0)
    def _():
        carry[...] = jnp.zeros_like(carry)

    lane = lax.broadcasted_iota(jnp.int32, (n_rows, LANES), 1)

    for blk in range(n_blk):
        for half in range(halves):
            xs = [u_ref[blk, pl.ds(half * per_blk + jl, n_rows, stride=S5_CHUNK), :] for jl in range(per_blk)]
            ys = _block_transpose8(xs, lane)
            for gl in range(per_blk):
                ug_scr[blk * per_blk + gl, :, half * LANES:(half + 1) * LANES] = ys[gl].astype(BF16)

    for g in range(S5_GROUPS):
        e = jnp.dot(ug_scr[g], we_ref[g], preferred_element_type=F32)
        e_scr[g * n_rows:(g + 1) * n_rows, :] = e[:, :LANES]
        es_scr[g * n_rows:(g + 1) * n_rows, :] = e[:, LANES:]

    c1, c2, c3 = coef_ref[0], coef_ref[1], coef_ref[2]

    def step(i, vs):
        v, vp = vs
        r = (n_rows - 1 - i) if reverse else i
        rows = pl.ds(r, S5_GROUPS, stride=n_rows)
        s_scr[rows, :] = v
        v_new = v * c1 + vp * c2 + e_scr[rows, :]
        vp_new = vp * c1 + v * c3 + es_scr[rows, :]
        return v_new, vp_new

    v, vp = lax.fori_loop(0, n_rows, step, (carry[0], carry[1]), unroll=8)
    carry[0] = v
    carry[1] = vp

    for blk in range(n_blk):
        for gl in range(per_blk):
            g = blk * per_blk + gl
            s_in = s_scr[g * n_rows:(g + 1) * n_rows, :].astype(BF16)
            yg_scr[gl] = (jnp.dot(ug_scr[g], wt_ref[g], preferred_element_type=F32)
                          + jnp.dot(s_in, ws_ref[g], preferred_element_type=F32))
        for half in range(halves):
            zs = [yg_scr[gl, :, half * LANES:(half + 1) * LANES] for gl in range(per_blk)]
            ws = _block_transpose8(zs, lane)
            for tl in range(per_blk):
                rows = pl.ds(half * per_blk + tl, n_rows, stride=S5_CHUNK)
                out = ws[tl]
                if add_prev:
                    out = out + prev_ref[blk, rows, :]
                y_ref[blk, rows, :] = out


def _s5_direction(u4, tables, prev, reverse, layer):
    n_blk, t, _ = u4.shape
    n_chunks = t // S5_CHUNK
    rows = min(S5_TILE, n_chunks)
    n_tiles = n_chunks // rows
    wt, we, ws, coef = tables
    tile = (lambda i: (0, n_tiles - 1 - i, 0)) if reverse else (lambda i: (0, i, 0))
    table = lambda i: (layer, int(reverse), 0, 0, 0)
    once = pl.Buffered(1)
    in_specs = [
        pl.BlockSpec((n_blk, rows * S5_CHUNK, LANES), tile),
        pl.BlockSpec((None, None, S5_GROUPS, 256, 256), table, pipeline_mode=once),
        pl.BlockSpec((None, None, S5_GROUPS, 256, 256), table, pipeline_mode=once),
        pl.BlockSpec((None, None, S5_GROUPS, LANES, 256), table, pipeline_mode=once),
        pl.BlockSpec((None, None, 3, S5_GROUPS, LANES), table, pipeline_mode=once),
    ]
    args = [u4, wt, we, ws, coef]
    if prev is not None:
        in_specs.append(pl.BlockSpec((n_blk, rows * S5_CHUNK, LANES), tile))
        args.append(prev)
    return pl.pallas_call(
        functools.partial(_s5_scan_kernel, reverse=reverse, add_prev=prev is not None),
        grid=(n_tiles,),
        in_specs=in_specs,
        out_specs=pl.BlockSpec((n_blk, rows * S5_CHUNK, LANES), tile),
        out_shape=jax.ShapeDtypeStruct((n_blk, t, LANES), F32),
        scratch_shapes=[
            pltpu.VMEM((S5_GROUPS, rows, 256), BF16),
            pltpu.VMEM((LANES // S5_GROUP, rows, 256), F32),
            pltpu.VMEM((S5_GROUPS * rows, LANES), F32),
            pltpu.VMEM((S5_GROUPS * rows, LANES), F32),
            pltpu.VMEM((S5_GROUPS * rows, LANES), F32),
            pltpu.VMEM((2, S5_GROUPS, LANES), F32),
        ],
        compiler_params=_params("arbitrary"),
        name="s5_bwd" if reverse else "s5_fwd",
    )(*args)


def _s5_mix(u4, tables, layer):
    y = _s5_direction(u4, tables, None, False, layer)
    return _s5_direction(u4, tables, y, True, layer)


def _bmm(a, b):
    return lax.dot_general(a, b, (((2,), (1,)), ((0,), (0,))), preferred_element_type=F32)


def _dn_kernel(qf, kf, vf, gf, gtf, qb, kb_, vb, gb, gtb, of_ref, ob_ref,
               state, wq_scr, kdt_scr, at_scr, u_scr, gam_scr):
    c_len = DN_CHUNK
    n_ch = gtf.shape[0]
    n_gate = 2 * DN_HEADS
    n_chain = 2 * DN_HEADS

    @pl.when(pl.program_id(0) == 0)
    def _():
        state[...] = jnp.zeros_like(state)

    ri = lax.broadcasted_iota(jnp.int32, (c_len, c_len), 0)
    ci = lax.broadcasted_iota(jnp.int32, (c_len, c_len), 1)
    eye = jnp.where(ri == ci, 1.0, 0.0).astype(F32)
    dirs = ((qf, kf, vf, gf, gtf, ri >= ci, ri > ci, c_len - 1),
            (qb, kb_, vb, gb, gtb, ri <= ci, ri < ci, 0))

    for d, (q_ref, k_ref, v_ref, gate_ref, gate_t_ref, incl, strict, last) in enumerate(dirs):
        gates = gate_ref[...].reshape(n_ch, c_len, LANES)
        gates_t = gate_t_ref[...]
        for h in range(DN_HEADS):
            idx = d * DN_HEADS + h
            hs = slice(h * DN_HEAD_DIM, (h + 1) * DN_HEAD_DIM)
            gcol = gates[:, :, 2 * n_gate + idx:2 * n_gate + idx + 1]
            bcol = gates[:, :, n_gate + idx:n_gate + idx + 1]
            grow = gates_t[:, 2 * n_gate + idx:2 * n_gate + idx + 1, :]
            glast = grow[:, :, last:last + 1]
            qh = q_ref[:, hs].reshape(n_ch, c_len, DN_HEAD_DIM)
            kh = k_ref[:, hs].reshape(n_ch, c_len, DN_HEAD_DIM)
            vh = v_ref[:, hs].reshape(n_ch, c_len, DN_HEAD_DIM)
            kb = kh.astype(BF16)
            qk_kk = lax.dot_general(jnp.concatenate([qh.astype(BF16), kb], axis=1), kb,
                                    (((2,), (2,)), ((0,), (0,))), preferred_element_type=F32)
            qk, kk = qk_kk[:, :c_len], qk_kk[:, c_len:]
            decay = jnp.where(incl, jnp.exp(jnp.where(incl, gcol - grow, 0.0)), 0.0)
            a_mat = jnp.where(strict, bcol * kk * decay, 0.0)
            pw = -a_mat
            inv = eye + pw
            for _ in range(5):
                pwb = pw.astype(BF16)
                pw = _bmm(pwb, pwb)
                inv = inv + _bmm(inv.astype(BF16), pw.astype(BF16))
            egc = jnp.exp(gcol)
            rhs = jnp.concatenate([vh * bcol, kh * (bcol * egc)], axis=2).astype(BF16)
            uw = _bmm(inv.astype(BF16), rhs)
            wq = jnp.concatenate([uw[:, :, DN_HEAD_DIM:], qh * egc], axis=1).astype(BF16)
            attn = jnp.where(incl, qk * decay, 0.0).astype(BF16)
            k_dec_t = jnp.swapaxes(kh * jnp.exp(glast - gcol), 1, 2).astype(BF16)
            gamma = jnp.broadcast_to(jnp.exp(glast), (n_ch, 1, LANES))
            for c in range(n_ch):
                slot = (c if d == 0 else n_ch - 1 - c) * n_chain + idx
                wq_scr[slot] = wq[c]
                kdt_scr[slot] = k_dec_t[c]
                at_scr[slot] = attn[c]
                u_scr[slot] = uw[c, :, :DN_HEAD_DIM]
                gam_scr[slot] = gamma[c]

    for step in range(n_ch):
        grp = slice(step * n_chain, (step + 1) * n_chain)
        s = state[...]
        wq_s = _bmm(wq_scr[grp], s.astype(BF16))
        v_nb = (u_scr[grp] - wq_s[:, :c_len]).astype(BF16)
        o = wq_s[:, c_len:] + _bmm(at_scr[grp], v_nb)
        state[...] = s * gam_scr[grp] + _bmm(kdt_scr[grp], v_nb)
        for d, o_ref in enumerate((of_ref, ob_ref)):
            c = step if d == 0 else n_ch - 1 - step
            for h in range(DN_HEADS):
                o_ref[c * c_len:(c + 1) * c_len, h * DN_HEAD_DIM:(h + 1) * DN_HEAD_DIM] = o[d * DN_HEADS + h]


def _dn_mix(q, k, v, gates, gates_t):
    t = q.shape[0]
    n_chunks = t // DN_CHUNK
    n_ch = min(DN_STEP_CHUNKS, n_chunks)
    n_steps = n_chunks // n_ch
    rows = n_ch * DN_CHUNK
    n_inst = n_ch * 2 * DN_HEADS
    specs = []
    for row, row3 in ((lambda i: (i, 0), lambda i: (i, 0, 0)),
                      (lambda i: (n_steps - 1 - i, 0), lambda i: (n_steps - 1 - i, 0, 0))):
        specs += [
            pl.BlockSpec((rows, DN_WIDTH), row),
            pl.BlockSpec((rows, DN_WIDTH), row),
            pl.BlockSpec((rows, DN_WIDTH), row),
            pl.BlockSpec((rows, LANES), row),
            pl.BlockSpec((n_ch, 6 * DN_HEADS, DN_CHUNK), row3),
        ]
    return pl.pallas_call(
        _dn_kernel,
        grid=(n_steps,),
        in_specs=specs,
        out_specs=[
            pl.BlockSpec((rows, DN_WIDTH), lambda i: (i, 0)),
            pl.BlockSpec((rows, DN_WIDTH), lambda i: (n_steps - 1 - i, 0)),
        ],
        out_shape=[jax.ShapeDtypeStruct((t, DN_WIDTH), F32), jax.ShapeDtypeStruct((t, DN_WIDTH), F32)],
        scratch_shapes=[
            pltpu.VMEM((2 * DN_HEADS, DN_HEAD_DIM, DN_HEAD_DIM), F32),
            pltpu.VMEM((n_inst, 2 * DN_CHUNK, DN_HEAD_DIM), BF16),
            pltpu.VMEM((n_inst, DN_HEAD_DIM, DN_CHUNK), BF16),
            pltpu.VMEM((n_inst, DN_CHUNK, DN_CHUNK), BF16),
            pltpu.VMEM((n_inst, DN_CHUNK, DN_HEAD_DIM), F32),
            pltpu.VMEM((n_inst, 1, LANES), F32),
        ],
        compiler_params=_params("arbitrary"),
        name="dn_mix",
    )(q, k, v, gates, gates_t, q, k, v, gates, gates_t)


def _post_kernel(x_ref, u_ref, ys_ref, of_ref, ob_ref, z_ref, d_ref, wglu_ref, bglu_ref, nw_ref,
                 wout_ref, nffn_ref, wr_ref, br_ref, x1_ref, h_ref, route_ref, cnt_ref, base):
    i = pl.program_id(0)
    tm = x_ref.shape[0]

    @pl.when(i == 0)
    def _():
        base[...] = jnp.zeros_like(base)

    y = jnp.concatenate([ys_ref[b] + d_ref[:, b * LANES:(b + 1) * LANES] * u_ref[b]
                         for b in range(S5_WIDTH // LANES)], axis=1)
    y = 0.5 * y * (1.0 + lax.erf(y * (2.0 ** -0.5)))
    gate = jnp.dot(y.astype(BF16), wglu_ref[...], preferred_element_type=F32) + bglu_ref[...]
    y_s5 = y * jax.nn.sigmoid(gate)
    acc = x_ref[...] + jnp.dot(y_s5.astype(BF16), wout_ref[0:S5_WIDTH, :], preferred_element_type=F32)
    for h in range(DN_HEADS):
        hs = slice(h * DN_HEAD_DIM, (h + 1) * DN_HEAD_DIM)
        o = of_ref[:, hs] + ob_ref[:, hs]
        zh = z_ref[:, hs]
        o = o * lax.rsqrt(jnp.mean(o * o, axis=-1, keepdims=True) + NORM_EPS) * nw_ref[...]
        y_dn = o * (zh * jax.nn.sigmoid(zh))
        acc = acc + jnp.dot(y_dn.astype(BF16),
                            wout_ref[S5_WIDTH + h * DN_HEAD_DIM:S5_WIDTH + (h + 1) * DN_HEAD_DIM, :],
                            preferred_element_type=F32)
    x1_ref[...] = acc
    hn = acc * lax.rsqrt(jnp.mean(acc * acc, axis=-1, keepdims=True) + NORM_EPS) * nffn_ref[...]
    _matrix_to_rows(h_ref, hn)

    logits = _dot_split(hn, wr_ref[...]) + br_ref[...]
    lane_i = lax.broadcasted_iota(jnp.int32, logits.shape, 1)
    lane = lane_i.astype(F32)
    neg = jnp.float32(-jnp.inf)
    big = jnp.float32(LANES)
    gl = jnp.where(lane_i < N_EXPERT_GROUPS, logits, neg)
    gmax = jnp.max(gl, axis=-1, keepdims=True)
    g_sel = jnp.min(jnp.where(gl == gmax, lane, big), axis=-1, keepdims=True)
    p_group = 1.0 / jnp.sum(jnp.exp(gl - gmax), axis=-1, keepdims=True)
    lo = N_EXPERT_GROUPS + g_sel * EXPERTS_PER_GROUP
    el = jnp.where((lane >= lo) & (lane < lo + EXPERTS_PER_GROUP), logits, neg)
    top1 = jnp.max(el, axis=-1, keepdims=True)
    idx1 = jnp.min(jnp.where(el == top1, lane, big), axis=-1, keepdims=True)
    el2 = jnp.where(lane == idx1, neg, el)
    top2 = jnp.max(el2, axis=-1, keepdims=True)
    idx2 = jnp.min(jnp.where(el2 == top2, lane, big), axis=-1, keepdims=True)
    e21 = jnp.exp(top2 - top1)
    w1 = p_group / (1.0 + e21)
    w2 = w1 * e21
    oh1 = jnp.where(lane == idx1, 1.0, 0.0).astype(F32)
    oh2 = jnp.where(lane == idx2, 1.0, 0.0).astype(F32)
    ri = lax.broadcasted_iota(jnp.int32, (tm, tm), 0)
    ci = lax.broadcasted_iota(jnp.int32, (tm, tm), 1)
    before = jnp.where(ri > ci, 1.0, 0.0).astype(BF16)
    ohs = oh1 + oh2
    prior = jnp.dot(before, ohs.astype(BF16), preferred_element_type=F32) + base[0:1, :]
    rank1 = jnp.sum(oh1 * prior, axis=-1, keepdims=True)
    rank2 = jnp.sum(oh2 * prior, axis=-1, keepdims=True)
    base[0:1, :] = base[0:1, :] + jnp.sum(ohs, axis=0, keepdims=True)
    e1 = idx1 - N_EXPERT_GROUPS
    e2 = idx2 - N_EXPERT_GROUPS
    route = jnp.where(lane_i == 0, e1, jnp.where(lane_i == 1, e2, jnp.where(lane_i == 2, w1, jnp.where(
        lane_i == 3, w2, jnp.where(lane_i == 4, rank1, jnp.where(lane_i == 5, rank2, 0.0))))))
    route_ref[...] = route
    cnt_ref[...] = base[...]


def _post(x, u, ys, o_f, o_b, z, s5_d, w_glu, b_glu, dn_norm_w, w_out, norm_ffn, w_rg, b_rg, w_re, b_re):
    t = x.shape[0]
    tm = min(ROW_TILE, t)
    wr = jnp.concatenate([w_rg, w_re.transpose(1, 0, 2).reshape(D_MODEL, N_EXPERTS)], axis=1)
    wr = _split_weight(jnp.pad(wr.astype(F32), ((0, 0), (0, LANES - N_EXPERT_GROUPS - N_EXPERTS))))
    br = jnp.pad(jnp.concatenate([b_rg, b_re.reshape(-1)]).astype(F32),
                 (0, LANES - N_EXPERT_GROUPS - N_EXPERTS)).reshape(1, LANES)
    row = lambda i: (i, 0)
    const = lambda i: (0, 0)
    return pl.pallas_call(
        _post_kernel,
        grid=(t // tm,),
        in_specs=[
            pl.BlockSpec((tm, D_MODEL), row),
            pl.BlockSpec((S5_WIDTH // LANES, tm, LANES), lambda i: (0, i, 0)),
            pl.BlockSpec((S5_WIDTH // LANES, tm, LANES), lambda i: (0, i, 0)),
            pl.BlockSpec((tm, DN_WIDTH), row),
            pl.BlockSpec((tm, DN_WIDTH), row),
            pl.BlockSpec((tm, DN_WIDTH), row),
            pl.BlockSpec((1, S5_WIDTH), const),
            pl.BlockSpec((S5_WIDTH, S5_WIDTH), const),
            pl.BlockSpec((1, S5_WIDTH), const),
            pl.BlockSpec((1, DN_HEAD_DIM), const),
            pl.BlockSpec((D_MODEL, D_MODEL), const),
            pl.BlockSpec((1, D_MODEL), const),
            pl.BlockSpec((D_MODEL, 2 * LANES), const),
            pl.BlockSpec((1, LANES), const),
        ],
        out_specs=[
            pl.BlockSpec((tm, D_MODEL), row),
            pl.BlockSpec((tm * (D_MODEL // LANES), LANES), row),
            pl.BlockSpec((tm, LANES), row),
            pl.BlockSpec((SUBLANES, LANES), const),
        ],
        out_shape=[
            jax.ShapeDtypeStruct((t, D_MODEL), F32),
            jax.ShapeDtypeStruct((t * (D_MODEL // LANES), LANES), F32),
            jax.ShapeDtypeStruct((t, LANES), F32),
            jax.ShapeDtypeStruct((SUBLANES, LANES), F32),
        ],
        scratch_shapes=[pltpu.VMEM((SUBLANES, LANES), F32)],
        compiler_params=_params("arbitrary"),
        name="mixer_post",
    )(x, u, ys, o_f, o_b, z, s5_d.reshape(1, -1).astype(F32), w_glu.astype(BF16),
      b_glu.reshape(1, -1).astype(F32), dn_norm_w.reshape(1, -1).astype(F32), w_out.astype(BF16),
      norm_ffn.reshape(1, -1).astype(F32), wr, br)


ROW_SPLIT = D_MODEL // LANES


def _rows_to_matrix(ref, n_rows, lead=None):
    parts = []
    for s in range(ROW_SPLIT):
        idx = pl.ds(s, n_rows, stride=ROW_SPLIT)
        parts.append(ref[idx, :] if lead is None else ref[lead, idx, :])
    return jnp.concatenate(parts, axis=1)


def _matrix_to_rows(ref, val):
    n_rows = val.shape[0]
    for s in range(ROW_SPLIT):
        ref[pl.ds(s, n_rows, stride=ROW_SPLIT), :] = val[:, s * LANES:(s + 1) * LANES]


def _row(ref, r):
    return ref.at[pl.ds(pl.multiple_of(r * ROW_SPLIT, ROW_SPLIT), ROW_SPLIT), :]


def _dispatch_kernel(dest_ref, h_ref, zeros_ref, xs_ref, sem):
    del zeros_ref
    i = pl.program_id(0)
    tm = h_ref.shape[0] // ROW_SPLIT

    def copy(r, k):
        slot = dest_ref[2 * (i * tm + r) + k]
        return pltpu.make_async_copy(_row(h_ref, r), _row(xs_ref, slot), sem)

    def start(r, _):
        copy(r, 0).start(priority=0)
        copy(r, 1).start(priority=1)
        return 0

    def wait(r, _):
        copy(r, 0).wait()
        copy(r, 1).wait()
        return 0

    lax.fori_loop(0, tm, start, 0, unroll=8)
    lax.fori_loop(0, tm, wait, 0, unroll=8)


def _dispatch(h, dest, n_slots):
    t = h.shape[0] // ROW_SPLIT
    tm = min(MOE_TOK_TILE, t)
    grid_spec = pltpu.PrefetchScalarGridSpec(
        num_scalar_prefetch=1,
        grid=(t // tm,),
        in_specs=[
            pl.BlockSpec((tm * ROW_SPLIT, LANES), lambda i, dest: (i, 0)),
            pl.BlockSpec(memory_space=pl.ANY),
        ],
        out_specs=pl.BlockSpec(memory_space=pl.ANY),
        scratch_shapes=[pltpu.SemaphoreType.DMA(())],
    )
    return pl.pallas_call(
        _dispatch_kernel,
        grid_spec=grid_spec,
        out_shape=jax.ShapeDtypeStruct((n_slots * ROW_SPLIT, LANES), F32),
        input_output_aliases={2: 0},
        compiler_params=_params("arbitrary"),
        name="moe_dispatch",
    )(dest, h, jnp.zeros((n_slots * ROW_SPLIT, LANES), F32))


def _expert_kernel(be_ref, run_ref, nxt_ref, used_ref, xs_ref, wg_ref, wu_ref, wd_ref, ys_ref,
                   wg_buf, wu_buf, wd_buf, wgu_b, wd_b, wsem, *, layer):
    i = pl.program_id(0)
    blk = ys_ref.shape[0] // ROW_SPLIT

    def weight_copies(expert, slot):
        return (pltpu.make_async_copy(wg_ref.at[layer, expert], wg_buf.at[slot], wsem.at[slot]),
                pltpu.make_async_copy(wu_ref.at[layer, expert], wu_buf.at[slot], wsem.at[slot]),
                pltpu.make_async_copy(wd_ref.at[layer, expert], wd_buf.at[slot], wsem.at[slot]))

    run = run_ref[i]
    new_run = (i == 0) | (run != run_ref[jnp.maximum(i - 1, 0)])
    wslot = run % 2

    @pl.when(i == 0)
    def _():
        for c in weight_copies(be_ref[0], 0):
            c.start()

    @pl.when(new_run)
    def _():
        for c in weight_copies(be_ref[i], wslot):
            c.wait()

        @pl.when(nxt_ref[i] >= 0)
        def _():
            for c in weight_copies(nxt_ref[i], 1 - wslot):
                c.start()

        wgu_b[:, :D_EXPERT] = wg_buf[wslot].astype(BF16)
        wgu_b[:, D_EXPERT:] = wu_buf[wslot].astype(BF16)
        wd_b[...] = wd_buf[wslot].astype(BF16)

    @pl.when(i < used_ref[0])
    def _():
        xb = _rows_to_matrix(xs_ref, blk).astype(BF16)
        gu = jnp.dot(xb, wgu_b[...], preferred_element_type=F32)
        g, u = gu[:, :D_EXPERT], gu[:, D_EXPERT:]
        hid = (g * jax.nn.sigmoid(g) * u).astype(BF16)
        _matrix_to_rows(ys_ref, jnp.dot(hid, wd_b[...], preferred_element_type=F32))

    @pl.when(i >= used_ref[0])
    def _():
        ys_ref[...] = jnp.zeros_like(ys_ref)


def _experts(xs, block_expert, block_run, next_expert, used, w_gate, w_up, w_down, layer):
    n_slots = xs.shape[0] // ROW_SPLIT
    n_blocks = n_slots // MOE_BLOCK
    grid_spec = pltpu.PrefetchScalarGridSpec(
        num_scalar_prefetch=4,
        grid=(n_blocks,),
        in_specs=[pl.BlockSpec((MOE_BLOCK * ROW_SPLIT, LANES), lambda i, *_: (i, 0))]
        + [pl.BlockSpec(memory_space=pl.ANY)] * 3,
        out_specs=pl.BlockSpec((MOE_BLOCK * ROW_SPLIT, LANES), lambda i, *_: (i, 0)),
        scratch_shapes=[
            pltpu.VMEM((2, D_MODEL, D_EXPERT), F32),
            pltpu.VMEM((2, D_MODEL, D_EXPERT), F32),
            pltpu.VMEM((2, D_EXPERT, D_MODEL), F32),
            pltpu.VMEM((D_MODEL, 2 * D_EXPERT), BF16),
            pltpu.VMEM((D_EXPERT, D_MODEL), BF16),
            pltpu.SemaphoreType.DMA((2,)),
        ],
    )
    return pl.pallas_call(
        functools.partial(_expert_kernel, layer=layer),
        grid_spec=grid_spec,
        out_shape=jax.ShapeDtypeStruct((n_slots * ROW_SPLIT, LANES), F32),
        compiler_params=_params("arbitrary"),
        name="moe_experts",
    )(block_expert, block_run, next_expert, used, xs, w_gate, w_up, w_down)


def _combine_kernel(dest_ref, x1_ref, route_ref, ys_ref, nw_ref, out_ref, buf, sem, *, final_norm):
    i = pl.program_id(0)
    tm = x1_ref.shape[0]

    def copy(r, k):
        slot = dest_ref[2 * (i * tm + r) + k]
        dst = buf.at[k, pl.ds(pl.multiple_of(r * ROW_SPLIT, ROW_SPLIT), ROW_SPLIT), :]
        return pltpu.make_async_copy(_row(ys_ref, slot), dst, sem)

    def start(r, _):
        copy(r, 0).start(priority=0)
        copy(r, 1).start(priority=1)
        return 0

    def wait(r, _):
        copy(r, 0).wait()
        copy(r, 1).wait()
        return 0

    lax.fori_loop(0, tm, start, 0, unroll=8)
    lax.fori_loop(0, tm, wait, 0, unroll=8)
    route = route_ref[...]
    out = (x1_ref[...] + route[:, 2:3] * _rows_to_matrix(buf, tm, 0)
           + route[:, 3:4] * _rows_to_matrix(buf, tm, 1))
    if final_norm:
        out = out * lax.rsqrt(jnp.mean(out * out, axis=-1, keepdims=True) + NORM_EPS) * nw_ref[...]
    out_ref[...] = out


def _combine(x1, route, ys, dest, norm_w, final_norm):
    t = x1.shape[0]
    tm = min(MOE_TOK_TILE, t)
    grid_spec = pltpu.PrefetchScalarGridSpec(
        num_scalar_prefetch=1,
        grid=(t // tm,),
        in_specs=[
            pl.BlockSpec((tm, D_MODEL), lambda i, dest: (i, 0)),
            pl.BlockSpec((tm, LANES), lambda i, dest: (i, 0)),
            pl.BlockSpec(memory_space=pl.ANY),
            pl.BlockSpec((1, D_MODEL), lambda i, dest: (0, 0)),
        ],
        out_specs=pl.BlockSpec((tm, D_MODEL), lambda i, dest: (i, 0)),
        scratch_shapes=[pltpu.VMEM((2, tm * ROW_SPLIT, LANES), F32), pltpu.SemaphoreType.DMA(())],
    )
    return pl.pallas_call(
        functools.partial(_combine_kernel, final_norm=final_norm),
        grid_spec=grid_spec,
        out_shape=jax.ShapeDtypeStruct((t, D_MODEL), F32),
        compiler_params=_params("arbitrary"),
        name="moe_combine",
    )(dest, x1, route, ys, norm_w.reshape(1, D_MODEL).astype(F32))


def _moe(x1, h, route, counts, w_gate, w_up, w_down, layer, norm_w, final_norm):
    t = x1.shape[0]
    n_blocks = -(-(2 * t) // MOE_BLOCK) + N_EXPERTS
    n_slots = n_blocks * MOE_BLOCK
    cnt = counts[0, N_EXPERT_GROUPS:N_EXPERT_GROUPS + N_EXPERTS].astype(jnp.int32)
    padded = ((cnt + MOE_BLOCK - 1) // MOE_BLOCK) * MOE_BLOCK
    pad_end = jnp.cumsum(padded)
    pad_start = pad_end - padded
    expert = route[:, 0:2].astype(jnp.int32)
    rank = route[:, 4:6].astype(jnp.int32)
    experts = jnp.arange(N_EXPERTS, dtype=jnp.int32)
    dest = (jnp.sum(jnp.where(expert[..., None] == experts, pad_start, 0), axis=-1) + rank).reshape(-1)
    blocks = jnp.arange(n_blocks, dtype=jnp.int32)
    block_expert = jnp.minimum(jnp.sum(pad_end[None, :] <= (blocks * MOE_BLOCK)[:, None], axis=1),
                               N_EXPERTS - 1).astype(jnp.int32)
    starts_run = jnp.concatenate([jnp.ones((1,), bool), block_expert[1:] != block_expert[:-1]])
    block_run = (jnp.cumsum(starts_run) - 1).astype(jnp.int32)
    later_start = starts_run[None, :] & (blocks[None, :] > blocks[:, None])
    next_expert = jnp.where(jnp.any(later_start, axis=1),
                            block_expert[jnp.argmax(later_start, axis=1)], -1).astype(jnp.int32)
    used = (pad_end[-1:] // MOE_BLOCK).astype(jnp.int32)
    xs = _dispatch(h, dest, n_slots)
    ys = _experts(xs, block_expert, block_run, next_expert, used, w_gate, w_up, w_down, layer)
    return _combine(x1, route, ys, dest, norm_w, final_norm)


def kernel(x, norm_mix, w_in, s5_lam_re, s5_lam_im, s5_log_dt, s5_b_re, s5_b_im, s5_c_re, s5_c_im,
           s5_d, s5_w_glu, s5_b_glu, gdn_conv_w, gdn_a_log, gdn_dt_bias, gdn_norm_w, w_out, norm_ffn,
           router_w_group, router_b_group, router_w_expert, router_b_expert,
           expert_w_gate, expert_w_up, expert_w_down, norm_final):
    bsz, seq, d = x.shape
    depth = norm_mix.shape[0]
    xt = x.astype(F32).reshape(bsz * seq, d)
    tables = _s5_tables(s5_lam_re, s5_lam_im, s5_log_dt, s5_b_re, s5_b_im, s5_c_re, s5_c_im)
    for i in range(depth):
        u, z, q, k, v, gates = _inproj(xt, norm_mix[i], w_in[i], gdn_conv_w[i], gdn_a_log[i],
                                       gdn_dt_bias[i])
        ys = _s5_mix(u, tables, i)
        gates_t = (gates[:, :6 * DN_HEADS].reshape(-1, DN_CHUNK, 6 * DN_HEADS).transpose(0, 2, 1))
        o_f, o_b = _dn_mix(q, k, v, gates, gates_t)
        x1, h, route, counts = _post(xt, u, ys, o_f, o_b, z, s5_d[i], s5_w_glu[i], s5_b_glu[i],
                                     gdn_norm_w[i], w_out[i], norm_ffn[i], router_w_group[i],
                                     router_b_group[i], router_w_expert[i], router_b_expert[i])
        xt = _moe(x1, h, route, counts, expert_w_gate, expert_w_up, expert_w_down, i,
                  norm_final, i == depth - 1)
    return xt.reshape(bsz, seq, d)
```

```python
import functools
import math

import jax
import jax.numpy as jnp
import numpy as np
from jax import lax
from jax.experimental import pallas as pl
from jax.experimental.pallas import tpu as pltpu

F32 = jnp.float32
BF16 = jnp.bfloat16
HIGHEST = lax.Precision.HIGHEST

D_MODEL = 1024
S5_WIDTH = 512
S5_GROUP = 16
S5_GROUPS = 32
S5_STATE = 64
S5_MAX_RE = -1e-4
DN_HEADS = 4
DN_HEAD_DIM = 128
DN_WIDTH = 512
DN_CONV = 5
DN_CHUNK = 64
N_EXPERT_GROUPS = 4
EXPERTS_PER_GROUP = 8
N_EXPERTS = 32
D_EXPERT = 512
NORM_EPS = 1e-6

LANES = 128
SUBLANES = 8
VMEM_LIMIT = 56 * 1024 * 1024

S5_CHUNK = 16
S5_TILE = 128
ROW_TILE = 512
DN_STEP_CHUNKS = 8
MOE_BLOCK = 256
MOE_TOK_TILE = 256


def _params(*sem):
    return pltpu.CompilerParams(dimension_semantics=sem, vmem_limit_bytes=VMEM_LIMIT)


def _split_weight(w):
    hi = w.astype(BF16)
    lo = (w - hi.astype(F32)).astype(BF16)
    return jnp.concatenate([hi, lo], axis=1)


def _dot_split(a, w_split):
    a_hi = a.astype(BF16)
    a_lo = (a - a_hi.astype(F32)).astype(BF16)
    p = jnp.dot(a_hi, w_split, preferred_element_type=F32)
    q = jnp.dot(a_lo, w_split[:, :LANES], preferred_element_type=F32)
    return p[:, :LANES] + p[:, LANES:] + q


def _inproj_kernel(x_ref, xp_ref, xn_ref, nw_ref, w_ref, wab_ref, cw_ref, gp_ref,
                   u_ref, z_ref, q_ref, k_ref, v_ref, gate_ref, ext):
    i = pl.program_id(0)
    tm = x_ref.shape[0]
    pad = DN_CONV // 2

    def norm(x):
        return x * lax.rsqrt(jnp.mean(x * x, axis=-1, keepdims=True) + NORM_EPS) * nw_ref[...]

    h = norm(x_ref[...])
    h_prev = norm(jnp.where(i > 0, xp_ref[...], 0.0))
    h_next = norm(jnp.where(i < pl.num_programs(0) - 1, xn_ref[...], 0.0))
    hb = h.astype(BF16)
    for blk in range(S5_WIDTH // LANES):
        u_ref[blk] = jnp.dot(hb, w_ref[:, blk * LANES:(blk + 1) * LANES], preferred_element_type=F32)
    z_ref[...] = jnp.dot(hb, w_ref[:, S5_WIDTH + 3 * DN_WIDTH:S5_WIDTH + 4 * DN_WIDTH],
                         preferred_element_type=F32)
    h_ext = jnp.concatenate([h_prev, h, h_next], axis=0).astype(BF16)
    ext[...] = jnp.dot(h_ext, w_ref[:, S5_WIDTH:S5_WIDTH + 3 * DN_WIDTH], preferred_element_type=F32)

    outs = (q_ref, k_ref, v_ref)
    for part in range(3):
        cols = slice(part * DN_WIDTH, (part + 1) * DN_WIDTH)
        acc = ext[pl.ds(SUBLANES - pad, tm), cols] * cw_ref[0:1, cols]
        for tap in range(1, DN_CONV):
            acc = acc + ext[pl.ds(SUBLANES - pad + tap, tm), cols] * cw_ref[tap:tap + 1, cols]
        act = acc * jax.nn.sigmoid(acc)
        if part == 2:
            v_ref[...] = act
            continue
        scale = DN_HEAD_DIM ** -0.5 if part == 0 else 1.0
        for hd in range(DN_HEADS):
            hs = slice(hd * DN_HEAD_DIM, (hd + 1) * DN_HEAD_DIM)
            xh = act[:, hs]
            inv = lax.rsqrt(jnp.sum(xh * xh, axis=-1, keepdims=True) + NORM_EPS)
            outs[part][:, hs] = xh * inv * scale

    ab = _dot_split(h, wab_ref[...])
    lane = lax.broadcasted_iota(jnp.int32, ab.shape, 1)
    pre = ab + gp_ref[1:2, :]
    softplus = jnp.maximum(pre, 0.0) + jnp.log1p(jnp.exp(-jnp.abs(pre)))
    n_gate = 2 * DN_HEADS
    g = jnp.where(lane < n_gate, gp_ref[0:1, :] * softplus, 0.0)
    hi = g.astype(BF16).astype(F32)
    r1 = g - hi
    mid = r1.astype(BF16).astype(F32)
    lo = (r1 - mid).astype(BF16).astype(F32)
    pieces = (hi + pltpu.roll(mid, n_gate, 1) + pltpu.roll(lo, 2 * n_gate, 1)).astype(BF16)
    ri = lax.broadcasted_iota(jnp.int32, (tm, tm), 0)
    ci = lax.broadcasted_iota(jnp.int32, (tm, tm), 1)
    chunk_bits = DN_CHUNK.bit_length() - 1
    same = lax.shift_right_logical(ri, chunk_bits) == lax.shift_right_logical(ci, chunk_bits)
    tri_lo = jnp.where(same & (ri >= ci), 1.0, 0.0).astype(BF16)
    tri_up = jnp.where(same & (ri <= ci), 1.0, 0.0).astype(BF16)
    pref = jnp.dot(tri_lo, pieces, preferred_element_type=F32)
    suff = jnp.dot(tri_up, pieces, preferred_element_type=F32)
    part = jnp.where(jnp.bitwise_and(lane, n_gate - 1) < DN_HEADS, pref, suff)
    gsum = part + pltpu.roll(part, LANES - n_gate, 1) + pltpu.roll(part, LANES - 2 * n_gate, 1)
    gate_ref[...] = jnp.where(lane < n_gate, g, jnp.where(
        lane < 2 * n_gate, jax.nn.sigmoid(ab), jnp.where(
            lane < 3 * n_gate, pltpu.roll(gsum, 2 * n_gate, 1), 0.0)))


def _inproj(x, norm_w, w_in, conv_w, a_log, dt_bias):
    t = x.shape[0]
    n_main = S5_WIDTH + 4 * DN_WIDTH
    w_main = w_in[:, :n_main].astype(BF16)
    w_ab = _split_weight(jnp.pad(w_in[:, n_main:].astype(F32), ((0, 0), (0, LANES - 4 * DN_HEADS))))
    gp = jnp.zeros((SUBLANES, LANES), F32)
    gp = gp.at[0, :2 * DN_HEADS].set(-jnp.exp(a_log.astype(F32)).reshape(-1))
    gp = gp.at[1, :2 * DN_HEADS].set(dt_bias.astype(F32).reshape(-1))
    cw = jnp.pad(conv_w.astype(F32), ((0, SUBLANES - DN_CONV), (0, 0)))
    tm = min(ROW_TILE, t)
    nb = tm // SUBLANES
    last = t // SUBLANES - 1
    row = lambda i: (i, 0)
    const = lambda i: (0, 0)
    return pl.pallas_call(
        _inproj_kernel,
        grid=(t // tm,),
        in_specs=[
            pl.BlockSpec((tm, D_MODEL), row),
            pl.BlockSpec((SUBLANES, D_MODEL), lambda i: (jnp.maximum(i * nb - 1, 0), 0)),
            pl.BlockSpec((SUBLANES, D_MODEL), lambda i: (jnp.minimum((i + 1) * nb, last), 0)),
            pl.BlockSpec((1, D_MODEL), const),
            pl.BlockSpec((D_MODEL, n_main), const),
            pl.BlockSpec((D_MODEL, 2 * LANES), const),
            pl.BlockSpec((SUBLANES, 3 * DN_WIDTH), const),
            pl.BlockSpec((SUBLANES, LANES), const),
        ],
        out_specs=[
            pl.BlockSpec((S5_WIDTH // LANES, tm, LANES), lambda i: (0, i, 0)),
            pl.BlockSpec((tm, DN_WIDTH), row),
            pl.BlockSpec((tm, DN_WIDTH), row),
            pl.BlockSpec((tm, DN_WIDTH), row),
            pl.BlockSpec((tm, DN_WIDTH), row),
            pl.BlockSpec((tm, LANES), row),
        ],
        out_shape=[
            jax.ShapeDtypeStruct((S5_WIDTH // LANES, t, LANES), F32),
            jax.ShapeDtypeStruct((t, DN_WIDTH), F32),
            jax.ShapeDtypeStruct((t, DN_WIDTH), F32),
            jax.ShapeDtypeStruct((t, DN_WIDTH), F32),
            jax.ShapeDtypeStruct((t, DN_WIDTH), F32),
            jax.ShapeDtypeStruct((t, LANES), F32),
        ],
        scratch_shapes=[pltpu.VMEM((tm + 2 * SUBLANES, 3 * DN_WIDTH), F32)],
        compiler_params=_params("parallel"),
        name="inproj",
    )(x, x, x, norm_w.reshape(1, D_MODEL), w_main, w_ab, cw, gp)


def _toeplitz_kernel(k_ref, o_ref):
    c_len = S5_CHUNK
    lane = lax.broadcasted_iota(jnp.int32, (S5_GROUP, LANES), 1)
    for b in range(k_ref.shape[0]):
        tiles = [k_ref[b, :, t * LANES:(t + 1) * LANES] for t in range(k_ref.shape[2] // LANES)]
        for j in range(c_len):
            first, shift = divmod((c_len - 1 - j) * S5_GROUP, LANES)
            for half in range(2):
                piece = tiles[first + half]
                if shift:
                    piece = jnp.where(lane < LANES - shift,
                                      pltpu.roll(piece, LANES - shift, 1),
                                      pltpu.roll(tiles[first + half + 1], LANES - shift, 1))
                o_ref[b, j * S5_GROUP:(j + 1) * S5_GROUP, half * LANES:(half + 1) * LANES] = piece.astype(BF16)


def _toeplitz(kpad):
    n, q, width = kpad.shape
    per_step = 8
    return pl.pallas_call(
        _toeplitz_kernel,
        grid=(n // per_step,),
        in_specs=[pl.BlockSpec((per_step, q, width), lambda i: (i, 0, 0))],
        out_specs=pl.BlockSpec((per_step, S5_CHUNK * q, 256), lambda i: (i, 0, 0)),
        out_shape=jax.ShapeDtypeStruct((n, S5_CHUNK * q, 256), BF16),
        compiler_params=_params("parallel"),
        name="s5_toeplitz",
    )(kpad)


def _s5_tables(lam_re, lam_im, log_dt, b_re, b_im, c_re, c_im):
    c_len = S5_CHUNK
    lr = jnp.minimum(lam_re.astype(F32), S5_MAX_RE)
    li = lam_im.astype(F32)
    dt = jnp.exp(log_dt.astype(F32))[..., None]
    zr, zi = lr * dt, li * dt
    e1 = jnp.exp(zr)
    ar, ai = e1 * jnp.cos(zi), e1 * jnp.sin(zi)
    den = lr * lr + li * li
    nr, ni = ar - 1.0, ai
    fr = (nr * lr + ni * li) / den
    fi = (ni * lr - nr * li) / den
    bbr = (fr[..., None] * b_re - fi[..., None] * b_im).swapaxes(-1, -2)
    bbi = (fr[..., None] * b_im + fi[..., None] * b_re).swapaxes(-1, -2)
    tau = jnp.arange(c_len + 1, dtype=F32)[:, None]
    mag = jnp.exp(tau * zr[..., None, :])
    pr = mag * jnp.cos(tau * zi[..., None, :])
    pi = mag * jnp.sin(tau * zi[..., None, :])
    prq, piq = pr[..., :, None, :], pi[..., :, None, :]
    m_r = prq * bbr[..., None, :, :] - piq * bbi[..., None, :, :]
    m_i = prq * bbi[..., None, :, :] + piq * bbr[..., None, :, :]
    kern = (jnp.einsum('ldgpn,ldgtqn->ldgqtp', c_re, m_r[..., :c_len, :, :], precision=HIGHEST)
            - jnp.einsum('ldgpn,ldgtqn->ldgqtp', c_im, m_i[..., :c_len, :, :], precision=HIGHEST))

    def per_direction(x, axis, fwd_flipped):
        f, b = x[:, 0], x[:, 1]
        f, b = (jnp.flip(f, axis), b) if fwd_flipped else (f, jnp.flip(b, axis))
        return jnp.stack([f, b], axis=1)

    padded = jnp.pad(kern, ((0, 0),) * 4 + ((c_len - 1, 0), (0, 0)))
    padded = per_direction(padded, 3, False)
    padded = jnp.pad(padded, ((0, 0),) * 4 + ((0, 1), (0, 0)))
    lead = padded.shape[:3]
    wt = _toeplitz(padded.reshape(-1, S5_GROUP, 2 * c_len * S5_GROUP)).reshape(lead + (256, 256))
    er = per_direction(m_r[..., :c_len, :, :], 2, True).reshape(wt.shape[:3] + (256, S5_STATE))
    ei = per_direction(m_i[..., :c_len, :, :], 2, True).reshape(wt.shape[:3] + (256, S5_STATE))
    we = jnp.concatenate([er, ei, ei, er], axis=-1)
    c_rt, c_it = c_re.swapaxes(-1, -2)[..., None, :], c_im.swapaxes(-1, -2)[..., None, :]
    p_rt, p_it = pr.swapaxes(-1, -2)[..., 1:, None], pi.swapaxes(-1, -2)[..., 1:, None]
    sr = per_direction(c_rt * p_rt - c_it * p_it, 3, False).reshape(wt.shape[:3] + (S5_STATE, 256))
    si = per_direction(c_rt * p_it + c_it * p_rt, 3, False).reshape(wt.shape[:3] + (S5_STATE, 256))
    ws = jnp.concatenate([sr, -si], axis=3)
    a_r, a_i = pr[..., c_len, :], pi[..., c_len, :]
    coef = jnp.stack([jnp.concatenate([a_r, a_r], -1),
                      jnp.concatenate([-a_i, a_i], -1),
                      jnp.concatenate([a_i, -a_i], -1)], axis=2)
    return wt.astype(BF16), we.astype(BF16), ws.astype(BF16), coef


def _block_transpose8(xs, lane):
    for k in (2, 1, 0):
        shift = S5_GROUP << k
        bit = jnp.bitwise_and(lax.shift_right_logical(lane, 4 + k), 1)
        new = list(xs)
        for a in range(8):
            if (a >> k) & 1:
                continue
            b = a + (1 << k)
            new[a] = jnp.where(bit == 0, xs[a], pltpu.roll(xs[b], shift, 1))
            new[b] = jnp.where(bit == 1, xs[b], pltpu.roll(xs[a], LANES - shift, 1))
        xs = new
    return xs


def _s5_scan_kernel(u_ref, wt_ref, we_ref, ws_ref, coef_ref, *rest, reverse, add_prev):
    if add_prev:
        prev_ref, y_ref, ug_scr, yg_scr, e_scr, es_scr, s_scr, carry = rest
    else:
        y_ref, ug_scr, yg_scr, e_scr, es_scr, s_scr, carry = rest
    n_rows = ug_scr.shape[1]
    n_blk = S5_WIDTH // LANES
    per_blk = LANES // S5_GROUP
    halves = S5_CHUNK // per_blk

    @pl.when(pl.program_id(0) == 0)
    def _():
        carry[...] = jnp.zeros_like(carry)

    lane = lax.broadcasted_iota(jnp.int32, (n_rows, LANES), 1)

    for blk in range(n_blk):
        for half in range(halves):
            xs = [u_ref[blk, pl.ds(half * per_blk + jl, n_rows, stride=S5_CHUNK), :] for jl in range(per_blk)]
            ys = _block_transpose8(xs, lane)
            for gl in range(per_blk):
                ug_scr[blk * per_blk + gl, :, half * LANES:(half + 1) * LANES] = ys[gl].astype(BF16)

    for g in range(S5_GROUPS):
        e = jnp.dot(ug_scr[g], we_ref[g], preferred_element_type=F32)
        e_scr[g * n_rows:(g + 1) * n_rows, :] = e[:, :LANES]
        es_scr[g * n_rows:(g + 1) * n_rows, :] = e[:, LANES:]

    c1, c2, c3 = coef_ref[0], coef_ref[1], coef_ref[2]

    def step(i, vs):
        v, vp = vs
        r = (n_rows - 1 - i) if reverse else i
        rows = pl.ds(r, S5_GROUPS, stride=n_rows)
        s_scr[rows, :] = v
        v_new = v * c1 + vp * c2 + e_scr[rows, :]
        vp_new = vp * c1 + v * c3 + es_scr[rows, :]
        return v_new, vp_new

    v, vp = lax.fori_loop(0, n_rows, step, (carry[0], carry[1]), unroll=8)
    carry[0] = v
    carry[1] = vp

    for blk in range(n_blk):
        for gl in range(per_blk):
            g = blk * per_blk + gl
            s_in = s_scr[g * n_rows:(g + 1) * n_rows, :].astype(BF16)
            yg_scr[gl] = (jnp.dot(ug_scr[g], wt_ref[g], preferred_element_type=F32)
                          + jnp.dot(s_in, ws_ref[g], preferred_element_type=F32))
        for half in range(halves):
            zs = [yg_scr[gl, :, half * LANES:(half + 1) * LANES] for gl in range(per_blk)]
            ws = _block_transpose8(zs, lane)
            for tl in range(per_blk):
                rows = pl.ds(half * per_blk + tl, n_rows, stride=S5_CHUNK)
                out = ws[tl]
                if add_prev:
                    out = out + prev_ref[blk, rows, :]
                y_ref[blk, rows, :] = out


def _s5_direction(u4, tables, prev, reverse, layer):
    n_blk, t, _ = u4.shape
    n_chunks = t // S5_CHUNK
    rows = min(S5_TILE, n_chunks)
    n_tiles = n_chunks // rows
    wt, we, ws, coef = tables
    tile = (lambda i: (0, n_tiles - 1 - i, 0)) if reverse else (lambda i: (0, i, 0))
    table = lambda i: (layer, int(reverse), 0, 0, 0)
    once = pl.Buffered(1)
    in_specs = [
        pl.BlockSpec((n_blk, rows * S5_CHUNK, LANES), tile),
        pl.BlockSpec((None, None, S5_GROUPS, 256, 256), table, pipeline_mode=once),
        pl.BlockSpec((None, None, S5_GROUPS, 256, 256), table, pipeline_mode=once),
        pl.BlockSpec((None, None, S5_GROUPS, LANES, 256), table, pipeline_mode=once),
        pl.BlockSpec((None, None, 3, S5_GROUPS, LANES), table, pipeline_mode=once),
    ]
    args = [u4, wt, we, ws, coef]
    if prev is not None:
        in_specs.append(pl.BlockSpec((n_blk, rows * S5_CHUNK, LANES), tile))
        args.append(prev)
    return pl.pallas_call(
        functools.partial(_s5_scan_kernel, reverse=reverse, add_prev=prev is not None),
        grid=(n_tiles,),
        in_specs=in_specs,
        out_specs=pl.BlockSpec((n_blk, rows * S5_CHUNK, LANES), tile),
        out_shape=jax.ShapeDtypeStruct((n_blk, t, LANES), F32),
        scratch_shapes=[
            pltpu.VMEM((S5_GROUPS, rows, 256), BF16),
            pltpu.VMEM((LANES // S5_GROUP, rows, 256), F32),
            pltpu.VMEM((S5_GROUPS * rows, LANES), F32),
            pltpu.VMEM((S5_GROUPS * rows, LANES), F32),
            pltpu.VMEM((S5_GROUPS * rows, LANES), F32),
            pltpu.VMEM((2, S5_GROUPS, LANES), F32),
        ],
        compiler_params=_params("arbitrary"),
        name="s5_bwd" if reverse else "s5_fwd",
    )(*args)


def _s5_mix(u4, tables, layer):
    y = _s5_direction(u4, tables, None, False, layer)
    return _s5_direction(u4, tables, y, True, layer)


def _bmm(a, b):
    return lax.dot_general(a, b, (((2,), (1,)), ((0,), (0,))), preferred_element_type=F32)


def _dn_kernel(qf, kf, vf, gf, gtf, qb, kb_, vb, gb, gtb, of_ref, ob_ref,
               state, wq_scr, kdt_scr, at_scr, u_scr, gam_scr):
    c_len = DN_CHUNK
    n_ch = gtf.shape[0]
    n_gate = 2 * DN_HEADS
    n_chain = 2 * DN_HEADS

    @pl.when(pl.program_id(0) == 0)
    def _():
        state[...] = jnp.zeros_like(state)

    ri = lax.broadcasted_iota(jnp.int32, (c_len, c_len), 0)
    ci = lax.broadcasted_iota(jnp.int32, (c_len, c_len), 1)
    eye = jnp.where(ri == ci, 1.0, 0.0).astype(F32)
    dirs = ((qf, kf, vf, gf, gtf, ri >= ci, ri > ci, c_len - 1),
            (qb, kb_, vb, gb, gtb, ri <= ci, ri < ci, 0))

    for d, (q_ref, k_ref, v_ref, gate_ref, gate_t_ref, incl, strict, last) in enumerate(dirs):
        gates = gate_ref[...].reshape(n_ch, c_len, LANES)
        gates_t = gate_t_ref[...]
        for h in range(DN_HEADS):
            idx = d * DN_HEADS + h
            hs = slice(h * DN_HEAD_DIM, (h + 1) * DN_HEAD_DIM)
            gcol = gates[:, :, 2 * n_gate + idx:2 * n_gate + idx + 1]
            bcol = gates[:, :, n_gate + idx:n_gate + idx + 1]
            grow = gates_t[:, 2 * n_gate + idx:2 * n_gate + idx + 1, :]
            glast = grow[:, :, last:last + 1]
            qh = q_ref[:, hs].reshape(n_ch, c_len, DN_HEAD_DIM)
            kh = k_ref[:, hs].reshape(n_ch, c_len, DN_HEAD_DIM)
            vh = v_ref[:, hs].reshape(n_ch, c_len, DN_HEAD_DIM)
            kb = kh.astype(BF16)
            qk_kk = lax.dot_general(jnp.concatenate([qh.astype(BF16), kb], axis=1), kb,
                                    (((2,), (2,)), ((0,), (0,))), preferred_element_type=F32)
            qk, kk = qk_kk[:, :c_len], qk_kk[:, c_len:]
            decay = jnp.where(incl, jnp.exp(jnp.where(incl, gcol - grow, 0.0)), 0.0)
            a_mat = jnp.where(strict, bcol * kk * decay, 0.0)
            pw = -a_mat
            inv = eye + pw
            for _ in range(5):
                pwb = pw.astype(BF16)
                pw = _bmm(pwb, pwb)
                inv = inv + _bmm(inv.astype(BF16), pw.astype(BF16))
            egc = jnp.exp(gcol)
            rhs = jnp.concatenate([vh * bcol, kh * (bcol * egc)], axis=2).astype(BF16)
            uw = _bmm(inv.astype(BF16), rhs)
            wq = jnp.concatenate([uw[:, :, DN_HEAD_DIM:], qh * egc], axis=1).astype(BF16)
            attn = jnp.where(incl, qk * decay, 0.0).astype(BF16)
            k_dec_t = jnp.swapaxes(kh * jnp.exp(glast - gcol), 1, 2).astype(BF16)
            gamma = jnp.broadcast_to(jnp.exp(glast), (n_ch, 1, LANES))
            for c in range(n_ch):
                slot = (c if d == 0 else n_ch - 1 - c) * n_chain + idx
                wq_scr[slot] = wq[c]
                kdt_scr[slot] = k_dec_t[c]
                at_scr[slot] = attn[c]
                u_scr[slot] = uw[c, :, :DN_HEAD_DIM]
                gam_scr[slot] = gamma[c]

    for step in range(n_ch):
        grp = slice(step * n_chain, (step + 1) * n_chain)
        s = state[...]
        wq_s = _bmm(wq_scr[grp], s.astype(BF16))
        v_nb = (u_scr[grp] - wq_s[:, :c_len]).astype(BF16)
        o = wq_s[:, c_len:] + _bmm(at_scr[grp], v_nb)
        state[...] = s * gam_scr[grp] + _bmm(kdt_scr[grp], v_nb)
        for d, o_ref in enumerate((of_ref, ob_ref)):
            c = step if d == 0 else n_ch - 1 - step
            for h in range(DN_HEADS):
                o_ref[c * c_len:(c + 1) * c_len, h * DN_HEAD_DIM:(h + 1) * DN_HEAD_DIM] = o[d * DN_HEADS + h]


def _dn_mix(q, k, v, gates, gates_t):
    t = q.shape[0]
    n_chunks = t // DN_CHUNK
    n_ch = min(DN_STEP_CHUNKS, n_chunks)
    n_steps = n_chunks // n_ch
    rows = n_ch * DN_CHUNK
    n_inst = n_ch * 2 * DN_HEADS
    specs = []
    for row, row3 in ((lambda i: (i, 0), lambda i: (i, 0, 0)),
                      (lambda i: (n_steps - 1 - i, 0), lambda i: (n_steps - 1 - i, 0, 0))):
        specs += [
            pl.BlockSpec((rows, DN_WIDTH), row),
            pl.BlockSpec((rows, DN_WIDTH), row),
            pl.BlockSpec((rows, DN_WIDTH), row),
            pl.BlockSpec((rows, LANES), row),
            pl.BlockSpec((n_ch, 6 * DN_HEADS, DN_CHUNK), row3),
        ]
    return pl.pallas_call(
        _dn_kernel,
        grid=(n_steps,),
        in_specs=specs,
        out_specs=[
            pl.BlockSpec((rows, DN_WIDTH), lambda i: (i, 0)),
            pl.BlockSpec((rows, DN_WIDTH), lambda i: (n_steps - 1 - i, 0)),
        ],
        out_shape=[jax.ShapeDtypeStruct((t, DN_WIDTH), F32), jax.ShapeDtypeStruct((t, DN_WIDTH), F32)],
        scratch_shapes=[
            pltpu.VMEM((2 * DN_HEADS, DN_HEAD_DIM, DN_HEAD_DIM), F32),
            pltpu.VMEM((n_inst, 2 * DN_CHUNK, DN_HEAD_DIM), BF16),
            pltpu.VMEM((n_inst, DN_HEAD_DIM, DN_CHUNK), BF16),
            pltpu.VMEM((n_inst, DN_CHUNK, DN_CHUNK), BF16),
            pltpu.VMEM((n_inst, DN_CHUNK, DN_HEAD_DIM), F32),
            pltpu.VMEM((n_inst, 1, LANES), F32),
        ],
        compiler_params=_params("arbitrary"),
        name="dn_mix",
    )(q, k, v, gates, gates_t, q, k, v, gates, gates_t)


def _post_kernel(x_ref, u_ref, ys_ref, of_ref, ob_ref, z_ref, d_ref, wglu_ref, bglu_ref, nw_ref,
                 wout_ref, nffn_ref, wr_ref, br_ref, x1_ref, h_ref, route_ref, cnt_ref, base):
    i = pl.program_id(0)
    tm = x_ref.shape[0]

    @pl.when(i == 0)
    def _():
        base[...] = jnp.zeros_like(base)

    y = jnp.concatenate([ys_ref[b] + d_ref[:, b * LANES:(b + 1) * LANES] * u_ref[b]
                         for b in range(S5_WIDTH // LANES)], axis=1)
    y = 0.5 * y * (1.0 + lax.erf(y * (2.0 ** -0.5)))
    gate = jnp.dot(y.astype(BF16), wglu_ref[...], preferred_element_type=F32) + bglu_ref[...]
    y_s5 = y * jax.nn.sigmoid(gate)
    acc = x_ref[...] + jnp.dot(y_s5.astype(BF16), wout_ref[0:S5_WIDTH, :], preferred_element_type=F32)
    for h in range(DN_HEADS):
        hs = slice(h * DN_HEAD_DIM, (h + 1) * DN_HEAD_DIM)
        o = of_ref[:, hs] + ob_ref[:, hs]
        zh = z_ref[:, hs]
        o = o * lax.rsqrt(jnp.mean(o * o, axis=-1, keepdims=True) + NORM_EPS) * nw_ref[...]
        y_dn = o * (zh * jax.nn.sigmoid(zh))
        acc = acc + jnp.dot(y_dn.astype(BF16),
                            wout_ref[S5_WIDTH + h * DN_HEAD_DIM:S5_WIDTH + (h + 1) * DN_HEAD_DIM, :],
                            preferred_element_type=F32)
    x1_ref[...] = acc
    hn = acc * lax.rsqrt(jnp.mean(acc * acc, axis=-1, keepdims=True) + NORM_EPS) * nffn_ref[...]
    _matrix_to_rows(h_ref, hn)

    logits = _dot_split(hn, wr_ref[...]) + br_ref[...]
    lane_i = lax.broadcasted_iota(jnp.int32, logits.shape, 1)
    lane = lane_i.astype(F32)
    neg = jnp.float32(-jnp.inf)
    big = jnp.float32(LANES)
    gl = jnp.where(lane_i < N_EXPERT_GROUPS, logits, neg)
    gmax = jnp.max(gl, axis=-1, keepdims=True)
    g_sel = jnp.min(jnp.where(gl == gmax, lane, big), axis=-1, keepdims=True)
    p_group = 1.0 / jnp.sum(jnp.exp(gl - gmax), axis=-1, keepdims=True)
    lo = N_EXPERT_GROUPS + g_sel * EXPERTS_PER_GROUP
    el = jnp.where((lane >= lo) & (lane < lo + EXPERTS_PER_GROUP), logits, neg)
    top1 = jnp.max(el, axis=-1, keepdims=True)
    idx1 = jnp.min(jnp.where(el == top1, lane, big), axis=-1, keepdims=True)
    el2 = jnp.where(lane == idx1, neg, el)
    top2 = jnp.max(el2, axis=-1, keepdims=True)
    idx2 = jnp.min(jnp.where(el2 == top2, lane, big), axis=-1, keepdims=True)
    e21 = jnp.exp(top2 - top1)
    w1 = p_group / (1.0 + e21)
    w2 = w1 * e21
    oh1 = jnp.where(lane == idx1, 1.0, 0.0).astype(F32)
    oh2 = jnp.where(lane == idx2, 1.0, 0.0).astype(F32)
    ri = lax.broadcasted_iota(jnp.int32, (tm, tm), 0)
    ci = lax.broadcasted_iota(jnp.int32, (tm, tm), 1)
    before = jnp.where(ri > ci, 1.0, 0.0).astype(BF16)
    ohs = oh1 + oh2
    prior = jnp.dot(before, ohs.astype(BF16), preferred_element_type=F32) + base[0:1, :]
    rank1 = jnp.sum(oh1 * prior, axis=-1, keepdims=True)
    rank2 = jnp.sum(oh2 * prior, axis=-1, keepdims=True)
    base[0:1, :] = base[0:1, :] + jnp.sum(ohs, axis=0, keepdims=True)
    e1 = idx1 - N_EXPERT_GROUPS
    e2 = idx2 - N_EXPERT_GROUPS
    route = jnp.where(lane_i == 0, e1, jnp.where(lane_i == 1, e2, jnp.where(lane_i == 2, w1, jnp.where(
        lane_i == 3, w2, jnp.where(lane_i == 4, rank1, jnp.where(lane_i == 5, rank2, 0.0))))))
    route_ref[...] = route
    cnt_ref[...] = base[...]


def _post(x, u, ys, o_f, o_b, z, s5_d, w_glu, b_glu, dn_norm_w, w_out, norm_ffn, w_rg, b_rg, w_re, b_re):
    t = x.shape[0]
    tm = min(ROW_TILE, t)
    wr = jnp.concatenate([w_rg, w_re.transpose(1, 0, 2).reshape(D_MODEL, N_EXPERTS)], axis=1)
    wr = _split_weight(jnp.pad(wr.astype(F32), ((0, 0), (0, LANES - N_EXPERT_GROUPS - N_EXPERTS))))
    br = jnp.pad(jnp.concatenate([b_rg, b_re.reshape(-1)]).astype(F32),
                 (0, LANES - N_EXPERT_GROUPS - N_EXPERTS)).reshape(1, LANES)
    row = lambda i: (i, 0)
    const = lambda i: (0, 0)
    return pl.pallas_call(
        _post_kernel,
        grid=(t // tm,),
        in_specs=[
            pl.BlockSpec((tm, D_MODEL), row),
            pl.BlockSpec((S5_WIDTH // LANES, tm, LANES), lambda i: (0, i, 0)),
            pl.BlockSpec((S5_WIDTH // LANES, tm, LANES), lambda i: (0, i, 0)),
            pl.BlockSpec((tm, DN_WIDTH), row),
            pl.BlockSpec((tm, DN_WIDTH), row),
            pl.BlockSpec((tm, DN_WIDTH), row),
            pl.BlockSpec((1, S5_WIDTH), const),
            pl.BlockSpec((S5_WIDTH, S5_WIDTH), const),
            pl.BlockSpec((1, S5_WIDTH), const),
            pl.BlockSpec((1, DN_HEAD_DIM), const),
            pl.BlockSpec((D_MODEL, D_MODEL), const),
            pl.BlockSpec((1, D_MODEL), const),
            pl.BlockSpec((D_MODEL, 2 * LANES), const),
            pl.BlockSpec((1, LANES), const),
        ],
        out_specs=[
            pl.BlockSpec((tm, D_MODEL), row),
            pl.BlockSpec((tm * (D_MODEL // LANES), LANES), row),
            pl.BlockSpec((tm, LANES), row),
            pl.BlockSpec((SUBLANES, LANES), const),
        ],
        out_shape=[
            jax.ShapeDtypeStruct((t, D_MODEL), F32),
            jax.ShapeDtypeStruct((t * (D_MODEL // LANES), LANES), F32),
            jax.ShapeDtypeStruct((t, LANES), F32),
            jax.ShapeDtypeStruct((SUBLANES, LANES), F32),
        ],
        scratch_shapes=[pltpu.VMEM((SUBLANES, LANES), F32)],
        compiler_params=_params("arbitrary"),
        name="mixer_post",
    )(x, u, ys, o_f, o_b, z, s5_d.reshape(1, -1).astype(F32), w_glu.astype(BF16),
      b_glu.reshape(1, -1).astype(F32), dn_norm_w.reshape(1, -1).astype(F32), w_out.astype(BF16),
      norm_ffn.reshape(1, -1).astype(F32), wr, br)


ROW_SPLIT = D_MODEL // LANES


def _rows_to_matrix(ref, n_rows, lead=None):
    parts = []
    for s in range(ROW_SPLIT):
        idx = pl.ds(s, n_rows, stride=ROW_SPLIT)
        parts.append(ref[idx, :] if lead is None else ref[lead, idx, :])
    return jnp.concatenate(parts, axis=1)


def _matrix_to_rows(ref, val):
    n_rows = val.shape[0]
    for s in range(ROW_SPLIT):
        ref[pl.ds(s, n_rows, stride=ROW_SPLIT), :] = val[:, s * LANES:(s + 1) * LANES]


def _row(ref, r):
    return ref.at[pl.ds(pl.multiple_of(r * ROW_SPLIT, ROW_SPLIT), ROW_SPLIT), :]


def _dispatch_kernel(dest_ref, h_ref, zeros_ref, xs_ref, sem):
    del zeros_ref
    i = pl.program_id(0)
    tm = h_ref.shape[0] // ROW_SPLIT

    def copy(r, k):
        slot = dest_ref[2 * (i * tm + r) + k]
        return pltpu.make_async_copy(_row(h_ref, r), _row(xs_ref, slot), sem)

    def start(r, _):
        copy(r, 0).start(priority=0)
        copy(r, 1).start(priority=1)
        return 0

    def wait(r, _):
        copy(r, 0).wait()
        copy(r, 1).wait()
        return 0

    lax.fori_loop(0, tm, start, 0, unroll=8)
    lax.fori_loop(0, tm, wait, 0, unroll=8)


def _dispatch(h, dest, n_slots):
    t = h.shape[0] // ROW_SPLIT
    tm = min(MOE_TOK_TILE, t)
    grid_spec = pltpu.PrefetchScalarGridSpec(
        num_scalar_prefetch=1,
        grid=(t // tm,),
        in_specs=[
            pl.BlockSpec((tm * ROW_SPLIT, LANES), lambda i, dest: (i, 0)),
            pl.BlockSpec(memory_space=pl.ANY),
        ],
        out_specs=pl.BlockSpec(memory_space=pl.ANY),
        scratch_shapes=[pltpu.SemaphoreType.DMA(())],
    )
    return pl.pallas_call(
        _dispatch_kernel,
        grid_spec=grid_spec,
        out_shape=jax.ShapeDtypeStruct((n_slots * ROW_SPLIT, LANES), F32),
        input_output_aliases={2: 0},
        compiler_params=_params("arbitrary"),
        name="moe_dispatch",
    )(dest, h, jnp.zeros((n_slots * ROW_SPLIT, LANES), F32))


def _expert_kernel(be_ref, run_ref, nxt_ref, used_ref, xs_ref, wg_ref, wu_ref, wd_ref, ys_ref,
                   wg_buf, wu_buf, wd_buf, wgu_b, wd_b, wsem, *, layer):
    i = pl.program_id(0)
    blk = ys_ref.shape[0] // ROW_SPLIT

    def weight_copies(expert, slot):
        return (pltpu.make_async_copy(wg_ref.at[layer, expert], wg_buf.at[slot], wsem.at[slot]),
                pltpu.make_async_copy(wu_ref.at[layer, expert], wu_buf.at[slot], wsem.at[slot]),
                pltpu.make_async_copy(wd_ref.at[layer, expert], wd_buf.at[slot], wsem.at[slot]))

    run = run_ref[i]
    new_run = (i == 0) | (run != run_ref[jnp.maximum(i - 1, 0)])
    wslot = run % 2

    @pl.when(i == 0)
    def _():
        for c in weight_copies(be_ref[0], 0):
            c.start()

    @pl.when(new_run)
    def _():
        for c in weight_copies(be_ref[i], wslot):
            c.wait()

        @pl.when(nxt_ref[i] >= 0)
        def _():
            for c in weight_copies(nxt_ref[i], 1 - wslot):
                c.start()

        wgu_b[:, :D_EXPERT] = wg_buf[wslot].astype(BF16)
        wgu_b[:, D_EXPERT:] = wu_buf[wslot].astype(BF16)
        wd_b[...] = wd_buf[wslot].astype(BF16)

    @pl.when(i < used_ref[0])
    def _():
        xb = _rows_to_matrix(xs_ref, blk).astype(BF16)
        gu = jnp.dot(xb, wgu_b[...], preferred_element_type=F32)
        g, u = gu[:, :D_EXPERT], gu[:, D_EXPERT:]
        hid = (g * jax.nn.sigmoid(g) * u).astype(BF16)
        _matrix_to_rows(ys_ref, jnp.dot(hid, wd_b[...], preferred_element_type=F32))

    @pl.when(i >= used_ref[0])
    def _():
        ys_ref[...] = jnp.zeros_like(ys_ref)


def _experts(xs, block_expert, block_run, next_expert, used, w_gate, w_up, w_down, layer):
    n_slots = xs.shape[0] // ROW_SPLIT
    n_blocks = n_slots // MOE_BLOCK
    grid_spec = pltpu.PrefetchScalarGridSpec(
        num_scalar_prefetch=4,
        grid=(n_blocks,),
        in_specs=[pl.BlockSpec((MOE_BLOCK * ROW_SPLIT, LANES), lambda i, *_: (i, 0))]
        + [pl.BlockSpec(memory_space=pl.ANY)] * 3,
        out_specs=pl.BlockSpec((MOE_BLOCK * ROW_SPLIT, LANES), lambda i, *_: (i, 0)),
        scratch_shapes=[
            pltpu.VMEM((2, D_MODEL, D_EXPERT), F32),
            pltpu.VMEM((2, D_MODEL, D_EXPERT), F32),
            pltpu.VMEM((2, D_EXPERT, D_MODEL), F32),
            pltpu.VMEM((D_MODEL, 2 * D_EXPERT), BF16),
            pltpu.VMEM((D_EXPERT, D_MODEL), BF16),
            pltpu.SemaphoreType.DMA((2,)),
        ],
    )
    return pl.pallas_call(
        functools.partial(_expert_kernel, layer=layer),
        grid_spec=grid_spec,
        out_shape=jax.ShapeDtypeStruct((n_slots * ROW_SPLIT, LANES), F32),
        compiler_params=_params("arbitrary"),
        name="moe_experts",
    )(block_expert, block_run, next_expert, used, xs, w_gate, w_up, w_down)


def _combine_kernel(dest_ref, x1_ref, route_ref, ys_ref, nw_ref, out_ref, buf, sem, *, final_norm):
    i = pl.program_id(0)
    tm = x1_ref.shape[0]

    def copy(r, k):
        slot = dest_ref[2 * (i * tm + r) + k]
        dst = buf.at[k, pl.ds(pl.multiple_of(r * ROW_SPLIT, ROW_SPLIT), ROW_SPLIT), :]
        return pltpu.make_async_copy(_row(ys_ref, slot), dst, sem)

    def start(r, _):
        copy(r, 0).start(priority=0)
        copy(r, 1).start(priority=1)
        return 0

    def wait(r, _):
        copy(r, 0).wait()
        copy(r, 1).wait()
        return 0

    lax.fori_loop(0, tm, start, 0, unroll=8)
    lax.fori_loop(0, tm, wait, 0, unroll=8)
    route = route_ref[...]
    out = (x1_ref[...] + route[:, 2:3] * _rows_to_matrix(buf, tm, 0)
           + route[:, 3:4] * _rows_to_matrix(buf, tm, 1))
    if final_norm:
        out = out * lax.rsqrt(jnp.mean(out * out, axis=-1, keepdims=True) + NORM_EPS) * nw_ref[...]
    out_ref[...] = out


def _combine(x1, route, ys, dest, norm_w, final_norm):
    t = x1.shape[0]
    tm = min(MOE_TOK_TILE, t)
    grid_spec = pltpu.PrefetchScalarGridSpec(
        num_scalar_prefetch=1,
        grid=(t // tm,),
        in_specs=[
            pl.BlockSpec((tm, D_MODEL), lambda i, dest: (i, 0)),
            pl.BlockSpec((tm, LANES), lambda i, dest: (i, 0)),
            pl.BlockSpec(memory_space=pl.ANY),
            pl.BlockSpec((1, D_MODEL), lambda i, dest: (0, 0)),
        ],
        out_specs=pl.BlockSpec((tm, D_MODEL), lambda i, dest: (i, 0)),
        scratch_shapes=[pltpu.VMEM((2, tm * ROW_SPLIT, LANES), F32), pltpu.SemaphoreType.DMA(())],
    )
    return pl.pallas_call(
        functools.partial(_combine_kernel, final_norm=final_norm),
        grid_spec=grid_spec,
        out_shape=jax.ShapeDtypeStruct((t, D_MODEL), F32),
        compiler_params=_params("arbitrary"),
        name="moe_combine",
    )(dest, x1, route, ys, norm_w.reshape(1, D_MODEL).astype(F32))


def _moe(x1, h, route, counts, w_gate, w_up, w_down, layer, norm_w, final_norm):
    t = x1.shape[0]
    n_blocks = -(-(2 * t) // MOE_BLOCK) + N_EXPERTS
    n_slots = n_blocks * MOE_BLOCK
    cnt = counts[0, N_EXPERT_GROUPS:N_EXPERT_GROUPS + N_EXPERTS].astype(jnp.int32)
    padded = ((cnt + MOE_BLOCK - 1) // MOE_BLOCK) * MOE_BLOCK
    pad_end = jnp.cumsum(padded)
    pad_start = pad_end - padded
    expert = route[:, 0:2].astype(jnp.int32)
    rank = route[:, 4:6].astype(jnp.int32)
    experts = jnp.arange(N_EXPERTS, dtype=jnp.int32)
    dest = (jnp.sum(jnp.where(expert[..., None] == experts, pad_start, 0), axis=-1) + rank).reshape(-1)
    blocks = jnp.arange(n_blocks, dtype=jnp.int32)
    block_expert = jnp.minimum(jnp.sum(pad_end[None, :] <= (blocks * MOE_BLOCK)[:, None], axis=1),
                               N_EXPERTS - 1).astype(jnp.int32)
    starts_run = jnp.concatenate([jnp.ones((1,), bool), block_expert[1:] != block_expert[:-1]])
    block_run = (jnp.cumsum(starts_run) - 1).astype(jnp.int32)
    later_start = starts_run[None, :] & (blocks[None, :] > blocks[:, None])
    next_expert = jnp.where(jnp.any(later_start, axis=1),
                            block_expert[jnp.argmax(later_start, axis=1)], -1).astype(jnp.int32)
    used = (pad_end[-1:] // MOE_BLOCK).astype(jnp.int32)
    xs = _dispatch(h, dest, n_slots)
    ys = _experts(xs, block_expert, block_run, next_expert, used, w_gate, w_up, w_down, layer)
    return _combine(x1, route, ys, dest, norm_w, final_norm)


def kernel(x, norm_mix, w_in, s5_lam_re, s5_lam_im, s5_log_dt, s5_b_re, s5_b_im, s5_c_re, s5_c_im,
           s5_d, s5_w_glu, s5_b_glu, gdn_conv_w, gdn_a_log, gdn_dt_bias, gdn_norm_w, w_out, norm_ffn,
           router_w_group, router_b_group, router_w_expert, router_b_expert,
           expert_w_gate, expert_w_up, expert_w_down, norm_final):
    bsz, seq, d = x.shape
    depth = norm_mix.shape[0]
    xt = x.astype(F32).reshape(bsz * seq, d)
    tables = _s5_tables(s5_lam_re, s5_lam_im, s5_log_dt, s5_b_re, s5_b_im, s5_c_re, s5_c_im)
    for i in range(depth):
        u, z, q, k, v, gates = _inproj(xt, norm_mix[i], w_in[i], gdn_conv_w[i], gdn_a_log[i],
                                       gdn_dt_bias[i])
        ys = _s5_mix(u, tables, i)
        gates_t = (gates[:, :6 * DN_HEADS].reshape(-1, DN_CHUNK, 6 * DN_HEADS).transpose(0, 2, 1))
        o_f, o_b = _dn_mix(q, k, v, gates, gates_t)
        x1, h, route, counts = _post(xt, u, ys, o_f, o_b, z, s5_d[i], s5_w_glu[i], s5_b_glu[i],
                                     gdn_norm_w[i], w_out[i], norm_ffn[i], router_w_group[i],
                                     router_b_group[i], router_w_expert[i], router_b_expert[i])
        xt = _moe(x1, h, route, counts, expert_w_gate, expert_w_up, expert_w_down, i,
                  norm_final, i == depth - 1)
    return xt.reshape(bsz, seq, d)
```

```python
import functools
import math

import jax
import jax.numpy as jnp
import numpy as np
from jax import lax
from jax.experimental import pallas as pl
from jax.experimental.pallas import tpu as pltpu

F32 = jnp.float32
BF16 = jnp.bfloat16
HIGHEST = lax.Precision.HIGHEST

D_MODEL = 1024
S5_WIDTH = 512
S5_GROUP = 16
S5_GROUPS = 32
S5_STATE = 64
S5_MAX_RE = -1e-4
DN_HEADS = 4
DN_HEAD_DIM = 128
DN_WIDTH = 512
DN_CONV = 5
DN_CHUNK = 64
N_EXPERT_GROUPS = 4
EXPERTS_PER_GROUP = 8
N_EXPERTS = 32
D_EXPERT = 512
NORM_EPS = 1e-6

LANES = 128
SUBLANES = 8
VMEM_LIMIT = 56 * 1024 * 1024

S5_CHUNK = 16
S5_TILE = 128
ROW_TILE = 512
DN_STEP_CHUNKS = 16
MOE_BLOCK = 256
MOE_TOK_TILE = 256


def _params(*sem):
    return pltpu.CompilerParams(dimension_semantics=sem, vmem_limit_bytes=VMEM_LIMIT)


def _split_weight(w):
    hi = w.astype(BF16)
    lo = (w - hi.astype(F32)).astype(BF16)
    return jnp.concatenate([hi, lo], axis=1)


def _dot_split(a, w_split):
    a_hi = a.astype(BF16)
    a_lo = (a - a_hi.astype(F32)).astype(BF16)
    p = jnp.dot(a_hi, w_split, preferred_element_type=F32)
    q = jnp.dot(a_lo, w_split[:, :LANES], preferred_element_type=F32)
    return p[:, :LANES] + p[:, LANES:] + q


def _inproj_kernel(x_ref, xp_ref, xn_ref, nw_ref, w_ref, wab_ref, cw_ref, gp_ref,
                   u_ref, z_ref, q_ref, k_ref, v_ref, gate_ref, ext):
    i = pl.program_id(0)
    tm = x_ref.shape[0]
    pad = DN_CONV // 2

    def norm(x):
        return x * lax.rsqrt(jnp.mean(x * x, axis=-1, keepdims=True) + NORM_EPS) * nw_ref[...]

    h = norm(x_ref[...])
    h_prev = norm(jnp.where(i > 0, xp_ref[...], 0.0))
    h_next = norm(jnp.where(i < pl.num_programs(0) - 1, xn_ref[...], 0.0))
    hb = h.astype(BF16)
    for blk in range(S5_WIDTH // LANES):
        u_ref[blk] = jnp.dot(hb, w_ref[:, blk * LANES:(blk + 1) * LANES], preferred_element_type=F32)
    z_ref[...] = jnp.dot(hb, w_ref[:, S5_WIDTH + 3 * DN_WIDTH:S5_WIDTH + 4 * DN_WIDTH],
                         preferred_element_type=F32)
    h_ext = jnp.concatenate([h_prev, h, h_next], axis=0).astype(BF16)
    ext[...] = jnp.dot(h_ext, w_ref[:, S5_WIDTH:S5_WIDTH + 3 * DN_WIDTH], preferred_element_type=F32)

    outs = (q_ref, k_ref, v_ref)
    for part in range(3):
        cols = slice(part * DN_WIDTH, (part + 1) * DN_WIDTH)
        acc = ext[pl.ds(SUBLANES - pad, tm), cols] * cw_ref[0:1, cols]
        for tap in range(1, DN_CONV):
            acc = acc + ext[pl.ds(SUBLANES - pad + tap, tm), cols] * cw_ref[tap:tap + 1, cols]
        act = acc * jax.nn.sigmoid(acc)
        if part == 2:
            v_ref[...] = act.astype(BF16)
            continue
        scale = DN_HEAD_DIM ** -0.5 if part == 0 else 1.0
        for hd in range(DN_HEADS):
            hs = slice(hd * DN_HEAD_DIM, (hd + 1) * DN_HEAD_DIM)
            xh = act[:, hs]
            inv = lax.rsqrt(jnp.sum(xh * xh, axis=-1, keepdims=True) + NORM_EPS)
            outs[part][:, hs] = (xh * inv * scale).astype(BF16)

    ab = _dot_split(h, wab_ref[...])
    lane = lax.broadcasted_iota(jnp.int32, ab.shape, 1)
    pre = ab + gp_ref[1:2, :]
    softplus = jnp.maximum(pre, 0.0) + jnp.log1p(jnp.exp(-jnp.abs(pre)))
    n_gate = 2 * DN_HEADS
    g = jnp.where(lane < n_gate, gp_ref[0:1, :] * softplus, 0.0)
    hi = g.astype(BF16).astype(F32)
    r1 = g - hi
    mid = r1.astype(BF16).astype(F32)
    lo = (r1 - mid).astype(BF16).astype(F32)
    pieces = (hi + pltpu.roll(mid, n_gate, 1) + pltpu.roll(lo, 2 * n_gate, 1)).astype(BF16)
    ri = lax.broadcasted_iota(jnp.int32, (tm, tm), 0)
    ci = lax.broadcasted_iota(jnp.int32, (tm, tm), 1)
    chunk_bits = DN_CHUNK.bit_length() - 1
    same = lax.shift_right_logical(ri, chunk_bits) == lax.shift_right_logical(ci, chunk_bits)
    tri_lo = jnp.where(same & (ri >= ci), 1.0, 0.0).astype(BF16)
    tri_up = jnp.where(same & (ri <= ci), 1.0, 0.0).astype(BF16)
    pref = jnp.dot(tri_lo, pieces, preferred_element_type=F32)
    suff = jnp.dot(tri_up, pieces, preferred_element_type=F32)
    part = jnp.where(jnp.bitwise_and(lane, n_gate - 1) < DN_HEADS, pref, suff)
    gsum = part + pltpu.roll(part, LANES - n_gate, 1) + pltpu.roll(part, LANES - 2 * n_gate, 1)
    gate_ref[...] = jnp.where(lane < n_gate, g, jnp.where(
        lane < 2 * n_gate, jax.nn.sigmoid(ab), jnp.where(
            lane < 3 * n_gate, pltpu.roll(gsum, 2 * n_gate, 1), 0.0)))


def _inproj(x, norm_w, w_in, conv_w, a_log, dt_bias):
    t = x.shape[0]
    n_main = S5_WIDTH + 4 * DN_WIDTH
    w_main = w_in[:, :n_main].astype(BF16)
    w_ab = _split_weight(jnp.pad(w_in[:, n_main:].astype(F32), ((0, 0), (0, LANES - 4 * DN_HEADS))))
    gp = jnp.zeros((SUBLANES, LANES), F32)
    gp = gp.at[0, :2 * DN_HEADS].set(-jnp.exp(a_log.astype(F32)).reshape(-1))
    gp = gp.at[1, :2 * DN_HEADS].set(dt_bias.astype(F32).reshape(-1))
    cw = jnp.pad(conv_w.astype(F32), ((0, SUBLANES - DN_CONV), (0, 0)))
    tm = min(ROW_TILE, t)
    nb = tm // SUBLANES
    last = t // SUBLANES - 1
    row = lambda i: (i, 0)
    const = lambda i: (0, 0)
    return pl.pallas_call(
        _inproj_kernel,
        grid=(t // tm,),
        in_specs=[
            pl.BlockSpec((tm, D_MODEL), row),
            pl.BlockSpec((SUBLANES, D_MODEL), lambda i: (jnp.maximum(i * nb - 1, 0), 0)),
            pl.BlockSpec((SUBLANES, D_MODEL), lambda i: (jnp.minimum((i + 1) * nb, last), 0)),
            pl.BlockSpec((1, D_MODEL), const),
            pl.BlockSpec((D_MODEL, n_main), const),
            pl.BlockSpec((D_MODEL, 2 * LANES), const),
            pl.BlockSpec((SUBLANES, 3 * DN_WIDTH), const),
            pl.BlockSpec((SUBLANES, LANES), const),
        ],
        out_specs=[
            pl.BlockSpec((S5_WIDTH // LANES, tm, LANES), lambda i: (0, i, 0)),
            pl.BlockSpec((tm, DN_WIDTH), row),
            pl.BlockSpec((tm, DN_WIDTH), row),
            pl.BlockSpec((tm, DN_WIDTH), row),
            pl.BlockSpec((tm, DN_WIDTH), row),
            pl.BlockSpec((tm, LANES), row),
        ],
        out_shape=[
            jax.ShapeDtypeStruct((S5_WIDTH // LANES, t, LANES), F32),
            jax.ShapeDtypeStruct((t, DN_WIDTH), F32),
            jax.ShapeDtypeStruct((t, DN_WIDTH), BF16),
            jax.ShapeDtypeStruct((t, DN_WIDTH), BF16),
            jax.ShapeDtypeStruct((t, DN_WIDTH), BF16),
            jax.ShapeDtypeStruct((t, LANES), F32),
        ],
        scratch_shapes=[pltpu.VMEM((tm + 2 * SUBLANES, 3 * DN_WIDTH), F32)],
        compiler_params=_params("parallel"),
        name="inproj",
    )(x, x, x, norm_w.reshape(1, D_MODEL), w_main, w_ab, cw, gp)


def _toeplitz_kernel(k_ref, o_ref):
    c_len = S5_CHUNK
    lane = lax.broadcasted_iota(jnp.int32, (S5_GROUP, LANES), 1)
    for b in range(k_ref.shape[0]):
        tiles = [k_ref[b, :, t * LANES:(t + 1) * LANES] for t in range(k_ref.shape[2] // LANES)]
        for j in range(c_len):
            first, shift = divmod((c_len - 1 - j) * S5_GROUP, LANES)
            for half in range(2):
                piece = tiles[first + half]
                if shift:
                    piece = jnp.where(lane < LANES - shift,
                                      pltpu.roll(piece, LANES - shift, 1),
                                      pltpu.roll(tiles[first + half + 1], LANES - shift, 1))
                o_ref[b, j * S5_GROUP:(j + 1) * S5_GROUP, half * LANES:(half + 1) * LANES] = piece.astype(BF16)


def _toeplitz(kpad):
    n, q, width = kpad.shape
    per_step = 8
    return pl.pallas_call(
        _toeplitz_kernel,
        grid=(n // per_step,),
        in_specs=[pl.BlockSpec((per_step, q, width), lambda i: (i, 0, 0))],
        out_specs=pl.BlockSpec((per_step, S5_CHUNK * q, 256), lambda i: (i, 0, 0)),
        out_shape=jax.ShapeDtypeStruct((n, S5_CHUNK * q, 256), BF16),
        compiler_params=_params("parallel"),
        name="s5_toeplitz",
    )(kpad)


def _s5_tables(lam_re, lam_im, log_dt, b_re, b_im, c_re, c_im):
    c_len = S5_CHUNK
    lr = jnp.minimum(lam_re.astype(F32), S5_MAX_RE)
    li = lam_im.astype(F32)
    dt = jnp.exp(log_dt.astype(F32))[..., None]
    zr, zi = lr * dt, li * dt
    e1 = jnp.exp(zr)
    ar, ai = e1 * jnp.cos(zi), e1 * jnp.sin(zi)
    den = lr * lr + li * li
    nr, ni = ar - 1.0, ai
    fr = (nr * lr + ni * li) / den
    fi = (ni * lr - nr * li) / den
    bbr = (fr[..., None] * b_re - fi[..., None] * b_im).swapaxes(-1, -2)
    bbi = (fr[..., None] * b_im + fi[..., None] * b_re).swapaxes(-1, -2)
    tau = jnp.arange(c_len + 1, dtype=F32)[:, None]
    mag = jnp.exp(tau * zr[..., None, :])
    pr = mag * jnp.cos(tau * zi[..., None, :])
    pi = mag * jnp.sin(tau * zi[..., None, :])
    prq, piq = pr[..., :, None, :], pi[..., :, None, :]
    m_r = prq * bbr[..., None, :, :] - piq * bbi[..., None, :, :]
    m_i = prq * bbi[..., None, :, :] + piq * bbr[..., None, :, :]
    kern = (jnp.einsum('ldgpn,ldgtqn->ldgqtp', c_re, m_r[..., :c_len, :, :], precision=HIGHEST)
            - jnp.einsum('ldgpn,ldgtqn->ldgqtp', c_im, m_i[..., :c_len, :, :], precision=HIGHEST))

    def per_direction(x, axis, fwd_flipped):
        f, b = x[:, 0], x[:, 1]
        f, b = (jnp.flip(f, axis), b) if fwd_flipped else (f, jnp.flip(b, axis))
        return jnp.stack([f, b], axis=1)

    padded = jnp.pad(kern, ((0, 0),) * 4 + ((c_len - 1, 0), (0, 0)))
    padded = per_direction(padded, 3, False)
    padded = jnp.pad(padded, ((0, 0),) * 4 + ((0, 1), (0, 0)))
    lead = padded.shape[:3]
    wt = _toeplitz(padded.reshape(-1, S5_GROUP, 2 * c_len * S5_GROUP)).reshape(lead + (256, 256))
    er = per_direction(m_r[..., :c_len, :, :], 2, True).reshape(wt.shape[:3] + (256, S5_STATE))
    ei = per_direction(m_i[..., :c_len, :, :], 2, True).reshape(wt.shape[:3] + (256, S5_STATE))
    we = jnp.concatenate([er, ei, ei, er], axis=-1)
    c_rt, c_it = c_re.swapaxes(-1, -2)[..., None, :], c_im.swapaxes(-1, -2)[..., None, :]
    p_rt, p_it = pr.swapaxes(-1, -2)[..., 1:, None], pi.swapaxes(-1, -2)[..., 1:, None]
    sr = per_direction(c_rt * p_rt - c_it * p_it, 3, False).reshape(wt.shape[:3] + (S5_STATE, 256))
    si = per_direction(c_rt * p_it + c_it * p_rt, 3, False).reshape(wt.shape[:3] + (S5_STATE, 256))
    ws = jnp.concatenate([sr, -si], axis=3)
    a_r, a_i = pr[..., c_len, :], pi[..., c_len, :]
    coef = jnp.stack([jnp.concatenate([a_r, a_r], -1),
                      jnp.concatenate([-a_i, a_i], -1),
                      jnp.concatenate([a_i, -a_i], -1)], axis=2)
    return wt.astype(BF16), we.astype(BF16), ws.astype(BF16), coef


def _block_transpose8(xs, lane):
    for k in (2, 1, 0):
        shift = S5_GROUP << k
        bit = jnp.bitwise_and(lax.shift_right_logical(lane, 4 + k), 1)
        new = list(xs)
        for a in range(8):
            if (a >> k) & 1:
                continue
            b = a + (1 << k)
            new[a] = jnp.where(bit == 0, xs[a], pltpu.roll(xs[b], shift, 1))
            new[b] = jnp.where(bit == 1, xs[b], pltpu.roll(xs[a], LANES - shift, 1))
        xs = new
    return xs


def _s5_scan_kernel(u_ref, wt_ref, we_ref, ws_ref, coef_ref, *rest, reverse, add_prev):
    if add_prev:
        prev_ref, y_ref, ug_scr, yg_scr, e_scr, es_scr, s_scr, carry = rest
    else:
        y_ref, ug_scr, yg_scr, e_scr, es_scr, s_scr, carry = rest
    n_rows = ug_scr.shape[1]
    n_blk = S5_WIDTH // LANES
    per_blk = LANES // S5_GROUP
    halves = S5_CHUNK // per_blk

    @pl.when(pl.program_id(0) == 0)
    def _():
        carry[...] = jnp.zeros_like(carry)

    lane = lax.broadcasted_iota(jnp.int32, (n_rows, LANES), 1)

    for blk in range(n_blk):
        for half in range(halves):
            xs = [u_ref[blk, pl.ds(half * per_blk + jl, n_rows, stride=S5_CHUNK), :] for jl in range(per_blk)]
            ys = _block_transpose8(xs, lane)
            for gl in range(per_blk):
                ug_scr[blk * per_blk + gl, :, half * LANES:(half + 1) * LANES] = ys[gl].astype(BF16)

    for g in range(S5_GROUPS):
        e = jnp.dot(ug_scr[g], we_ref[g], preferred_element_type=F32)
        e_scr[g * n_rows:(g + 1) * n_rows, :] = e[:, :LANES]
        es_scr[g * n_rows:(g + 1) * n_rows, :] = e[:, LANES:]

    c1, c2, c3 = coef_ref[0], coef_ref[1], coef_ref[2]

    def step(i, vs):
        v, vp = vs
        r = (n_rows - 1 - i) if reverse else i
        rows = pl.ds(r, S5_GROUPS, stride=n_rows)
        s_scr[rows, :] = v
        v_new = v * c1 + vp * c2 + e_scr[rows, :]
        vp_new = vp * c1 + v * c3 + es_scr[rows, :]
        return v_new, vp_new

    v, vp = lax.fori_loop(0, n_rows, step, (carry[0], carry[1]), unroll=8)
    carry[0] = v
    carry[1] = vp

    for blk in range(n_blk):
        for gl in range(per_blk):
            g = blk * per_blk + gl
            s_in = s_scr[g * n_rows:(g + 1) * n_rows, :].astype(BF16)
            yg_scr[gl] = (jnp.dot(ug_scr[g], wt_ref[g], preferred_element_type=F32)
                          + jnp.dot(s_in, ws_ref[g], preferred_element_type=F32))
        for half in range(halves):
            zs = [yg_scr[gl, :, half * LANES:(half + 1) * LANES] for gl in range(per_blk)]
            ws = _block_transpose8(zs, lane)
            for tl in range(per_blk):
                rows = pl.ds(half * per_blk + tl, n_rows, stride=S5_CHUNK)
                out = ws[tl]
                if add_prev:
                    out = out + prev_ref[blk, rows, :]
                y_ref[blk, rows, :] = out


def _s5_direction(u4, tables, prev, reverse, layer):
    n_blk, t, _ = u4.shape
    n_chunks = t // S5_CHUNK
    rows = min(S5_TILE, n_chunks)
    n_tiles = n_chunks // rows
    wt, we, ws, coef = tables
    tile = (lambda i: (0, n_tiles - 1 - i, 0)) if reverse else (lambda i: (0, i, 0))
    table = lambda i: (layer, int(reverse), 0, 0, 0)
    once = pl.Buffered(1)
    in_specs = [
        pl.BlockSpec((n_blk, rows * S5_CHUNK, LANES), tile),
        pl.BlockSpec((None, None, S5_GROUPS, 256, 256), table, pipeline_mode=once),
        pl.BlockSpec((None, None, S5_GROUPS, 256, 256), table, pipeline_mode=once),
        pl.BlockSpec((None, None, S5_GROUPS, LANES, 256), table, pipeline_mode=once),
        pl.BlockSpec((None, None, 3, S5_GROUPS, LANES), table, pipeline_mode=once),
    ]
    args = [u4, wt, we, ws, coef]
    if prev is not None:
        in_specs.append(pl.BlockSpec((n_blk, rows * S5_CHUNK, LANES), tile))
        args.append(prev)
    return pl.pallas_call(
        functools.partial(_s5_scan_kernel, reverse=reverse, add_prev=prev is not None),
        grid=(n_tiles,),
        in_specs=in_specs,
        out_specs=pl.BlockSpec((n_blk, rows * S5_CHUNK, LANES), tile),
        out_shape=jax.ShapeDtypeStruct((n_blk, t, LANES), F32),
        scratch_shapes=[
            pltpu.VMEM((S5_GROUPS, rows, 256), BF16),
            pltpu.VMEM((LANES // S5_GROUP, rows, 256), F32),
            pltpu.VMEM((S5_GROUPS * rows, LANES), F32),
            pltpu.VMEM((S5_GROUPS * rows, LANES), F32),
            pltpu.VMEM((S5_GROUPS * rows, LANES), F32),
            pltpu.VMEM((2, S5_GROUPS, LANES), F32),
        ],
        compiler_params=_params("arbitrary"),
        name="s5_bwd" if reverse else "s5_fwd",
    )(*args)


def _s5_mix(u4, tables, layer):
    y = _s5_direction(u4, tables, None, False, layer)
    return _s5_direction(u4, tables, y, True, layer)


def _bmm(a, b):
    return lax.dot_general(a, b, (((2,), (1,)), ((0,), (0,))), preferred_element_type=F32)


def _dn_kernel(qf, kf, vf, gf, gtf, qb, kb_, vb, gb, gtb, of_ref, ob_ref,
               state, wq_scr, kdt_scr, at_scr, u_scr, gam_scr):
    c_len = DN_CHUNK
    n_ch = gtf.shape[0]
    n_gate = 2 * DN_HEADS
    n_chain = 2 * DN_HEADS

    @pl.when(pl.program_id(0) == 0)
    def _():
        state[...] = jnp.zeros_like(state)

    ri = lax.broadcasted_iota(jnp.int32, (c_len, c_len), 0)
    ci = lax.broadcasted_iota(jnp.int32, (c_len, c_len), 1)
    eye = jnp.where(ri == ci, 1.0, 0.0).astype(F32)
    dirs = ((qf, kf, vf, gf, gtf, ri >= ci, ri > ci, c_len - 1),
            (qb, kb_, vb, gb, gtb, ri <= ci, ri < ci, 0))

    for d, (q_ref, k_ref, v_ref, gate_ref, gate_t_ref, incl, strict, last) in enumerate(dirs):
        gates = gate_ref[...].reshape(n_ch, c_len, LANES)
        gates_t = gate_t_ref[...]
        for h in range(DN_HEADS):
            idx = d * DN_HEADS + h
            hs = slice(h * DN_HEAD_DIM, (h + 1) * DN_HEAD_DIM)
            gcol = gates[:, :, 2 * n_gate + idx:2 * n_gate + idx + 1]
            bcol = gates[:, :, n_gate + idx:n_gate + idx + 1]
            grow = gates_t[:, 2 * n_gate + idx:2 * n_gate + idx + 1, :]
            glast = grow[:, :, last:last + 1]
            qb = q_ref[:, hs].reshape(n_ch, c_len, DN_HEAD_DIM)
            kb = k_ref[:, hs].reshape(n_ch, c_len, DN_HEAD_DIM)
            qh, kh = qb.astype(F32), kb.astype(F32)
            vh = v_ref[:, hs].reshape(n_ch, c_len, DN_HEAD_DIM).astype(F32)
            qk_kk = lax.dot_general(jnp.concatenate([qb, kb], axis=1), kb,
                                    (((2,), (2,)), ((0,), (0,))), preferred_element_type=F32)
            qk, kk = qk_kk[:, :c_len], qk_kk[:, c_len:]
            decay = jnp.where(incl, jnp.exp(jnp.where(incl, gcol - grow, 0.0)), 0.0)
            a_mat = jnp.where(strict, bcol * kk * decay, 0.0)
            pw = -a_mat
            inv = eye + pw
            for _ in range(5):
                pwb = pw.astype(BF16)
                pw = _bmm(pwb, pwb)
                inv = inv + _bmm(inv.astype(BF16), pw.astype(BF16))
            egc = jnp.exp(gcol)
            rhs = jnp.concatenate([vh * bcol, kh * (bcol * egc)], axis=2).astype(BF16)
            uw = _bmm(inv.astype(BF16), rhs)
            wq = jnp.concatenate([uw[:, :, DN_HEAD_DIM:], qh * egc], axis=1).astype(BF16)
            attn = jnp.where(incl, qk * decay, 0.0).astype(BF16)
            k_dec_t = jnp.swapaxes(kh * jnp.exp(glast - gcol), 1, 2).astype(BF16)
            gamma = jnp.broadcast_to(jnp.exp(glast), (n_ch, 1, LANES))
            for c in range(n_ch):
                slot = (c if d == 0 else n_ch - 1 - c) * n_chain + idx
                wq_scr[slot] = wq[c]
                kdt_scr[slot] = k_dec_t[c]
                at_scr[slot] = attn[c]
                u_scr[slot] = uw[c, :, :DN_HEAD_DIM]
                gam_scr[slot] = gamma[c]

    for step in range(n_ch):
        grp = slice(step * n_chain, (step + 1) * n_chain)
        s = state[...]
        wq_s = _bmm(wq_scr[grp], s.astype(BF16))
        v_nb = (u_scr[grp] - wq_s[:, :c_len]).astype(BF16)
        o = wq_s[:, c_len:] + _bmm(at_scr[grp], v_nb)
        state[...] = s * gam_scr[grp] + _bmm(kdt_scr[grp], v_nb)
        for d, o_ref in enumerate((of_ref, ob_ref)):
            c = step if d == 0 else n_ch - 1 - step
            for h in range(DN_HEADS):
                o_ref[c * c_len:(c + 1) * c_len, h * DN_HEAD_DIM:(h + 1) * DN_HEAD_DIM] = (
                    o[d * DN_HEADS + h].astype(BF16))


def _dn_mix(q, k, v, gates, gates_t):
    t = q.shape[0]
    n_chunks = t // DN_CHUNK
    n_ch = min(DN_STEP_CHUNKS, n_chunks)
    n_steps = n_chunks // n_ch
    rows = n_ch * DN_CHUNK
    n_inst = n_ch * 2 * DN_HEADS
    specs = []
    for row, row3 in ((lambda i: (i, 0), lambda i: (i, 0, 0)),
                      (lambda i: (n_steps - 1 - i, 0), lambda i: (n_steps - 1 - i, 0, 0))):
        specs += [
            pl.BlockSpec((rows, DN_WIDTH), row),
            pl.BlockSpec((rows, DN_WIDTH), row),
            pl.BlockSpec((rows, DN_WIDTH), row),
            pl.BlockSpec((rows, LANES), row),
            pl.BlockSpec((n_ch, 6 * DN_HEADS, DN_CHUNK), row3),
        ]
    return pl.pallas_call(
        _dn_kernel,
        grid=(n_steps,),
        in_specs=specs,
        out_specs=[
            pl.BlockSpec((rows, DN_WIDTH), lambda i: (i, 0)),
            pl.BlockSpec((rows, DN_WIDTH), lambda i: (n_steps - 1 - i, 0)),
        ],
        out_shape=[jax.ShapeDtypeStruct((t, DN_WIDTH), BF16), jax.ShapeDtypeStruct((t, DN_WIDTH), BF16)],
        scratch_shapes=[
            pltpu.VMEM((2 * DN_HEADS, DN_HEAD_DIM, DN_HEAD_DIM), F32),
            pltpu.VMEM((n_inst, 2 * DN_CHUNK, DN_HEAD_DIM), BF16),
            pltpu.VMEM((n_inst, DN_HEAD_DIM, DN_CHUNK), BF16),
            pltpu.VMEM((n_inst, DN_CHUNK, DN_CHUNK), BF16),
            pltpu.VMEM((n_inst, DN_CHUNK, DN_HEAD_DIM), F32),
            pltpu.VMEM((n_inst, 1, LANES), F32),
        ],
        compiler_params=_params("arbitrary"),
        name="dn_mix",
    )(q, k, v, gates, gates_t, q, k, v, gates, gates_t)


def _post_kernel(x_ref, u_ref, ys_ref, of_ref, ob_ref, z_ref, d_ref, wglu_ref, bglu_ref, nw_ref,
                 wout_ref, nffn_ref, wr_ref, br_ref, x1_ref, h_ref, route_ref, cnt_ref, base):
    i = pl.program_id(0)
    tm = x_ref.shape[0]

    @pl.when(i == 0)
    def _():
        base[...] = jnp.zeros_like(base)

    y = jnp.concatenate([ys_ref[b] + d_ref[:, b * LANES:(b + 1) * LANES] * u_ref[b]
                         for b in range(S5_WIDTH // LANES)], axis=1)
    y = 0.5 * y * (1.0 + lax.erf(y * (2.0 ** -0.5)))
    gate = jnp.dot(y.astype(BF16), wglu_ref[...], preferred_element_type=F32) + bglu_ref[...]
    y_s5 = y * jax.nn.sigmoid(gate)
    acc = x_ref[...] + jnp.dot(y_s5.astype(BF16), wout_ref[0:S5_WIDTH, :], preferred_element_type=F32)
    for h in range(DN_HEADS):
        hs = slice(h * DN_HEAD_DIM, (h + 1) * DN_HEAD_DIM)
        o = of_ref[:, hs].astype(F32) + ob_ref[:, hs].astype(F32)
        zh = z_ref[:, hs]
        o = o * lax.rsqrt(jnp.mean(o * o, axis=-1, keepdims=True) + NORM_EPS) * nw_ref[...]
        y_dn = o * (zh * jax.nn.sigmoid(zh))
        acc = acc + jnp.dot(y_dn.astype(BF16),
                            wout_ref[S5_WIDTH + h * DN_HEAD_DIM:S5_WIDTH + (h + 1) * DN_HEAD_DIM, :],
                            preferred_element_type=F32)
    x1_ref[...] = acc
    hn = acc * lax.rsqrt(jnp.mean(acc * acc, axis=-1, keepdims=True) + NORM_EPS) * nffn_ref[...]
    _matrix_to_rows(h_ref, hn)

    logits = _dot_split(hn, wr_ref[...]) + br_ref[...]
    lane_i = lax.broadcasted_iota(jnp.int32, logits.shape, 1)
    lane = lane_i.astype(F32)
    neg = jnp.float32(-jnp.inf)
    big = jnp.float32(LANES)
    gl = jnp.where(lane_i < N_EXPERT_GROUPS, logits, neg)
    gmax = jnp.max(gl, axis=-1, keepdims=True)
    g_sel = jnp.min(jnp.where(gl == gmax, lane, big), axis=-1, keepdims=True)
    p_group = 1.0 / jnp.sum(jnp.exp(gl - gmax), axis=-1, keepdims=True)
    lo = N_EXPERT_GROUPS + g_sel * EXPERTS_PER_GROUP
    el = jnp.where((lane >= lo) & (lane < lo + EXPERTS_PER_GROUP), logits, neg)
    top1 = jnp.max(el, axis=-1, keepdims=True)
    idx1 = jnp.min(jnp.where(el == top1, lane, big), axis=-1, keepdims=True)
    el2 = jnp.where(lane == idx1, neg, el)
    top2 = jnp.max(el2, axis=-1, keepdims=True)
    idx2 = jnp.min(jnp.where(el2 == top2, lane, big), axis=-1, keepdims=True)
    e21 = jnp.exp(top2 - top1)
    w1 = p_group / (1.0 + e21)
    w2 = w1 * e21
    oh1 = jnp.where(lane == idx1, 1.0, 0.0).astype(F32)
    oh2 = jnp.where(lane == idx2, 1.0, 0.0).astype(F32)
    ri = lax.broadcasted_iota(jnp.int32, (tm, tm), 0)
    ci = lax.broadcasted_iota(jnp.int32, (tm, tm), 1)
    before = jnp.where(ri > ci, 1.0, 0.0).astype(BF16)
    ohs = oh1 + oh2
    prior = jnp.dot(before, ohs.astype(BF16), preferred_element_type=F32) + base[0:1, :]
    rank1 = jnp.sum(oh1 * prior, axis=-1, keepdims=True)
    rank2 = jnp.sum(oh2 * prior, axis=-1, keepdims=True)
    base[0:1, :] = base[0:1, :] + jnp.sum(ohs, axis=0, keepdims=True)
    e1 = idx1 - N_EXPERT_GROUPS
    e2 = idx2 - N_EXPERT_GROUPS
    route = jnp.where(lane_i == 0, e1, jnp.where(lane_i == 1, e2, jnp.where(lane_i == 2, w1, jnp.where(
        lane_i == 3, w2, jnp.where(lane_i == 4, rank1, jnp.where(lane_i == 5, rank2, 0.0))))))
    route_ref[...] = route
    cnt_ref[...] = base[...]


def _post(x, u, ys, o_f, o_b, z, s5_d, w_glu, b_glu, dn_norm_w, w_out, norm_ffn, w_rg, b_rg, w_re, b_re):
    t = x.shape[0]
    tm = min(ROW_TILE, t)
    wr = jnp.concatenate([w_rg, w_re.transpose(1, 0, 2).reshape(D_MODEL, N_EXPERTS)], axis=1)
    wr = _split_weight(jnp.pad(wr.astype(F32), ((0, 0), (0, LANES - N_EXPERT_GROUPS - N_EXPERTS))))
    br = jnp.pad(jnp.concatenate([b_rg, b_re.reshape(-1)]).astype(F32),
                 (0, LANES - N_EXPERT_GROUPS - N_EXPERTS)).reshape(1, LANES)
    row = lambda i: (i, 0)
    const = lambda i: (0, 0)
    return pl.pallas_call(
        _post_kernel,
        grid=(t // tm,),
        in_specs=[
            pl.BlockSpec((tm, D_MODEL), row),
            pl.BlockSpec((S5_WIDTH // LANES, tm, LANES), lambda i: (0, i, 0)),
            pl.BlockSpec((S5_WIDTH // LANES, tm, LANES), lambda i: (0, i, 0)),
            pl.BlockSpec((tm, DN_WIDTH), row),
            pl.BlockSpec((tm, DN_WIDTH), row),
            pl.BlockSpec((tm, DN_WIDTH), row),
            pl.BlockSpec((1, S5_WIDTH), const),
            pl.BlockSpec((S5_WIDTH, S5_WIDTH), const),
            pl.BlockSpec((1, S5_WIDTH), const),
            pl.BlockSpec((1, DN_HEAD_DIM), const),
            pl.BlockSpec((D_MODEL, D_MODEL), const),
            pl.BlockSpec((1, D_MODEL), const),
            pl.BlockSpec((D_MODEL, 2 * LANES), const),
            pl.BlockSpec((1, LANES), const),
        ],
        out_specs=[
            pl.BlockSpec((tm, D_MODEL), row),
            pl.BlockSpec((tm * (D_MODEL // LANES), LANES), row),
            pl.BlockSpec((tm, LANES), row),
            pl.BlockSpec((SUBLANES, LANES), const),
        ],
        out_shape=[
            jax.ShapeDtypeStruct((t, D_MODEL), F32),
            jax.ShapeDtypeStruct((t * (D_MODEL // LANES), LANES), F32),
            jax.ShapeDtypeStruct((t, LANES), F32),
            jax.ShapeDtypeStruct((SUBLANES, LANES), F32),
        ],
        scratch_shapes=[pltpu.VMEM((SUBLANES, LANES), F32)],
        compiler_params=_params("arbitrary"),
        name="mixer_post",
    )(x, u, ys, o_f, o_b, z, s5_d.reshape(1, -1).astype(F32), w_glu.astype(BF16),
      b_glu.reshape(1, -1).astype(F32), dn_norm_w.reshape(1, -1).astype(F32), w_out.astype(BF16),
      norm_ffn.reshape(1, -1).astype(F32), wr, br)


ROW_SPLIT = D_MODEL // LANES


def _rows_to_matrix(ref, n_rows, lead=None):
    parts = []
    for s in range(ROW_SPLIT):
        idx = pl.ds(s, n_rows, stride=ROW_SPLIT)
        parts.append(ref[idx, :] if lead is None else ref[lead, idx, :])
    return jnp.concatenate(parts, axis=1)


def _matrix_to_rows(ref, val):
    n_rows = val.shape[0]
    for s in range(ROW_SPLIT):
        ref[pl.ds(s, n_rows, stride=ROW_SPLIT), :] = val[:, s * LANES:(s + 1) * LANES]


def _row(ref, r):
    return ref.at[pl.ds(pl.multiple_of(r * ROW_SPLIT, ROW_SPLIT), ROW_SPLIT), :]


def _dispatch_kernel(dest_ref, h_ref, zeros_ref, xs_ref, sem):
    del zeros_ref
    i = pl.program_id(0)
    tm = h_ref.shape[0] // ROW_SPLIT

    def copy(r, k):
        slot = dest_ref[2 * (i * tm + r) + k]
        return pltpu.make_async_copy(_row(h_ref, r), _row(xs_ref, slot), sem)

    def start(r, _):
        copy(r, 0).start(priority=0)
        copy(r, 1).start(priority=1)
        return 0

    def wait(r, _):
        copy(r, 0).wait()
        copy(r, 1).wait()
        return 0

    lax.fori_loop(0, tm, start, 0, unroll=8)
    lax.fori_loop(0, tm, wait, 0, unroll=8)


def _dispatch(h, dest, n_slots):
    t = h.shape[0] // ROW_SPLIT
    tm = min(MOE_TOK_TILE, t)
    grid_spec = pltpu.PrefetchScalarGridSpec(
        num_scalar_prefetch=1,
        grid=(t // tm,),
        in_specs=[
            pl.BlockSpec((tm * ROW_SPLIT, LANES), lambda i, dest: (i, 0)),
            pl.BlockSpec(memory_space=pl.ANY),
        ],
        out_specs=pl.BlockSpec(memory_space=pl.ANY),
        scratch_shapes=[pltpu.SemaphoreType.DMA(())],
    )
    return pl.pallas_call(
        _dispatch_kernel,
        grid_spec=grid_spec,
        out_shape=jax.ShapeDtypeStruct((n_slots * ROW_SPLIT, LANES), F32),
        input_output_aliases={2: 0},
        compiler_params=_params("arbitrary"),
        name="moe_dispatch",
    )(dest, h, jnp.zeros((n_slots * ROW_SPLIT, LANES), F32))


def _expert_kernel(be_ref, run_ref, nxt_ref, used_ref, xs_ref, wg_ref, wu_ref, wd_ref, ys_ref,
                   wg_buf, wu_buf, wd_buf, wgu_b, wd_b, wsem, *, layer):
    i = pl.program_id(0)
    blk = ys_ref.shape[0] // ROW_SPLIT

    def weight_copies(expert, slot):
        return (pltpu.make_async_copy(wg_ref.at[layer, expert], wg_buf.at[slot], wsem.at[slot]),
                pltpu.make_async_copy(wu_ref.at[layer, expert], wu_buf.at[slot], wsem.at[slot]),
                pltpu.make_async_copy(wd_ref.at[layer, expert], wd_buf.at[slot], wsem.at[slot]))

    run = run_ref[i]
    new_run = (i == 0) | (run != run_ref[jnp.maximum(i - 1, 0)])
    wslot = run % 2

    @pl.when(i == 0)
    def _():
        for c in weight_copies(be_ref[0], 0):
            c.start()

    @pl.when(new_run)
    def _():
        for c in weight_copies(be_ref[i], wslot):
            c.wait()

        @pl.when(nxt_ref[i] >= 0)
        def _():
            for c in weight_copies(nxt_ref[i], 1 - wslot):
                c.start()

        wgu_b[:, :D_EXPERT] = wg_buf[wslot].astype(BF16)
        wgu_b[:, D_EXPERT:] = wu_buf[wslot].astype(BF16)
        wd_b[...] = wd_buf[wslot].astype(BF16)

    @pl.when(i < used_ref[0])
    def _():
        xb = _rows_to_matrix(xs_ref, blk).astype(BF16)
        gu = jnp.dot(xb, wgu_b[...], preferred_element_type=F32)
        g, u = gu[:, :D_EXPERT], gu[:, D_EXPERT:]
        hid = (g * jax.nn.sigmoid(g) * u).astype(BF16)
        _matrix_to_rows(ys_ref, jnp.dot(hid, wd_b[...], preferred_element_type=F32))

    @pl.when(i >= used_ref[0])
    def _():
        ys_ref[...] = jnp.zeros_like(ys_ref)


def _experts(xs, block_expert, block_run, next_expert, used, w_gate, w_up, w_down, layer):
    n_slots = xs.shape[0] // ROW_SPLIT
    n_blocks = n_slots // MOE_BLOCK
    grid_spec = pltpu.PrefetchScalarGridSpec(
        num_scalar_prefetch=4,
        grid=(n_blocks,),
        in_specs=[pl.BlockSpec((MOE_BLOCK * ROW_SPLIT, LANES), lambda i, *_: (i, 0))]
        + [pl.BlockSpec(memory_space=pl.ANY)] * 3,
        out_specs=pl.BlockSpec((MOE_BLOCK * ROW_SPLIT, LANES), lambda i, *_: (i, 0)),
        scratch_shapes=[
            pltpu.VMEM((2, D_MODEL, D_EXPERT), F32),
            pltpu.VMEM((2, D_MODEL, D_EXPERT), F32),
            pltpu.VMEM((2, D_EXPERT, D_MODEL), F32),
            pltpu.VMEM((D_MODEL, 2 * D_EXPERT), BF16),
            pltpu.VMEM((D_EXPERT, D_MODEL), BF16),
            pltpu.SemaphoreType.DMA((2,)),
        ],
    )
    return pl.pallas_call(
        functools.partial(_expert_kernel, layer=layer),
        grid_spec=grid_spec,
        out_shape=jax.ShapeDtypeStruct((n_slots * ROW_SPLIT, LANES), F32),
        compiler_params=_params("arbitrary"),
        name="moe_experts",
    )(block_expert, block_run, next_expert, used, xs, w_gate, w_up, w_down)


def _combine_kernel(dest_ref, x1_ref, route_ref, ys_ref, nw_ref, out_ref, buf, sem, *, final_norm):
    i = pl.program_id(0)
    tm = x1_ref.shape[0]

    def copy(r, k):
        slot = dest_ref[2 * (i * tm + r) + k]
        dst = buf.at[k, pl.ds(pl.multiple_of(r * ROW_SPLIT, ROW_SPLIT), ROW_SPLIT), :]
        return pltpu.make_async_copy(_row(ys_ref, slot), dst, sem)

    def start(r, _):
        copy(r, 0).start(priority=0)
        copy(r, 1).start(priority=1)
        return 0

    def wait(r, _):
        copy(r, 0).wait()
        copy(r, 1).wait()
        return 0

    lax.fori_loop(0, tm, start, 0, unroll=8)
    lax.fori_loop(0, tm, wait, 0, unroll=8)
    route = route_ref[...]
    out = (x1_ref[...] + route[:, 2:3] * _rows_to_matrix(buf, tm, 0)
           + route[:, 3:4] * _rows_to_matrix(buf, tm, 1))
    if final_norm:
        out = out * lax.rsqrt(jnp.mean(out * out, axis=-1, keepdims=True) + NORM_EPS) * nw_ref[...]
    out_ref[...] = out


def _combine(x1, route, ys, dest, norm_w, final_norm):
    t = x1.shape[0]
    tm = min(MOE_TOK_TILE, t)
    grid_spec = pltpu.PrefetchScalarGridSpec(
        num_scalar_prefetch=1,
        grid=(t // tm,),
        in_specs=[
            pl.BlockSpec((tm, D_MODEL), lambda i, dest: (i, 0)),
            pl.BlockSpec((tm, LANES), lambda i, dest: (i, 0)),
            pl.BlockSpec(memory_space=pl.ANY),
            pl.BlockSpec((1, D_MODEL), lambda i, dest: (0, 0)),
        ],
        out_specs=pl.BlockSpec((tm, D_MODEL), lambda i, dest: (i, 0)),
        scratch_shapes=[pltpu.VMEM((2, tm * ROW_SPLIT, LANES), F32), pltpu.SemaphoreType.DMA(())],
    )
    return pl.pallas_call(
        functools.partial(_combine_kernel, final_norm=final_norm),
        grid_spec=grid_spec,
        out_shape=jax.ShapeDtypeStruct((t, D_MODEL), F32),
        compiler_params=_params("arbitrary"),
        name="moe_combine",
    )(dest, x1, route, ys, norm_w.reshape(1, D_MODEL).astype(F32))


def _moe(x1, h, route, counts, w_gate, w_up, w_down, layer, norm_w, final_norm):
    t = x1.shape[0]
    n_blocks = -(-(2 * t) // MOE_BLOCK) + N_EXPERTS
    n_slots = n_blocks * MOE_BLOCK
    cnt = counts[0, N_EXPERT_GROUPS:N_EXPERT_GROUPS + N_EXPERTS].astype(jnp.int32)
    padded = ((cnt + MOE_BLOCK - 1) // MOE_BLOCK) * MOE_BLOCK
    pad_end = jnp.cumsum(padded)
    pad_start = pad_end - padded
    expert = route[:, 0:2].astype(jnp.int32)
    rank = route[:, 4:6].astype(jnp.int32)
    experts = jnp.arange(N_EXPERTS, dtype=jnp.int32)
    dest = (jnp.sum(jnp.where(expert[..., None] == experts, pad_start, 0), axis=-1) + rank).reshape(-1)
    blocks = jnp.arange(n_blocks, dtype=jnp.int32)
    block_expert = jnp.minimum(jnp.sum(pad_end[None, :] <= (blocks * MOE_BLOCK)[:, None], axis=1),
                               N_EXPERTS - 1).astype(jnp.int32)
    starts_run = jnp.concatenate([jnp.ones((1,), bool), block_expert[1:] != block_expert[:-1]])
    block_run = (jnp.cumsum(starts_run) - 1).astype(jnp.int32)
    later_start = starts_run[None, :] & (blocks[None, :] > blocks[:, None])
    next_expert = jnp.where(jnp.any(later_start, axis=1),
                            block_expert[jnp.argmax(later_start, axis=1)], -1).astype(jnp.int32)
    used = (pad_end[-1:] // MOE_BLOCK).astype(jnp.int32)
    xs = _dispatch(h, dest, n_slots)
    ys = _experts(xs, block_expert, block_run, next_expert, used, w_gate, w_up, w_down, layer)
    return _combine(x1, route, ys, dest, norm_w, final_norm)


def kernel(x, norm_mix, w_in, s5_lam_re, s5_lam_im, s5_log_dt, s5_b_re, s5_b_im, s5_c_re, s5_c_im,
           s5_d, s5_w_glu, s5_b_glu, gdn_conv_w, gdn_a_log, gdn_dt_bias, gdn_norm_w, w_out, norm_ffn,
           router_w_group, router_b_group, router_w_expert, router_b_expert,
           expert_w_gate, expert_w_up, expert_w_down, norm_final):
    bsz, seq, d = x.shape
    depth = norm_mix.shape[0]
    xt = x.astype(F32).reshape(bsz * seq, d)
    tables = _s5_tables(s5_lam_re, s5_lam_im, s5_log_dt, s5_b_re, s5_b_im, s5_c_re, s5_c_im)
    for i in range(depth):
        u, z, q, k, v, gates = _inproj(xt, norm_mix[i], w_in[i], gdn_conv_w[i], gdn_a_log[i],
                                       gdn_dt_bias[i])
        ys = _s5_mix(u, tables, i)
        gates_t = (gates[:, :6 * DN_HEADS].reshape(-1, DN_CHUNK, 6 * DN_HEADS).transpose(0, 2, 1))
        o_f, o_b = _dn_mix(q, k, v, gates, gates_t)
        x1, h, route, counts = _post(xt, u, ys, o_f, o_b, z, s5_d[i], s5_w_glu[i], s5_b_glu[i],
                                     gdn_norm_w[i], w_out[i], norm_ffn[i], router_w_group[i],
                                     router_b_group[i], router_w_expert[i], router_b_expert[i])
        xt = _moe(x1, h, route, counts, expert_w_gate, expert_w_up, expert_w_down, i,
                  norm_final, i == depth - 1)
    return xt.reshape(bsz, seq, d)
```

```python
import functools
import math

import jax
import jax.numpy as jnp
import numpy as np
from jax import lax
from jax.experimental import pallas as pl
from jax.experimental.pallas import tpu as pltpu

F32 = jnp.float32
BF16 = jnp.bfloat16
HIGHEST = lax.Precision.HIGHEST

D_MODEL = 1024
S5_WIDTH = 512
S5_GROUP = 16
S5_GROUPS = 32
S5_STATE = 64
S5_MAX_RE = -1e-4
DN_HEADS = 4
DN_HEAD_DIM = 128
DN_WIDTH = 512
DN_CONV = 5
DN_CHUNK = 64
N_EXPERT_GROUPS = 4
EXPERTS_PER_GROUP = 8
N_EXPERTS = 32
D_EXPERT = 512
NORM_EPS = 1e-6

LANES = 128
SUBLANES = 8
VMEM_LIMIT = 56 * 1024 * 1024

S5_CHUNK = 16
S5_TILE = 128
ROW_TILE = 512
DN_STEP_CHUNKS = 16
MOE_BLOCK = 256
MOE_TOK_TILE = 256


def _params(*sem):
    return pltpu.CompilerParams(dimension_semantics=sem, vmem_limit_bytes=VMEM_LIMIT)


def _split_weight(w):
    hi = w.astype(BF16)
    lo = (w - hi.astype(F32)).astype(BF16)
    return jnp.concatenate([hi, lo], axis=1)


def _dot_split(a, w_split):
    a_hi = a.astype(BF16)
    a_lo = (a - a_hi.astype(F32)).astype(BF16)
    p = jnp.dot(a_hi, w_split, preferred_element_type=F32)
    q = jnp.dot(a_lo, w_split[:, :LANES], preferred_element_type=F32)
    return p[:, :LANES] + p[:, LANES:] + q


def _inproj_kernel(x_ref, xp_ref, xn_ref, nw_ref, w_ref, wab_ref, cw_ref, gp_ref, tri_ref,
                   u_ref, z_ref, q_ref, k_ref, v_ref, gate_ref, ext):
    i = pl.program_id(0)
    tm = x_ref.shape[0]
    pad = DN_CONV // 2

    def norm(x):
        return x * lax.rsqrt(jnp.mean(x * x, axis=-1, keepdims=True) + NORM_EPS) * nw_ref[...]

    h = norm(x_ref[...])
    h_prev = norm(jnp.where(i > 0, xp_ref[...], 0.0))
    h_next = norm(jnp.where(i < pl.num_programs(0) - 1, xn_ref[...], 0.0))
    hb = h.astype(BF16)
    for blk in range(S5_WIDTH // LANES):
        u_ref[blk] = jnp.dot(hb, w_ref[:, blk * LANES:(blk + 1) * LANES], preferred_element_type=F32)
    z_ref[...] = jnp.dot(hb, w_ref[:, S5_WIDTH + 3 * DN_WIDTH:S5_WIDTH + 4 * DN_WIDTH],
                         preferred_element_type=F32)
    h_ext = jnp.concatenate([h_prev, h, h_next], axis=0).astype(BF16)
    ext[...] = jnp.dot(h_ext, w_ref[:, S5_WIDTH:S5_WIDTH + 3 * DN_WIDTH], preferred_element_type=F32)

    outs = (q_ref, k_ref, v_ref)
    for part in range(3):
        cols = slice(part * DN_WIDTH, (part + 1) * DN_WIDTH)
        acc = ext[pl.ds(SUBLANES - pad, tm), cols] * cw_ref[0:1, cols]
        for tap in range(1, DN_CONV):
            acc = acc + ext[pl.ds(SUBLANES - pad + tap, tm), cols] * cw_ref[tap:tap + 1, cols]
        act = acc * jax.nn.sigmoid(acc)
        if part == 2:
            v_ref[...] = act.astype(BF16)
            continue
        scale = DN_HEAD_DIM ** -0.5 if part == 0 else 1.0
        for hd in range(DN_HEADS):
            hs = slice(hd * DN_HEAD_DIM, (hd + 1) * DN_HEAD_DIM)
            xh = act[:, hs]
            inv = lax.rsqrt(jnp.sum(xh * xh, axis=-1, keepdims=True) + NORM_EPS)
            outs[part][:, hs] = (xh * inv * scale).astype(BF16)

    ab = _dot_split(h, wab_ref[...])
    lane = lax.broadcasted_iota(jnp.int32, ab.shape, 1)
    pre = ab + gp_ref[1:2, :]
    softplus = jnp.maximum(pre, 0.0) + jnp.log1p(jnp.exp(-jnp.abs(pre)))
    n_gate = 2 * DN_HEADS
    g = jnp.where(lane < n_gate, gp_ref[0:1, :] * softplus, 0.0)
    hi = g.astype(BF16).astype(F32)
    r1 = g - hi
    mid = r1.astype(BF16).astype(F32)
    lo = (r1 - mid).astype(BF16).astype(F32)
    pieces = (hi + pltpu.roll(mid, n_gate, 1) + pltpu.roll(lo, 2 * n_gate, 1)).astype(BF16)
    pref = jnp.dot(tri_ref[0], pieces, preferred_element_type=F32)
    suff = jnp.dot(tri_ref[1], pieces, preferred_element_type=F32)
    part = jnp.where(jnp.bitwise_and(lane, n_gate - 1) < DN_HEADS, pref, suff)
    gsum = part + pltpu.roll(part, LANES - n_gate, 1) + pltpu.roll(part, LANES - 2 * n_gate, 1)
    gate_ref[...] = jnp.where(lane < n_gate, g, jnp.where(
        lane < 2 * n_gate, jax.nn.sigmoid(ab), jnp.where(
            lane < 3 * n_gate, pltpu.roll(gsum, 2 * n_gate, 1), 0.0)))


def _inproj(x, norm_w, w_in, conv_w, a_log, dt_bias):
    t = x.shape[0]
    n_main = S5_WIDTH + 4 * DN_WIDTH
    w_main = w_in[:, :n_main].astype(BF16)
    w_ab = _split_weight(jnp.pad(w_in[:, n_main:].astype(F32), ((0, 0), (0, LANES - 4 * DN_HEADS))))
    gp = jnp.zeros((SUBLANES, LANES), F32)
    gp = gp.at[0, :2 * DN_HEADS].set(-jnp.exp(a_log.astype(F32)).reshape(-1))
    gp = gp.at[1, :2 * DN_HEADS].set(dt_bias.astype(F32).reshape(-1))
    cw = jnp.pad(conv_w.astype(F32), ((0, SUBLANES - DN_CONV), (0, 0)))
    tm = min(ROW_TILE, t)
    nb = tm // SUBLANES
    last = t // SUBLANES - 1
    step = np.arange(tm)
    same = (step[:, None] // DN_CHUNK) == (step[None, :] // DN_CHUNK)
    tri = jnp.asarray(np.stack([same & (step[:, None] >= step[None, :]),
                                same & (step[:, None] <= step[None, :])]), BF16)
    row = lambda i: (i, 0)
    const = lambda i: (0, 0)
    return pl.pallas_call(
        _inproj_kernel,
        grid=(t // tm,),
        in_specs=[
            pl.BlockSpec((tm, D_MODEL), row),
            pl.BlockSpec((SUBLANES, D_MODEL), lambda i: (jnp.maximum(i * nb - 1, 0), 0)),
            pl.BlockSpec((SUBLANES, D_MODEL), lambda i: (jnp.minimum((i + 1) * nb, last), 0)),
            pl.BlockSpec((1, D_MODEL), const),
            pl.BlockSpec((D_MODEL, n_main), const),
            pl.BlockSpec((D_MODEL, 2 * LANES), const),
            pl.BlockSpec((SUBLANES, 3 * DN_WIDTH), const),
            pl.BlockSpec((SUBLANES, LANES), const),
            pl.BlockSpec((2, tm, tm), lambda i: (0, 0, 0)),
        ],
        out_specs=[
            pl.BlockSpec((S5_WIDTH // LANES, tm, LANES), lambda i: (0, i, 0)),
            pl.BlockSpec((tm, DN_WIDTH), row),
            pl.BlockSpec((tm, DN_WIDTH), row),
            pl.BlockSpec((tm, DN_WIDTH), row),
            pl.BlockSpec((tm, DN_WIDTH), row),
            pl.BlockSpec((tm, LANES), row),
        ],
        out_shape=[
            jax.ShapeDtypeStruct((S5_WIDTH // LANES, t, LANES), F32),
            jax.ShapeDtypeStruct((t, DN_WIDTH), F32),
            jax.ShapeDtypeStruct((t, DN_WIDTH), BF16),
            jax.ShapeDtypeStruct((t, DN_WIDTH), BF16),
            jax.ShapeDtypeStruct((t, DN_WIDTH), BF16),
            jax.ShapeDtypeStruct((t, LANES), F32),
        ],
        scratch_shapes=[pltpu.VMEM((tm + 2 * SUBLANES, 3 * DN_WIDTH), F32)],
        compiler_params=_params("parallel"),
        name="inproj",
    )(x, x, x, norm_w.reshape(1, D_MODEL), w_main, w_ab, cw, gp, tri)


def _toeplitz_kernel(k_ref, o_ref):
    c_len = S5_CHUNK
    lane = lax.broadcasted_iota(jnp.int32, (S5_GROUP, LANES), 1)
    for b in range(k_ref.shape[0]):
        tiles = [k_ref[b, :, t * LANES:(t + 1) * LANES] for t in range(k_ref.shape[2] // LANES)]
        for j in range(c_len):
            first, shift = divmod((c_len - 1 - j) * S5_GROUP, LANES)
            for half in range(2):
                piece = tiles[first + half]
                if shift:
                    piece = jnp.where(lane < LANES - shift,
                                      pltpu.roll(piece, LANES - shift, 1),
                                      pltpu.roll(tiles[first + half + 1], LANES - shift, 1))
                o_ref[b, j * S5_GROUP:(j + 1) * S5_GROUP, half * LANES:(half + 1) * LANES] = piece.astype(BF16)


def _toeplitz(kpad):
    n, q, width = kpad.shape
    per_step = 8
    return pl.pallas_call(
        _toeplitz_kernel,
        grid=(n // per_step,),
        in_specs=[pl.BlockSpec((per_step, q, width), lambda i: (i, 0, 0))],
        out_specs=pl.BlockSpec((per_step, S5_CHUNK * q, 256), lambda i: (i, 0, 0)),
        out_shape=jax.ShapeDtypeStruct((n, S5_CHUNK * q, 256), BF16),
        compiler_params=_params("parallel"),
        name="s5_toeplitz",
    )(kpad)


def _s5_tables(lam_re, lam_im, log_dt, b_re, b_im, c_re, c_im):
    c_len = S5_CHUNK
    lr = jnp.minimum(lam_re.astype(F32), S5_MAX_RE)
    li = lam_im.astype(F32)
    dt = jnp.exp(log_dt.astype(F32))[..., None]
    zr, zi = lr * dt, li * dt
    e1 = jnp.exp(zr)
    ar, ai = e1 * jnp.cos(zi), e1 * jnp.sin(zi)
    den = lr * lr + li * li
    nr, ni = ar - 1.0, ai
    fr = (nr * lr + ni * li) / den
    fi = (ni * lr - nr * li) / den
    bbr = (fr[..., None] * b_re - fi[..., None] * b_im).swapaxes(-1, -2)
    bbi = (fr[..., None] * b_im + fi[..., None] * b_re).swapaxes(-1, -2)
    tau = jnp.arange(c_len + 1, dtype=F32)[:, None]
    mag = jnp.exp(tau * zr[..., None, :])
    pr = mag * jnp.cos(tau * zi[..., None, :])
    pi = mag * jnp.sin(tau * zi[..., None, :])
    prq, piq = pr[..., :, None, :], pi[..., :, None, :]
    m_r = prq * bbr[..., None, :, :] - piq * bbi[..., None, :, :]
    m_i = prq * bbi[..., None, :, :] + piq * bbr[..., None, :, :]
    kern = (jnp.einsum('ldgpn,ldgtqn->ldgqtp', c_re, m_r[..., :c_len, :, :], precision=HIGHEST)
            - jnp.einsum('ldgpn,ldgtqn->ldgqtp', c_im, m_i[..., :c_len, :, :], precision=HIGHEST))

    def per_direction(x, axis, fwd_flipped):
        f, b = x[:, 0], x[:, 1]
        f, b = (jnp.flip(f, axis), b) if fwd_flipped else (f, jnp.flip(b, axis))
        return jnp.stack([f, b], axis=1)

    padded = jnp.pad(kern, ((0, 0),) * 4 + ((c_len - 1, 0), (0, 0)))
    padded = per_direction(padded, 3, False)
    padded = jnp.pad(padded, ((0, 0),) * 4 + ((0, 1), (0, 0)))
    lead = padded.shape[:3]
    wt = _toeplitz(padded.reshape(-1, S5_GROUP, 2 * c_len * S5_GROUP)).reshape(lead + (256, 256))
    er = per_direction(m_r[..., :c_len, :, :], 2, True).reshape(wt.shape[:3] + (256, S5_STATE))
    ei = per_direction(m_i[..., :c_len, :, :], 2, True).reshape(wt.shape[:3] + (256, S5_STATE))
    we = jnp.concatenate([er, ei, ei, er], axis=-1)
    c_rt, c_it = c_re.swapaxes(-1, -2)[..., None, :], c_im.swapaxes(-1, -2)[..., None, :]
    p_rt, p_it = pr.swapaxes(-1, -2)[..., 1:, None], pi.swapaxes(-1, -2)[..., 1:, None]
    sr = per_direction(c_rt * p_rt - c_it * p_it, 3, False).reshape(wt.shape[:3] + (S5_STATE, 256))
    si = per_direction(c_rt * p_it + c_it * p_rt, 3, False).reshape(wt.shape[:3] + (S5_STATE, 256))
    ws = jnp.concatenate([sr, -si], axis=3)
    a_r, a_i = pr[..., c_len, :], pi[..., c_len, :]
    coef = jnp.stack([jnp.concatenate([a_r, a_r], -1),
                      jnp.concatenate([-a_i, a_i], -1),
                      jnp.concatenate([a_i, -a_i], -1)], axis=2)
    return wt.astype(BF16), we.astype(BF16), ws.astype(BF16), coef


def _block_transpose8(xs, lane):
    for k in (2, 1, 0):
        shift = S5_GROUP << k
        bit = jnp.bitwise_and(lax.shift_right_logical(lane, 4 + k), 1)
        new = list(xs)
        for a in range(8):
            if (a >> k) & 1:
                continue
            b = a + (1 << k)
            if 2 * shift == LANES:
                both = pltpu.roll(jnp.where(bit == 0, xs[b], xs[a]), shift, 1)
                new[a] = jnp.where(bit == 0, xs[a], both)
                new[b] = jnp.where(bit == 1, xs[b], both)
            else:
                new[a] = jnp.where(bit == 0, xs[a], pltpu.roll(xs[b], shift, 1))
                new[b] = jnp.where(bit == 1, xs[b], pltpu.roll(xs[a], LANES - shift, 1))
        xs = new
    return xs


def _s5_scan_kernel(u_ref, wt_ref, we_ref, ws_ref, coef_ref, *rest, reverse, add_prev):
    if add_prev:
        prev_ref, y_ref, ug_scr, yg_scr, e_scr, es_scr, s_scr, carry = rest
    else:
        y_ref, ug_scr, yg_scr, e_scr, es_scr, s_scr, carry = rest
    n_rows = ug_scr.shape[1]
    n_blk = S5_WIDTH // LANES
    per_blk = LANES // S5_GROUP
    halves = S5_CHUNK // per_blk

    @pl.when(pl.program_id(0) == 0)
    def _():
        carry[...] = jnp.zeros_like(carry)

    lane = lax.broadcasted_iota(jnp.int32, (n_rows, LANES), 1)

    for blk in range(n_blk):
        for half in range(halves):
            xs = [u_ref[blk, pl.ds(half * per_blk + jl, n_rows, stride=S5_CHUNK), :].astype(BF16)
                  for jl in range(per_blk)]
            ys = _block_transpose8(xs, lane)
            for gl in range(per_blk):
                ug_scr[blk * per_blk + gl, :, half * LANES:(half + 1) * LANES] = ys[gl].astype(BF16)

    for g in range(S5_GROUPS):
        e = jnp.dot(ug_scr[g], we_ref[g], preferred_element_type=F32)
        e_scr[g * n_rows:(g + 1) * n_rows, :] = e[:, :LANES]
        es_scr[g * n_rows:(g + 1) * n_rows, :] = e[:, LANES:]

    c1, c2, c3 = coef_ref[0], coef_ref[1], coef_ref[2]

    def step(i, vs):
        v, vp = vs
        r = (n_rows - 1 - i) if reverse else i
        rows = pl.ds(r, S5_GROUPS, stride=n_rows)
        s_scr[rows, :] = v
        v_new = v * c1 + vp * c2 + e_scr[rows, :]
        vp_new = vp * c1 + v * c3 + es_scr[rows, :]
        return v_new, vp_new

    v, vp = lax.fori_loop(0, n_rows, step, (carry[0], carry[1]), unroll=8)
    carry[0] = v
    carry[1] = vp

    for blk in range(n_blk):
        for gl in range(per_blk):
            g = blk * per_blk + gl
            s_in = s_scr[g * n_rows:(g + 1) * n_rows, :].astype(BF16)
            yg_scr[gl] = (jnp.dot(ug_scr[g], wt_ref[g], preferred_element_type=F32)
                          + jnp.dot(s_in, ws_ref[g], preferred_element_type=F32))
        for half in range(halves):
            zs = [yg_scr[gl, :, half * LANES:(half + 1) * LANES] for gl in range(per_blk)]
            ws = _block_transpose8(zs, lane)
            for tl in range(per_blk):
                rows = pl.ds(half * per_blk + tl, n_rows, stride=S5_CHUNK)
                out = ws[tl]
                if add_prev:
                    out = out + prev_ref[blk, rows, :]
                y_ref[blk, rows, :] = out


def _s5_direction(u4, tables, prev, reverse, layer):
    n_blk, t, _ = u4.shape
    n_chunks = t // S5_CHUNK
    rows = min(S5_TILE, n_chunks)
    n_tiles = n_chunks // rows
    wt, we, ws, coef = tables
    tile = (lambda i: (0, n_tiles - 1 - i, 0)) if reverse else (lambda i: (0, i, 0))
    table = lambda i: (layer, int(reverse), 0, 0, 0)
    once = pl.Buffered(1)
    in_specs = [
        pl.BlockSpec((n_blk, rows * S5_CHUNK, LANES), tile),
        pl.BlockSpec((None, None, S5_GROUPS, 256, 256), table, pipeline_mode=once),
        pl.BlockSpec((None, None, S5_GROUPS, 256, 256), table, pipeline_mode=once),
        pl.BlockSpec((None, None, S5_GROUPS, LANES, 256), table, pipeline_mode=once),
        pl.BlockSpec((None, None, 3, S5_GROUPS, LANES), table, pipeline_mode=once),
    ]
    args = [u4, wt, we, ws, coef]
    if prev is not None:
        in_specs.append(pl.BlockSpec((n_blk, rows * S5_CHUNK, LANES), tile))
        args.append(prev)
    return pl.pallas_call(
        functools.partial(_s5_scan_kernel, reverse=reverse, add_prev=prev is not None),
        grid=(n_tiles,),
        in_specs=in_specs,
        out_specs=pl.BlockSpec((n_blk, rows * S5_CHUNK, LANES), tile),
        out_shape=jax.ShapeDtypeStruct((n_blk, t, LANES), F32),
        scratch_shapes=[
            pltpu.VMEM((S5_GROUPS, rows, 256), BF16),
            pltpu.VMEM((LANES // S5_GROUP, rows, 256), F32),
            pltpu.VMEM((S5_GROUPS * rows, LANES), F32),
            pltpu.VMEM((S5_GROUPS * rows, LANES), F32),
            pltpu.VMEM((S5_GROUPS * rows, LANES), F32),
            pltpu.VMEM((2, S5_GROUPS, LANES), F32),
        ],
        compiler_params=_params("arbitrary"),
        name="s5_bwd" if reverse else "s5_fwd",
    )(*args)


def _s5_mix(u4, tables, layer):
    y = _s5_direction(u4, tables, None, False, layer)
    return _s5_direction(u4, tables, y, True, layer)


def _bmm(a, b):
    return lax.dot_general(a, b, (((2,), (1,)), ((0,), (0,))), preferred_element_type=F32)


def _dn_kernel(qf, kf, vf, gf, gtf, qb, kb_, vb, gb, gtb, of_ref, ob_ref,
               state, wq_scr, kdt_scr, at_scr, u_scr, gam_scr):
    c_len = DN_CHUNK
    n_ch = gtf.shape[0]
    n_gate = 2 * DN_HEADS
    n_chain = 2 * DN_HEADS

    @pl.when(pl.program_id(0) == 0)
    def _():
        state[...] = jnp.zeros_like(state)

    ri = lax.broadcasted_iota(jnp.int32, (c_len, c_len), 0)
    ci = lax.broadcasted_iota(jnp.int32, (c_len, c_len), 1)
    eye = jnp.where(ri == ci, 1.0, 0.0).astype(F32)
    dirs = ((qf, kf, vf, gf, gtf, ri >= ci, ri > ci, c_len - 1),
            (qb, kb_, vb, gb, gtb, ri <= ci, ri < ci, 0))

    for d, (q_ref, k_ref, v_ref, gate_ref, gate_t_ref, incl, strict, last) in enumerate(dirs):
        gates = gate_ref[...].reshape(n_ch, c_len, LANES)
        gates_t = gate_t_ref[...]
        for h in range(DN_HEADS):
            idx = d * DN_HEADS + h
            hs = slice(h * DN_HEAD_DIM, (h + 1) * DN_HEAD_DIM)
            gcol = gates[:, :, 2 * n_gate + idx:2 * n_gate + idx + 1]
            bcol = gates[:, :, n_gate + idx:n_gate + idx + 1]
            grow = gates_t[:, 2 * n_gate + idx:2 * n_gate + idx + 1, :]
            glast = grow[:, :, last:last + 1]
            qb = q_ref[:, hs].reshape(n_ch, c_len, DN_HEAD_DIM)
            kb = k_ref[:, hs].reshape(n_ch, c_len, DN_HEAD_DIM)
            qh, kh = qb.astype(F32), kb.astype(F32)
            vh = v_ref[:, hs].reshape(n_ch, c_len, DN_HEAD_DIM).astype(F32)
            qk_kk = lax.dot_general(jnp.concatenate([qb, kb], axis=1), kb,
                                    (((2,), (2,)), ((0,), (0,))), preferred_element_type=F32)
            qk, kk = qk_kk[:, :c_len], qk_kk[:, c_len:]
            decay = jnp.where(incl, jnp.exp(jnp.where(incl, gcol - grow, 0.0)), 0.0)
            a_mat = jnp.where(strict, bcol * kk * decay, 0.0)
            pw = -a_mat
            inv = eye + pw
            for _ in range(5):
                pwb = pw.astype(BF16)
                pw = _bmm(pwb, pwb)
                inv = inv + _bmm(inv.astype(BF16), pw.astype(BF16))
            egc = jnp.exp(gcol)
            rhs = jnp.concatenate([vh * bcol, kh * (bcol * egc)], axis=2).astype(BF16)
            uw = _bmm(inv.astype(BF16), rhs)
            wq = jnp.concatenate([uw[:, :, DN_HEAD_DIM:], qh * egc], axis=1).astype(BF16)
            attn = jnp.where(incl, qk * decay, 0.0).astype(BF16)
            k_dec_t = jnp.swapaxes(kh * jnp.exp(glast - gcol), 1, 2).astype(BF16)
            gamma = jnp.broadcast_to(jnp.exp(glast), (n_ch, 1, LANES))
            for c in range(n_ch):
                slot = (c if d == 0 else n_ch - 1 - c) * n_chain + idx
                wq_scr[slot] = wq[c]
                kdt_scr[slot] = k_dec_t[c]
                at_scr[slot] = attn[c]
                u_scr[slot] = uw[c, :, :DN_HEAD_DIM]
                gam_scr[slot] = gamma[c]

    for step in range(n_ch):
        grp = slice(step * n_chain, (step + 1) * n_chain)
        s = state[...]
        wq_s = _bmm(wq_scr[grp], s.astype(BF16))
        v_nb = (u_scr[grp] - wq_s[:, :c_len]).astype(BF16)
        o = wq_s[:, c_len:] + _bmm(at_scr[grp], v_nb)
        state[...] = s * gam_scr[grp] + _bmm(kdt_scr[grp], v_nb)
        for d, o_ref in enumerate((of_ref, ob_ref)):
            c = step if d == 0 else n_ch - 1 - step
            for h in range(DN_HEADS):
                o_ref[c * c_len:(c + 1) * c_len, h * DN_HEAD_DIM:(h + 1) * DN_HEAD_DIM] = (
                    o[d * DN_HEADS + h].astype(BF16))


def _dn_mix(q, k, v, gates, gates_t):
    t = q.shape[0]
    n_chunks = t // DN_CHUNK
    n_ch = min(DN_STEP_CHUNKS, n_chunks)
    n_steps = n_chunks // n_ch
    rows = n_ch * DN_CHUNK
    n_inst = n_ch * 2 * DN_HEADS
    specs = []
    for row, row3 in ((lambda i: (i, 0), lambda i: (i, 0, 0)),
                      (lambda i: (n_steps - 1 - i, 0), lambda i: (n_steps - 1 - i, 0, 0))):
        specs += [
            pl.BlockSpec((rows, DN_WIDTH), row),
            pl.BlockSpec((rows, DN_WIDTH), row),
            pl.BlockSpec((rows, DN_WIDTH), row),
            pl.BlockSpec((rows, LANES), row),
            pl.BlockSpec((n_ch, 6 * DN_HEADS, DN_CHUNK), row3),
        ]
    return pl.pallas_call(
        _dn_kernel,
        grid=(n_steps,),
        in_specs=specs,
        out_specs=[
            pl.BlockSpec((rows, DN_WIDTH), lambda i: (i, 0)),
            pl.BlockSpec((rows, DN_WIDTH), lambda i: (n_steps - 1 - i, 0)),
        ],
        out_shape=[jax.ShapeDtypeStruct((t, DN_WIDTH), BF16), jax.ShapeDtypeStruct((t, DN_WIDTH), BF16)],
        scratch_shapes=[
            pltpu.VMEM((2 * DN_HEADS, DN_HEAD_DIM, DN_HEAD_DIM), F32),
            pltpu.VMEM((n_inst, 2 * DN_CHUNK, DN_HEAD_DIM), BF16),
            pltpu.VMEM((n_inst, DN_HEAD_DIM, DN_CHUNK), BF16),
            pltpu.VMEM((n_inst, DN_CHUNK, DN_CHUNK), BF16),
            pltpu.VMEM((n_inst, DN_CHUNK, DN_HEAD_DIM), F32),
            pltpu.VMEM((n_inst, 1, LANES), F32),
        ],
        compiler_params=_params("arbitrary"),
        name="dn_mix",
    )(q, k, v, gates, gates_t, q, k, v, gates, gates_t)


def _post_kernel(x_ref, u_ref, ys_ref, of_ref, ob_ref, z_ref, d_ref, wglu_ref, bglu_ref, nw_ref,
                 wout_ref, nffn_ref, wr_ref, br_ref, before_ref, x1_ref, h_ref, route_ref, cnt_ref, base):
    i = pl.program_id(0)
    tm = x_ref.shape[0]

    @pl.when(i == 0)
    def _():
        base[...] = jnp.zeros_like(base)

    y = jnp.concatenate([ys_ref[b] + d_ref[:, b * LANES:(b + 1) * LANES] * u_ref[b]
                         for b in range(S5_WIDTH // LANES)], axis=1)
    y = 0.5 * y * (1.0 + lax.erf(y * (2.0 ** -0.5)))
    gate = jnp.dot(y.astype(BF16), wglu_ref[...], preferred_element_type=F32) + bglu_ref[...]
    y_s5 = y * jax.nn.sigmoid(gate)
    acc = x_ref[...] + jnp.dot(y_s5.astype(BF16), wout_ref[0:S5_WIDTH, :], preferred_element_type=F32)
    for h in range(DN_HEADS):
        hs = slice(h * DN_HEAD_DIM, (h + 1) * DN_HEAD_DIM)
        o = of_ref[:, hs].astype(F32) + ob_ref[:, hs].astype(F32)
        zh = z_ref[:, hs]
        o = o * lax.rsqrt(jnp.mean(o * o, axis=-1, keepdims=True) + NORM_EPS) * nw_ref[...]
        y_dn = o * (zh * jax.nn.sigmoid(zh))
        acc = acc + jnp.dot(y_dn.astype(BF16),
                            wout_ref[S5_WIDTH + h * DN_HEAD_DIM:S5_WIDTH + (h + 1) * DN_HEAD_DIM, :],
                            preferred_element_type=F32)
    x1_ref[...] = acc
    hn = acc * lax.rsqrt(jnp.mean(acc * acc, axis=-1, keepdims=True) + NORM_EPS) * nffn_ref[...]
    _matrix_to_rows(h_ref, hn)

    logits = _dot_split(hn, wr_ref[...]) + br_ref[...]
    lane_i = lax.broadcasted_iota(jnp.int32, logits.shape, 1)
    lane = lane_i.astype(F32)
    neg = jnp.float32(-jnp.inf)
    big = jnp.float32(LANES)
    gl = jnp.where(lane_i < N_EXPERT_GROUPS, logits, neg)
    gmax = jnp.max(gl, axis=-1, keepdims=True)
    g_sel = jnp.min(jnp.where(gl == gmax, lane, big), axis=-1, keepdims=True)
    p_group = 1.0 / jnp.sum(jnp.exp(gl - gmax), axis=-1, keepdims=True)
    lo = N_EXPERT_GROUPS + g_sel * EXPERTS_PER_GROUP
    el = jnp.where((lane >= lo) & (lane < lo + EXPERTS_PER_GROUP), logits, neg)
    top1 = jnp.max(el, axis=-1, keepdims=True)
    idx1 = jnp.min(jnp.where(el == top1, lane, big), axis=-1, keepdims=True)
    el2 = jnp.where(lane == idx1, neg, el)
    top2 = jnp.max(el2, axis=-1, keepdims=True)
    idx2 = jnp.min(jnp.where(el2 == top2, lane, big), axis=-1, keepdims=True)
    e21 = jnp.exp(top2 - top1)
    w1 = p_group / (1.0 + e21)
    w2 = w1 * e21
    oh1 = jnp.where(lane == idx1, 1.0, 0.0).astype(F32)
    oh2 = jnp.where(lane == idx2, 1.0, 0.0).astype(F32)
    ohs = oh1 + oh2
    prior = jnp.dot(before_ref[...], ohs.astype(BF16), preferred_element_type=F32) + base[0:1, :]
    rank1 = jnp.sum(oh1 * prior, axis=-1, keepdims=True)
    rank2 = jnp.sum(oh2 * prior, axis=-1, keepdims=True)
    base[0:1, :] = base[0:1, :] + jnp.sum(ohs, axis=0, keepdims=True)
    e1 = idx1 - N_EXPERT_GROUPS
    e2 = idx2 - N_EXPERT_GROUPS
    route = jnp.where(lane_i == 0, e1, jnp.where(lane_i == 1, e2, jnp.where(lane_i == 2, w1, jnp.where(
        lane_i == 3, w2, jnp.where(lane_i == 4, rank1, jnp.where(lane_i == 5, rank2, 0.0))))))
    route_ref[...] = route
    cnt_ref[...] = base[...]


def _post(x, u, ys, o_f, o_b, z, s5_d, w_glu, b_glu, dn_norm_w, w_out, norm_ffn, w_rg, b_rg, w_re, b_re):
    t = x.shape[0]
    tm = min(ROW_TILE, t)
    wr = jnp.concatenate([w_rg, w_re.transpose(1, 0, 2).reshape(D_MODEL, N_EXPERTS)], axis=1)
    wr = _split_weight(jnp.pad(wr.astype(F32), ((0, 0), (0, LANES - N_EXPERT_GROUPS - N_EXPERTS))))
    br = jnp.pad(jnp.concatenate([b_rg, b_re.reshape(-1)]).astype(F32),
                 (0, LANES - N_EXPERT_GROUPS - N_EXPERTS)).reshape(1, LANES)
    row = lambda i: (i, 0)
    const = lambda i: (0, 0)
    return pl.pallas_call(
        _post_kernel,
        grid=(t // tm,),
        in_specs=[
            pl.BlockSpec((tm, D_MODEL), row),
            pl.BlockSpec((S5_WIDTH // LANES, tm, LANES), lambda i: (0, i, 0)),
            pl.BlockSpec((S5_WIDTH // LANES, tm, LANES), lambda i: (0, i, 0)),
            pl.BlockSpec((tm, DN_WIDTH), row),
            pl.BlockSpec((tm, DN_WIDTH), row),
            pl.BlockSpec((tm, DN_WIDTH), row),
            pl.BlockSpec((1, S5_WIDTH), const),
            pl.BlockSpec((S5_WIDTH, S5_WIDTH), const),
            pl.BlockSpec((1, S5_WIDTH), const),
            pl.BlockSpec((1, DN_HEAD_DIM), const),
            pl.BlockSpec((D_MODEL, D_MODEL), const),
            pl.BlockSpec((1, D_MODEL), const),
            pl.BlockSpec((D_MODEL, 2 * LANES), const),
            pl.BlockSpec((1, LANES), const),
            pl.BlockSpec((tm, tm), const),
        ],
        out_specs=[
            pl.BlockSpec((tm, D_MODEL), row),
            pl.BlockSpec((tm * (D_MODEL // LANES), LANES), row),
            pl.BlockSpec((tm, LANES), row),
            pl.BlockSpec((SUBLANES, LANES), const),
        ],
        out_shape=[
            jax.ShapeDtypeStruct((t, D_MODEL), F32),
            jax.ShapeDtypeStruct((t * (D_MODEL // LANES), LANES), F32),
            jax.ShapeDtypeStruct((t, LANES), F32),
            jax.ShapeDtypeStruct((SUBLANES, LANES), F32),
        ],
        scratch_shapes=[pltpu.VMEM((SUBLANES, LANES), F32)],
        compiler_params=_params("arbitrary"),
        name="mixer_post",
    )(x, u, ys, o_f, o_b, z, s5_d.reshape(1, -1).astype(F32), w_glu.astype(BF16),
      b_glu.reshape(1, -1).astype(F32), dn_norm_w.reshape(1, -1).astype(F32), w_out.astype(BF16),
      norm_ffn.reshape(1, -1).astype(F32), wr, br,
      jnp.asarray(np.tril(np.ones((tm, tm), np.float32), -1), BF16))


ROW_SPLIT = D_MODEL // LANES


def _rows_to_matrix(ref, n_rows, lead=None):
    parts = []
    for s in range(ROW_SPLIT):
        idx = pl.ds(s, n_rows, stride=ROW_SPLIT)
        parts.append(ref[idx, :] if lead is None else ref[lead, idx, :])
    return jnp.concatenate(parts, axis=1)


def _matrix_to_rows(ref, val):
    n_rows = val.shape[0]
    for s in range(ROW_SPLIT):
        ref[pl.ds(s, n_rows, stride=ROW_SPLIT), :] = val[:, s * LANES:(s + 1) * LANES]


def _row(ref, r):
    return ref.at[pl.ds(pl.multiple_of(r * ROW_SPLIT, ROW_SPLIT), ROW_SPLIT), :]


def _dispatch_kernel(dest_ref, zblk_ref, h_ref, xs_ref, zero_buf, sem, zsem):
    i = pl.program_id(0)
    tm = h_ref.shape[0] // ROW_SPLIT
    blk_rows = zero_buf.shape[0]

    @pl.when(i == 0)
    def _():
        zero_buf[...] = jnp.zeros_like(zero_buf)

        def zero_copy(j):
            start = pl.multiple_of(zblk_ref[j] * blk_rows, blk_rows)
            return pltpu.make_async_copy(zero_buf, xs_ref.at[pl.ds(start, blk_rows), :], zsem)

        def zstart(j, _):
            @pl.when(zblk_ref[j] >= 0)
            def _():
                zero_copy(j).start()
            return 0

        def zwait(j, _):
            @pl.when(zblk_ref[j] >= 0)
            def _():
                zero_copy(j).wait()
            return 0

        lax.fori_loop(0, zblk_ref.shape[0], zstart, 0)
        lax.fori_loop(0, zblk_ref.shape[0], zwait, 0)

    def copy(r, k):
        slot = dest_ref[2 * (i * tm + r) + k]
        return pltpu.make_async_copy(_row(h_ref, r), _row(xs_ref, slot), sem)

    def start(r, _):
        copy(r, 0).start(priority=0)
        copy(r, 1).start(priority=1)
        return 0

    def wait(r, _):
        copy(r, 0).wait()
        copy(r, 1).wait()
        return 0

    lax.fori_loop(0, tm, start, 0, unroll=8)
    lax.fori_loop(0, tm, wait, 0, unroll=8)


def _dispatch(h, dest, zero_blocks, n_slots):
    t = h.shape[0] // ROW_SPLIT
    tm = min(MOE_TOK_TILE, t)
    grid_spec = pltpu.PrefetchScalarGridSpec(
        num_scalar_prefetch=2,
        grid=(t // tm,),
        in_specs=[pl.BlockSpec((tm * ROW_SPLIT, LANES), lambda i, *_: (i, 0))],
        out_specs=pl.BlockSpec(memory_space=pl.ANY),
        scratch_shapes=[pltpu.VMEM((MOE_BLOCK * ROW_SPLIT, LANES), F32),
                        pltpu.SemaphoreType.DMA(()), pltpu.SemaphoreType.DMA(())],
    )
    return pl.pallas_call(
        _dispatch_kernel,
        grid_spec=grid_spec,
        out_shape=jax.ShapeDtypeStruct((n_slots * ROW_SPLIT, LANES), F32),
        compiler_params=_params("arbitrary"),
        name="moe_dispatch",
    )(dest, zero_blocks, h)


def _expert_kernel(be_ref, run_ref, nxt_ref, used_ref, xs_ref, wg_ref, wu_ref, wd_ref, ys_ref,
                   wg_buf, wu_buf, wd_buf, wgu_b, wd_b, wsem, *, layer):
    i = pl.program_id(0)
    blk = ys_ref.shape[0] // ROW_SPLIT

    def weight_copies(expert, slot):
        return (pltpu.make_async_copy(wg_ref.at[layer, expert], wg_buf.at[slot], wsem.at[slot]),
                pltpu.make_async_copy(wu_ref.at[layer, expert], wu_buf.at[slot], wsem.at[slot]),
                pltpu.make_async_copy(wd_ref.at[layer, expert], wd_buf.at[slot], wsem.at[slot]))

    run = run_ref[i]
    new_run = (i == 0) | (run != run_ref[jnp.maximum(i - 1, 0)])
    wslot = run % 2

    @pl.when(i == 0)
    def _():
        for c in weight_copies(be_ref[0], 0):
            c.start()

    @pl.when(new_run)
    def _():
        for c in weight_copies(be_ref[i], wslot):
            c.wait()

        @pl.when(nxt_ref[i] >= 0)
        def _():
            for c in weight_copies(nxt_ref[i], 1 - wslot):
                c.start()

        wgu_b[:, :D_EXPERT] = wg_buf[wslot].astype(BF16)
        wgu_b[:, D_EXPERT:] = wu_buf[wslot].astype(BF16)
        wd_b[...] = wd_buf[wslot].astype(BF16)

    @pl.when(i < used_ref[0])
    def _():
        xb = _rows_to_matrix(xs_ref, blk).astype(BF16)
        gu = jnp.dot(xb, wgu_b[...], preferred_element_type=F32)
        g, u = gu[:, :D_EXPERT], gu[:, D_EXPERT:]
        hid = (g * jax.nn.sigmoid(g) * u).astype(BF16)
        _matrix_to_rows(ys_ref, jnp.dot(hid, wd_b[...], preferred_element_type=F32))

    @pl.when(i >= used_ref[0])
    def _():
        ys_ref[...] = jnp.zeros_like(ys_ref)


def _experts(xs, block_expert, block_run, next_expert, used, w_gate, w_up, w_down, layer):
    n_slots = xs.shape[0] // ROW_SPLIT
    n_blocks = n_slots // MOE_BLOCK
    grid_spec = pltpu.PrefetchScalarGridSpec(
        num_scalar_prefetch=4,
        grid=(n_blocks,),
        in_specs=[pl.BlockSpec((MOE_BLOCK * ROW_SPLIT, LANES), lambda i, *_: (i, 0))]
        + [pl.BlockSpec(memory_space=pl.ANY)] * 3,
        out_specs=pl.BlockSpec((MOE_BLOCK * ROW_SPLIT, LANES), lambda i, *_: (i, 0)),
        scratch_shapes=[
            pltpu.VMEM((2, D_MODEL, D_EXPERT), F32),
            pltpu.VMEM((2, D_MODEL, D_EXPERT), F32),
            pltpu.VMEM((2, D_EXPERT, D_MODEL), F32),
            pltpu.VMEM((D_MODEL, 2 * D_EXPERT), BF16),
            pltpu.VMEM((D_EXPERT, D_MODEL), BF16),
            pltpu.SemaphoreType.DMA((2,)),
        ],
    )
    return pl.pallas_call(
        functools.partial(_expert_kernel, layer=layer),
        grid_spec=grid_spec,
        out_shape=jax.ShapeDtypeStruct((n_slots * ROW_SPLIT, LANES), F32),
        compiler_params=_params("arbitrary"),
        name="moe_experts",
    )(block_expert, block_run, next_expert, used, xs, w_gate, w_up, w_down)


def _combine_kernel(dest_ref, x1_ref, route_ref, ys_ref, nw_ref, out_ref, buf, sem, *, final_norm):
    i = pl.program_id(0)
    tm = x1_ref.shape[0]

    def copy(r, k):
        slot = dest_ref[2 * (i * tm + r) + k]
        dst = buf.at[k, pl.ds(pl.multiple_of(r * ROW_SPLIT, ROW_SPLIT), ROW_SPLIT), :]
        return pltpu.make_async_copy(_row(ys_ref, slot), dst, sem)

    def start(r, _):
        copy(r, 0).start(priority=0)
        copy(r, 1).start(priority=1)
        return 0

    def wait(r, _):
        copy(r, 0).wait()
        copy(r, 1).wait()
        return 0

    lax.fori_loop(0, tm, start, 0, unroll=8)
    lax.fori_loop(0, tm, wait, 0, unroll=8)
    route = route_ref[...]
    out = (x1_ref[...] + route[:, 2:3] * _rows_to_matrix(buf, tm, 0)
           + route[:, 3:4] * _rows_to_matrix(buf, tm, 1))
    if final_norm:
        out = out * lax.rsqrt(jnp.mean(out * out, axis=-1, keepdims=True) + NORM_EPS) * nw_ref[...]
    out_ref[...] = out


def _combine(x1, route, ys, dest, norm_w, final_norm):
    t = x1.shape[0]
    tm = min(MOE_TOK_TILE, t)
    grid_spec = pltpu.PrefetchScalarGridSpec(
        num_scalar_prefetch=1,
        grid=(t // tm,),
        in_specs=[
            pl.BlockSpec((tm, D_MODEL), lambda i, dest: (i, 0)),
            pl.BlockSpec((tm, LANES), lambda i, dest: (i, 0)),
            pl.BlockSpec(memory_space=pl.ANY),
            pl.BlockSpec((1, D_MODEL), lambda i, dest: (0, 0)),
        ],
        out_specs=pl.BlockSpec((tm, D_MODEL), lambda i, dest: (i, 0)),
        scratch_shapes=[pltpu.VMEM((2, tm * ROW_SPLIT, LANES), F32), pltpu.SemaphoreType.DMA(())],
    )
    return pl.pallas_call(
        functools.partial(_combine_kernel, final_norm=final_norm),
        grid_spec=grid_spec,
        out_shape=jax.ShapeDtypeStruct((t, D_MODEL), F32),
        compiler_params=_params("arbitrary"),
        name="moe_combine",
    )(dest, x1, route, ys, norm_w.reshape(1, D_MODEL).astype(F32))


def _moe(x1, h, route, counts, w_gate, w_up, w_down, layer, norm_w, final_norm):
    t = x1.shape[0]
    n_blocks = -(-(2 * t) // MOE_BLOCK) + N_EXPERTS
    n_slots = n_blocks * MOE_BLOCK
    cnt = counts[0, N_EXPERT_GROUPS:N_EXPERT_GROUPS + N_EXPERTS].astype(jnp.int32)
    padded = ((cnt + MOE_BLOCK - 1) // MOE_BLOCK) * MOE_BLOCK
    pad_end = jnp.cumsum(padded)
    pad_start = pad_end - padded
    expert = route[:, 0:2].astype(jnp.int32)
    rank = route[:, 4:6].astype(jnp.int32)
    experts = jnp.arange(N_EXPERTS, dtype=jnp.int32)
    dest = (jnp.sum(jnp.where(expert[..., None] == experts, pad_start, 0), axis=-1) + rank).reshape(-1)
    blocks = jnp.arange(n_blocks, dtype=jnp.int32)
    block_expert = jnp.minimum(jnp.sum(pad_end[None, :] <= (blocks * MOE_BLOCK)[:, None], axis=1),
                               N_EXPERTS - 1).astype(jnp.int32)
    starts_run = jnp.concatenate([jnp.ones((1,), bool), block_expert[1:] != block_expert[:-1]])
    block_run = (jnp.cumsum(starts_run) - 1).astype(jnp.int32)
    later_start = starts_run[None, :] & (blocks[None, :] > blocks[:, None])
    next_expert = jnp.where(jnp.any(later_start, axis=1),
                            block_expert[jnp.argmax(later_start, axis=1)], -1).astype(jnp.int32)
    used = (pad_end[-1:] // MOE_BLOCK).astype(jnp.int32)
    tail = blocks[n_blocks - N_EXPERTS:]
    zero_blocks = jnp.concatenate([jnp.where(padded > 0, pad_end // MOE_BLOCK - 1, -1),
                                   jnp.where(tail >= used, tail, -1)]).astype(jnp.int32)
    xs = _dispatch(h, dest, zero_blocks, n_slots)
    ys = _experts(xs, block_expert, block_run, next_expert, used, w_gate, w_up, w_down, layer)
    return _combine(x1, route, ys, dest, norm_w, final_norm)


def kernel(x, norm_mix, w_in, s5_lam_re, s5_lam_im, s5_log_dt, s5_b_re, s5_b_im, s5_c_re, s5_c_im,
           s5_d, s5_w_glu, s5_b_glu, gdn_conv_w, gdn_a_log, gdn_dt_bias, gdn_norm_w, w_out, norm_ffn,
           router_w_group, router_b_group, router_w_expert, router_b_expert,
           expert_w_gate, expert_w_up, expert_w_down, norm_final):
    bsz, seq, d = x.shape
    depth = norm_mix.shape[0]
    xt = x.astype(F32).reshape(bsz * seq, d)
    tables = _s5_tables(s5_lam_re, s5_lam_im, s5_log_dt, s5_b_re, s5_b_im, s5_c_re, s5_c_im)
    for i in range(depth):
        u, z, q, k, v, gates = _inproj(xt, norm_mix[i], w_in[i], gdn_conv_w[i], gdn_a_log[i],
                                       gdn_dt_bias[i])
        ys = _s5_mix(u, tables, i)
        gates_t = (gates[:, :6 * DN_HEADS].reshape(-1, DN_CHUNK, 6 * DN_HEADS).transpose(0, 2, 1))
        o_f, o_b = _dn_mix(q, k, v, gates, gates_t)
        x1, h, route, counts = _post(xt, u, ys, o_f, o_b, z, s5_d[i], s5_w_glu[i], s5_b_glu[i],
                                     gdn_norm_w[i], w_out[i], norm_ffn[i], router_w_group[i],
                                     router_b_group[i], router_w_expert[i], router_b_expert[i])
        xt = _moe(x1, h, route, counts, expert_w_gate, expert_w_up, expert_w_down, i,
                  norm_final, i == depth - 1)
    return xt.reshape(bsz, seq, d)
```

```python
import functools
import math

import jax
import jax.numpy as jnp
import numpy as np
from jax import lax
from jax.experimental import pallas as pl
from jax.experimental.pallas import tpu as pltpu

F32 = jnp.float32
BF16 = jnp.bfloat16
HIGHEST = lax.Precision.HIGHEST

D_MODEL = 1024
S5_WIDTH = 512
S5_GROUP = 16
S5_GROUPS = 32
S5_STATE = 64
S5_MAX_RE = -1e-4
DN_HEADS = 4
DN_HEAD_DIM = 128
DN_WIDTH = 512
DN_CONV = 5
DN_CHUNK = 64
N_EXPERT_GROUPS = 4
EXPERTS_PER_GROUP = 8
N_EXPERTS = 32
D_EXPERT = 512
NORM_EPS = 1e-6

LANES = 128
SUBLANES = 8
VMEM_LIMIT = 56 * 1024 * 1024

S5_CHUNK = 16
S5_TILE = 128
ROW_TILE = 512
DN_STEP_CHUNKS = 16
MOE_BLOCK = 256
MOE_TOK_TILE = 256


def _params(*sem):
    return pltpu.CompilerParams(dimension_semantics=sem, vmem_limit_bytes=VMEM_LIMIT)


def _silu(x):
    half = 0.5 * x
    return half + half * jnp.tanh(half)


def _sigmoid(x):
    return 0.5 + 0.5 * jnp.tanh(0.5 * x)


def _split_weight(w):
    hi = w.astype(BF16)
    lo = (w - hi.astype(F32)).astype(BF16)
    return jnp.concatenate([hi, lo], axis=1)


def _dot_split(a, w_split):
    a_hi = a.astype(BF16)
    a_lo = (a - a_hi.astype(F32)).astype(BF16)
    p = jnp.dot(a_hi, w_split, preferred_element_type=F32)
    q = jnp.dot(a_lo, w_split[:, :LANES], preferred_element_type=F32)
    return p[:, :LANES] + p[:, LANES:] + q


def _inproj_kernel(x_ref, xp_ref, xn_ref, nw_ref, w_ref, wab_ref, cw_ref, gp_ref, tri_ref,
                   u_ref, z_ref, q_ref, k_ref, v_ref, gate_ref, ext):
    i = pl.program_id(0)
    tm = x_ref.shape[0]
    pad = DN_CONV // 2

    def norm(x):
        return x * lax.rsqrt(jnp.mean(x * x, axis=-1, keepdims=True) + NORM_EPS) * nw_ref[...]

    h = norm(x_ref[...])
    h_prev = norm(jnp.where(i > 0, xp_ref[...], 0.0))
    h_next = norm(jnp.where(i < pl.num_programs(0) - 1, xn_ref[...], 0.0))
    hb = h.astype(BF16)
    for blk in range(S5_WIDTH // LANES):
        u_ref[blk] = jnp.dot(hb, w_ref[:, blk * LANES:(blk + 1) * LANES], preferred_element_type=F32)
    z_ref[...] = jnp.dot(hb, w_ref[:, S5_WIDTH + 3 * DN_WIDTH:S5_WIDTH + 4 * DN_WIDTH],
                         preferred_element_type=F32)
    h_ext = jnp.concatenate([h_prev, h, h_next], axis=0).astype(BF16)
    ext[...] = jnp.dot(h_ext, w_ref[:, S5_WIDTH:S5_WIDTH + 3 * DN_WIDTH], preferred_element_type=F32)

    outs = (q_ref, k_ref, v_ref)
    for part in range(3):
        cols = slice(part * DN_WIDTH, (part + 1) * DN_WIDTH)
        acc = ext[pl.ds(SUBLANES - pad, tm), cols] * cw_ref[0:1, cols]
        for tap in range(1, DN_CONV):
            acc = acc + ext[pl.ds(SUBLANES - pad + tap, tm), cols] * cw_ref[tap:tap + 1, cols]
        act = _silu(acc)
        if part == 2:
            v_ref[...] = act.astype(BF16)
            continue
        scale = DN_HEAD_DIM ** -0.5 if part == 0 else 1.0
        for hd in range(DN_HEADS):
            hs = slice(hd * DN_HEAD_DIM, (hd + 1) * DN_HEAD_DIM)
            xh = act[:, hs]
            inv = lax.rsqrt(jnp.sum(xh * xh, axis=-1, keepdims=True) + NORM_EPS)
            outs[part][:, hs] = (xh * inv * scale).astype(BF16)

    ab = _dot_split(h, wab_ref[...])
    lane = lax.broadcasted_iota(jnp.int32, ab.shape, 1)
    pre = ab + gp_ref[1:2, :]
    softplus = jnp.maximum(pre, 0.0) + jnp.log1p(jnp.exp(-jnp.abs(pre)))
    n_gate = 2 * DN_HEADS
    g = jnp.where(lane < n_gate, gp_ref[0:1, :] * softplus, 0.0)
    hi = g.astype(BF16).astype(F32)
    r1 = g - hi
    mid = r1.astype(BF16).astype(F32)
    lo = (r1 - mid).astype(BF16).astype(F32)
    pieces = (hi + pltpu.roll(mid, n_gate, 1) + pltpu.roll(lo, 2 * n_gate, 1)).astype(BF16)
    pref = jnp.dot(tri_ref[0], pieces, preferred_element_type=F32)
    suff = jnp.dot(tri_ref[1], pieces, preferred_element_type=F32)
    part = jnp.where(jnp.bitwise_and(lane, n_gate - 1) < DN_HEADS, pref, suff)
    gsum = part + pltpu.roll(part, LANES - n_gate, 1) + pltpu.roll(part, LANES - 2 * n_gate, 1)
    gate_ref[...] = jnp.where(lane < n_gate, g, jnp.where(
        lane < 2 * n_gate, jax.nn.sigmoid(ab), jnp.where(
            lane < 3 * n_gate, pltpu.roll(gsum, 2 * n_gate, 1), 0.0)))


def _inproj(x, norm_w, w_in, conv_w, a_log, dt_bias):
    t = x.shape[0]
    n_main = S5_WIDTH + 4 * DN_WIDTH
    w_main = w_in[:, :n_main].astype(BF16)
    w_ab = _split_weight(jnp.pad(w_in[:, n_main:].astype(F32), ((0, 0), (0, LANES - 4 * DN_HEADS))))
    gp = jnp.zeros((SUBLANES, LANES), F32)
    gp = gp.at[0, :2 * DN_HEADS].set(-jnp.exp(a_log.astype(F32)).reshape(-1))
    gp = gp.at[1, :2 * DN_HEADS].set(dt_bias.astype(F32).reshape(-1))
    cw = jnp.pad(conv_w.astype(F32), ((0, SUBLANES - DN_CONV), (0, 0)))
    tm = min(ROW_TILE, t)
    nb = tm // SUBLANES
    last = t // SUBLANES - 1
    step = np.arange(tm)
    same = (step[:, None] // DN_CHUNK) == (step[None, :] // DN_CHUNK)
    tri = jnp.asarray(np.stack([same & (step[:, None] >= step[None, :]),
                                same & (step[:, None] <= step[None, :])]), BF16)
    row = lambda i: (i, 0)
    const = lambda i: (0, 0)
    return pl.pallas_call(
        _inproj_kernel,
        grid=(t // tm,),
        in_specs=[
            pl.BlockSpec((tm, D_MODEL), row),
            pl.BlockSpec((SUBLANES, D_MODEL), lambda i: (jnp.maximum(i * nb - 1, 0), 0)),
            pl.BlockSpec((SUBLANES, D_MODEL), lambda i: (jnp.minimum((i + 1) * nb, last), 0)),
            pl.BlockSpec((1, D_MODEL), const),
            pl.BlockSpec((D_MODEL, n_main), const),
            pl.BlockSpec((D_MODEL, 2 * LANES), const),
            pl.BlockSpec((SUBLANES, 3 * DN_WIDTH), const),
            pl.BlockSpec((SUBLANES, LANES), const),
            pl.BlockSpec((2, tm, tm), lambda i: (0, 0, 0)),
        ],
        out_specs=[
            pl.BlockSpec((S5_WIDTH // LANES, tm, LANES), lambda i: (0, i, 0)),
            pl.BlockSpec((tm, DN_WIDTH), row),
            pl.BlockSpec((tm, DN_WIDTH), row),
            pl.BlockSpec((tm, DN_WIDTH), row),
            pl.BlockSpec((tm, DN_WIDTH), row),
            pl.BlockSpec((tm, LANES), row),
        ],
        out_shape=[
            jax.ShapeDtypeStruct((S5_WIDTH // LANES, t, LANES), F32),
            jax.ShapeDtypeStruct((t, DN_WIDTH), F32),
            jax.ShapeDtypeStruct((t, DN_WIDTH), BF16),
            jax.ShapeDtypeStruct((t, DN_WIDTH), BF16),
            jax.ShapeDtypeStruct((t, DN_WIDTH), BF16),
            jax.ShapeDtypeStruct((t, LANES), F32),
        ],
        scratch_shapes=[pltpu.VMEM((tm + 2 * SUBLANES, 3 * DN_WIDTH), F32)],
        compiler_params=_params("parallel"),
        name="inproj",
    )(x, x, x, norm_w.reshape(1, D_MODEL), w_main, w_ab, cw, gp, tri)


def _toeplitz_kernel(k_ref, o_ref):
    c_len = S5_CHUNK
    lane = lax.broadcasted_iota(jnp.int32, (S5_GROUP, LANES), 1)
    for b in range(k_ref.shape[0]):
        tiles = [k_ref[b, :, t * LANES:(t + 1) * LANES] for t in range(k_ref.shape[2] // LANES)]
        for j in range(c_len):
            first, shift = divmod((c_len - 1 - j) * S5_GROUP, LANES)
            for half in range(2):
                piece = tiles[first + half]
                if shift:
                    piece = jnp.where(lane < LANES - shift,
                                      pltpu.roll(piece, LANES - shift, 1),
                                      pltpu.roll(tiles[first + half + 1], LANES - shift, 1))
                o_ref[b, j * S5_GROUP:(j + 1) * S5_GROUP, half * LANES:(half + 1) * LANES] = piece.astype(BF16)


def _toeplitz(kpad):
    n, q, width = kpad.shape
    per_step = 8
    return pl.pallas_call(
        _toeplitz_kernel,
        grid=(n // per_step,),
        in_specs=[pl.BlockSpec((per_step, q, width), lambda i: (i, 0, 0))],
        out_specs=pl.BlockSpec((per_step, S5_CHUNK * q, 256), lambda i: (i, 0, 0)),
        out_shape=jax.ShapeDtypeStruct((n, S5_CHUNK * q, 256), BF16),
        compiler_params=_params("parallel"),
        name="s5_toeplitz",
    )(kpad)


def _s5_tables(lam_re, lam_im, log_dt, b_re, b_im, c_re, c_im):
    c_len = S5_CHUNK
    lr = jnp.minimum(lam_re.astype(F32), S5_MAX_RE)
    li = lam_im.astype(F32)
    dt = jnp.exp(log_dt.astype(F32))[..., None]
    zr, zi = lr * dt, li * dt
    e1 = jnp.exp(zr)
    ar, ai = e1 * jnp.cos(zi), e1 * jnp.sin(zi)
    den = lr * lr + li * li
    nr, ni = ar - 1.0, ai
    fr = (nr * lr + ni * li) / den
    fi = (ni * lr - nr * li) / den
    bbr = (fr[..., None] * b_re - fi[..., None] * b_im).swapaxes(-1, -2)
    bbi = (fr[..., None] * b_im + fi[..., None] * b_re).swapaxes(-1, -2)
    tau = jnp.arange(c_len + 1, dtype=F32)[:, None]
    mag = jnp.exp(tau * zr[..., None, :])
    pr = mag * jnp.cos(tau * zi[..., None, :])
    pi = mag * jnp.sin(tau * zi[..., None, :])
    prq, piq = pr[..., :, None, :], pi[..., :, None, :]
    m_r = prq * bbr[..., None, :, :] - piq * bbi[..., None, :, :]
    m_i = prq * bbi[..., None, :, :] + piq * bbr[..., None, :, :]
    kern = (jnp.einsum('ldgpn,ldgtqn->ldgqtp', c_re, m_r[..., :c_len, :, :], precision=HIGHEST)
            - jnp.einsum('ldgpn,ldgtqn->ldgqtp', c_im, m_i[..., :c_len, :, :], precision=HIGHEST))

    def per_direction(x, axis, fwd_flipped):
        f, b = x[:, 0], x[:, 1]
        f, b = (jnp.flip(f, axis), b) if fwd_flipped else (f, jnp.flip(b, axis))
        return jnp.stack([f, b], axis=1)

    padded = jnp.pad(kern, ((0, 0),) * 4 + ((c_len - 1, 0), (0, 0)))
    padded = per_direction(padded, 3, False)
    padded = jnp.pad(padded, ((0, 0),) * 4 + ((0, 1), (0, 0)))
    lead = padded.shape[:3]
    wt = _toeplitz(padded.reshape(-1, S5_GROUP, 2 * c_len * S5_GROUP)).reshape(lead + (256, 256))
    er = per_direction(m_r[..., :c_len, :, :], 2, True).reshape(wt.shape[:3] + (256, S5_STATE))
    ei = per_direction(m_i[..., :c_len, :, :], 2, True).reshape(wt.shape[:3] + (256, S5_STATE))
    we = jnp.concatenate([er, ei, ei, er], axis=-1)
    c_rt, c_it = c_re.swapaxes(-1, -2)[..., None, :], c_im.swapaxes(-1, -2)[..., None, :]
    p_rt, p_it = pr.swapaxes(-1, -2)[..., 1:, None], pi.swapaxes(-1, -2)[..., 1:, None]
    sr = per_direction(c_rt * p_rt - c_it * p_it, 3, False).reshape(wt.shape[:3] + (S5_STATE, 256))
    si = per_direction(c_rt * p_it + c_it * p_rt, 3, False).reshape(wt.shape[:3] + (S5_STATE, 256))
    ws = jnp.concatenate([sr, -si], axis=3)
    a_r, a_i = pr[..., c_len, :], pi[..., c_len, :]
    coef = jnp.stack([jnp.concatenate([a_r, a_r], -1),
                      jnp.concatenate([-a_i, a_i], -1),
                      jnp.concatenate([a_i, -a_i], -1)], axis=2)
    return wt.astype(BF16), we.astype(BF16), ws.astype(BF16), coef


def _block_transpose8(xs, lane):
    for k in (2, 1, 0):
        shift = S5_GROUP << k
        bit = jnp.bitwise_and(lax.shift_right_logical(lane, 4 + k), 1)
        new = list(xs)
        for a in range(8):
            if (a >> k) & 1:
                continue
            b = a + (1 << k)
            if 2 * shift == LANES:
                both = pltpu.roll(jnp.where(bit == 0, xs[b], xs[a]), shift, 1)
                new[a] = jnp.where(bit == 0, xs[a], both)
                new[b] = jnp.where(bit == 1, xs[b], both)
            else:
                new[a] = jnp.where(bit == 0, xs[a], pltpu.roll(xs[b], shift, 1))
                new[b] = jnp.where(bit == 1, xs[b], pltpu.roll(xs[a], LANES - shift, 1))
        xs = new
    return xs


def _s5_scan_kernel(u_ref, wt_ref, we_ref, ws_ref, coef_ref, *rest, reverse, add_prev):
    if add_prev:
        prev_ref, y_ref, ug_scr, yg_scr, e_scr, es_scr, s_scr, carry = rest
    else:
        y_ref, ug_scr, yg_scr, e_scr, es_scr, s_scr, carry = rest
    n_rows = ug_scr.shape[1]
    n_blk = S5_WIDTH // LANES
    per_blk = LANES // S5_GROUP
    halves = S5_CHUNK // per_blk

    @pl.when(pl.program_id(0) == 0)
    def _():
        carry[...] = jnp.zeros_like(carry)

    lane = lax.broadcasted_iota(jnp.int32, (n_rows, LANES), 1)

    for blk in range(n_blk):
        for half in range(halves):
            xs = [u_ref[blk, pl.ds(half * per_blk + jl, n_rows, stride=S5_CHUNK), :].astype(BF16)
                  for jl in range(per_blk)]
            ys = _block_transpose8(xs, lane)
            for gl in range(per_blk):
                ug_scr[blk * per_blk + gl, :, half * LANES:(half + 1) * LANES] = ys[gl].astype(BF16)

    for g in range(S5_GROUPS):
        e = jnp.dot(ug_scr[g], we_ref[g], preferred_element_type=F32)
        e_scr[g * n_rows:(g + 1) * n_rows, :] = e[:, :LANES]
        es_scr[g * n_rows:(g + 1) * n_rows, :] = e[:, LANES:]

    c1, c2, c3 = coef_ref[0], coef_ref[1], coef_ref[2]

    def step(i, vs):
        v, vp = vs
        r = (n_rows - 1 - i) if reverse else i
        rows = pl.ds(r, S5_GROUPS, stride=n_rows)
        s_scr[rows, :] = v
        v_new = v * c1 + vp * c2 + e_scr[rows, :]
        vp_new = vp * c1 + v * c3 + es_scr[rows, :]
        return v_new, vp_new

    v, vp = lax.fori_loop(0, n_rows, step, (carry[0], carry[1]), unroll=8)
    carry[0] = v
    carry[1] = vp

    for blk in range(n_blk):
        for gl in range(per_blk):
            g = blk * per_blk + gl
            s_in = s_scr[g * n_rows:(g + 1) * n_rows, :].astype(BF16)
            yg_scr[gl] = (jnp.dot(ug_scr[g], wt_ref[g], preferred_element_type=F32)
                          + jnp.dot(s_in, ws_ref[g], preferred_element_type=F32))
        for half in range(halves):
            zs = [yg_scr[gl, :, half * LANES:(half + 1) * LANES] for gl in range(per_blk)]
            ws = _block_transpose8(zs, lane)
            for tl in range(per_blk):
                rows = pl.ds(half * per_blk + tl, n_rows, stride=S5_CHUNK)
                out = ws[tl]
                if add_prev:
                    out = out + prev_ref[blk, rows, :]
                y_ref[blk, rows, :] = out


def _s5_direction(u4, tables, prev, reverse, layer):
    n_blk, t, _ = u4.shape
    n_chunks = t // S5_CHUNK
    rows = min(S5_TILE, n_chunks)
    n_tiles = n_chunks // rows
    wt, we, ws, coef = tables
    tile = (lambda i: (0, n_tiles - 1 - i, 0)) if reverse else (lambda i: (0, i, 0))
    table = lambda i: (layer, int(reverse), 0, 0, 0)
    once = pl.Buffered(1)
    in_specs = [
        pl.BlockSpec((n_blk, rows * S5_CHUNK, LANES), tile),
        pl.BlockSpec((None, None, S5_GROUPS, 256, 256), table, pipeline_mode=once),
        pl.BlockSpec((None, None, S5_GROUPS, 256, 256), table, pipeline_mode=once),
        pl.BlockSpec((None, None, S5_GROUPS, LANES, 256), table, pipeline_mode=once),
        pl.BlockSpec((None, None, 3, S5_GROUPS, LANES), table, pipeline_mode=once),
    ]
    args = [u4, wt, we, ws, coef]
    if prev is not None:
        in_specs.append(pl.BlockSpec((n_blk, rows * S5_CHUNK, LANES), tile))
        args.append(prev)
    return pl.pallas_call(
        functools.partial(_s5_scan_kernel, reverse=reverse, add_prev=prev is not None),
        grid=(n_tiles,),
        in_specs=in_specs,
        out_specs=pl.BlockSpec((n_blk, rows * S5_CHUNK, LANES), tile),
        out_shape=jax.ShapeDtypeStruct((n_blk, t, LANES), F32),
        scratch_shapes=[
            pltpu.VMEM((S5_GROUPS, rows, 256), BF16),
            pltpu.VMEM((LANES // S5_GROUP, rows, 256), F32),
            pltpu.VMEM((S5_GROUPS * rows, LANES), F32),
            pltpu.VMEM((S5_GROUPS * rows, LANES), F32),
            pltpu.VMEM((S5_GROUPS * rows, LANES), F32),
            pltpu.VMEM((2, S5_GROUPS, LANES), F32),
        ],
        compiler_params=_params("arbitrary"),
        name="s5_bwd" if reverse else "s5_fwd",
    )(*args)


def _s5_mix(u4, tables, layer):
    y = _s5_direction(u4, tables, None, False, layer)
    return _s5_direction(u4, tables, y, True, layer)


def _bmm(a, b):
    return lax.dot_general(a, b, (((2,), (1,)), ((0,), (0,))), preferred_element_type=F32)


def _dn_kernel(qf, kf, vf, gf, gtf, qb, kb_, vb, gb, gtb, of_ref, ob_ref,
               state, wq_scr, kdt_scr, at_scr, u_scr, gam_scr):
    c_len = DN_CHUNK
    n_ch = gtf.shape[0]
    n_gate = 2 * DN_HEADS
    n_chain = 2 * DN_HEADS

    @pl.when(pl.program_id(0) == 0)
    def _():
        state[...] = jnp.zeros_like(state)

    ri = lax.broadcasted_iota(jnp.int32, (c_len, c_len), 0)
    ci = lax.broadcasted_iota(jnp.int32, (c_len, c_len), 1)
    eye = jnp.where(ri == ci, 1.0, 0.0).astype(F32)
    dirs = ((qf, kf, vf, gf, gtf, ri >= ci, ri > ci, c_len - 1),
            (qb, kb_, vb, gb, gtb, ri <= ci, ri < ci, 0))

    for d, (q_ref, k_ref, v_ref, gate_ref, gate_t_ref, incl, strict, last) in enumerate(dirs):
        gates = gate_ref[...].reshape(n_ch, c_len, LANES)
        gates_t = gate_t_ref[...]
        for h in range(DN_HEADS):
            idx = d * DN_HEADS + h
            hs = slice(h * DN_HEAD_DIM, (h + 1) * DN_HEAD_DIM)
            gcol = gates[:, :, 2 * n_gate + idx:2 * n_gate + idx + 1]
            bcol = gates[:, :, n_gate + idx:n_gate + idx + 1]
            grow = gates_t[:, 2 * n_gate + idx:2 * n_gate + idx + 1, :]
            glast = grow[:, :, last:last + 1]
            qb = q_ref[:, hs].reshape(n_ch, c_len, DN_HEAD_DIM)
            kb = k_ref[:, hs].reshape(n_ch, c_len, DN_HEAD_DIM)
            qh, kh = qb.astype(F32), kb.astype(F32)
            vh = v_ref[:, hs].reshape(n_ch, c_len, DN_HEAD_DIM).astype(F32)
            qk_kk = lax.dot_general(jnp.concatenate([qb, kb], axis=1), kb,
                                    (((2,), (2,)), ((0,), (0,))), preferred_element_type=F32)
            qk, kk = qk_kk[:, :c_len], qk_kk[:, c_len:]
            decay = jnp.where(incl, jnp.exp(jnp.where(incl, gcol - grow, 0.0)), 0.0)
            a_mat = jnp.where(strict, bcol * kk * decay, 0.0)
            pw = -a_mat
            inv = eye + pw
            for _ in range(5):
                pwb = pw.astype(BF16)
                pw = _bmm(pwb, pwb)
                inv = inv + _bmm(inv.astype(BF16), pw.astype(BF16))
            egc = jnp.exp(gcol)
            rhs = jnp.concatenate([vh * bcol, kh * (bcol * egc)], axis=2).astype(BF16)
            uw = _bmm(inv.astype(BF16), rhs)
            wq = jnp.concatenate([uw[:, :, DN_HEAD_DIM:], qh * egc], axis=1).astype(BF16)
            attn = jnp.where(incl, qk * decay, 0.0).astype(BF16)
            k_dec_t = jnp.swapaxes(kh * jnp.exp(glast - gcol), 1, 2).astype(BF16)
            gamma = jnp.broadcast_to(jnp.exp(glast), (n_ch, 1, LANES))
            for c in range(n_ch):
                slot = (c if d == 0 else n_ch - 1 - c) * n_chain + idx
                wq_scr[slot] = wq[c]
                kdt_scr[slot] = k_dec_t[c]
                at_scr[slot] = attn[c]
                u_scr[slot] = uw[c, :, :DN_HEAD_DIM]
                gam_scr[slot] = gamma[c]

    for step in range(n_ch):
        grp = slice(step * n_chain, (step + 1) * n_chain)
        s = state[...]
        wq_s = _bmm(wq_scr[grp], s.astype(BF16))
        v_nb = (u_scr[grp] - wq_s[:, :c_len]).astype(BF16)
        o = wq_s[:, c_len:] + _bmm(at_scr[grp], v_nb)
        state[...] = s * gam_scr[grp] + _bmm(kdt_scr[grp], v_nb)
        for d, o_ref in enumerate((of_ref, ob_ref)):
            c = step if d == 0 else n_ch - 1 - step
            for h in range(DN_HEADS):
                o_ref[c * c_len:(c + 1) * c_len, h * DN_HEAD_DIM:(h + 1) * DN_HEAD_DIM] = (
                    o[d * DN_HEADS + h].astype(BF16))


def _dn_mix(q, k, v, gates, gates_t):
    t = q.shape[0]
    n_chunks = t // DN_CHUNK
    n_ch = min(DN_STEP_CHUNKS, n_chunks)
    n_steps = n_chunks // n_ch
    rows = n_ch * DN_CHUNK
    n_inst = n_ch * 2 * DN_HEADS
    specs = []
    for row, row3 in ((lambda i: (i, 0), lambda i: (i, 0, 0)),
                      (lambda i: (n_steps - 1 - i, 0), lambda i: (n_steps - 1 - i, 0, 0))):
        specs += [
            pl.BlockSpec((rows, DN_WIDTH), row),
            pl.BlockSpec((rows, DN_WIDTH), row),
            pl.BlockSpec((rows, DN_WIDTH), row),
            pl.BlockSpec((rows, LANES), row),
            pl.BlockSpec((n_ch, 6 * DN_HEADS, DN_CHUNK), row3),
        ]
    return pl.pallas_call(
        _dn_kernel,
        grid=(n_steps,),
        in_specs=specs,
        out_specs=[
            pl.BlockSpec((rows, DN_WIDTH), lambda i: (i, 0)),
            pl.BlockSpec((rows, DN_WIDTH), lambda i: (n_steps - 1 - i, 0)),
        ],
        out_shape=[jax.ShapeDtypeStruct((t, DN_WIDTH), BF16), jax.ShapeDtypeStruct((t, DN_WIDTH), BF16)],
        scratch_shapes=[
            pltpu.VMEM((2 * DN_HEADS, DN_HEAD_DIM, DN_HEAD_DIM), F32),
            pltpu.VMEM((n_inst, 2 * DN_CHUNK, DN_HEAD_DIM), BF16),
            pltpu.VMEM((n_inst, DN_HEAD_DIM, DN_CHUNK), BF16),
            pltpu.VMEM((n_inst, DN_CHUNK, DN_CHUNK), BF16),
            pltpu.VMEM((n_inst, DN_CHUNK, DN_HEAD_DIM), F32),
            pltpu.VMEM((n_inst, 1, LANES), F32),
        ],
        compiler_params=_params("arbitrary"),
        name="dn_mix",
    )(q, k, v, gates, gates_t, q, k, v, gates, gates_t)


def _post_kernel(x_ref, u_ref, ys_ref, of_ref, ob_ref, z_ref, d_ref, wglu_ref, bglu_ref, nw_ref,
                 wout_ref, nffn_ref, wr_ref, br_ref, before_ref, x1_ref, h_ref, route_ref, cnt_ref, base):
    i = pl.program_id(0)
    tm = x_ref.shape[0]

    @pl.when(i == 0)
    def _():
        base[...] = jnp.zeros_like(base)

    y = jnp.concatenate([ys_ref[b] + d_ref[:, b * LANES:(b + 1) * LANES] * u_ref[b]
                         for b in range(S5_WIDTH // LANES)], axis=1)
    y = 0.5 * y * (1.0 + lax.erf(y * (2.0 ** -0.5)))
    gate = jnp.dot(y.astype(BF16), wglu_ref[...], preferred_element_type=F32) + bglu_ref[...]
    y_s5 = y * _sigmoid(gate)
    acc = x_ref[...] + jnp.dot(y_s5.astype(BF16), wout_ref[0:S5_WIDTH, :], preferred_element_type=F32)
    for h in range(DN_HEADS):
        hs = slice(h * DN_HEAD_DIM, (h + 1) * DN_HEAD_DIM)
        o = of_ref[:, hs].astype(F32) + ob_ref[:, hs].astype(F32)
        zh = z_ref[:, hs]
        o = o * lax.rsqrt(jnp.mean(o * o, axis=-1, keepdims=True) + NORM_EPS) * nw_ref[...]
        y_dn = o * _silu(zh)
        acc = acc + jnp.dot(y_dn.astype(BF16),
                            wout_ref[S5_WIDTH + h * DN_HEAD_DIM:S5_WIDTH + (h + 1) * DN_HEAD_DIM, :],
                            preferred_element_type=F32)
    x1_ref[...] = acc
    hn = acc * lax.rsqrt(jnp.mean(acc * acc, axis=-1, keepdims=True) + NORM_EPS) * nffn_ref[...]
    _matrix_to_rows(h_ref, hn)

    logits = _dot_split(hn, wr_ref[...]) + br_ref[...]
    lane_i = lax.broadcasted_iota(jnp.int32, logits.shape, 1)
    lane = lane_i.astype(F32)
    neg = jnp.float32(-jnp.inf)
    big = jnp.float32(LANES)
    gl = jnp.where(lane_i < N_EXPERT_GROUPS, logits, neg)
    gmax = jnp.max(gl, axis=-1, keepdims=True)
    g_sel = jnp.min(jnp.where(gl == gmax, lane, big), axis=-1, keepdims=True)
    p_group = 1.0 / jnp.sum(jnp.exp(gl - gmax), axis=-1, keepdims=True)
    lo = N_EXPERT_GROUPS + g_sel * EXPERTS_PER_GROUP
    el = jnp.where((lane >= lo) & (lane < lo + EXPERTS_PER_GROUP), logits, neg)
    top1 = jnp.max(el, axis=-1, keepdims=True)
    idx1 = jnp.min(jnp.where(el == top1, lane, big), axis=-1, keepdims=True)
    el2 = jnp.where(lane == idx1, neg, el)
    top2 = jnp.max(el2, axis=-1, keepdims=True)
    idx2 = jnp.min(jnp.where(el2 == top2, lane, big), axis=-1, keepdims=True)
    e21 = jnp.exp(top2 - top1)
    w1 = p_group / (1.0 + e21)
    w2 = w1 * e21
    oh1 = jnp.where(lane == idx1, 1.0, 0.0).astype(F32)
    oh2 = jnp.where(lane == idx2, 1.0, 0.0).astype(F32)
    ohs = oh1 + oh2
    prior = jnp.dot(before_ref[...], ohs.astype(BF16), preferred_element_type=F32) + base[0:1, :]
    rank1 = jnp.sum(oh1 * prior, axis=-1, keepdims=True)
    rank2 = jnp.sum(oh2 * prior, axis=-1, keepdims=True)
    base[0:1, :] = base[0:1, :] + jnp.sum(ohs, axis=0, keepdims=True)
    e1 = idx1 - N_EXPERT_GROUPS
    e2 = idx2 - N_EXPERT_GROUPS
    route = jnp.where(lane_i == 0, e1, jnp.where(lane_i == 1, e2, jnp.where(lane_i == 2, w1, jnp.where(
        lane_i == 3, w2, jnp.where(lane_i == 4, rank1, jnp.where(lane_i == 5, rank2, 0.0))))))
    route_ref[...] = route
    cnt_ref[...] = base[...]


def _post(x, u, ys, o_f, o_b, z, s5_d, w_glu, b_glu, dn_norm_w, w_out, norm_ffn, w_rg, b_rg, w_re, b_re):
    t = x.shape[0]
    tm = min(ROW_TILE, t)
    wr = jnp.concatenate([w_rg, w_re.transpose(1, 0, 2).reshape(D_MODEL, N_EXPERTS)], axis=1)
    wr = _split_weight(jnp.pad(wr.astype(F32), ((0, 0), (0, LANES - N_EXPERT_GROUPS - N_EXPERTS))))
    br = jnp.pad(jnp.concatenate([b_rg, b_re.reshape(-1)]).astype(F32),
                 (0, LANES - N_EXPERT_GROUPS - N_EXPERTS)).reshape(1, LANES)
    row = lambda i: (i, 0)
    const = lambda i: (0, 0)
    return pl.pallas_call(
        _post_kernel,
        grid=(t // tm,),
        in_specs=[
            pl.BlockSpec((tm, D_MODEL), row),
            pl.BlockSpec((S5_WIDTH // LANES, tm, LANES), lambda i: (0, i, 0)),
            pl.BlockSpec((S5_WIDTH // LANES, tm, LANES), lambda i: (0, i, 0)),
            pl.BlockSpec((tm, DN_WIDTH), row),
            pl.BlockSpec((tm, DN_WIDTH), row),
            pl.BlockSpec((tm, DN_WIDTH), row),
            pl.BlockSpec((1, S5_WIDTH), const),
            pl.BlockSpec((S5_WIDTH, S5_WIDTH), const),
            pl.BlockSpec((1, S5_WIDTH), const),
            pl.BlockSpec((1, DN_HEAD_DIM), const),
            pl.BlockSpec((D_MODEL, D_MODEL), const),
            pl.BlockSpec((1, D_MODEL), const),
            pl.BlockSpec((D_MODEL, 2 * LANES), const),
            pl.BlockSpec((1, LANES), const),
            pl.BlockSpec((tm, tm), const),
        ],
        out_specs=[
            pl.BlockSpec((tm, D_MODEL), row),
            pl.BlockSpec((tm * ROW_SPLIT, LANES), row),
            pl.BlockSpec((tm, LANES), row),
            pl.BlockSpec((SUBLANES, LANES), const),
        ],
        out_shape=[
            jax.ShapeDtypeStruct((t, D_MODEL), F32),
            jax.ShapeDtypeStruct((t * ROW_SPLIT, LANES), U32),
            jax.ShapeDtypeStruct((t, LANES), F32),
            jax.ShapeDtypeStruct((SUBLANES, LANES), F32),
        ],
        scratch_shapes=[pltpu.VMEM((SUBLANES, LANES), F32)],
        compiler_params=_params("arbitrary"),
        name="mixer_post",
    )(x, u, ys, o_f, o_b, z, s5_d.reshape(1, -1).astype(F32), w_glu.astype(BF16),
      b_glu.reshape(1, -1).astype(F32), dn_norm_w.reshape(1, -1).astype(F32), w_out.astype(BF16),
      norm_ffn.reshape(1, -1).astype(F32), wr, br,
      jnp.asarray(np.tril(np.ones((tm, tm), np.float32), -1), BF16))


ROW_SPLIT = D_MODEL // (2 * LANES)
U32 = jnp.uint32


def _rows_to_matrix(ref, n_rows, lead=None):
    low, high = [], []
    for s in range(ROW_SPLIT):
        idx = pl.ds(s, n_rows, stride=ROW_SPLIT)
        word = ref[idx, :] if lead is None else ref[lead, idx, :]
        low.append(pltpu.bitcast(word << 16, F32))
        high.append(pltpu.bitcast(word & jnp.uint32(0xFFFF0000), F32))
    return jnp.concatenate(low + high, axis=1)


def _matrix_to_rows(ref, val):
    n_rows = val.shape[0]
    half = ROW_SPLIT * LANES
    bits = pltpu.bitcast(val.astype(BF16).astype(F32), U32)
    for s in range(ROW_SPLIT):
        low = bits[:, s * LANES:(s + 1) * LANES] >> 16
        high = bits[:, half + s * LANES:half + (s + 1) * LANES]
        ref[pl.ds(s, n_rows, stride=ROW_SPLIT), :] = high | low


def _row(ref, r):
    return ref.at[pl.ds(pl.multiple_of(r * ROW_SPLIT, ROW_SPLIT), ROW_SPLIT), :]


def _dispatch_kernel(dest_ref, zblk_ref, h_ref, xs_ref, zero_buf, sem, zsem):
    i = pl.program_id(0)
    tm = h_ref.shape[0] // ROW_SPLIT
    blk_rows = zero_buf.shape[0]

    @pl.when(i == 0)
    def _():
        zero_buf[...] = jnp.zeros_like(zero_buf)

        def zero_copy(j):
            start = pl.multiple_of(zblk_ref[j] * blk_rows, blk_rows)
            return pltpu.make_async_copy(zero_buf, xs_ref.at[pl.ds(start, blk_rows), :], zsem)

        def zstart(j, _):
            @pl.when(zblk_ref[j] >= 0)
            def _():
                zero_copy(j).start()
            return 0

        def zwait(j, _):
            @pl.when(zblk_ref[j] >= 0)
            def _():
                zero_copy(j).wait()
            return 0

        lax.fori_loop(0, zblk_ref.shape[0], zstart, 0)
        lax.fori_loop(0, zblk_ref.shape[0], zwait, 0)

    def copy(r, k):
        slot = dest_ref[2 * (i * tm + r) + k]
        return pltpu.make_async_copy(_row(h_ref, r), _row(xs_ref, slot), sem)

    def start(r, _):
        copy(r, 0).start(priority=0)
        copy(r, 1).start(priority=1)
        return 0

    def wait(r, _):
        copy(r, 0).wait()
        copy(r, 1).wait()
        return 0

    lax.fori_loop(0, tm, start, 0, unroll=8)
    lax.fori_loop(0, tm, wait, 0, unroll=8)


def _dispatch(h, dest, zero_blocks, n_slots):
    t = h.shape[0] // ROW_SPLIT
    tm = min(MOE_TOK_TILE, t)
    grid_spec = pltpu.PrefetchScalarGridSpec(
        num_scalar_prefetch=2,
        grid=(t // tm,),
        in_specs=[pl.BlockSpec((tm * ROW_SPLIT, LANES), lambda i, *_: (i, 0))],
        out_specs=pl.BlockSpec(memory_space=pl.ANY),
        scratch_shapes=[pltpu.VMEM((MOE_BLOCK * ROW_SPLIT, LANES), U32),
                        pltpu.SemaphoreType.DMA(()), pltpu.SemaphoreType.DMA(())],
    )
    return pl.pallas_call(
        _dispatch_kernel,
        grid_spec=grid_spec,
        out_shape=jax.ShapeDtypeStruct((n_slots * ROW_SPLIT, LANES), U32),
        compiler_params=_params("arbitrary"),
        name="moe_dispatch",
    )(dest, zero_blocks, h)


def _expert_kernel(be_ref, run_ref, nxt_ref, used_ref, xs_ref, wg_ref, wu_ref, wd_ref, ys_ref,
                   wg_buf, wu_buf, wd_buf, wgu_b, wd_b, wsem, *, layer):
    i = pl.program_id(0)
    blk = ys_ref.shape[0] // ROW_SPLIT

    def weight_copies(expert, slot):
        return (pltpu.make_async_copy(wg_ref.at[layer, expert], wg_buf.at[slot], wsem.at[slot]),
                pltpu.make_async_copy(wu_ref.at[layer, expert], wu_buf.at[slot], wsem.at[slot]),
                pltpu.make_async_copy(wd_ref.at[layer, expert], wd_buf.at[slot], wsem.at[slot]))

    run = run_ref[i]
    new_run = (i == 0) | (run != run_ref[jnp.maximum(i - 1, 0)])
    wslot = run % 2

    @pl.when(i == 0)
    def _():
        for c in weight_copies(be_ref[0], 0):
            c.start()

    @pl.when(new_run)
    def _():
        for c in weight_copies(be_ref[i], wslot):
            c.wait()

        @pl.when(nxt_ref[i] >= 0)
        def _():
            for c in weight_copies(nxt_ref[i], 1 - wslot):
                c.start()

        wgu_b[:, :D_EXPERT] = wg_buf[wslot].astype(BF16)
        wgu_b[:, D_EXPERT:] = wu_buf[wslot].astype(BF16)
        wd_b[...] = wd_buf[wslot].astype(BF16)

    @pl.when(i < used_ref[0])
    def _():
        xb = _rows_to_matrix(xs_ref, blk).astype(BF16)
        gu = jnp.dot(xb, wgu_b[...], preferred_element_type=F32)
        g, u = gu[:, :D_EXPERT], gu[:, D_EXPERT:]
        hid = (_silu(g) * u).astype(BF16)
        _matrix_to_rows(ys_ref, jnp.dot(hid, wd_b[...], preferred_element_type=F32))

    @pl.when(i >= used_ref[0])
    def _():
        ys_ref[...] = jnp.zeros_like(ys_ref)


def _experts(xs, block_expert, block_run, next_expert, used, w_gate, w_up, w_down, layer):
    n_slots = xs.shape[0] // ROW_SPLIT
    n_blocks = n_slots // MOE_BLOCK
    grid_spec = pltpu.PrefetchScalarGridSpec(
        num_scalar_prefetch=4,
        grid=(n_blocks,),
        in_specs=[pl.BlockSpec((MOE_BLOCK * ROW_SPLIT, LANES), lambda i, *_: (i, 0))]
        + [pl.BlockSpec(memory_space=pl.ANY)] * 3,
        out_specs=pl.BlockSpec((MOE_BLOCK * ROW_SPLIT, LANES), lambda i, *_: (i, 0)),
        scratch_shapes=[
            pltpu.VMEM((2, D_MODEL, D_EXPERT), F32),
            pltpu.VMEM((2, D_MODEL, D_EXPERT), F32),
            pltpu.VMEM((2, D_EXPERT, D_MODEL), F32),
            pltpu.VMEM((D_MODEL, 2 * D_EXPERT), BF16),
            pltpu.VMEM((D_EXPERT, D_MODEL), BF16),
            pltpu.SemaphoreType.DMA((2,)),
        ],
    )
    return pl.pallas_call(
        functools.partial(_expert_kernel, layer=layer),
        grid_spec=grid_spec,
        out_shape=jax.ShapeDtypeStruct((n_slots * ROW_SPLIT, LANES), U32),
        compiler_params=_params("arbitrary"),
        name="moe_experts",
    )(block_expert, block_run, next_expert, used, xs, w_gate, w_up, w_down)


def _combine_kernel(dest_ref, x1_ref, route_ref, ys_ref, nw_ref, out_ref, buf, sem, *, final_norm):
    i = pl.program_id(0)
    tm = x1_ref.shape[0]

    def copy(r, k):
        slot = dest_ref[2 * (i * tm + r) + k]
        dst = buf.at[k, pl.ds(pl.multiple_of(r * ROW_SPLIT, ROW_SPLIT), ROW_SPLIT), :]
        return pltpu.make_async_copy(_row(ys_ref, slot), dst, sem)

    def start(r, _):
        copy(r, 0).start(priority=0)
        copy(r, 1).start(priority=1)
        return 0

    def wait(r, _):
        copy(r, 0).wait()
        copy(r, 1).wait()
        return 0

    lax.fori_loop(0, tm, start, 0, unroll=8)
    lax.fori_loop(0, tm, wait, 0, unroll=8)
    route = route_ref[...]
    out = (x1_ref[...] + route[:, 2:3] * _rows_to_matrix(buf, tm, 0)
           + route[:, 3:4] * _rows_to_matrix(buf, tm, 1))
    if final_norm:
        out = out * lax.rsqrt(jnp.mean(out * out, axis=-1, keepdims=True) + NORM_EPS) * nw_ref[...]
    out_ref[...] = out


def _combine(x1, route, ys, dest, norm_w, final_norm):
    t = x1.shape[0]
    tm = min(MOE_TOK_TILE, t)
    grid_spec = pltpu.PrefetchScalarGridSpec(
        num_scalar_prefetch=1,
        grid=(t // tm,),
        in_specs=[
            pl.BlockSpec((tm, D_MODEL), lambda i, dest: (i, 0)),
            pl.BlockSpec((tm, LANES), lambda i, dest: (i, 0)),
            pl.BlockSpec(memory_space=pl.ANY),
            pl.BlockSpec((1, D_MODEL), lambda i, dest: (0, 0)),
        ],
        out_specs=pl.BlockSpec((tm, D_MODEL), lambda i, dest: (i, 0)),
        scratch_shapes=[pltpu.VMEM((2, tm * ROW_SPLIT, LANES), U32), pltpu.SemaphoreType.DMA(())],
    )
    return pl.pallas_call(
        functools.partial(_combine_kernel, final_norm=final_norm),
        grid_spec=grid_spec,
        out_shape=jax.ShapeDtypeStruct((t, D_MODEL), F32),
        compiler_params=_params("arbitrary"),
        name="moe_combine",
    )(dest, x1, route, ys, norm_w.reshape(1, D_MODEL).astype(F32))


def _moe(x1, h, route, counts, w_gate, w_up, w_down, layer, norm_w, final_norm):
    t = x1.shape[0]
    n_blocks = -(-(2 * t) // MOE_BLOCK) + N_EXPERTS
    n_slots = n_blocks * MOE_BLOCK
    cnt = counts[0, N_EXPERT_GROUPS:N_EXPERT_GROUPS + N_EXPERTS].astype(jnp.int32)
    padded = ((cnt + MOE_BLOCK - 1) // MOE_BLOCK) * MOE_BLOCK
    pad_end = jnp.cumsum(padded)
    pad_start = pad_end - padded
    expert = route[:, 0:2].astype(jnp.int32)
    rank = route[:, 4:6].astype(jnp.int32)
    experts = jnp.arange(N_EXPERTS, dtype=jnp.int32)
    dest = (jnp.sum(jnp.where(expert[..., None] == experts, pad_start, 0), axis=-1) + rank).reshape(-1)
    blocks = jnp.arange(n_blocks, dtype=jnp.int32)
    block_expert = jnp.minimum(jnp.sum(pad_end[None, :] <= (blocks * MOE_BLOCK)[:, None], axis=1),
                               N_EXPERTS - 1).astype(jnp.int32)
    starts_run = jnp.concatenate([jnp.ones((1,), bool), block_expert[1:] != block_expert[:-1]])
    block_run = (jnp.cumsum(starts_run) - 1).astype(jnp.int32)
    later_start = starts_run[None, :] & (blocks[None, :] > blocks[:, None])
    next_expert = jnp.where(jnp.any(later_start, axis=1),
                            block_expert[jnp.argmax(later_start, axis=1)], -1).astype(jnp.int32)
    used = (pad_end[-1:] // MOE_BLOCK).astype(jnp.int32)
    tail = blocks[n_blocks - N_EXPERTS:]
    zero_blocks = jnp.concatenate([jnp.where(padded > 0, pad_end // MOE_BLOCK - 1, -1),
                                   jnp.where(tail >= used, tail, -1)]).astype(jnp.int32)
    xs = _dispatch(h, dest, zero_blocks, n_slots)
    ys = _experts(xs, block_expert, block_run, next_expert, used, w_gate, w_up, w_down, layer)
    return _combine(x1, route, ys, dest, norm_w, final_norm)


def kernel(x, norm_mix, w_in, s5_lam_re, s5_lam_im, s5_log_dt, s5_b_re, s5_b_im, s5_c_re, s5_c_im,
           s5_d, s5_w_glu, s5_b_glu, gdn_conv_w, gdn_a_log, gdn_dt_bias, gdn_norm_w, w_out, norm_ffn,
           router_w_group, router_b_group, router_w_expert, router_b_expert,
           expert_w_gate, expert_w_up, expert_w_down, norm_final):
    bsz, seq, d = x.shape
    depth = norm_mix.shape[0]
    xt = x.astype(F32).reshape(bsz * seq, d)
    tables = _s5_tables(s5_lam_re, s5_lam_im, s5_log_dt, s5_b_re, s5_b_im, s5_c_re, s5_c_im)
    for i in range(depth):
        u, z, q, k, v, gates = _inproj(xt, norm_mix[i], w_in[i], gdn_conv_w[i], gdn_a_log[i],
                                       gdn_dt_bias[i])
        ys = _s5_mix(u, tables, i)
        gates_t = (gates[:, :6 * DN_HEADS].reshape(-1, DN_CHUNK, 6 * DN_HEADS).transpose(0, 2, 1))
        o_f, o_b = _dn_mix(q, k, v, gates, gates_t)
        x1, h, route, counts = _post(xt, u, ys, o_f, o_b, z, s5_d[i], s5_w_glu[i], s5_b_glu[i],
                                     gdn_norm_w[i], w_out[i], norm_ffn[i], router_w_group[i],
                                     router_b_group[i], router_w_expert[i], router_b_expert[i])
        xt = _moe(x1, h, route, counts, expert_w_gate, expert_w_up, expert_w_down, i,
                  norm_final, i == depth - 1)
    return xt.reshape(bsz, seq, d)
```

```python
import functools
import math

import jax
import jax.numpy as jnp
import numpy as np
from jax import lax
from jax.experimental import pallas as pl
from jax.experimental.pallas import tpu as pltpu

F32 = jnp.float32
BF16 = jnp.bfloat16
HIGHEST = lax.Precision.HIGHEST

D_MODEL = 1024
S5_WIDTH = 512
S5_GROUP = 16
S5_GROUPS = 32
S5_STATE = 64
S5_MAX_RE = -1e-4
DN_HEADS = 4
DN_HEAD_DIM = 128
DN_WIDTH = 512
DN_CONV = 5
DN_CHUNK = 64
N_EXPERT_GROUPS = 4
EXPERTS_PER_GROUP = 8
N_EXPERTS = 32
D_EXPERT = 512
NORM_EPS = 1e-6

LANES = 128
SUBLANES = 8
VMEM_LIMIT = 56 * 1024 * 1024

S5_CHUNK = 16
S5_TILE = 128
ROW_TILE = 512
DN_STEP_CHUNKS = 16
MOE_BLOCK = 256
MOE_TOK_TILE = 256


def _params(*sem):
    return pltpu.CompilerParams(dimension_semantics=sem, vmem_limit_bytes=VMEM_LIMIT)


def _silu(x):
    half = 0.5 * x
    return half + half * jnp.tanh(half)


def _sigmoid(x):
    return 0.5 + 0.5 * jnp.tanh(0.5 * x)


def _split_weight(w):
    hi = w.astype(BF16)
    lo = (w - hi.astype(F32)).astype(BF16)
    return jnp.concatenate([hi, lo], axis=1)


def _dot_split(a, w_split):
    a_hi = a.astype(BF16)
    a_lo = (a - a_hi.astype(F32)).astype(BF16)
    p = jnp.dot(a_hi, w_split, preferred_element_type=F32)
    q = jnp.dot(a_lo, w_split[:, :LANES], preferred_element_type=F32)
    return p[:, :LANES] + p[:, LANES:] + q


def _inproj_kernel(x_ref, xp_ref, xn_ref, nw_ref, w_ref, wab_ref, cw_ref, gp_ref, tri_ref,
                   u_ref, z_ref, q_ref, k_ref, v_ref, gate_ref, ext):
    i = pl.program_id(0)
    tm = x_ref.shape[0]
    pad = DN_CONV // 2

    def norm(x):
        return x * lax.rsqrt(jnp.mean(x * x, axis=-1, keepdims=True) + NORM_EPS) * nw_ref[...]

    h = norm(x_ref[...])
    h_prev = norm(jnp.where(i > 0, xp_ref[...], 0.0))
    h_next = norm(jnp.where(i < pl.num_programs(0) - 1, xn_ref[...], 0.0))
    hb = h.astype(BF16)
    for blk in range(S5_WIDTH // LANES):
        u_ref[blk] = jnp.dot(hb, w_ref[:, blk * LANES:(blk + 1) * LANES], preferred_element_type=F32)
    z_ref[...] = jnp.dot(hb, w_ref[:, S5_WIDTH + 3 * DN_WIDTH:S5_WIDTH + 4 * DN_WIDTH],
                         preferred_element_type=F32)
    h_ext = jnp.concatenate([h_prev, h, h_next], axis=0).astype(BF16)
    ext[...] = jnp.dot(h_ext, w_ref[:, S5_WIDTH:S5_WIDTH + 3 * DN_WIDTH], preferred_element_type=F32)

    outs = (q_ref, k_ref, v_ref)
    for part in range(3):
        cols = slice(part * DN_WIDTH, (part + 1) * DN_WIDTH)
        acc = ext[pl.ds(SUBLANES - pad, tm), cols] * cw_ref[0:1, cols]
        for tap in range(1, DN_CONV):
            acc = acc + ext[pl.ds(SUBLANES - pad + tap, tm), cols] * cw_ref[tap:tap + 1, cols]
        act = _silu(acc)
        if part == 2:
            v_ref[...] = act.astype(BF16)
            continue
        scale = DN_HEAD_DIM ** -0.5 if part == 0 else 1.0
        for hd in range(DN_HEADS):
            hs = slice(hd * DN_HEAD_DIM, (hd + 1) * DN_HEAD_DIM)
            xh = act[:, hs]
            inv = lax.rsqrt(jnp.sum(xh * xh, axis=-1, keepdims=True) + NORM_EPS)
            outs[part][:, hs] = (xh * inv * scale).astype(BF16)

    ab = _dot_split(h, wab_ref[...])
    lane = lax.broadcasted_iota(jnp.int32, ab.shape, 1)
    pre = ab + gp_ref[1:2, :]
    softplus = jnp.maximum(pre, 0.0) + jnp.log1p(jnp.exp(-jnp.abs(pre)))
    n_gate = 2 * DN_HEADS
    g = jnp.where(lane < n_gate, gp_ref[0:1, :] * softplus, 0.0)
    hi = g.astype(BF16).astype(F32)
    r1 = g - hi
    mid = r1.astype(BF16).astype(F32)
    lo = (r1 - mid).astype(BF16).astype(F32)
    pieces = (hi + pltpu.roll(mid, n_gate, 1) + pltpu.roll(lo, 2 * n_gate, 1)).astype(BF16)
    pref = jnp.dot(tri_ref[0], pieces, preferred_element_type=F32)
    suff = jnp.dot(tri_ref[1], pieces, preferred_element_type=F32)
    part = jnp.where(jnp.bitwise_and(lane, n_gate - 1) < DN_HEADS, pref, suff)
    gsum = part + pltpu.roll(part, LANES - n_gate, 1) + pltpu.roll(part, LANES - 2 * n_gate, 1)
    gate_ref[...] = jnp.where(lane < n_gate, g, jnp.where(
        lane < 2 * n_gate, jax.nn.sigmoid(ab), jnp.where(
            lane < 3 * n_gate, pltpu.roll(gsum, 2 * n_gate, 1), 0.0)))


def _inproj(x, norm_w, w_main, w_gates, layer, conv_w, a_log, dt_bias):
    t = x.shape[0]
    n_main = S5_WIDTH + 4 * DN_WIDTH
    w_ab = _split_weight(jnp.pad(w_gates.astype(F32), ((0, 0), (0, LANES - 4 * DN_HEADS))))
    gp = jnp.zeros((SUBLANES, LANES), F32)
    gp = gp.at[0, :2 * DN_HEADS].set(-jnp.exp(a_log.astype(F32)).reshape(-1))
    gp = gp.at[1, :2 * DN_HEADS].set(dt_bias.astype(F32).reshape(-1))
    cw = jnp.pad(conv_w.astype(F32), ((0, SUBLANES - DN_CONV), (0, 0)))
    tm = min(ROW_TILE, t)
    nb = tm // SUBLANES
    last = t // SUBLANES - 1
    step = np.arange(tm)
    same = (step[:, None] // DN_CHUNK) == (step[None, :] // DN_CHUNK)
    tri = jnp.asarray(np.stack([same & (step[:, None] >= step[None, :]),
                                same & (step[:, None] <= step[None, :])]), BF16)
    row = lambda i: (i, 0)
    const = lambda i: (0, 0)
    return pl.pallas_call(
        _inproj_kernel,
        grid=(t // tm,),
        in_specs=[
            pl.BlockSpec((tm, D_MODEL), row),
            pl.BlockSpec((SUBLANES, D_MODEL), lambda i: (jnp.maximum(i * nb - 1, 0), 0)),
            pl.BlockSpec((SUBLANES, D_MODEL), lambda i: (jnp.minimum((i + 1) * nb, last), 0)),
            pl.BlockSpec((1, D_MODEL), const),
            pl.BlockSpec((None, D_MODEL, n_main), lambda i: (layer, 0, 0)),
            pl.BlockSpec((D_MODEL, 2 * LANES), const),
            pl.BlockSpec((SUBLANES, 3 * DN_WIDTH), const),
            pl.BlockSpec((SUBLANES, LANES), const),
            pl.BlockSpec((2, tm, tm), lambda i: (0, 0, 0)),
        ],
        out_specs=[
            pl.BlockSpec((S5_WIDTH // LANES, tm, LANES), lambda i: (0, i, 0)),
            pl.BlockSpec((tm, DN_WIDTH), row),
            pl.BlockSpec((tm, DN_WIDTH), row),
            pl.BlockSpec((tm, DN_WIDTH), row),
            pl.BlockSpec((tm, DN_WIDTH), row),
            pl.BlockSpec((tm, LANES), row),
        ],
        out_shape=[
            jax.ShapeDtypeStruct((S5_WIDTH // LANES, t, LANES), F32),
            jax.ShapeDtypeStruct((t, DN_WIDTH), F32),
            jax.ShapeDtypeStruct((t, DN_WIDTH), BF16),
            jax.ShapeDtypeStruct((t, DN_WIDTH), BF16),
            jax.ShapeDtypeStruct((t, DN_WIDTH), BF16),
            jax.ShapeDtypeStruct((t, LANES), F32),
        ],
        scratch_shapes=[pltpu.VMEM((tm + 2 * SUBLANES, 3 * DN_WIDTH), F32)],
        compiler_params=_params("parallel"),
        name="inproj",
    )(x, x, x, norm_w.reshape(1, D_MODEL), w_main, w_ab, cw, gp, tri)


def _toeplitz_kernel(k_ref, o_ref):
    c_len = S5_CHUNK
    lane = lax.broadcasted_iota(jnp.int32, (S5_GROUP, LANES), 1)
    for b in range(k_ref.shape[0]):
        tiles = [k_ref[b, :, t * LANES:(t + 1) * LANES] for t in range(k_ref.shape[2] // LANES)]
        for j in range(c_len):
            first, shift = divmod((c_len - 1 - j) * S5_GROUP, LANES)
            for half in range(2):
                piece = tiles[first + half]
                if shift:
                    piece = jnp.where(lane < LANES - shift,
                                      pltpu.roll(piece, LANES - shift, 1),
                                      pltpu.roll(tiles[first + half + 1], LANES - shift, 1))
                o_ref[b, j * S5_GROUP:(j + 1) * S5_GROUP, half * LANES:(half + 1) * LANES] = piece.astype(BF16)


def _toeplitz(kpad):
    n, q, width = kpad.shape
    per_step = 8
    return pl.pallas_call(
        _toeplitz_kernel,
        grid=(n // per_step,),
        in_specs=[pl.BlockSpec((per_step, q, width), lambda i: (i, 0, 0))],
        out_specs=pl.BlockSpec((per_step, S5_CHUNK * q, 256), lambda i: (i, 0, 0)),
        out_shape=jax.ShapeDtypeStruct((n, S5_CHUNK * q, 256), BF16),
        compiler_params=_params("parallel"),
        name="s5_toeplitz",
    )(kpad)


def _s5_tables(lam_re, lam_im, log_dt, b_re, b_im, c_re, c_im):
    c_len = S5_CHUNK
    lr = jnp.minimum(lam_re.astype(F32), S5_MAX_RE)
    li = lam_im.astype(F32)
    dt = jnp.exp(log_dt.astype(F32))[..., None]
    zr, zi = lr * dt, li * dt
    e1 = jnp.exp(zr)
    ar, ai = e1 * jnp.cos(zi), e1 * jnp.sin(zi)
    den = lr * lr + li * li
    nr, ni = ar - 1.0, ai
    fr = (nr * lr + ni * li) / den
    fi = (ni * lr - nr * li) / den
    bbr = (fr[..., None] * b_re - fi[..., None] * b_im).swapaxes(-1, -2)
    bbi = (fr[..., None] * b_im + fi[..., None] * b_re).swapaxes(-1, -2)
    tau = jnp.arange(c_len + 1, dtype=F32)[:, None]
    mag = jnp.exp(tau * zr[..., None, :])
    pr = mag * jnp.cos(tau * zi[..., None, :])
    pi = mag * jnp.sin(tau * zi[..., None, :])
    prq, piq = pr[..., :, None, :], pi[..., :, None, :]
    m_r = prq * bbr[..., None, :, :] - piq * bbi[..., None, :, :]
    m_i = prq * bbi[..., None, :, :] + piq * bbr[..., None, :, :]
    kern = (jnp.einsum('ldgpn,ldgtqn->ldgqtp', c_re, m_r[..., :c_len, :, :], precision=HIGHEST)
            - jnp.einsum('ldgpn,ldgtqn->ldgqtp', c_im, m_i[..., :c_len, :, :], precision=HIGHEST))

    def per_direction(x, axis, fwd_flipped):
        f, b = x[:, 0], x[:, 1]
        f, b = (jnp.flip(f, axis), b) if fwd_flipped else (f, jnp.flip(b, axis))
        return jnp.stack([f, b], axis=1)

    padded = jnp.pad(kern, ((0, 0),) * 4 + ((c_len - 1, 0), (0, 0)))
    padded = per_direction(padded, 3, False)
    padded = jnp.pad(padded, ((0, 0),) * 4 + ((0, 1), (0, 0)))
    lead = padded.shape[:3]
    wt = _toeplitz(padded.reshape(-1, S5_GROUP, 2 * c_len * S5_GROUP)).reshape(lead + (256, 256))
    er = per_direction(m_r[..., :c_len, :, :], 2, True).reshape(wt.shape[:3] + (256, S5_STATE))
    ei = per_direction(m_i[..., :c_len, :, :], 2, True).reshape(wt.shape[:3] + (256, S5_STATE))
    we = jnp.concatenate([er, ei, ei, er], axis=-1)
    c_rt, c_it = c_re.swapaxes(-1, -2)[..., None, :], c_im.swapaxes(-1, -2)[..., None, :]
    p_rt, p_it = pr.swapaxes(-1, -2)[..., 1:, None], pi.swapaxes(-1, -2)[..., 1:, None]
    sr = per_direction(c_rt * p_rt - c_it * p_it, 3, False).reshape(wt.shape[:3] + (S5_STATE, 256))
    si = per_direction(c_rt * p_it + c_it * p_rt, 3, False).reshape(wt.shape[:3] + (S5_STATE, 256))
    ws = jnp.concatenate([sr, -si], axis=3)
    a_r, a_i = pr[..., c_len, :], pi[..., c_len, :]
    coef = jnp.stack([jnp.concatenate([a_r, a_r], -1),
                      jnp.concatenate([-a_i, a_i], -1),
                      jnp.concatenate([a_i, -a_i], -1)], axis=2)
    return wt.astype(BF16), we.astype(BF16), ws.astype(BF16), coef


def _block_transpose8(xs, lane):
    for k in (2, 1, 0):
        shift = S5_GROUP << k
        bit = jnp.bitwise_and(lax.shift_right_logical(lane, 4 + k), 1)
        new = list(xs)
        for a in range(8):
            if (a >> k) & 1:
                continue
            b = a + (1 << k)
            if 2 * shift == LANES:
                both = pltpu.roll(jnp.where(bit == 0, xs[b], xs[a]), shift, 1)
                new[a] = jnp.where(bit == 0, xs[a], both)
                new[b] = jnp.where(bit == 1, xs[b], both)
            else:
                new[a] = jnp.where(bit == 0, xs[a], pltpu.roll(xs[b], shift, 1))
                new[b] = jnp.where(bit == 1, xs[b], pltpu.roll(xs[a], LANES - shift, 1))
        xs = new
    return xs


def _s5_scan_kernel(u_ref, wt_ref, we_ref, ws_ref, coef_ref, *rest, reverse, add_prev):
    if add_prev:
        prev_ref, y_ref, ug_scr, yg_scr, e_scr, es_scr, s_scr, carry = rest
    else:
        y_ref, ug_scr, yg_scr, e_scr, es_scr, s_scr, carry = rest
    n_rows = ug_scr.shape[1]
    n_blk = S5_WIDTH // LANES
    per_blk = LANES // S5_GROUP
    halves = S5_CHUNK // per_blk

    @pl.when(pl.program_id(0) == 0)
    def _():
        carry[...] = jnp.zeros_like(carry)

    lane = lax.broadcasted_iota(jnp.int32, (n_rows, LANES), 1)

    for blk in range(n_blk):
        for half in range(halves):
            xs = [u_ref[blk, pl.ds(half * per_blk + jl, n_rows, stride=S5_CHUNK), :].astype(BF16)
                  for jl in range(per_blk)]
            ys = _block_transpose8(xs, lane)
            for gl in range(per_blk):
                ug_scr[blk * per_blk + gl, :, half * LANES:(half + 1) * LANES] = ys[gl].astype(BF16)

    for g in range(S5_GROUPS):
        e = jnp.dot(ug_scr[g], we_ref[g], preferred_element_type=F32)
        e_scr[g * n_rows:(g + 1) * n_rows, :] = e[:, :LANES]
        es_scr[g * n_rows:(g + 1) * n_rows, :] = e[:, LANES:]

    c1, c2, c3 = coef_ref[0], coef_ref[1], coef_ref[2]

    def step(i, vs):
        v, vp = vs
        r = (n_rows - 1 - i) if reverse else i
        rows = pl.ds(r, S5_GROUPS, stride=n_rows)
        s_scr[rows, :] = v
        v_new = v * c1 + vp * c2 + e_scr[rows, :]
        vp_new = vp * c1 + v * c3 + es_scr[rows, :]
        return v_new, vp_new

    v, vp = lax.fori_loop(0, n_rows, step, (carry[0], carry[1]), unroll=8)
    carry[0] = v
    carry[1] = vp

    for blk in range(n_blk):
        for gl in range(per_blk):
            g = blk * per_blk + gl
            s_in = s_scr[g * n_rows:(g + 1) * n_rows, :].astype(BF16)
            yg_scr[gl] = (jnp.dot(ug_scr[g], wt_ref[g], preferred_element_type=F32)
                          + jnp.dot(s_in, ws_ref[g], preferred_element_type=F32))
        for half in range(halves):
            zs = [yg_scr[gl, :, half * LANES:(half + 1) * LANES].astype(BF16) for gl in range(per_blk)]
            ws = [w.astype(F32) for w in _block_transpose8(zs, lane)]
            for tl in range(per_blk):
                rows = pl.ds(half * per_blk + tl, n_rows, stride=S5_CHUNK)
                out = ws[tl]
                if add_prev:
                    out = out + prev_ref[blk, rows, :]
                y_ref[blk, rows, :] = out


def _s5_direction(u4, tables, prev, reverse, layer):
    n_blk, t, _ = u4.shape
    n_chunks = t // S5_CHUNK
    rows = min(S5_TILE, n_chunks)
    n_tiles = n_chunks // rows
    wt, we, ws, coef = tables
    tile = (lambda i: (0, n_tiles - 1 - i, 0)) if reverse else (lambda i: (0, i, 0))
    table = lambda i: (layer, int(reverse), 0, 0, 0)
    once = pl.Buffered(1)
    in_specs = [
        pl.BlockSpec((n_blk, rows * S5_CHUNK, LANES), tile),
        pl.BlockSpec((None, None, S5_GROUPS, 256, 256), table, pipeline_mode=once),
        pl.BlockSpec((None, None, S5_GROUPS, 256, 256), table, pipeline_mode=once),
        pl.BlockSpec((None, None, S5_GROUPS, LANES, 256), table, pipeline_mode=once),
        pl.BlockSpec((None, None, 3, S5_GROUPS, LANES), table, pipeline_mode=once),
    ]
    args = [u4, wt, we, ws, coef]
    if prev is not None:
        in_specs.append(pl.BlockSpec((n_blk, rows * S5_CHUNK, LANES), tile))
        args.append(prev)
    return pl.pallas_call(
        functools.partial(_s5_scan_kernel, reverse=reverse, add_prev=prev is not None),
        grid=(n_tiles,),
        in_specs=in_specs,
        out_specs=pl.BlockSpec((n_blk, rows * S5_CHUNK, LANES), tile),
        out_shape=jax.ShapeDtypeStruct((n_blk, t, LANES), F32),
        scratch_shapes=[
            pltpu.VMEM((S5_GROUPS, rows, 256), BF16),
            pltpu.VMEM((LANES // S5_GROUP, rows, 256), F32),
            pltpu.VMEM((S5_GROUPS * rows, LANES), F32),
            pltpu.VMEM((S5_GROUPS * rows, LANES), F32),
            pltpu.VMEM((S5_GROUPS * rows, LANES), F32),
            pltpu.VMEM((2, S5_GROUPS, LANES), F32),
        ],
        compiler_params=_params("arbitrary"),
        name="s5_bwd" if reverse else "s5_fwd",
    )(*args)


def _s5_mix(u4, tables, layer):
    y = _s5_direction(u4, tables, None, False, layer)
    return _s5_direction(u4, tables, y, True, layer)


def _bmm(a, b):
    return lax.dot_general(a, b, (((2,), (1,)), ((0,), (0,))), preferred_element_type=F32)


def _dn_kernel(qf, kf, vf, gf, gtf, qb, kb_, vb, gb, gtb, of_ref, ob_ref,
               state, wq_scr, kdt_scr, at_scr, u_scr, gam_scr):
    c_len = DN_CHUNK
    n_ch = gtf.shape[0]
    n_gate = 2 * DN_HEADS
    n_chain = 2 * DN_HEADS

    @pl.when(pl.program_id(0) == 0)
    def _():
        state[...] = jnp.zeros_like(state)

    ri = lax.broadcasted_iota(jnp.int32, (c_len, c_len), 0)
    ci = lax.broadcasted_iota(jnp.int32, (c_len, c_len), 1)
    eye = jnp.where(ri == ci, 1.0, 0.0).astype(F32)
    dirs = ((qf, kf, vf, gf, gtf, ri >= ci, ri > ci, c_len - 1),
            (qb, kb_, vb, gb, gtb, ri <= ci, ri < ci, 0))

    for d, (q_ref, k_ref, v_ref, gate_ref, gate_t_ref, incl, strict, last) in enumerate(dirs):
        gates = gate_ref[...].reshape(n_ch, c_len, LANES)
        gates_t = gate_t_ref[...]
        for h in range(DN_HEADS):
            idx = d * DN_HEADS + h
            hs = slice(h * DN_HEAD_DIM, (h + 1) * DN_HEAD_DIM)
            gcol = gates[:, :, 2 * n_gate + idx:2 * n_gate + idx + 1]
            bcol = gates[:, :, n_gate + idx:n_gate + idx + 1]
            grow = gates_t[:, 2 * n_gate + idx:2 * n_gate + idx + 1, :]
            glast = grow[:, :, last:last + 1]
            qb = q_ref[:, hs].reshape(n_ch, c_len, DN_HEAD_DIM)
            kb = k_ref[:, hs].reshape(n_ch, c_len, DN_HEAD_DIM)
            qh, kh = qb.astype(F32), kb.astype(F32)
            vh = v_ref[:, hs].reshape(n_ch, c_len, DN_HEAD_DIM).astype(F32)
            qk_kk = lax.dot_general(jnp.concatenate([qb, kb], axis=1), kb,
                                    (((2,), (2,)), ((0,), (0,))), preferred_element_type=F32)
            qk, kk = qk_kk[:, :c_len], qk_kk[:, c_len:]
            decay = jnp.where(incl, jnp.exp(jnp.where(incl, gcol - grow, 0.0)), 0.0)
            a_mat = jnp.where(strict, bcol * kk * decay, 0.0)
            pw = -a_mat
            inv = eye + pw
            for _ in range(5):
                pwb = pw.astype(BF16)
                pw = _bmm(pwb, pwb)
                inv = inv + _bmm(inv.astype(BF16), pw.astype(BF16))
            egc = jnp.exp(gcol)
            rhs = jnp.concatenate([vh * bcol, kh * (bcol * egc)], axis=2).astype(BF16)
            uw = _bmm(inv.astype(BF16), rhs)
            wq = jnp.concatenate([uw[:, :, DN_HEAD_DIM:], qh * egc], axis=1).astype(BF16)
            attn = jnp.where(incl, qk * decay, 0.0).astype(BF16)
            k_dec_t = jnp.swapaxes(kh * jnp.exp(glast - gcol), 1, 2).astype(BF16)
            gamma = jnp.broadcast_to(jnp.exp(glast), (n_ch, 1, LANES))
            for c in range(n_ch):
                slot = (c if d == 0 else n_ch - 1 - c) * n_chain + idx
                wq_scr[slot] = wq[c]
                kdt_scr[slot] = k_dec_t[c]
                at_scr[slot] = attn[c]
                u_scr[slot] = uw[c, :, :DN_HEAD_DIM]
                gam_scr[slot] = gamma[c]

    for step in range(n_ch):
        grp = slice(step * n_chain, (step + 1) * n_chain)
        s = state[...]
        wq_s = _bmm(wq_scr[grp], s.astype(BF16))
        v_nb = (u_scr[grp] - wq_s[:, :c_len]).astype(BF16)
        o = wq_s[:, c_len:] + _bmm(at_scr[grp], v_nb)
        state[...] = s * gam_scr[grp] + _bmm(kdt_scr[grp], v_nb)
        for d, o_ref in enumerate((of_ref, ob_ref)):
            c = step if d == 0 else n_ch - 1 - step
            for h in range(DN_HEADS):
                o_ref[c * c_len:(c + 1) * c_len, h * DN_HEAD_DIM:(h + 1) * DN_HEAD_DIM] = (
                    o[d * DN_HEADS + h].astype(BF16))


def _dn_mix(q, k, v, gates, gates_t):
    t = q.shape[0]
    n_chunks = t // DN_CHUNK
    n_ch = min(DN_STEP_CHUNKS, n_chunks)
    n_steps = n_chunks // n_ch
    rows = n_ch * DN_CHUNK
    n_inst = n_ch * 2 * DN_HEADS
    specs = []
    for row, row3 in ((lambda i: (i, 0), lambda i: (i, 0, 0)),
                      (lambda i: (n_steps - 1 - i, 0), lambda i: (n_steps - 1 - i, 0, 0))):
        specs += [
            pl.BlockSpec((rows, DN_WIDTH), row),
            pl.BlockSpec((rows, DN_WIDTH), row),
            pl.BlockSpec((rows, DN_WIDTH), row),
            pl.BlockSpec((rows, LANES), row),
            pl.BlockSpec((n_ch, 6 * DN_HEADS, DN_CHUNK), row3),
        ]
    return pl.pallas_call(
        _dn_kernel,
        grid=(n_steps,),
        in_specs=specs,
        out_specs=[
            pl.BlockSpec((rows, DN_WIDTH), lambda i: (i, 0)),
            pl.BlockSpec((rows, DN_WIDTH), lambda i: (n_steps - 1 - i, 0)),
        ],
        out_shape=[jax.ShapeDtypeStruct((t, DN_WIDTH), BF16), jax.ShapeDtypeStruct((t, DN_WIDTH), BF16)],
        scratch_shapes=[
            pltpu.VMEM((2 * DN_HEADS, DN_HEAD_DIM, DN_HEAD_DIM), F32),
            pltpu.VMEM((n_inst, 2 * DN_CHUNK, DN_HEAD_DIM), BF16),
            pltpu.VMEM((n_inst, DN_HEAD_DIM, DN_CHUNK), BF16),
            pltpu.VMEM((n_inst, DN_CHUNK, DN_CHUNK), BF16),
            pltpu.VMEM((n_inst, DN_CHUNK, DN_HEAD_DIM), F32),
            pltpu.VMEM((n_inst, 1, LANES), F32),
        ],
        compiler_params=_params("arbitrary"),
        name="dn_mix",
    )(q, k, v, gates, gates_t, q, k, v, gates, gates_t)


def _post_kernel(x_ref, u_ref, ys_ref, of_ref, ob_ref, z_ref, d_ref, wglu_ref, bglu_ref, nw_ref,
                 wout_ref, nffn_ref, wr_ref, br_ref, before_ref, x1_ref, h_ref, route_ref, cnt_ref, base):
    i = pl.program_id(0)
    tm = x_ref.shape[0]

    @pl.when(i == 0)
    def _():
        base[...] = jnp.zeros_like(base)

    y = jnp.concatenate([ys_ref[b] + d_ref[:, b * LANES:(b + 1) * LANES] * u_ref[b]
                         for b in range(S5_WIDTH // LANES)], axis=1)
    y = 0.5 * y * (1.0 + lax.erf(y * (2.0 ** -0.5)))
    gate = jnp.dot(y.astype(BF16), wglu_ref[...], preferred_element_type=F32) + bglu_ref[...]
    y_s5 = y * _sigmoid(gate)
    acc = x_ref[...] + jnp.dot(y_s5.astype(BF16), wout_ref[0:S5_WIDTH, :], preferred_element_type=F32)
    for h in range(DN_HEADS):
        hs = slice(h * DN_HEAD_DIM, (h + 1) * DN_HEAD_DIM)
        o = of_ref[:, hs].astype(F32) + ob_ref[:, hs].astype(F32)
        zh = z_ref[:, hs]
        o = o * lax.rsqrt(jnp.mean(o * o, axis=-1, keepdims=True) + NORM_EPS) * nw_ref[...]
        y_dn = o * _silu(zh)
        acc = acc + jnp.dot(y_dn.astype(BF16),
                            wout_ref[S5_WIDTH + h * DN_HEAD_DIM:S5_WIDTH + (h + 1) * DN_HEAD_DIM, :],
                            preferred_element_type=F32)
    x1_ref[...] = acc
    hn = acc * lax.rsqrt(jnp.mean(acc * acc, axis=-1, keepdims=True) + NORM_EPS) * nffn_ref[...]
    _matrix_to_rows(h_ref, hn)

    logits = _dot_split(hn, wr_ref[...]) + br_ref[...]
    lane_i = lax.broadcasted_iota(jnp.int32, logits.shape, 1)
    lane = lane_i.astype(F32)
    neg = jnp.float32(-jnp.inf)
    big = jnp.float32(LANES)
    gl = jnp.where(lane_i < N_EXPERT_GROUPS, logits, neg)
    gmax = jnp.max(gl, axis=-1, keepdims=True)
    g_sel = jnp.min(jnp.where(gl == gmax, lane, big), axis=-1, keepdims=True)
    p_group = 1.0 / jnp.sum(jnp.exp(gl - gmax), axis=-1, keepdims=True)
    lo = N_EXPERT_GROUPS + g_sel * EXPERTS_PER_GROUP
    el = jnp.where((lane >= lo) & (lane < lo + EXPERTS_PER_GROUP), logits, neg)
    top1 = jnp.max(el, axis=-1, keepdims=True)
    idx1 = jnp.min(jnp.where(el == top1, lane, big), axis=-1, keepdims=True)
    el2 = jnp.where(lane == idx1, neg, el)
    top2 = jnp.max(el2, axis=-1, keepdims=True)
    idx2 = jnp.min(jnp.where(el2 == top2, lane, big), axis=-1, keepdims=True)
    e21 = jnp.exp(top2 - top1)
    w1 = p_group / (1.0 + e21)
    w2 = w1 * e21
    oh1 = jnp.where(lane == idx1, 1.0, 0.0).astype(F32)
    oh2 = jnp.where(lane == idx2, 1.0, 0.0).astype(F32)
    ohs = oh1 + oh2
    prior = jnp.dot(before_ref[...], ohs.astype(BF16), preferred_element_type=F32) + base[0:1, :]
    rank1 = jnp.sum(oh1 * prior, axis=-1, keepdims=True)
    rank2 = jnp.sum(oh2 * prior, axis=-1, keepdims=True)
    base[0:1, :] = base[0:1, :] + jnp.sum(ohs, axis=0, keepdims=True)
    e1 = idx1 - N_EXPERT_GROUPS
    e2 = idx2 - N_EXPERT_GROUPS
    route = jnp.where(lane_i == 0, e1, jnp.where(lane_i == 1, e2, jnp.where(lane_i == 2, w1, jnp.where(
        lane_i == 3, w2, jnp.where(lane_i == 4, rank1, jnp.where(lane_i == 5, rank2, 0.0))))))
    route_ref[...] = route
    cnt_ref[...] = base[...]


def _post(x, u, ys, o_f, o_b, z, s5_d, w_glu, b_glu, dn_norm_w, w_out, layer, norm_ffn, w_rg, b_rg, w_re,
          b_re):
    t = x.shape[0]
    tm = min(ROW_TILE, t)
    wr = jnp.concatenate([w_rg, w_re.transpose(1, 0, 2).reshape(D_MODEL, N_EXPERTS)], axis=1)
    wr = _split_weight(jnp.pad(wr.astype(F32), ((0, 0), (0, LANES - N_EXPERT_GROUPS - N_EXPERTS))))
    br = jnp.pad(jnp.concatenate([b_rg, b_re.reshape(-1)]).astype(F32),
                 (0, LANES - N_EXPERT_GROUPS - N_EXPERTS)).reshape(1, LANES)
    row = lambda i: (i, 0)
    const = lambda i: (0, 0)
    return pl.pallas_call(
        _post_kernel,
        grid=(t // tm,),
        in_specs=[
            pl.BlockSpec((tm, D_MODEL), row),
            pl.BlockSpec((S5_WIDTH // LANES, tm, LANES), lambda i: (0, i, 0)),
            pl.BlockSpec((S5_WIDTH // LANES, tm, LANES), lambda i: (0, i, 0)),
            pl.BlockSpec((tm, DN_WIDTH), row),
            pl.BlockSpec((tm, DN_WIDTH), row),
            pl.BlockSpec((tm, DN_WIDTH), row),
            pl.BlockSpec((1, S5_WIDTH), const),
            pl.BlockSpec((S5_WIDTH, S5_WIDTH), const),
            pl.BlockSpec((1, S5_WIDTH), const),
            pl.BlockSpec((1, DN_HEAD_DIM), const),
            pl.BlockSpec((None, D_MODEL, D_MODEL), lambda i: (layer, 0, 0)),
            pl.BlockSpec((1, D_MODEL), const),
            pl.BlockSpec((D_MODEL, 2 * LANES), const),
            pl.BlockSpec((1, LANES), const),
            pl.BlockSpec((tm, tm), const),
        ],
        out_specs=[
            pl.BlockSpec((tm, D_MODEL), row),
            pl.BlockSpec((tm * ROW_SPLIT, LANES), row),
            pl.BlockSpec((tm, LANES), row),
            pl.BlockSpec((SUBLANES, LANES), const),
        ],
        out_shape=[
            jax.ShapeDtypeStruct((t, D_MODEL), F32),
            jax.ShapeDtypeStruct((t * ROW_SPLIT, LANES), U32),
            jax.ShapeDtypeStruct((t, LANES), F32),
            jax.ShapeDtypeStruct((SUBLANES, LANES), F32),
        ],
        scratch_shapes=[pltpu.VMEM((SUBLANES, LANES), F32)],
        compiler_params=_params("arbitrary"),
        name="mixer_post",
    )(x, u, ys, o_f, o_b, z, s5_d.reshape(1, -1).astype(F32), w_glu.astype(BF16),
      b_glu.reshape(1, -1).astype(F32), dn_norm_w.reshape(1, -1).astype(F32), w_out,
      norm_ffn.reshape(1, -1).astype(F32), wr, br,
      jnp.asarray(np.tril(np.ones((tm, tm), np.float32), -1), BF16))


ROW_SPLIT = D_MODEL // (2 * LANES)
U32 = jnp.uint32


def _rows_to_matrix(ref, n_rows, lead=None):
    low, high = [], []
    for s in range(ROW_SPLIT):
        idx = pl.ds(s, n_rows, stride=ROW_SPLIT)
        word = ref[idx, :] if lead is None else ref[lead, idx, :]
        low.append(pltpu.bitcast(word << 16, F32))
        high.append(pltpu.bitcast(word & jnp.uint32(0xFFFF0000), F32))
    return jnp.concatenate(low + high, axis=1)


def _matrix_to_rows(ref, val):
    n_rows = val.shape[0]
    half = ROW_SPLIT * LANES
    bits = pltpu.bitcast(val.astype(BF16).astype(F32), U32)
    for s in range(ROW_SPLIT):
        low = bits[:, s * LANES:(s + 1) * LANES] >> 16
        high = bits[:, half + s * LANES:half + (s + 1) * LANES]
        ref[pl.ds(s, n_rows, stride=ROW_SPLIT), :] = high | low


def _row(ref, r):
    return ref.at[pl.ds(pl.multiple_of(r * ROW_SPLIT, ROW_SPLIT), ROW_SPLIT), :]


def _dispatch_kernel(dest_ref, zblk_ref, h_ref, xs_ref, zero_buf, sem, zsem):
    i = pl.program_id(0)
    tm = h_ref.shape[0] // ROW_SPLIT
    blk_rows = zero_buf.shape[0]

    @pl.when(i == 0)
    def _():
        zero_buf[...] = jnp.zeros_like(zero_buf)

        def zero_copy(j):
            start = pl.multiple_of(zblk_ref[j] * blk_rows, blk_rows)
            return pltpu.make_async_copy(zero_buf, xs_ref.at[pl.ds(start, blk_rows), :], zsem)

        def zstart(j, _):
            @pl.when(zblk_ref[j] >= 0)
            def _():
                zero_copy(j).start()
            return 0

        def zwait(j, _):
            @pl.when(zblk_ref[j] >= 0)
            def _():
                zero_copy(j).wait()
            return 0

        lax.fori_loop(0, zblk_ref.shape[0], zstart, 0)
        lax.fori_loop(0, zblk_ref.shape[0], zwait, 0)

    def copy(r, k):
        slot = dest_ref[2 * (i * tm + r) + k]
        return pltpu.make_async_copy(_row(h_ref, r), _row(xs_ref, slot), sem)

    def start(r, _):
        copy(r, 0).start(priority=0)
        copy(r, 1).start(priority=1)
        return 0

    def wait(r, _):
        copy(r, 0).wait()
        copy(r, 1).wait()
        return 0

    lax.fori_loop(0, tm, start, 0, unroll=8)
    lax.fori_loop(0, tm, wait, 0, unroll=8)


def _dispatch(h, dest, zero_blocks, n_slots):
    t = h.shape[0] // ROW_SPLIT
    tm = min(MOE_TOK_TILE, t)
    grid_spec = pltpu.PrefetchScalarGridSpec(
        num_scalar_prefetch=2,
        grid=(t // tm,),
        in_specs=[pl.BlockSpec((tm * ROW_SPLIT, LANES), lambda i, *_: (i, 0))],
        out_specs=pl.BlockSpec(memory_space=pl.ANY),
        scratch_shapes=[pltpu.VMEM((MOE_BLOCK * ROW_SPLIT, LANES), U32),
                        pltpu.SemaphoreType.DMA(()), pltpu.SemaphoreType.DMA(())],
    )
    return pl.pallas_call(
        _dispatch_kernel,
        grid_spec=grid_spec,
        out_shape=jax.ShapeDtypeStruct((n_slots * ROW_SPLIT, LANES), U32),
        compiler_params=_params("arbitrary"),
        name="moe_dispatch",
    )(dest, zero_blocks, h)


def _expert_kernel(be_ref, run_ref, nxt_ref, used_ref, xs_ref, wg_ref, wu_ref, wd_ref, ys_ref,
                   wg_buf, wu_buf, wd_buf, wgu_b, wd_b, wsem, *, layer):
    i = pl.program_id(0)
    blk = ys_ref.shape[0] // ROW_SPLIT

    def weight_copies(expert, slot):
        return (pltpu.make_async_copy(wg_ref.at[layer, expert], wg_buf.at[slot], wsem.at[slot]),
                pltpu.make_async_copy(wu_ref.at[layer, expert], wu_buf.at[slot], wsem.at[slot]),
                pltpu.make_async_copy(wd_ref.at[layer, expert], wd_buf.at[slot], wsem.at[slot]))

    run = run_ref[i]
    new_run = (i == 0) | (run != run_ref[jnp.maximum(i - 1, 0)])
    wslot = run % 2

    @pl.when(i == 0)
    def _():
        for c in weight_copies(be_ref[0], 0):
            c.start()

    @pl.when(new_run)
    def _():
        for c in weight_copies(be_ref[i], wslot):
            c.wait()

        @pl.when(nxt_ref[i] >= 0)
        def _():
            for c in weight_copies(nxt_ref[i], 1 - wslot):
                c.start()

        wgu_b[:, :D_EXPERT] = wg_buf[wslot].astype(BF16)
        wgu_b[:, D_EXPERT:] = wu_buf[wslot].astype(BF16)
        wd_b[...] = wd_buf[wslot].astype(BF16)

    @pl.when(i < used_ref[0])
    def _():
        xb = _rows_to_matrix(xs_ref, blk).astype(BF16)
        gu = jnp.dot(xb, wgu_b[...], preferred_element_type=F32)
        g, u = gu[:, :D_EXPERT], gu[:, D_EXPERT:]
        hid = (_silu(g) * u).astype(BF16)
        _matrix_to_rows(ys_ref, jnp.dot(hid, wd_b[...], preferred_element_type=F32))

    @pl.when(i >= used_ref[0])
    def _():
        ys_ref[...] = jnp.zeros_like(ys_ref)


def _experts(xs, block_expert, block_run, next_expert, used, w_gate, w_up, w_down, layer):
    n_slots = xs.shape[0] // ROW_SPLIT
    n_blocks = n_slots // MOE_BLOCK
    grid_spec = pltpu.PrefetchScalarGridSpec(
        num_scalar_prefetch=4,
        grid=(n_blocks,),
        in_specs=[pl.BlockSpec((MOE_BLOCK * ROW_SPLIT, LANES), lambda i, *_: (i, 0))]
        + [pl.BlockSpec(memory_space=pl.ANY)] * 3,
        out_specs=pl.BlockSpec((MOE_BLOCK * ROW_SPLIT, LANES), lambda i, *_: (i, 0)),
        scratch_shapes=[
            pltpu.VMEM((2, D_MODEL, D_EXPERT), F32),
            pltpu.VMEM((2, D_MODEL, D_EXPERT), F32),
            pltpu.VMEM((2, D_EXPERT, D_MODEL), F32),
            pltpu.VMEM((D_MODEL, 2 * D_EXPERT), BF16),
            pltpu.VMEM((D_EXPERT, D_MODEL), BF16),
            pltpu.SemaphoreType.DMA((2,)),
        ],
    )
    return pl.pallas_call(
        functools.partial(_expert_kernel, layer=layer),
        grid_spec=grid_spec,
        out_shape=jax.ShapeDtypeStruct((n_slots * ROW_SPLIT, LANES), U32),
        compiler_params=_params("arbitrary"),
        name="moe_experts",
    )(block_expert, block_run, next_expert, used, xs, w_gate, w_up, w_down)


def _combine_kernel(dest_ref, x1_ref, route_ref, ys_ref, nw_ref, out_ref, buf, sem, *, final_norm):
    i = pl.program_id(0)
    tm = x1_ref.shape[0]

    def copy(r, k):
        slot = dest_ref[2 * (i * tm + r) + k]
        dst = buf.at[k, pl.ds(pl.multiple_of(r * ROW_SPLIT, ROW_SPLIT), ROW_SPLIT), :]
        return pltpu.make_async_copy(_row(ys_ref, slot), dst, sem)

    def start(r, _):
        copy(r, 0).start(priority=0)
        copy(r, 1).start(priority=1)
        return 0

    def wait(r, _):
        copy(r, 0).wait()
        copy(r, 1).wait()
        return 0

    lax.fori_loop(0, tm, start, 0, unroll=8)
    lax.fori_loop(0, tm, wait, 0, unroll=8)
    route = route_ref[...]
    out = (x1_ref[...] + route[:, 2:3] * _rows_to_matrix(buf, tm, 0)
           + route[:, 3:4] * _rows_to_matrix(buf, tm, 1))
    if final_norm:
        out = out * lax.rsqrt(jnp.mean(out * out, axis=-1, keepdims=True) + NORM_EPS) * nw_ref[...]
    out_ref[...] = out


def _combine(x1, route, ys, dest, norm_w, final_norm):
    t = x1.shape[0]
    tm = min(MOE_TOK_TILE, t)
    grid_spec = pltpu.PrefetchScalarGridSpec(
        num_scalar_prefetch=1,
        grid=(t // tm,),
        in_specs=[
            pl.BlockSpec((tm, D_MODEL), lambda i, dest: (i, 0)),
            pl.BlockSpec((tm, LANES), lambda i, dest: (i, 0)),
            pl.BlockSpec(memory_space=pl.ANY),
            pl.BlockSpec((1, D_MODEL), lambda i, dest: (0, 0)),
        ],
        out_specs=pl.BlockSpec((tm, D_MODEL), lambda i, dest: (i, 0)),
        scratch_shapes=[pltpu.VMEM((2, tm * ROW_SPLIT, LANES), U32), pltpu.SemaphoreType.DMA(())],
    )
    return pl.pallas_call(
        functools.partial(_combine_kernel, final_norm=final_norm),
        grid_spec=grid_spec,
        out_shape=jax.ShapeDtypeStruct((t, D_MODEL), F32),
        compiler_params=_params("arbitrary"),
        name="moe_combine",
    )(dest, x1, route, ys, norm_w.reshape(1, D_MODEL).astype(F32))


def _moe(x1, h, route, counts, w_gate, w_up, w_down, layer, norm_w, final_norm):
    t = x1.shape[0]
    n_blocks = -(-(2 * t) // MOE_BLOCK) + N_EXPERTS
    n_slots = n_blocks * MOE_BLOCK
    cnt = counts[0, N_EXPERT_GROUPS:N_EXPERT_GROUPS + N_EXPERTS].astype(jnp.int32)
    padded = ((cnt + MOE_BLOCK - 1) // MOE_BLOCK) * MOE_BLOCK
    pad_end = jnp.cumsum(padded)
    pad_start = pad_end - padded
    expert = route[:, 0:2].astype(jnp.int32)
    rank = route[:, 4:6].astype(jnp.int32)
    experts = jnp.arange(N_EXPERTS, dtype=jnp.int32)
    dest = (jnp.sum(jnp.where(expert[..., None] == experts, pad_start, 0), axis=-1) + rank).reshape(-1)
    blocks = jnp.arange(n_blocks, dtype=jnp.int32)
    block_expert = jnp.minimum(jnp.sum(pad_end[None, :] <= (blocks * MOE_BLOCK)[:, None], axis=1),
                               N_EXPERTS - 1).astype(jnp.int32)
    starts_run = jnp.concatenate([jnp.ones((1,), bool), block_expert[1:] != block_expert[:-1]])
    block_run = (jnp.cumsum(starts_run) - 1).astype(jnp.int32)
    later_start = starts_run[None, :] & (blocks[None, :] > blocks[:, None])
    next_expert = jnp.where(jnp.any(later_start, axis=1),
                            block_expert[jnp.argmax(later_start, axis=1)], -1).astype(jnp.int32)
    used = (pad_end[-1:] // MOE_BLOCK).astype(jnp.int32)
    tail = blocks[n_blocks - N_EXPERTS:]
    zero_blocks = jnp.concatenate([jnp.where(padded > 0, pad_end // MOE_BLOCK - 1, -1),
                                   jnp.where(tail >= used, tail, -1)]).astype(jnp.int32)
    xs = _dispatch(h, dest, zero_blocks, n_slots)
    ys = _experts(xs, block_expert, block_run, next_expert, used, w_gate, w_up, w_down, layer)
    return _combine(x1, route, ys, dest, norm_w, final_norm)


def kernel(x, norm_mix, w_in, s5_lam_re, s5_lam_im, s5_log_dt, s5_b_re, s5_b_im, s5_c_re, s5_c_im,
           s5_d, s5_w_glu, s5_b_glu, gdn_conv_w, gdn_a_log, gdn_dt_bias, gdn_norm_w, w_out, norm_ffn,
           router_w_group, router_b_group, router_w_expert, router_b_expert,
           expert_w_gate, expert_w_up, expert_w_down, norm_final):
    bsz, seq, d = x.shape
    depth = norm_mix.shape[0]
    xt = x.astype(F32).reshape(bsz * seq, d)
    tables = _s5_tables(s5_lam_re, s5_lam_im, s5_log_dt, s5_b_re, s5_b_im, s5_c_re, s5_c_im)
    n_main = S5_WIDTH + 4 * DN_WIDTH
    w_main = w_in[:, :, :n_main].astype(BF16)
    w_out_b = w_out.astype(BF16)
    for i in range(depth):
        u, z, q, k, v, gates = _inproj(xt, norm_mix[i], w_main, w_in[i, :, n_main:], i, gdn_conv_w[i],
                                       gdn_a_log[i], gdn_dt_bias[i])
        ys = _s5_mix(u, tables, i)
        gates_t = (gates[:, :6 * DN_HEADS].reshape(-1, DN_CHUNK, 6 * DN_HEADS).transpose(0, 2, 1))
        o_f, o_b = _dn_mix(q, k, v, gates, gates_t)
        x1, h, route, counts = _post(xt, u, ys, o_f, o_b, z, s5_d[i], s5_w_glu[i], s5_b_glu[i],
                                     gdn_norm_w[i], w_out_b, i, norm_ffn[i], router_w_group[i],
                                     router_b_group[i], router_w_expert[i], router_b_expert[i])
        xt = _moe(x1, h, route, counts, expert_w_gate, expert_w_up, expert_w_down, i,
                  norm_final, i == depth - 1)
    return xt.reshape(bsz, seq, d)
```

```python
import functools
import math

import jax
import jax.numpy as jnp
import numpy as np
from jax import lax
from jax.experimental import pallas as pl
from jax.experimental.pallas import tpu as pltpu

F32 = jnp.float32
BF16 = jnp.bfloat16
HIGHEST = lax.Precision.HIGHEST

D_MODEL = 1024
S5_WIDTH = 512
S5_GROUP = 16
S5_GROUPS = 32
S5_STATE = 64
S5_MAX_RE = -1e-4
DN_HEADS = 4
DN_HEAD_DIM = 128
DN_WIDTH = 512
DN_CONV = 5
DN_CHUNK = 64
N_EXPERT_GROUPS = 4
EXPERTS_PER_GROUP = 8
N_EXPERTS = 32
D_EXPERT = 512
NORM_EPS = 1e-6

LANES = 128
SUBLANES = 8
VMEM_LIMIT = 56 * 1024 * 1024

S5_CHUNK = 16
S5_TILE = 128
ROW_TILE = 512
DN_STEP_CHUNKS = 16
MOE_BLOCK = 256
MOE_TOK_TILE = 512


def _params(*sem):
    return pltpu.CompilerParams(dimension_semantics=sem, vmem_limit_bytes=VMEM_LIMIT)


def _silu(x):
    half = 0.5 * x
    return half + half * jnp.tanh(half)


def _sigmoid(x):
    return 0.5 + 0.5 * jnp.tanh(0.5 * x)


def _split_weight(w):
    hi = w.astype(BF16)
    lo = (w - hi.astype(F32)).astype(BF16)
    return jnp.concatenate([hi, lo], axis=1)


def _dot_split(a, w_split):
    a_hi = a.astype(BF16)
    a_lo = (a - a_hi.astype(F32)).astype(BF16)
    p = jnp.dot(a_hi, w_split, preferred_element_type=F32)
    q = jnp.dot(a_lo, w_split[:, :LANES], preferred_element_type=F32)
    return p[:, :LANES] + p[:, LANES:] + q


def _inproj_kernel(x_ref, xp_ref, xn_ref, nw_ref, w_ref, wab_ref, cw_ref, gp_ref, tri_ref,
                   u_ref, z_ref, q_ref, k_ref, v_ref, gate_ref, ext):
    i = pl.program_id(0)
    tm = x_ref.shape[0]
    pad = DN_CONV // 2

    def norm(x):
        return x * lax.rsqrt(jnp.mean(x * x, axis=-1, keepdims=True) + NORM_EPS) * nw_ref[...]

    h = norm(x_ref[...])
    h_prev = norm(jnp.where(i > 0, xp_ref[...], 0.0))
    h_next = norm(jnp.where(i < pl.num_programs(0) - 1, xn_ref[...], 0.0))
    hb = h.astype(BF16)
    for blk in range(S5_WIDTH // LANES):
        u_ref[blk] = jnp.dot(hb, w_ref[:, blk * LANES:(blk + 1) * LANES], preferred_element_type=F32)
    z_ref[...] = jnp.dot(hb, w_ref[:, S5_WIDTH + 3 * DN_WIDTH:S5_WIDTH + 4 * DN_WIDTH],
                         preferred_element_type=F32)
    h_ext = jnp.concatenate([h_prev, h, h_next], axis=0).astype(BF16)
    ext[...] = jnp.dot(h_ext, w_ref[:, S5_WIDTH:S5_WIDTH + 3 * DN_WIDTH], preferred_element_type=F32)

    outs = (q_ref, k_ref, v_ref)
    for part in range(3):
        cols = slice(part * DN_WIDTH, (part + 1) * DN_WIDTH)
        acc = ext[pl.ds(SUBLANES - pad, tm), cols] * cw_ref[0:1, cols]
        for tap in range(1, DN_CONV):
            acc = acc + ext[pl.ds(SUBLANES - pad + tap, tm), cols] * cw_ref[tap:tap + 1, cols]
        act = _silu(acc)
        if part == 2:
            v_ref[...] = act.astype(BF16)
            continue
        scale = DN_HEAD_DIM ** -0.5 if part == 0 else 1.0
        for hd in range(DN_HEADS):
            hs = slice(hd * DN_HEAD_DIM, (hd + 1) * DN_HEAD_DIM)
            xh = act[:, hs]
            inv = lax.rsqrt(jnp.sum(xh * xh, axis=-1, keepdims=True) + NORM_EPS)
            outs[part][:, hs] = (xh * inv * scale).astype(BF16)

    ab = _dot_split(h, wab_ref[...])
    lane = lax.broadcasted_iota(jnp.int32, ab.shape, 1)
    pre = ab + gp_ref[1:2, :]
    softplus = jnp.maximum(pre, 0.0) + jnp.log1p(jnp.exp(-jnp.abs(pre)))
    n_gate = 2 * DN_HEADS
    g = jnp.where(lane < n_gate, gp_ref[0:1, :] * softplus, 0.0)
    hi = g.astype(BF16).astype(F32)
    r1 = g - hi
    mid = r1.astype(BF16).astype(F32)
    lo = (r1 - mid).astype(BF16).astype(F32)
    pieces = (hi + pltpu.roll(mid, n_gate, 1) + pltpu.roll(lo, 2 * n_gate, 1)).astype(BF16)
    pref = jnp.dot(tri_ref[0], pieces, preferred_element_type=F32)
    suff = jnp.dot(tri_ref[1], pieces, preferred_element_type=F32)
    part = jnp.where(jnp.bitwise_and(lane, n_gate - 1) < DN_HEADS, pref, suff)
    gsum = part + pltpu.roll(part, LANES - n_gate, 1) + pltpu.roll(part, LANES - 2 * n_gate, 1)
    gate_ref[...] = jnp.where(lane < n_gate, g, jnp.where(
        lane < 2 * n_gate, jax.nn.sigmoid(ab), jnp.where(
            lane < 3 * n_gate, pltpu.roll(gsum, 2 * n_gate, 1), 0.0)))


def _inproj(x, norm_w, w_main, w_gates, layer, conv_w, a_log, dt_bias):
    t = x.shape[0]
    n_main = S5_WIDTH + 4 * DN_WIDTH
    w_ab = _split_weight(jnp.pad(w_gates.astype(F32), ((0, 0), (0, LANES - 4 * DN_HEADS))))
    gp = jnp.zeros((SUBLANES, LANES), F32)
    gp = gp.at[0, :2 * DN_HEADS].set(-jnp.exp(a_log.astype(F32)).reshape(-1))
    gp = gp.at[1, :2 * DN_HEADS].set(dt_bias.astype(F32).reshape(-1))
    cw = jnp.pad(conv_w.astype(F32), ((0, SUBLANES - DN_CONV), (0, 0)))
    tm = min(ROW_TILE, t)
    nb = tm // SUBLANES
    last = t // SUBLANES - 1
    step = np.arange(tm)
    same = (step[:, None] // DN_CHUNK) == (step[None, :] // DN_CHUNK)
    tri = jnp.asarray(np.stack([same & (step[:, None] >= step[None, :]),
                                same & (step[:, None] <= step[None, :])]), BF16)
    row = lambda i: (i, 0)
    const = lambda i: (0, 0)
    return pl.pallas_call(
        _inproj_kernel,
        grid=(t // tm,),
        in_specs=[
            pl.BlockSpec((tm, D_MODEL), row),
            pl.BlockSpec((SUBLANES, D_MODEL), lambda i: (jnp.maximum(i * nb - 1, 0), 0)),
            pl.BlockSpec((SUBLANES, D_MODEL), lambda i: (jnp.minimum((i + 1) * nb, last), 0)),
            pl.BlockSpec((1, D_MODEL), const),
            pl.BlockSpec((None, D_MODEL, n_main), lambda i: (layer, 0, 0)),
            pl.BlockSpec((D_MODEL, 2 * LANES), const),
            pl.BlockSpec((SUBLANES, 3 * DN_WIDTH), const),
            pl.BlockSpec((SUBLANES, LANES), const),
            pl.BlockSpec((2, tm, tm), lambda i: (0, 0, 0)),
        ],
        out_specs=[
            pl.BlockSpec((S5_WIDTH // LANES, tm, LANES), lambda i: (0, i, 0)),
            pl.BlockSpec((tm, DN_WIDTH), row),
            pl.BlockSpec((tm, DN_WIDTH), row),
            pl.BlockSpec((tm, DN_WIDTH), row),
            pl.BlockSpec((tm, DN_WIDTH), row),
            pl.BlockSpec((tm, LANES), row),
        ],
        out_shape=[
            jax.ShapeDtypeStruct((S5_WIDTH // LANES, t, LANES), F32),
            jax.ShapeDtypeStruct((t, DN_WIDTH), F32),
            jax.ShapeDtypeStruct((t, DN_WIDTH), BF16),
            jax.ShapeDtypeStruct((t, DN_WIDTH), BF16),
            jax.ShapeDtypeStruct((t, DN_WIDTH), BF16),
            jax.ShapeDtypeStruct((t, LANES), F32),
        ],
        scratch_shapes=[pltpu.VMEM((tm + 2 * SUBLANES, 3 * DN_WIDTH), F32)],
        compiler_params=_params("parallel"),
        name="inproj",
    )(x, x, x, norm_w.reshape(1, D_MODEL), w_main, w_ab, cw, gp, tri)


def _toeplitz_kernel(k_ref, o_ref):
    c_len = S5_CHUNK
    lane = lax.broadcasted_iota(jnp.int32, (S5_GROUP, LANES), 1)
    for b in range(k_ref.shape[0]):
        tiles = [k_ref[b, :, t * LANES:(t + 1) * LANES] for t in range(k_ref.shape[2] // LANES)]
        for j in range(c_len):
            first, shift = divmod((c_len - 1 - j) * S5_GROUP, LANES)
            for half in range(2):
                piece = tiles[first + half]
                if shift:
                    piece = jnp.where(lane < LANES - shift,
                                      pltpu.roll(piece, LANES - shift, 1),
                                      pltpu.roll(tiles[first + half + 1], LANES - shift, 1))
                o_ref[b, j * S5_GROUP:(j + 1) * S5_GROUP, half * LANES:(half + 1) * LANES] = piece.astype(BF16)


def _toeplitz(kpad):
    n, q, width = kpad.shape
    per_step = 8
    return pl.pallas_call(
        _toeplitz_kernel,
        grid=(n // per_step,),
        in_specs=[pl.BlockSpec((per_step, q, width), lambda i: (i, 0, 0))],
        out_specs=pl.BlockSpec((per_step, S5_CHUNK * q, 256), lambda i: (i, 0, 0)),
        out_shape=jax.ShapeDtypeStruct((n, S5_CHUNK * q, 256), BF16),
        compiler_params=_params("parallel"),
        name="s5_toeplitz",
    )(kpad)


def _s5_tables(lam_re, lam_im, log_dt, b_re, b_im, c_re, c_im):
    c_len = S5_CHUNK
    lr = jnp.minimum(lam_re.astype(F32), S5_MAX_RE)
    li = lam_im.astype(F32)
    dt = jnp.exp(log_dt.astype(F32))[..., None]
    zr, zi = lr * dt, li * dt
    e1 = jnp.exp(zr)
    ar, ai = e1 * jnp.cos(zi), e1 * jnp.sin(zi)
    den = lr * lr + li * li
    nr, ni = ar - 1.0, ai
    fr = (nr * lr + ni * li) / den
    fi = (ni * lr - nr * li) / den
    bbr = (fr[..., None] * b_re - fi[..., None] * b_im).swapaxes(-1, -2)
    bbi = (fr[..., None] * b_im + fi[..., None] * b_re).swapaxes(-1, -2)
    tau = jnp.arange(c_len + 1, dtype=F32)[:, None]
    mag = jnp.exp(tau * zr[..., None, :])
    pr = mag * jnp.cos(tau * zi[..., None, :])
    pi = mag * jnp.sin(tau * zi[..., None, :])
    prq, piq = pr[..., :, None, :], pi[..., :, None, :]
    m_r = prq * bbr[..., None, :, :] - piq * bbi[..., None, :, :]
    m_i = prq * bbi[..., None, :, :] + piq * bbr[..., None, :, :]
    kern = (jnp.einsum('ldgpn,ldgtqn->ldgqtp', c_re, m_r[..., :c_len, :, :], precision=HIGHEST)
            - jnp.einsum('ldgpn,ldgtqn->ldgqtp', c_im, m_i[..., :c_len, :, :], precision=HIGHEST))

    def per_direction(x, axis, fwd_flipped):
        f, b = x[:, 0], x[:, 1]
        f, b = (jnp.flip(f, axis), b) if fwd_flipped else (f, jnp.flip(b, axis))
        return jnp.stack([f, b], axis=1)

    padded = jnp.pad(kern, ((0, 0),) * 4 + ((c_len - 1, 0), (0, 0)))
    padded = per_direction(padded, 3, False)
    padded = jnp.pad(padded, ((0, 0),) * 4 + ((0, 1), (0, 0)))
    lead = padded.shape[:3]
    wt = _toeplitz(padded.reshape(-1, S5_GROUP, 2 * c_len * S5_GROUP)).reshape(lead + (256, 256))
    er = per_direction(m_r[..., :c_len, :, :], 2, True).reshape(wt.shape[:3] + (256, S5_STATE))
    ei = per_direction(m_i[..., :c_len, :, :], 2, True).reshape(wt.shape[:3] + (256, S5_STATE))
    we = jnp.concatenate([er, ei, ei, er], axis=-1)
    c_rt, c_it = c_re.swapaxes(-1, -2)[..., None, :], c_im.swapaxes(-1, -2)[..., None, :]
    p_rt, p_it = pr.swapaxes(-1, -2)[..., 1:, None], pi.swapaxes(-1, -2)[..., 1:, None]
    sr = per_direction(c_rt * p_rt - c_it * p_it, 3, False).reshape(wt.shape[:3] + (S5_STATE, 256))
    si = per_direction(c_rt * p_it + c_it * p_rt, 3, False).reshape(wt.shape[:3] + (S5_STATE, 256))
    ws = jnp.concatenate([sr, -si], axis=3)
    a_r, a_i = pr[..., c_len, :], pi[..., c_len, :]
    coef = jnp.stack([jnp.concatenate([a_r, a_r], -1),
                      jnp.concatenate([-a_i, a_i], -1),
                      jnp.concatenate([a_i, -a_i], -1)], axis=2)
    return wt.astype(BF16), we.astype(BF16), ws.astype(BF16), coef


def _block_transpose8(xs, lane):
    for k in (2, 1, 0):
        shift = S5_GROUP << k
        bit = jnp.bitwise_and(lax.shift_right_logical(lane, 4 + k), 1)
        new = list(xs)
        for a in range(8):
            if (a >> k) & 1:
                continue
            b = a + (1 << k)
            if 2 * shift == LANES:
                both = pltpu.roll(jnp.where(bit == 0, xs[b], xs[a]), shift, 1)
                new[a] = jnp.where(bit == 0, xs[a], both)
                new[b] = jnp.where(bit == 1, xs[b], both)
            else:
                new[a] = jnp.where(bit == 0, xs[a], pltpu.roll(xs[b], shift, 1))
                new[b] = jnp.where(bit == 1, xs[b], pltpu.roll(xs[a], LANES - shift, 1))
        xs = new
    return xs


def _s5_scan_kernel(u_ref, wt_ref, we_ref, ws_ref, coef_ref, *rest, reverse, add_prev):
    if add_prev:
        prev_ref, y_ref, ug_scr, yg_scr, e_scr, es_scr, s_scr, carry = rest
    else:
        y_ref, ug_scr, yg_scr, e_scr, es_scr, s_scr, carry = rest
    n_rows = ug_scr.shape[1]
    n_blk = S5_WIDTH // LANES
    per_blk = LANES // S5_GROUP
    halves = S5_CHUNK // per_blk

    @pl.when(pl.program_id(0) == 0)
    def _():
        carry[...] = jnp.zeros_like(carry)

    lane = lax.broadcasted_iota(jnp.int32, (n_rows, LANES), 1)

    for blk in range(n_blk):
        for half in range(halves):
            xs = [u_ref[blk, pl.ds(half * per_blk + jl, n_rows, stride=S5_CHUNK), :].astype(BF16)
                  for jl in range(per_blk)]
            ys = _block_transpose8(xs, lane)
            for gl in range(per_blk):
                ug_scr[blk * per_blk + gl, :, half * LANES:(half + 1) * LANES] = ys[gl].astype(BF16)

    for g in range(S5_GROUPS):
        e = jnp.dot(ug_scr[g], we_ref[g], preferred_element_type=F32)
        e_scr[g * n_rows:(g + 1) * n_rows, :] = e[:, :LANES]
        es_scr[g * n_rows:(g + 1) * n_rows, :] = e[:, LANES:]

    c1, c2, c3 = coef_ref[0], coef_ref[1], coef_ref[2]

    def step(i, vs):
        v, vp = vs
        r = (n_rows - 1 - i) if reverse else i
        rows = pl.ds(r, S5_GROUPS, stride=n_rows)
        s_scr[rows, :] = v
        v_new = v * c1 + vp * c2 + e_scr[rows, :]
        vp_new = vp * c1 + v * c3 + es_scr[rows, :]
        return v_new, vp_new

    v, vp = lax.fori_loop(0, n_rows, step, (carry[0], carry[1]), unroll=8)
    carry[0] = v
    carry[1] = vp

    for blk in range(n_blk):
        for gl in range(per_blk):
            g = blk * per_blk + gl
            s_in = s_scr[g * n_rows:(g + 1) * n_rows, :].astype(BF16)
            yg_scr[gl] = (jnp.dot(ug_scr[g], wt_ref[g], preferred_element_type=F32)
                          + jnp.dot(s_in, ws_ref[g], preferred_element_type=F32))
        for half in range(halves):
            zs = [yg_scr[gl, :, half * LANES:(half + 1) * LANES].astype(BF16) for gl in range(per_blk)]
            ws = [w.astype(F32) for w in _block_transpose8(zs, lane)]
            for tl in range(per_blk):
                rows = pl.ds(half * per_blk + tl, n_rows, stride=S5_CHUNK)
                out = ws[tl]
                if add_prev:
                    out = out + prev_ref[blk, rows, :]
                y_ref[blk, rows, :] = out


def _s5_direction(u4, tables, prev, reverse, layer):
    n_blk, t, _ = u4.shape
    n_chunks = t // S5_CHUNK
    rows = min(S5_TILE, n_chunks)
    n_tiles = n_chunks // rows
    wt, we, ws, coef = tables
    tile = (lambda i: (0, n_tiles - 1 - i, 0)) if reverse else (lambda i: (0, i, 0))
    table = lambda i: (layer, int(reverse), 0, 0, 0)
    once = pl.Buffered(1)
    in_specs = [
        pl.BlockSpec((n_blk, rows * S5_CHUNK, LANES), tile),
        pl.BlockSpec((None, None, S5_GROUPS, 256, 256), table, pipeline_mode=once),
        pl.BlockSpec((None, None, S5_GROUPS, 256, 256), table, pipeline_mode=once),
        pl.BlockSpec((None, None, S5_GROUPS, LANES, 256), table, pipeline_mode=once),
        pl.BlockSpec((None, None, 3, S5_GROUPS, LANES), table, pipeline_mode=once),
    ]
    args = [u4, wt, we, ws, coef]
    if prev is not None:
        in_specs.append(pl.BlockSpec((n_blk, rows * S5_CHUNK, LANES), tile))
        args.append(prev)
    return pl.pallas_call(
        functools.partial(_s5_scan_kernel, reverse=reverse, add_prev=prev is not None),
        grid=(n_tiles,),
        in_specs=in_specs,
        out_specs=pl.BlockSpec((n_blk, rows * S5_CHUNK, LANES), tile),
        out_shape=jax.ShapeDtypeStruct((n_blk, t, LANES), F32),
        scratch_shapes=[
            pltpu.VMEM((S5_GROUPS, rows, 256), BF16),
            pltpu.VMEM((LANES // S5_GROUP, rows, 256), F32),
            pltpu.VMEM((S5_GROUPS * rows, LANES), F32),
            pltpu.VMEM((S5_GROUPS * rows, LANES), F32),
            pltpu.VMEM((S5_GROUPS * rows, LANES), F32),
            pltpu.VMEM((2, S5_GROUPS, LANES), F32),
        ],
        compiler_params=_params("arbitrary"),
        name="s5_bwd" if reverse else "s5_fwd",
    )(*args)


def _s5_mix(u4, tables, layer):
    y = _s5_direction(u4, tables, None, False, layer)
    return _s5_direction(u4, tables, y, True, layer)


def _bmm(a, b):
    return lax.dot_general(a, b, (((2,), (1,)), ((0,), (0,))), preferred_element_type=F32)


def _dn_kernel(qf, kf, vf, gf, gtf, qb, kb_, vb, gb, gtb, of_ref, ob_ref,
               state, wq_scr, kdt_scr, at_scr, u_scr, gam_scr):
    c_len = DN_CHUNK
    n_ch = gtf.shape[0]
    n_gate = 2 * DN_HEADS
    n_chain = 2 * DN_HEADS

    @pl.when(pl.program_id(0) == 0)
    def _():
        state[...] = jnp.zeros_like(state)

    ri = lax.broadcasted_iota(jnp.int32, (c_len, c_len), 0)
    ci = lax.broadcasted_iota(jnp.int32, (c_len, c_len), 1)
    eye = jnp.where(ri == ci, 1.0, 0.0).astype(F32)
    dirs = ((qf, kf, vf, gf, gtf, ri >= ci, ri > ci, c_len - 1),
            (qb, kb_, vb, gb, gtb, ri <= ci, ri < ci, 0))

    for d, (q_ref, k_ref, v_ref, gate_ref, gate_t_ref, incl, strict, last) in enumerate(dirs):
        gates = gate_ref[...].reshape(n_ch, c_len, LANES)
        gates_t = gate_t_ref[...]
        for h in range(DN_HEADS):
            idx = d * DN_HEADS + h
            hs = slice(h * DN_HEAD_DIM, (h + 1) * DN_HEAD_DIM)
            gcol = gates[:, :, 2 * n_gate + idx:2 * n_gate + idx + 1]
            bcol = gates[:, :, n_gate + idx:n_gate + idx + 1]
            grow = gates_t[:, 2 * n_gate + idx:2 * n_gate + idx + 1, :]
            glast = grow[:, :, last:last + 1]
            qb = q_ref[:, hs].reshape(n_ch, c_len, DN_HEAD_DIM)
            kb = k_ref[:, hs].reshape(n_ch, c_len, DN_HEAD_DIM)
            qh, kh = qb.astype(F32), kb.astype(F32)
            vh = v_ref[:, hs].reshape(n_ch, c_len, DN_HEAD_DIM).astype(F32)
            qk_kk = lax.dot_general(jnp.concatenate([qb, kb], axis=1), kb,
                                    (((2,), (2,)), ((0,), (0,))), preferred_element_type=F32)
            qk, kk = qk_kk[:, :c_len], qk_kk[:, c_len:]
            decay = jnp.where(incl, jnp.exp(jnp.where(incl, gcol - grow, 0.0)), 0.0)
            a_mat = jnp.where(strict, bcol * kk * decay, 0.0)
            pw = -a_mat
            inv = eye + pw
            for _ in range(5):
                pwb = pw.astype(BF16)
                pw = _bmm(pwb, pwb)
                inv = inv + _bmm(inv.astype(BF16), pw.astype(BF16))
            egc = jnp.exp(gcol)
            rhs = jnp.concatenate([vh * bcol, kh * (bcol * egc)], axis=2).astype(BF16)
            uw = _bmm(inv.astype(BF16), rhs)
            wq = jnp.concatenate([uw[:, :, DN_HEAD_DIM:], qh * egc], axis=1).astype(BF16)
            attn = jnp.where(incl, qk * decay, 0.0).astype(BF16)
            k_dec_t = jnp.swapaxes(kh * jnp.exp(glast - gcol), 1, 2).astype(BF16)
            gamma = jnp.broadcast_to(jnp.exp(glast), (n_ch, 1, LANES))
            for c in range(n_ch):
                slot = (c if d == 0 else n_ch - 1 - c) * n_chain + idx
                wq_scr[slot] = wq[c]
                kdt_scr[slot] = k_dec_t[c]
                at_scr[slot] = attn[c]
                u_scr[slot] = uw[c, :, :DN_HEAD_DIM]
                gam_scr[slot] = gamma[c]

    for step in range(n_ch):
        grp = slice(step * n_chain, (step + 1) * n_chain)
        s = state[...]
        wq_s = _bmm(wq_scr[grp], s.astype(BF16))
        v_nb = (u_scr[grp] - wq_s[:, :c_len]).astype(BF16)
        o = wq_s[:, c_len:] + _bmm(at_scr[grp], v_nb)
        state[...] = s * gam_scr[grp] + _bmm(kdt_scr[grp], v_nb)
        for d, o_ref in enumerate((of_ref, ob_ref)):
            c = step if d == 0 else n_ch - 1 - step
            for h in range(DN_HEADS):
                o_ref[c * c_len:(c + 1) * c_len, h * DN_HEAD_DIM:(h + 1) * DN_HEAD_DIM] = (
                    o[d * DN_HEADS + h].astype(BF16))


def _dn_mix(q, k, v, gates, gates_t):
    t = q.shape[0]
    n_chunks = t // DN_CHUNK
    n_ch = min(DN_STEP_CHUNKS, n_chunks)
    n_steps = n_chunks // n_ch
    rows = n_ch * DN_CHUNK
    n_inst = n_ch * 2 * DN_HEADS
    specs = []
    for row, row3 in ((lambda i: (i, 0), lambda i: (i, 0, 0)),
                      (lambda i: (n_steps - 1 - i, 0), lambda i: (n_steps - 1 - i, 0, 0))):
        specs += [
            pl.BlockSpec((rows, DN_WIDTH), row),
            pl.BlockSpec((rows, DN_WIDTH), row),
            pl.BlockSpec((rows, DN_WIDTH), row),
            pl.BlockSpec((rows, LANES), row),
            pl.BlockSpec((n_ch, 6 * DN_HEADS, DN_CHUNK), row3),
        ]
    return pl.pallas_call(
        _dn_kernel,
        grid=(n_steps,),
        in_specs=specs,
        out_specs=[
            pl.BlockSpec((rows, DN_WIDTH), lambda i: (i, 0)),
            pl.BlockSpec((rows, DN_WIDTH), lambda i: (n_steps - 1 - i, 0)),
        ],
        out_shape=[jax.ShapeDtypeStruct((t, DN_WIDTH), BF16), jax.ShapeDtypeStruct((t, DN_WIDTH), BF16)],
        scratch_shapes=[
            pltpu.VMEM((2 * DN_HEADS, DN_HEAD_DIM, DN_HEAD_DIM), F32),
            pltpu.VMEM((n_inst, 2 * DN_CHUNK, DN_HEAD_DIM), BF16),
            pltpu.VMEM((n_inst, DN_HEAD_DIM, DN_CHUNK), BF16),
            pltpu.VMEM((n_inst, DN_CHUNK, DN_CHUNK), BF16),
            pltpu.VMEM((n_inst, DN_CHUNK, DN_HEAD_DIM), F32),
            pltpu.VMEM((n_inst, 1, LANES), F32),
        ],
        compiler_params=_params("arbitrary"),
        name="dn_mix",
    )(q, k, v, gates, gates_t, q, k, v, gates, gates_t)


def _post_kernel(x_ref, u_ref, ys_ref, of_ref, ob_ref, z_ref, d_ref, wglu_ref, bglu_ref, nw_ref,
                 wout_ref, nffn_ref, wr_ref, br_ref, before_ref, x1_ref, h_ref, route_ref, cnt_ref, base):
    i = pl.program_id(0)
    tm = x_ref.shape[0]

    @pl.when(i == 0)
    def _():
        base[...] = jnp.zeros_like(base)

    y = jnp.concatenate([ys_ref[b] + d_ref[:, b * LANES:(b + 1) * LANES] * u_ref[b]
                         for b in range(S5_WIDTH // LANES)], axis=1)
    y = 0.5 * y * (1.0 + lax.erf(y * (2.0 ** -0.5)))
    gate = jnp.dot(y.astype(BF16), wglu_ref[...], preferred_element_type=F32) + bglu_ref[...]
    y_s5 = y * _sigmoid(gate)
    acc = x_ref[...] + jnp.dot(y_s5.astype(BF16), wout_ref[0:S5_WIDTH, :], preferred_element_type=F32)
    for h in range(DN_HEADS):
        hs = slice(h * DN_HEAD_DIM, (h + 1) * DN_HEAD_DIM)
        o = of_ref[:, hs].astype(F32) + ob_ref[:, hs].astype(F32)
        zh = z_ref[:, hs]
        o = o * lax.rsqrt(jnp.mean(o * o, axis=-1, keepdims=True) + NORM_EPS) * nw_ref[...]
        y_dn = o * _silu(zh)
        acc = acc + jnp.dot(y_dn.astype(BF16),
                            wout_ref[S5_WIDTH + h * DN_HEAD_DIM:S5_WIDTH + (h + 1) * DN_HEAD_DIM, :],
                            preferred_element_type=F32)
    x1_ref[...] = acc
    hn = acc * lax.rsqrt(jnp.mean(acc * acc, axis=-1, keepdims=True) + NORM_EPS) * nffn_ref[...]
    _matrix_to_rows(h_ref, hn)

    logits = _dot_split(hn, wr_ref[...]) + br_ref[...]
    lane_i = lax.broadcasted_iota(jnp.int32, logits.shape, 1)
    lane = lane_i.astype(F32)
    neg = jnp.float32(-jnp.inf)
    big = jnp.float32(LANES)
    gl = jnp.where(lane_i < N_EXPERT_GROUPS, logits, neg)
    gmax = jnp.max(gl, axis=-1, keepdims=True)
    g_sel = jnp.min(jnp.where(gl == gmax, lane, big), axis=-1, keepdims=True)
    p_group = 1.0 / jnp.sum(jnp.exp(gl - gmax), axis=-1, keepdims=True)
    lo = N_EXPERT_GROUPS + g_sel * EXPERTS_PER_GROUP
    el = jnp.where((lane >= lo) & (lane < lo + EXPERTS_PER_GROUP), logits, neg)
    top1 = jnp.max(el, axis=-1, keepdims=True)
    idx1 = jnp.min(jnp.where(el == top1, lane, big), axis=-1, keepdims=True)
    el2 = jnp.where(lane == idx1, neg, el)
    top2 = jnp.max(el2, axis=-1, keepdims=True)
    idx2 = jnp.min(jnp.where(el2 == top2, lane, big), axis=-1, keepdims=True)
    e21 = jnp.exp(top2 - top1)
    w1 = p_group / (1.0 + e21)
    w2 = w1 * e21
    oh1 = jnp.where(lane == idx1, 1.0, 0.0).astype(F32)
    oh2 = jnp.where(lane == idx2, 1.0, 0.0).astype(F32)
    ohs = oh1 + oh2
    prior = jnp.dot(before_ref[...], ohs.astype(BF16), preferred_element_type=F32) + base[0:1, :]
    rank1 = jnp.sum(oh1 * prior, axis=-1, keepdims=True)
    rank2 = jnp.sum(oh2 * prior, axis=-1, keepdims=True)
    base[0:1, :] = base[0:1, :] + jnp.sum(ohs, axis=0, keepdims=True)
    e1 = idx1 - N_EXPERT_GROUPS
    e2 = idx2 - N_EXPERT_GROUPS
    route = jnp.where(lane_i == 0, e1, jnp.where(lane_i == 1, e2, jnp.where(lane_i == 2, w1, jnp.where(
        lane_i == 3, w2, jnp.where(lane_i == 4, rank1, jnp.where(lane_i == 5, rank2, 0.0))))))
    route_ref[...] = route
    cnt_ref[...] = base[...]


def _post(x, u, ys, o_f, o_b, z, s5_d, w_glu, b_glu, dn_norm_w, w_out, layer, norm_ffn, w_rg, b_rg, w_re,
          b_re):
    t = x.shape[0]
    tm = min(ROW_TILE, t)
    wr = jnp.concatenate([w_rg, w_re.transpose(1, 0, 2).reshape(D_MODEL, N_EXPERTS)], axis=1)
    wr = _split_weight(jnp.pad(wr.astype(F32), ((0, 0), (0, LANES - N_EXPERT_GROUPS - N_EXPERTS))))
    br = jnp.pad(jnp.concatenate([b_rg, b_re.reshape(-1)]).astype(F32),
                 (0, LANES - N_EXPERT_GROUPS - N_EXPERTS)).reshape(1, LANES)
    row = lambda i: (i, 0)
    const = lambda i: (0, 0)
    return pl.pallas_call(
        _post_kernel,
        grid=(t // tm,),
        in_specs=[
            pl.BlockSpec((tm, D_MODEL), row),
            pl.BlockSpec((S5_WIDTH // LANES, tm, LANES), lambda i: (0, i, 0)),
            pl.BlockSpec((S5_WIDTH // LANES, tm, LANES), lambda i: (0, i, 0)),
            pl.BlockSpec((tm, DN_WIDTH), row),
            pl.BlockSpec((tm, DN_WIDTH), row),
            pl.BlockSpec((tm, DN_WIDTH), row),
            pl.BlockSpec((1, S5_WIDTH), const),
            pl.BlockSpec((S5_WIDTH, S5_WIDTH), const),
            pl.BlockSpec((1, S5_WIDTH), const),
            pl.BlockSpec((1, DN_HEAD_DIM), const),
            pl.BlockSpec((None, D_MODEL, D_MODEL), lambda i: (layer, 0, 0)),
            pl.BlockSpec((1, D_MODEL), const),
            pl.BlockSpec((D_MODEL, 2 * LANES), const),
            pl.BlockSpec((1, LANES), const),
            pl.BlockSpec((tm, tm), const),
        ],
        out_specs=[
            pl.BlockSpec((tm, D_MODEL), row),
            pl.BlockSpec((tm * ROW_SPLIT, LANES), row),
            pl.BlockSpec((tm, LANES), row),
            pl.BlockSpec((SUBLANES, LANES), const),
        ],
        out_shape=[
            jax.ShapeDtypeStruct((t, D_MODEL), F32),
            jax.ShapeDtypeStruct((t * ROW_SPLIT, LANES), U32),
            jax.ShapeDtypeStruct((t, LANES), F32),
            jax.ShapeDtypeStruct((SUBLANES, LANES), F32),
        ],
        scratch_shapes=[pltpu.VMEM((SUBLANES, LANES), F32)],
        compiler_params=_params("arbitrary"),
        name="mixer_post",
    )(x, u, ys, o_f, o_b, z, s5_d.reshape(1, -1).astype(F32), w_glu.astype(BF16),
      b_glu.reshape(1, -1).astype(F32), dn_norm_w.reshape(1, -1).astype(F32), w_out,
      norm_ffn.reshape(1, -1).astype(F32), wr, br,
      jnp.asarray(np.tril(np.ones((tm, tm), np.float32), -1), BF16))


ROW_SPLIT = D_MODEL // (2 * LANES)
U32 = jnp.uint32


def _rows_to_matrix(ref, n_rows, lead=None):
    low, high = [], []
    for s in range(ROW_SPLIT):
        idx = pl.ds(s, n_rows, stride=ROW_SPLIT)
        word = ref[idx, :] if lead is None else ref[lead, idx, :]
        low.append(pltpu.bitcast(word << 16, F32))
        high.append(pltpu.bitcast(word & jnp.uint32(0xFFFF0000), F32))
    return jnp.concatenate(low + high, axis=1)


def _matrix_to_rows(ref, val):
    n_rows = val.shape[0]
    half = ROW_SPLIT * LANES
    bits = pltpu.bitcast(val.astype(BF16).astype(F32), U32)
    for s in range(ROW_SPLIT):
        low = bits[:, s * LANES:(s + 1) * LANES] >> 16
        high = bits[:, half + s * LANES:half + (s + 1) * LANES]
        ref[pl.ds(s, n_rows, stride=ROW_SPLIT), :] = high | low


def _row(ref, r):
    return ref.at[pl.ds(pl.multiple_of(r * ROW_SPLIT, ROW_SPLIT), ROW_SPLIT), :]


def _dispatch_kernel(dest_ref, zblk_ref, h_ref, xs_ref, zero_buf, sem, zsem):
    i = pl.program_id(0)
    tm = h_ref.shape[0] // ROW_SPLIT
    blk_rows = zero_buf.shape[0]

    @pl.when(i == 0)
    def _():
        zero_buf[...] = jnp.zeros_like(zero_buf)

        def zero_copy(j):
            start = pl.multiple_of(zblk_ref[j] * blk_rows, blk_rows)
            return pltpu.make_async_copy(zero_buf, xs_ref.at[pl.ds(start, blk_rows), :], zsem)

        def zstart(j, _):
            @pl.when(zblk_ref[j] >= 0)
            def _():
                zero_copy(j).start()
            return 0

        def zwait(j, _):
            @pl.when(zblk_ref[j] >= 0)
            def _():
                zero_copy(j).wait()
            return 0

        lax.fori_loop(0, zblk_ref.shape[0], zstart, 0)
        lax.fori_loop(0, zblk_ref.shape[0], zwait, 0)

    def copy(r, k):
        slot = dest_ref[2 * (i * tm + r) + k]
        return pltpu.make_async_copy(_row(h_ref, r), _row(xs_ref, slot), sem)

    def start(r, _):
        copy(r, 0).start(priority=0)
        copy(r, 1).start(priority=1)
        return 0

    def wait(r, _):
        copy(r, 0).wait()
        copy(r, 1).wait()
        return 0

    lax.fori_loop(0, tm, start, 0, unroll=8)
    lax.fori_loop(0, tm, wait, 0, unroll=8)


def _dispatch(h, dest, zero_blocks, n_slots):
    t = h.shape[0] // ROW_SPLIT
    tm = min(MOE_TOK_TILE, t)
    grid_spec = pltpu.PrefetchScalarGridSpec(
        num_scalar_prefetch=2,
        grid=(t // tm,),
        in_specs=[pl.BlockSpec((tm * ROW_SPLIT, LANES), lambda i, *_: (i, 0))],
        out_specs=pl.BlockSpec(memory_space=pl.ANY),
        scratch_shapes=[pltpu.VMEM((MOE_BLOCK * ROW_SPLIT, LANES), U32),
                        pltpu.SemaphoreType.DMA(()), pltpu.SemaphoreType.DMA(())],
    )
    return pl.pallas_call(
        _dispatch_kernel,
        grid_spec=grid_spec,
        out_shape=jax.ShapeDtypeStruct((n_slots * ROW_SPLIT, LANES), U32),
        compiler_params=_params("arbitrary"),
        name="moe_dispatch",
    )(dest, zero_blocks, h)


def _expert_kernel(be_ref, run_ref, nxt_ref, used_ref, xs_ref, wg_ref, wu_ref, wd_ref, ys_ref,
                   wg_buf, wu_buf, wd_buf, wgu_b, wd_b, wsem, *, layer):
    i = pl.program_id(0)
    blk = ys_ref.shape[0] // ROW_SPLIT

    def weight_copies(expert, slot):
        return (pltpu.make_async_copy(wg_ref.at[layer, expert], wg_buf.at[slot], wsem.at[slot]),
                pltpu.make_async_copy(wu_ref.at[layer, expert], wu_buf.at[slot], wsem.at[slot]),
                pltpu.make_async_copy(wd_ref.at[layer, expert], wd_buf.at[slot], wsem.at[slot]))

    run = run_ref[i]
    new_run = (i == 0) | (run != run_ref[jnp.maximum(i - 1, 0)])
    wslot = run % 2

    @pl.when(i == 0)
    def _():
        for c in weight_copies(be_ref[0], 0):
            c.start()

    @pl.when(new_run)
    def _():
        for c in weight_copies(be_ref[i], wslot):
            c.wait()

        @pl.when(nxt_ref[i] >= 0)
        def _():
            for c in weight_copies(nxt_ref[i], 1 - wslot):
                c.start()

        wgu_b[:, :D_EXPERT] = wg_buf[wslot].astype(BF16)
        wgu_b[:, D_EXPERT:] = wu_buf[wslot].astype(BF16)
        wd_b[...] = wd_buf[wslot].astype(BF16)

    @pl.when(i < used_ref[0])
    def _():
        xb = _rows_to_matrix(xs_ref, blk).astype(BF16)
        gu = jnp.dot(xb, wgu_b[...], preferred_element_type=F32)
        g, u = gu[:, :D_EXPERT], gu[:, D_EXPERT:]
        hid = (_silu(g) * u).astype(BF16)
        _matrix_to_rows(ys_ref, jnp.dot(hid, wd_b[...], preferred_element_type=F32))

    @pl.when(i >= used_ref[0])
    def _():
        ys_ref[...] = jnp.zeros_like(ys_ref)


def _experts(xs, block_expert, block_run, next_expert, used, w_gate, w_up, w_down, layer):
    n_slots = xs.shape[0] // ROW_SPLIT
    n_blocks = n_slots // MOE_BLOCK
    grid_spec = pltpu.PrefetchScalarGridSpec(
        num_scalar_prefetch=4,
        grid=(n_blocks,),
        in_specs=[pl.BlockSpec((MOE_BLOCK * ROW_SPLIT, LANES), lambda i, *_: (i, 0))]
        + [pl.BlockSpec(memory_space=pl.ANY)] * 3,
        out_specs=pl.BlockSpec((MOE_BLOCK * ROW_SPLIT, LANES), lambda i, *_: (i, 0)),
        scratch_shapes=[
            pltpu.VMEM((2, D_MODEL, D_EXPERT), F32),
            pltpu.VMEM((2, D_MODEL, D_EXPERT), F32),
            pltpu.VMEM((2, D_EXPERT, D_MODEL), F32),
            pltpu.VMEM((D_MODEL, 2 * D_EXPERT), BF16),
            pltpu.VMEM((D_EXPERT, D_MODEL), BF16),
            pltpu.SemaphoreType.DMA((2,)),
        ],
    )
    return pl.pallas_call(
        functools.partial(_expert_kernel, layer=layer),
        grid_spec=grid_spec,
        out_shape=jax.ShapeDtypeStruct((n_slots * ROW_SPLIT, LANES), U32),
        compiler_params=_params("arbitrary"),
        name="moe_experts",
    )(block_expert, block_run, next_expert, used, xs, w_gate, w_up, w_down)


def _combine_kernel(dest_ref, x1_ref, route_ref, ys_ref, nw_ref, out_ref, buf, sem, *, final_norm):
    i = pl.program_id(0)
    tm = x1_ref.shape[0]

    def copy(r, k):
        slot = dest_ref[2 * (i * tm + r) + k]
        dst = buf.at[k, pl.ds(pl.multiple_of(r * ROW_SPLIT, ROW_SPLIT), ROW_SPLIT), :]
        return pltpu.make_async_copy(_row(ys_ref, slot), dst, sem)

    def start(r, _):
        copy(r, 0).start(priority=0)
        copy(r, 1).start(priority=1)
        return 0

    def wait(r, _):
        copy(r, 0).wait()
        copy(r, 1).wait()
        return 0

    lax.fori_loop(0, tm, start, 0, unroll=8)
    lax.fori_loop(0, tm, wait, 0, unroll=8)
    route = route_ref[...]
    out = (x1_ref[...] + route[:, 2:3] * _rows_to_matrix(buf, tm, 0)
           + route[:, 3:4] * _rows_to_matrix(buf, tm, 1))
    if final_norm:
        out = out * lax.rsqrt(jnp.mean(out * out, axis=-1, keepdims=True) + NORM_EPS) * nw_ref[...]
    out_ref[...] = out


def _combine(x1, route, ys, dest, norm_w, final_norm):
    t = x1.shape[0]
    tm = min(MOE_TOK_TILE, t)
    grid_spec = pltpu.PrefetchScalarGridSpec(
        num_scalar_prefetch=1,
        grid=(t // tm,),
        in_specs=[
            pl.BlockSpec((tm, D_MODEL), lambda i, dest: (i, 0)),
            pl.BlockSpec((tm, LANES), lambda i, dest: (i, 0)),
            pl.BlockSpec(memory_space=pl.ANY),
            pl.BlockSpec((1, D_MODEL), lambda i, dest: (0, 0)),
        ],
        out_specs=pl.BlockSpec((tm, D_MODEL), lambda i, dest: (i, 0)),
        scratch_shapes=[pltpu.VMEM((2, tm * ROW_SPLIT, LANES), U32), pltpu.SemaphoreType.DMA(())],
    )
    return pl.pallas_call(
        functools.partial(_combine_kernel, final_norm=final_norm),
        grid_spec=grid_spec,
        out_shape=jax.ShapeDtypeStruct((t, D_MODEL), F32),
        compiler_params=_params("arbitrary"),
        name="moe_combine",
    )(dest, x1, route, ys, norm_w.reshape(1, D_MODEL).astype(F32))


def _moe(x1, h, route, counts, w_gate, w_up, w_down, layer, norm_w, final_norm):
    t = x1.shape[0]
    n_blocks = -(-(2 * t) // MOE_BLOCK) + N_EXPERTS
    n_slots = n_blocks * MOE_BLOCK
    cnt = counts[0, N_EXPERT_GROUPS:N_EXPERT_GROUPS + N_EXPERTS].astype(jnp.int32)
    padded = ((cnt + MOE_BLOCK - 1) // MOE_BLOCK) * MOE_BLOCK
    pad_end = jnp.cumsum(padded)
    pad_start = pad_end - padded
    expert = route[:, 0:2].astype(jnp.int32)
    rank = route[:, 4:6].astype(jnp.int32)
    experts = jnp.arange(N_EXPERTS, dtype=jnp.int32)
    dest = (jnp.sum(jnp.where(expert[..., None] == experts, pad_start, 0), axis=-1) + rank).reshape(-1)
    blocks = jnp.arange(n_blocks, dtype=jnp.int32)
    block_expert = jnp.minimum(jnp.sum(pad_end[None, :] <= (blocks * MOE_BLOCK)[:, None], axis=1),
                               N_EXPERTS - 1).astype(jnp.int32)
    starts_run = jnp.concatenate([jnp.ones((1,), bool), block_expert[1:] != block_expert[:-1]])
    block_run = (jnp.cumsum(starts_run) - 1).astype(jnp.int32)
    later_start = starts_run[None, :] & (blocks[None, :] > blocks[:, None])
    next_expert = jnp.where(jnp.any(later_start, axis=1),
                            block_expert[jnp.argmax(later_start, axis=1)], -1).astype(jnp.int32)
    used = (pad_end[-1:] // MOE_BLOCK).astype(jnp.int32)
    tail = blocks[n_blocks - N_EXPERTS:]
    zero_blocks = jnp.concatenate([jnp.where(padded > 0, pad_end // MOE_BLOCK - 1, -1),
                                   jnp.where(tail >= used, tail, -1)]).astype(jnp.int32)
    xs = _dispatch(h, dest, zero_blocks, n_slots)
    ys = _experts(xs, block_expert, block_run, next_expert, used, w_gate, w_up, w_down, layer)
    return _combine(x1, route, ys, dest, norm_w, final_norm)


def kernel(x, norm_mix, w_in, s5_lam_re, s5_lam_im, s5_log_dt, s5_b_re, s5_b_im, s5_c_re, s5_c_im,
           s5_d, s5_w_glu, s5_b_glu, gdn_conv_w, gdn_a_log, gdn_dt_bias, gdn_norm_w, w_out, norm_ffn,
           router_w_group, router_b_group, router_w_expert, router_b_expert,
           expert_w_gate, expert_w_up, expert_w_down, norm_final):
    bsz, seq, d = x.shape
    depth = norm_mix.shape[0]
    xt = x.astype(F32).reshape(bsz * seq, d)
    tables = _s5_tables(s5_lam_re, s5_lam_im, s5_log_dt, s5_b_re, s5_b_im, s5_c_re, s5_c_im)
    n_main = S5_WIDTH + 4 * DN_WIDTH
    w_main = w_in[:, :, :n_main].astype(BF16)
    w_out_b = w_out.astype(BF16)
    for i in range(depth):
        u, z, q, k, v, gates = _inproj(xt, norm_mix[i], w_main, w_in[i, :, n_main:], i, gdn_conv_w[i],
                                       gdn_a_log[i], gdn_dt_bias[i])
        ys = _s5_mix(u, tables, i)
        gates_t = (gates[:, :6 * DN_HEADS].reshape(-1, DN_CHUNK, 6 * DN_HEADS).transpose(0, 2, 1))
        o_f, o_b = _dn_mix(q, k, v, gates, gates_t)
        x1, h, route, counts = _post(xt, u, ys, o_f, o_b, z, s5_d[i], s5_w_glu[i], s5_b_glu[i],
                                     gdn_norm_w[i], w_out_b, i, norm_ffn[i], router_w_group[i],
                                     router_b_group[i], router_w_expert[i], router_b_expert[i])
        xt = _moe(x1, h, route, counts, expert_w_gate, expert_w_up, expert_w_down, i,
                  norm_final, i == depth - 1)
    return xt.reshape(bsz, seq, d)
```

```python
import functools
import math

import jax
import jax.numpy as jnp
import numpy as np
from jax import lax
from jax.experimental import pallas as pl
from jax.experimental.pallas import tpu as pltpu

F32 = jnp.float32
BF16 = jnp.bfloat16
HIGHEST = lax.Precision.HIGHEST

D_MODEL = 1024
S5_WIDTH = 512
S5_GROUP = 16
S5_GROUPS = 32
S5_STATE = 64
S5_MAX_RE = -1e-4
DN_HEADS = 4
DN_HEAD_DIM = 128
DN_WIDTH = 512
DN_CONV = 5
DN_CHUNK = 64
N_EXPERT_GROUPS = 4
EXPERTS_PER_GROUP = 8
N_EXPERTS = 32
D_EXPERT = 512
NORM_EPS = 1e-6

LANES = 128
SUBLANES = 8
VMEM_LIMIT = 56 * 1024 * 1024

S5_CHUNK = 16
S5_TILE = 128
ROW_TILE = 512
DN_STEP_CHUNKS = 16
MOE_BLOCK = 256
MOE_TOK_TILE = 1024


def _params(*sem):
    return pltpu.CompilerParams(dimension_semantics=sem, vmem_limit_bytes=VMEM_LIMIT)


def _silu(x):
    half = 0.5 * x
    return half + half * jnp.tanh(half)


def _sigmoid(x):
    return 0.5 + 0.5 * jnp.tanh(0.5 * x)


def _split_weight(w):
    hi = w.astype(BF16)
    lo = (w - hi.astype(F32)).astype(BF16)
    return jnp.concatenate([hi, lo], axis=1)


def _dot_split(a, w_split):
    a_hi = a.astype(BF16)
    a_lo = (a - a_hi.astype(F32)).astype(BF16)
    p = jnp.dot(a_hi, w_split, preferred_element_type=F32)
    q = jnp.dot(a_lo, w_split[:, :LANES], preferred_element_type=F32)
    return p[:, :LANES] + p[:, LANES:] + q


def _inproj_kernel(x_ref, xp_ref, xn_ref, nw_ref, w_ref, wab_ref, cw_ref, gp_ref, tri_ref,
                   u_ref, z_ref, q_ref, k_ref, v_ref, gate_ref, ext):
    i = pl.program_id(0)
    tm = x_ref.shape[0]
    pad = DN_CONV // 2

    def norm(x):
        return x * lax.rsqrt(jnp.mean(x * x, axis=-1, keepdims=True) + NORM_EPS) * nw_ref[...]

    h = norm(x_ref[...])
    h_prev = norm(jnp.where(i > 0, xp_ref[...], 0.0))
    h_next = norm(jnp.where(i < pl.num_programs(0) - 1, xn_ref[...], 0.0))
    hb = h.astype(BF16)
    for blk in range(S5_WIDTH // LANES):
        u_ref[blk] = jnp.dot(hb, w_ref[:, blk * LANES:(blk + 1) * LANES], preferred_element_type=F32)
    z_ref[...] = jnp.dot(hb, w_ref[:, S5_WIDTH + 3 * DN_WIDTH:S5_WIDTH + 4 * DN_WIDTH],
                         preferred_element_type=F32)
    h_ext = jnp.concatenate([h_prev, h, h_next], axis=0).astype(BF16)
    ext[...] = jnp.dot(h_ext, w_ref[:, S5_WIDTH:S5_WIDTH + 3 * DN_WIDTH], preferred_element_type=F32)

    outs = (q_ref, k_ref, v_ref)
    for part in range(3):
        cols = slice(part * DN_WIDTH, (part + 1) * DN_WIDTH)
        acc = ext[pl.ds(SUBLANES - pad, tm), cols] * cw_ref[0:1, cols]
        for tap in range(1, DN_CONV):
            acc = acc + ext[pl.ds(SUBLANES - pad + tap, tm), cols] * cw_ref[tap:tap + 1, cols]
        act = _silu(acc)
        if part == 2:
            v_ref[...] = act.astype(BF16)
            continue
        scale = DN_HEAD_DIM ** -0.5 if part == 0 else 1.0
        for hd in range(DN_HEADS):
            hs = slice(hd * DN_HEAD_DIM, (hd + 1) * DN_HEAD_DIM)
            xh = act[:, hs]
            inv = lax.rsqrt(jnp.sum(xh * xh, axis=-1, keepdims=True) + NORM_EPS)
            outs[part][:, hs] = (xh * inv * scale).astype(BF16)

    ab = _dot_split(h, wab_ref[...])
    lane = lax.broadcasted_iota(jnp.int32, ab.shape, 1)
    pre = ab + gp_ref[1:2, :]
    softplus = jnp.maximum(pre, 0.0) + jnp.log1p(jnp.exp(-jnp.abs(pre)))
    n_gate = 2 * DN_HEADS
    g = jnp.where(lane < n_gate, gp_ref[0:1, :] * softplus, 0.0)
    hi = g.astype(BF16).astype(F32)
    r1 = g - hi
    mid = r1.astype(BF16).astype(F32)
    lo = (r1 - mid).astype(BF16).astype(F32)
    pieces = (hi + pltpu.roll(mid, n_gate, 1) + pltpu.roll(lo, 2 * n_gate, 1)).astype(BF16)
    pref = jnp.dot(tri_ref[0], pieces, preferred_element_type=F32)
    suff = jnp.dot(tri_ref[1], pieces, preferred_element_type=F32)
    part = jnp.where(jnp.bitwise_and(lane, n_gate - 1) < DN_HEADS, pref, suff)
    gsum = part + pltpu.roll(part, LANES - n_gate, 1) + pltpu.roll(part, LANES - 2 * n_gate, 1)
    gate_ref[...] = jnp.where(lane < n_gate, g, jnp.where(
        lane < 2 * n_gate, jax.nn.sigmoid(ab), jnp.where(
            lane < 3 * n_gate, pltpu.roll(gsum, 2 * n_gate, 1), 0.0)))


def _inproj(x, norm_w, w_main, w_gates, layer, conv_w, a_log, dt_bias):
    t = x.shape[0]
    n_main = S5_WIDTH + 4 * DN_WIDTH
    w_ab = _split_weight(jnp.pad(w_gates.astype(F32), ((0, 0), (0, LANES - 4 * DN_HEADS))))
    gp = jnp.zeros((SUBLANES, LANES), F32)
    gp = gp.at[0, :2 * DN_HEADS].set(-jnp.exp(a_log.astype(F32)).reshape(-1))
    gp = gp.at[1, :2 * DN_HEADS].set(dt_bias.astype(F32).reshape(-1))
    cw = jnp.pad(conv_w.astype(F32), ((0, SUBLANES - DN_CONV), (0, 0)))
    tm = min(ROW_TILE, t)
    nb = tm // SUBLANES
    last = t // SUBLANES - 1
    step = np.arange(tm)
    same = (step[:, None] // DN_CHUNK) == (step[None, :] // DN_CHUNK)
    tri = jnp.asarray(np.stack([same & (step[:, None] >= step[None, :]),
                                same & (step[:, None] <= step[None, :])]), BF16)
    row = lambda i: (i, 0)
    const = lambda i: (0, 0)
    return pl.pallas_call(
        _inproj_kernel,
        grid=(t // tm,),
        in_specs=[
            pl.BlockSpec((tm, D_MODEL), row),
            pl.BlockSpec((SUBLANES, D_MODEL), lambda i: (jnp.maximum(i * nb - 1, 0), 0)),
            pl.BlockSpec((SUBLANES, D_MODEL), lambda i: (jnp.minimum((i + 1) * nb, last), 0)),
            pl.BlockSpec((1, D_MODEL), const),
            pl.BlockSpec((None, D_MODEL, n_main), lambda i: (layer, 0, 0)),
            pl.BlockSpec((D_MODEL, 2 * LANES), const),
            pl.BlockSpec((SUBLANES, 3 * DN_WIDTH), const),
            pl.BlockSpec((SUBLANES, LANES), const),
            pl.BlockSpec((2, tm, tm), lambda i: (0, 0, 0)),
        ],
        out_specs=[
            pl.BlockSpec((S5_WIDTH // LANES, tm, LANES), lambda i: (0, i, 0)),
            pl.BlockSpec((tm, DN_WIDTH), row),
            pl.BlockSpec((tm, DN_WIDTH), row),
            pl.BlockSpec((tm, DN_WIDTH), row),
            pl.BlockSpec((tm, DN_WIDTH), row),
            pl.BlockSpec((tm, LANES), row),
        ],
        out_shape=[
            jax.ShapeDtypeStruct((S5_WIDTH // LANES, t, LANES), F32),
            jax.ShapeDtypeStruct((t, DN_WIDTH), F32),
            jax.ShapeDtypeStruct((t, DN_WIDTH), BF16),
            jax.ShapeDtypeStruct((t, DN_WIDTH), BF16),
            jax.ShapeDtypeStruct((t, DN_WIDTH), BF16),
            jax.ShapeDtypeStruct((t, LANES), F32),
        ],
        scratch_shapes=[pltpu.VMEM((tm + 2 * SUBLANES, 3 * DN_WIDTH), F32)],
        compiler_params=_params("parallel"),
        name="inproj",
    )(x, x, x, norm_w.reshape(1, D_MODEL), w_main, w_ab, cw, gp, tri)


def _toeplitz_kernel(k_ref, o_ref):
    c_len = S5_CHUNK
    lane = lax.broadcasted_iota(jnp.int32, (S5_GROUP, LANES), 1)
    for b in range(k_ref.shape[0]):
        tiles = [k_ref[b, :, t * LANES:(t + 1) * LANES] for t in range(k_ref.shape[2] // LANES)]
        for j in range(c_len):
            first, shift = divmod((c_len - 1 - j) * S5_GROUP, LANES)
            for half in range(2):
                piece = tiles[first + half]
                if shift:
                    piece = jnp.where(lane < LANES - shift,
                                      pltpu.roll(piece, LANES - shift, 1),
                                      pltpu.roll(tiles[first + half + 1], LANES - shift, 1))
                o_ref[b, j * S5_GROUP:(j + 1) * S5_GROUP, half * LANES:(half + 1) * LANES] = piece.astype(BF16)


def _toeplitz(kpad):
    n, q, width = kpad.shape
    per_step = 8
    return pl.pallas_call(
        _toeplitz_kernel,
        grid=(n // per_step,),
        in_specs=[pl.BlockSpec((per_step, q, width), lambda i: (i, 0, 0))],
        out_specs=pl.BlockSpec((per_step, S5_CHUNK * q, 256), lambda i: (i, 0, 0)),
        out_shape=jax.ShapeDtypeStruct((n, S5_CHUNK * q, 256), BF16),
        compiler_params=_params("parallel"),
        name="s5_toeplitz",
    )(kpad)


def _s5_tables(lam_re, lam_im, log_dt, b_re, b_im, c_re, c_im):
    c_len = S5_CHUNK
    lr = jnp.minimum(lam_re.astype(F32), S5_MAX_RE)
    li = lam_im.astype(F32)
    dt = jnp.exp(log_dt.astype(F32))[..., None]
    zr, zi = lr * dt, li * dt
    e1 = jnp.exp(zr)
    ar, ai = e1 * jnp.cos(zi), e1 * jnp.sin(zi)
    den = lr * lr + li * li
    nr, ni = ar - 1.0, ai
    fr = (nr * lr + ni * li) / den
    fi = (ni * lr - nr * li) / den
    bbr = (fr[..., None] * b_re - fi[..., None] * b_im).swapaxes(-1, -2)
    bbi = (fr[..., None] * b_im + fi[..., None] * b_re).swapaxes(-1, -2)
    tau = jnp.arange(c_len + 1, dtype=F32)[:, None]
    mag = jnp.exp(tau * zr[..., None, :])
    pr = mag * jnp.cos(tau * zi[..., None, :])
    pi = mag * jnp.sin(tau * zi[..., None, :])
    prq, piq = pr[..., :, None, :], pi[..., :, None, :]
    m_r = prq * bbr[..., None, :, :] - piq * bbi[..., None, :, :]
    m_i = prq * bbi[..., None, :, :] + piq * bbr[..., None, :, :]
    kern = (jnp.einsum('ldgpn,ldgtqn->ldgqtp', c_re, m_r[..., :c_len, :, :], precision=HIGHEST)
            - jnp.einsum('ldgpn,ldgtqn->ldgqtp', c_im, m_i[..., :c_len, :, :], precision=HIGHEST))

    def per_direction(x, axis, fwd_flipped):
        f, b = x[:, 0], x[:, 1]
        f, b = (jnp.flip(f, axis), b) if fwd_flipped else (f, jnp.flip(b, axis))
        return jnp.stack([f, b], axis=1)

    padded = jnp.pad(kern, ((0, 0),) * 4 + ((c_len - 1, 0), (0, 0)))
    padded = per_direction(padded, 3, False)
    padded = jnp.pad(padded, ((0, 0),) * 4 + ((0, 1), (0, 0)))
    lead = padded.shape[:3]
    wt = _toeplitz(padded.reshape(-1, S5_GROUP, 2 * c_len * S5_GROUP)).reshape(lead + (256, 256))
    er = per_direction(m_r[..., :c_len, :, :], 2, True).reshape(wt.shape[:3] + (256, S5_STATE))
    ei = per_direction(m_i[..., :c_len, :, :], 2, True).reshape(wt.shape[:3] + (256, S5_STATE))
    we = jnp.concatenate([er, ei, ei, er], axis=-1)
    c_rt, c_it = c_re.swapaxes(-1, -2)[..., None, :], c_im.swapaxes(-1, -2)[..., None, :]
    p_rt, p_it = pr.swapaxes(-1, -2)[..., 1:, None], pi.swapaxes(-1, -2)[..., 1:, None]
    sr = per_direction(c_rt * p_rt - c_it * p_it, 3, False).reshape(wt.shape[:3] + (S5_STATE, 256))
    si = per_direction(c_rt * p_it + c_it * p_rt, 3, False).reshape(wt.shape[:3] + (S5_STATE, 256))
    ws = jnp.concatenate([sr, -si], axis=3)
    a_r, a_i = pr[..., c_len, :], pi[..., c_len, :]
    coef = jnp.stack([jnp.concatenate([a_r, a_r], -1),
                      jnp.concatenate([-a_i, a_i], -1),
                      jnp.concatenate([a_i, -a_i], -1)], axis=2)
    return wt.astype(BF16), we.astype(BF16), ws.astype(BF16), coef


def _block_transpose8(xs, lane):
    for k in (2, 1, 0):
        shift = S5_GROUP << k
        bit = jnp.bitwise_and(lax.shift_right_logical(lane, 4 + k), 1)
        new = list(xs)
        for a in range(8):
            if (a >> k) & 1:
                continue
            b = a + (1 << k)
            if 2 * shift == LANES:
                both = pltpu.roll(jnp.where(bit == 0, xs[b], xs[a]), shift, 1)
                new[a] = jnp.where(bit == 0, xs[a], both)
                new[b] = jnp.where(bit == 1, xs[b], both)
            else:
                new[a] = jnp.where(bit == 0, xs[a], pltpu.roll(xs[b], shift, 1))
                new[b] = jnp.where(bit == 1, xs[b], pltpu.roll(xs[a], LANES - shift, 1))
        xs = new
    return xs


def _s5_scan_kernel(u_ref, wt_ref, we_ref, ws_ref, coef_ref, *rest, reverse, add_prev):
    if add_prev:
        prev_ref, y_ref, ug_scr, yg_scr, e_scr, es_scr, s_scr, carry = rest
    else:
        y_ref, ug_scr, yg_scr, e_scr, es_scr, s_scr, carry = rest
    n_rows = ug_scr.shape[1]
    n_blk = S5_WIDTH // LANES
    per_blk = LANES // S5_GROUP
    halves = S5_CHUNK // per_blk

    @pl.when(pl.program_id(0) == 0)
    def _():
        carry[...] = jnp.zeros_like(carry)

    lane = lax.broadcasted_iota(jnp.int32, (n_rows, LANES), 1)

    for blk in range(n_blk):
        for half in range(halves):
            xs = [u_ref[blk, pl.ds(half * per_blk + jl, n_rows, stride=S5_CHUNK), :].astype(BF16)
                  for jl in range(per_blk)]
            ys = _block_transpose8(xs, lane)
            for gl in range(per_blk):
                ug_scr[blk * per_blk + gl, :, half * LANES:(half + 1) * LANES] = ys[gl].astype(BF16)

    for g in range(S5_GROUPS):
        e = jnp.dot(ug_scr[g], we_ref[g], preferred_element_type=F32)
        e_scr[g * n_rows:(g + 1) * n_rows, :] = e[:, :LANES]
        es_scr[g * n_rows:(g + 1) * n_rows, :] = e[:, LANES:]

    c1, c2, c3 = coef_ref[0], coef_ref[1], coef_ref[2]

    def step(i, vs):
        v, vp = vs
        r = (n_rows - 1 - i) if reverse else i
        rows = pl.ds(r, S5_GROUPS, stride=n_rows)
        s_scr[rows, :] = v
        v_new = v * c1 + vp * c2 + e_scr[rows, :]
        vp_new = vp * c1 + v * c3 + es_scr[rows, :]
        return v_new, vp_new

    v, vp = lax.fori_loop(0, n_rows, step, (carry[0], carry[1]), unroll=8)
    carry[0] = v
    carry[1] = vp

    for blk in range(n_blk):
        for gl in range(per_blk):
            g = blk * per_blk + gl
            s_in = s_scr[g * n_rows:(g + 1) * n_rows, :].astype(BF16)
            yg_scr[gl] = (jnp.dot(ug_scr[g], wt_ref[g], preferred_element_type=F32)
                          + jnp.dot(s_in, ws_ref[g], preferred_element_type=F32))
        for half in range(halves):
            zs = [yg_scr[gl, :, half * LANES:(half + 1) * LANES].astype(BF16) for gl in range(per_blk)]
            ws = [w.astype(F32) for w in _block_transpose8(zs, lane)]
            for tl in range(per_blk):
                rows = pl.ds(half * per_blk + tl, n_rows, stride=S5_CHUNK)
                out = ws[tl]
                if add_prev:
                    out = out + prev_ref[blk, rows, :]
                y_ref[blk, rows, :] = out


def _s5_direction(u4, tables, prev, reverse, layer):
    n_blk, t, _ = u4.shape
    n_chunks = t // S5_CHUNK
    rows = min(S5_TILE, n_chunks)
    n_tiles = n_chunks // rows
    wt, we, ws, coef = tables
    tile = (lambda i: (0, n_tiles - 1 - i, 0)) if reverse else (lambda i: (0, i, 0))
    table = lambda i: (layer, int(reverse), 0, 0, 0)
    once = pl.Buffered(1)
    in_specs = [
        pl.BlockSpec((n_blk, rows * S5_CHUNK, LANES), tile),
        pl.BlockSpec((None, None, S5_GROUPS, 256, 256), table, pipeline_mode=once),
        pl.BlockSpec((None, None, S5_GROUPS, 256, 256), table, pipeline_mode=once),
        pl.BlockSpec((None, None, S5_GROUPS, LANES, 256), table, pipeline_mode=once),
        pl.BlockSpec((None, None, 3, S5_GROUPS, LANES), table, pipeline_mode=once),
    ]
    args = [u4, wt, we, ws, coef]
    if prev is not None:
        in_specs.append(pl.BlockSpec((n_blk, rows * S5_CHUNK, LANES), tile))
        args.append(prev)
    return pl.pallas_call(
        functools.partial(_s5_scan_kernel, reverse=reverse, add_prev=prev is not None),
        grid=(n_tiles,),
        in_specs=in_specs,
        out_specs=pl.BlockSpec((n_blk, rows * S5_CHUNK, LANES), tile),
        out_shape=jax.ShapeDtypeStruct((n_blk, t, LANES), F32),
        scratch_shapes=[
            pltpu.VMEM((S5_GROUPS, rows, 256), BF16),
            pltpu.VMEM((LANES // S5_GROUP, rows, 256), F32),
            pltpu.VMEM((S5_GROUPS * rows, LANES), F32),
            pltpu.VMEM((S5_GROUPS * rows, LANES), F32),
            pltpu.VMEM((S5_GROUPS * rows, LANES), F32),
            pltpu.VMEM((2, S5_GROUPS, LANES), F32),
        ],
        compiler_params=_params("arbitrary"),
        name="s5_bwd" if reverse else "s5_fwd",
    )(*args)


def _s5_mix(u4, tables, layer):
    y = _s5_direction(u4, tables, None, False, layer)
    return _s5_direction(u4, tables, y, True, layer)


def _bmm(a, b):
    return lax.dot_general(a, b, (((2,), (1,)), ((0,), (0,))), preferred_element_type=F32)


def _dn_kernel(qf, kf, vf, gf, gtf, qb, kb_, vb, gb, gtb, of_ref, ob_ref,
               state, wq_scr, kdt_scr, at_scr, u_scr, gam_scr):
    c_len = DN_CHUNK
    n_ch = gtf.shape[0]
    n_gate = 2 * DN_HEADS
    n_chain = 2 * DN_HEADS

    @pl.when(pl.program_id(0) == 0)
    def _():
        state[...] = jnp.zeros_like(state)

    ri = lax.broadcasted_iota(jnp.int32, (c_len, c_len), 0)
    ci = lax.broadcasted_iota(jnp.int32, (c_len, c_len), 1)
    eye = jnp.where(ri == ci, 1.0, 0.0).astype(F32)
    dirs = ((qf, kf, vf, gf, gtf, ri >= ci, ri > ci, c_len - 1),
            (qb, kb_, vb, gb, gtb, ri <= ci, ri < ci, 0))

    for d, (q_ref, k_ref, v_ref, gate_ref, gate_t_ref, incl, strict, last) in enumerate(dirs):
        gates = gate_ref[...].reshape(n_ch, c_len, LANES)
        gates_t = gate_t_ref[...]
        for h in range(DN_HEADS):
            idx = d * DN_HEADS + h
            hs = slice(h * DN_HEAD_DIM, (h + 1) * DN_HEAD_DIM)
            gcol = gates[:, :, 2 * n_gate + idx:2 * n_gate + idx + 1]
            bcol = gates[:, :, n_gate + idx:n_gate + idx + 1]
            grow = gates_t[:, 2 * n_gate + idx:2 * n_gate + idx + 1, :]
            glast = grow[:, :, last:last + 1]
            qb = q_ref[:, hs].reshape(n_ch, c_len, DN_HEAD_DIM)
            kb = k_ref[:, hs].reshape(n_ch, c_len, DN_HEAD_DIM)
            qh, kh = qb.astype(F32), kb.astype(F32)
            vh = v_ref[:, hs].reshape(n_ch, c_len, DN_HEAD_DIM).astype(F32)
            qk_kk = lax.dot_general(jnp.concatenate([qb, kb], axis=1), kb,
                                    (((2,), (2,)), ((0,), (0,))), preferred_element_type=F32)
            qk, kk = qk_kk[:, :c_len], qk_kk[:, c_len:]
            decay = jnp.where(incl, jnp.exp(jnp.where(incl, gcol - grow, 0.0)), 0.0)
            a_mat = jnp.where(strict, bcol * kk * decay, 0.0)
            pw = -a_mat
            inv = eye + pw
            for _ in range(5):
                pwb = pw.astype(BF16)
                pw = _bmm(pwb, pwb)
                inv = inv + _bmm(inv.astype(BF16), pw.astype(BF16))
            egc = jnp.exp(gcol)
            rhs = jnp.concatenate([vh * bcol, kh * (bcol * egc)], axis=2).astype(BF16)
            uw = _bmm(inv.astype(BF16), rhs)
            wq = jnp.concatenate([uw[:, :, DN_HEAD_DIM:], qh * egc], axis=1).astype(BF16)
            attn = jnp.where(incl, qk * decay, 0.0).astype(BF16)
            k_dec_t = jnp.swapaxes(kh * jnp.exp(glast - gcol), 1, 2).astype(BF16)
            gamma = jnp.broadcast_to(jnp.exp(glast), (n_ch, 1, LANES))
            for c in range(n_ch):
                slot = (c if d == 0 else n_ch - 1 - c) * n_chain + idx
                wq_scr[slot] = wq[c]
                kdt_scr[slot] = k_dec_t[c]
                at_scr[slot] = attn[c]
                u_scr[slot] = uw[c, :, :DN_HEAD_DIM]
                gam_scr[slot] = gamma[c]

    for step in range(n_ch):
        grp = slice(step * n_chain, (step + 1) * n_chain)
        s = state[...]
        wq_s = _bmm(wq_scr[grp], s.astype(BF16))
        v_nb = (u_scr[grp] - wq_s[:, :c_len]).astype(BF16)
        o = wq_s[:, c_len:] + _bmm(at_scr[grp], v_nb)
        state[...] = s * gam_scr[grp] + _bmm(kdt_scr[grp], v_nb)
        for d, o_ref in enumerate((of_ref, ob_ref)):
            c = step if d == 0 else n_ch - 1 - step
            for h in range(DN_HEADS):
                o_ref[c * c_len:(c + 1) * c_len, h * DN_HEAD_DIM:(h + 1) * DN_HEAD_DIM] = (
                    o[d * DN_HEADS + h].astype(BF16))


def _dn_mix(q, k, v, gates, gates_t):
    t = q.shape[0]
    n_chunks = t // DN_CHUNK
    n_ch = min(DN_STEP_CHUNKS, n_chunks)
    n_steps = n_chunks // n_ch
    rows = n_ch * DN_CHUNK
    n_inst = n_ch * 2 * DN_HEADS
    specs = []
    for row, row3 in ((lambda i: (i, 0), lambda i: (i, 0, 0)),
                      (lambda i: (n_steps - 1 - i, 0), lambda i: (n_steps - 1 - i, 0, 0))):
        specs += [
            pl.BlockSpec((rows, DN_WIDTH), row),
            pl.BlockSpec((rows, DN_WIDTH), row),
            pl.BlockSpec((rows, DN_WIDTH), row),
            pl.BlockSpec((rows, LANES), row),
            pl.BlockSpec((n_ch, 6 * DN_HEADS, DN_CHUNK), row3),
        ]
    return pl.pallas_call(
        _dn_kernel,
        grid=(n_steps,),
        in_specs=specs,
        out_specs=[
            pl.BlockSpec((rows, DN_WIDTH), lambda i: (i, 0)),
            pl.BlockSpec((rows, DN_WIDTH), lambda i: (n_steps - 1 - i, 0)),
        ],
        out_shape=[jax.ShapeDtypeStruct((t, DN_WIDTH), BF16), jax.ShapeDtypeStruct((t, DN_WIDTH), BF16)],
        scratch_shapes=[
            pltpu.VMEM((2 * DN_HEADS, DN_HEAD_DIM, DN_HEAD_DIM), F32),
            pltpu.VMEM((n_inst, 2 * DN_CHUNK, DN_HEAD_DIM), BF16),
            pltpu.VMEM((n_inst, DN_HEAD_DIM, DN_CHUNK), BF16),
            pltpu.VMEM((n_inst, DN_CHUNK, DN_CHUNK), BF16),
            pltpu.VMEM((n_inst, DN_CHUNK, DN_HEAD_DIM), F32),
            pltpu.VMEM((n_inst, 1, LANES), F32),
        ],
        compiler_params=_params("arbitrary"),
        name="dn_mix",
    )(q, k, v, gates, gates_t, q, k, v, gates, gates_t)


def _post_kernel(x_ref, u_ref, ys_ref, of_ref, ob_ref, z_ref, d_ref, wglu_ref, bglu_ref, nw_ref,
                 wout_ref, nffn_ref, wr_ref, br_ref, before_ref, x1_ref, h_ref, route_ref, cnt_ref, base):
    i = pl.program_id(0)
    tm = x_ref.shape[0]

    @pl.when(i == 0)
    def _():
        base[...] = jnp.zeros_like(base)

    y = jnp.concatenate([ys_ref[b] + d_ref[:, b * LANES:(b + 1) * LANES] * u_ref[b]
                         for b in range(S5_WIDTH // LANES)], axis=1)
    y = 0.5 * y * (1.0 + lax.erf(y * (2.0 ** -0.5)))
    gate = jnp.dot(y.astype(BF16), wglu_ref[...], preferred_element_type=F32) + bglu_ref[...]
    y_s5 = y * _sigmoid(gate)
    acc = x_ref[...] + jnp.dot(y_s5.astype(BF16), wout_ref[0:S5_WIDTH, :], preferred_element_type=F32)
    for h in range(DN_HEADS):
        hs = slice(h * DN_HEAD_DIM, (h + 1) * DN_HEAD_DIM)
        o = of_ref[:, hs].astype(F32) + ob_ref[:, hs].astype(F32)
        zh = z_ref[:, hs]
        o = o * lax.rsqrt(jnp.mean(o * o, axis=-1, keepdims=True) + NORM_EPS) * nw_ref[...]
        y_dn = o * _silu(zh)
        acc = acc + jnp.dot(y_dn.astype(BF16),
                            wout_ref[S5_WIDTH + h * DN_HEAD_DIM:S5_WIDTH + (h + 1) * DN_HEAD_DIM, :],
                            preferred_element_type=F32)
    x1_ref[...] = acc
    hn = acc * lax.rsqrt(jnp.mean(acc * acc, axis=-1, keepdims=True) + NORM_EPS) * nffn_ref[...]
    _matrix_to_rows(h_ref, hn)

    logits = _dot_split(hn, wr_ref[...]) + br_ref[...]
    lane_i = lax.broadcasted_iota(jnp.int32, logits.shape, 1)
    lane = lane_i.astype(F32)
    neg = jnp.float32(-jnp.inf)
    big = jnp.float32(LANES)
    gl = jnp.where(lane_i < N_EXPERT_GROUPS, logits, neg)
    gmax = jnp.max(gl, axis=-1, keepdims=True)
    g_sel = jnp.min(jnp.where(gl == gmax, lane, big), axis=-1, keepdims=True)
    p_group = 1.0 / jnp.sum(jnp.exp(gl - gmax), axis=-1, keepdims=True)
    lo = N_EXPERT_GROUPS + g_sel * EXPERTS_PER_GROUP
    el = jnp.where((lane >= lo) & (lane < lo + EXPERTS_PER_GROUP), logits, neg)
    top1 = jnp.max(el, axis=-1, keepdims=True)
    idx1 = jnp.min(jnp.where(el == top1, lane, big), axis=-1, keepdims=True)
    el2 = jnp.where(lane == idx1, neg, el)
    top2 = jnp.max(el2, axis=-1, keepdims=True)
    idx2 = jnp.min(jnp.where(el2 == top2, lane, big), axis=-1, keepdims=True)
    e21 = jnp.exp(top2 - top1)
    w1 = p_group / (1.0 + e21)
    w2 = w1 * e21
    oh1 = jnp.where(lane == idx1, 1.0, 0.0).astype(F32)
    oh2 = jnp.where(lane == idx2, 1.0, 0.0).astype(F32)
    ohs = oh1 + oh2
    prior = jnp.dot(before_ref[...], ohs.astype(BF16), preferred_element_type=F32) + base[0:1, :]
    rank1 = jnp.sum(oh1 * prior, axis=-1, keepdims=True)
    rank2 = jnp.sum(oh2 * prior, axis=-1, keepdims=True)
    base[0:1, :] = base[0:1, :] + jnp.sum(ohs, axis=0, keepdims=True)
    e1 = idx1 - N_EXPERT_GROUPS
    e2 = idx2 - N_EXPERT_GROUPS
    route = jnp.where(lane_i == 0, e1, jnp.where(lane_i == 1, e2, jnp.where(lane_i == 2, w1, jnp.where(
        lane_i == 3, w2, jnp.where(lane_i == 4, rank1, jnp.where(lane_i == 5, rank2, 0.0))))))
    route_ref[...] = route
    cnt_ref[...] = base[...]


def _post(x, u, ys, o_f, o_b, z, s5_d, w_glu, b_glu, dn_norm_w, w_out, layer, norm_ffn, w_rg, b_rg, w_re,
          b_re):
    t = x.shape[0]
    tm = min(ROW_TILE, t)
    wr = jnp.concatenate([w_rg, w_re.transpose(1, 0, 2).reshape(D_MODEL, N_EXPERTS)], axis=1)
    wr = _split_weight(jnp.pad(wr.astype(F32), ((0, 0), (0, LANES - N_EXPERT_GROUPS - N_EXPERTS))))
    br = jnp.pad(jnp.concatenate([b_rg, b_re.reshape(-1)]).astype(F32),
                 (0, LANES - N_EXPERT_GROUPS - N_EXPERTS)).reshape(1, LANES)
    row = lambda i: (i, 0)
    const = lambda i: (0, 0)
    return pl.pallas_call(
        _post_kernel,
        grid=(t // tm,),
        in_specs=[
            pl.BlockSpec((tm, D_MODEL), row),
            pl.BlockSpec((S5_WIDTH // LANES, tm, LANES), lambda i: (0, i, 0)),
            pl.BlockSpec((S5_WIDTH // LANES, tm, LANES), lambda i: (0, i, 0)),
            pl.BlockSpec((tm, DN_WIDTH), row),
            pl.BlockSpec((tm, DN_WIDTH), row),
            pl.BlockSpec((tm, DN_WIDTH), row),
            pl.BlockSpec((1, S5_WIDTH), const),
            pl.BlockSpec((S5_WIDTH, S5_WIDTH), const),
            pl.BlockSpec((1, S5_WIDTH), const),
            pl.BlockSpec((1, DN_HEAD_DIM), const),
            pl.BlockSpec((None, D_MODEL, D_MODEL), lambda i: (layer, 0, 0)),
            pl.BlockSpec((1, D_MODEL), const),
            pl.BlockSpec((D_MODEL, 2 * LANES), const),
            pl.BlockSpec((1, LANES), const),
            pl.BlockSpec((tm, tm), const),
        ],
        out_specs=[
            pl.BlockSpec((tm, D_MODEL), row),
            pl.BlockSpec((tm * ROW_SPLIT, LANES), row),
            pl.BlockSpec((tm, LANES), row),
            pl.BlockSpec((SUBLANES, LANES), const),
        ],
        out_shape=[
            jax.ShapeDtypeStruct((t, D_MODEL), F32),
            jax.ShapeDtypeStruct((t * ROW_SPLIT, LANES), U32),
            jax.ShapeDtypeStruct((t, LANES), F32),
            jax.ShapeDtypeStruct((SUBLANES, LANES), F32),
        ],
        scratch_shapes=[pltpu.VMEM((SUBLANES, LANES), F32)],
        compiler_params=_params("arbitrary"),
        name="mixer_post",
    )(x, u, ys, o_f, o_b, z, s5_d.reshape(1, -1).astype(F32), w_glu.astype(BF16),
      b_glu.reshape(1, -1).astype(F32), dn_norm_w.reshape(1, -1).astype(F32), w_out,
      norm_ffn.reshape(1, -1).astype(F32), wr, br,
      jnp.asarray(np.tril(np.ones((tm, tm), np.float32), -1), BF16))


ROW_SPLIT = D_MODEL // (2 * LANES)
U32 = jnp.uint32


def _rows_to_matrix(ref, n_rows, lead=None):
    low, high = [], []
    for s in range(ROW_SPLIT):
        idx = pl.ds(s, n_rows, stride=ROW_SPLIT)
        word = ref[idx, :] if lead is None else ref[lead, idx, :]
        low.append(pltpu.bitcast(word << 16, F32))
        high.append(pltpu.bitcast(word & jnp.uint32(0xFFFF0000), F32))
    return jnp.concatenate(low + high, axis=1)


def _matrix_to_rows(ref, val):
    n_rows = val.shape[0]
    half = ROW_SPLIT * LANES
    bits = pltpu.bitcast(val.astype(BF16).astype(F32), U32)
    for s in range(ROW_SPLIT):
        low = bits[:, s * LANES:(s + 1) * LANES] >> 16
        high = bits[:, half + s * LANES:half + (s + 1) * LANES]
        ref[pl.ds(s, n_rows, stride=ROW_SPLIT), :] = high | low


def _row(ref, r):
    return ref.at[pl.ds(pl.multiple_of(r * ROW_SPLIT, ROW_SPLIT), ROW_SPLIT), :]


def _dispatch_kernel(dest_ref, zblk_ref, h_ref, xs_ref, zero_buf, sem, zsem):
    i = pl.program_id(0)
    tm = h_ref.shape[0] // ROW_SPLIT
    blk_rows = zero_buf.shape[0]

    @pl.when(i == 0)
    def _():
        zero_buf[...] = jnp.zeros_like(zero_buf)

        def zero_copy(j):
            start = pl.multiple_of(zblk_ref[j] * blk_rows, blk_rows)
            return pltpu.make_async_copy(zero_buf, xs_ref.at[pl.ds(start, blk_rows), :], zsem)

        def zstart(j, _):
            @pl.when(zblk_ref[j] >= 0)
            def _():
                zero_copy(j).start()
            return 0

        def zwait(j, _):
            @pl.when(zblk_ref[j] >= 0)
            def _():
                zero_copy(j).wait()
            return 0

        lax.fori_loop(0, zblk_ref.shape[0], zstart, 0)
        lax.fori_loop(0, zblk_ref.shape[0], zwait, 0)

    def copy(r, k):
        slot = dest_ref[2 * (i * tm + r) + k]
        return pltpu.make_async_copy(_row(h_ref, r), _row(xs_ref, slot), sem)

    def start(r, _):
        copy(r, 0).start(priority=0)
        copy(r, 1).start(priority=1)
        return 0

    def wait(r, _):
        copy(r, 0).wait()
        copy(r, 1).wait()
        return 0

    lax.fori_loop(0, tm, start, 0, unroll=8)
    lax.fori_loop(0, tm, wait, 0, unroll=8)


def _dispatch(h, dest, zero_blocks, n_slots):
    t = h.shape[0] // ROW_SPLIT
    tm = min(MOE_TOK_TILE, t)
    grid_spec = pltpu.PrefetchScalarGridSpec(
        num_scalar_prefetch=2,
        grid=(t // tm,),
        in_specs=[pl.BlockSpec((tm * ROW_SPLIT, LANES), lambda i, *_: (i, 0))],
        out_specs=pl.BlockSpec(memory_space=pl.ANY),
        scratch_shapes=[pltpu.VMEM((MOE_BLOCK * ROW_SPLIT, LANES), U32),
                        pltpu.SemaphoreType.DMA(()), pltpu.SemaphoreType.DMA(())],
    )
    return pl.pallas_call(
        _dispatch_kernel,
        grid_spec=grid_spec,
        out_shape=jax.ShapeDtypeStruct((n_slots * ROW_SPLIT, LANES), U32),
        compiler_params=_params("arbitrary"),
        name="moe_dispatch",
    )(dest, zero_blocks, h)


def _expert_kernel(be_ref, run_ref, nxt_ref, used_ref, xs_ref, wg_ref, wu_ref, wd_ref, ys_ref,
                   wg_buf, wu_buf, wd_buf, wgu_b, wd_b, wsem, *, layer):
    i = pl.program_id(0)
    blk = ys_ref.shape[0] // ROW_SPLIT

    def weight_copies(expert, slot):
        return (pltpu.make_async_copy(wg_ref.at[layer, expert], wg_buf.at[slot], wsem.at[slot]),
                pltpu.make_async_copy(wu_ref.at[layer, expert], wu_buf.at[slot], wsem.at[slot]),
                pltpu.make_async_copy(wd_ref.at[layer, expert], wd_buf.at[slot], wsem.at[slot]))

    run = run_ref[i]
    new_run = (i == 0) | (run != run_ref[jnp.maximum(i - 1, 0)])
    wslot = run % 2

    @pl.when(i == 0)
    def _():
        for c in weight_copies(be_ref[0], 0):
            c.start()

    @pl.when(new_run)
    def _():
        for c in weight_copies(be_ref[i], wslot):
            c.wait()

        @pl.when(nxt_ref[i] >= 0)
        def _():
            for c in weight_copies(nxt_ref[i], 1 - wslot):
                c.start()

        wgu_b[:, :D_EXPERT] = wg_buf[wslot].astype(BF16)
        wgu_b[:, D_EXPERT:] = wu_buf[wslot].astype(BF16)
        wd_b[...] = wd_buf[wslot].astype(BF16)

    @pl.when(i < used_ref[0])
    def _():
        xb = _rows_to_matrix(xs_ref, blk).astype(BF16)
        gu = jnp.dot(xb, wgu_b[...], preferred_element_type=F32)
        g, u = gu[:, :D_EXPERT], gu[:, D_EXPERT:]
        hid = (_silu(g) * u).astype(BF16)
        _matrix_to_rows(ys_ref, jnp.dot(hid, wd_b[...], preferred_element_type=F32))

    @pl.when(i >= used_ref[0])
    def _():
        ys_ref[...] = jnp.zeros_like(ys_ref)


def _experts(xs, block_expert, block_run, next_expert, used, w_gate, w_up, w_down, layer):
    n_slots = xs.shape[0] // ROW_SPLIT
    n_blocks = n_slots // MOE_BLOCK
    grid_spec = pltpu.PrefetchScalarGridSpec(
        num_scalar_prefetch=4,
        grid=(n_blocks,),
        in_specs=[pl.BlockSpec((MOE_BLOCK * ROW_SPLIT, LANES), lambda i, *_: (i, 0))]
        + [pl.BlockSpec(memory_space=pl.ANY)] * 3,
        out_specs=pl.BlockSpec((MOE_BLOCK * ROW_SPLIT, LANES), lambda i, *_: (i, 0)),
        scratch_shapes=[
            pltpu.VMEM((2, D_MODEL, D_EXPERT), F32),
            pltpu.VMEM((2, D_MODEL, D_EXPERT), F32),
            pltpu.VMEM((2, D_EXPERT, D_MODEL), F32),
            pltpu.VMEM((D_MODEL, 2 * D_EXPERT), BF16),
            pltpu.VMEM((D_EXPERT, D_MODEL), BF16),
            pltpu.SemaphoreType.DMA((2,)),
        ],
    )
    return pl.pallas_call(
        functools.partial(_expert_kernel, layer=layer),
        grid_spec=grid_spec,
        out_shape=jax.ShapeDtypeStruct((n_slots * ROW_SPLIT, LANES), U32),
        compiler_params=_params("arbitrary"),
        name="moe_experts",
    )(block_expert, block_run, next_expert, used, xs, w_gate, w_up, w_down)


def _combine_kernel(dest_ref, x1_ref, route_ref, ys_ref, nw_ref, out_ref, buf, sem, *, final_norm):
    i = pl.program_id(0)
    tm = x1_ref.shape[0]

    def copy(r, k):
        slot = dest_ref[2 * (i * tm + r) + k]
        dst = buf.at[k, pl.ds(pl.multiple_of(r * ROW_SPLIT, ROW_SPLIT), ROW_SPLIT), :]
        return pltpu.make_async_copy(_row(ys_ref, slot), dst, sem)

    def start(r, _):
        copy(r, 0).start(priority=0)
        copy(r, 1).start(priority=1)
        return 0

    def wait(r, _):
        copy(r, 0).wait()
        copy(r, 1).wait()
        return 0

    lax.fori_loop(0, tm, start, 0, unroll=8)
    lax.fori_loop(0, tm, wait, 0, unroll=8)
    route = route_ref[...]
    out = (x1_ref[...] + route[:, 2:3] * _rows_to_matrix(buf, tm, 0)
           + route[:, 3:4] * _rows_to_matrix(buf, tm, 1))
    if final_norm:
        out = out * lax.rsqrt(jnp.mean(out * out, axis=-1, keepdims=True) + NORM_EPS) * nw_ref[...]
    out_ref[...] = out


def _combine(x1, route, ys, dest, norm_w, final_norm):
    t = x1.shape[0]
    tm = min(MOE_TOK_TILE, t)
    grid_spec = pltpu.PrefetchScalarGridSpec(
        num_scalar_prefetch=1,
        grid=(t // tm,),
        in_specs=[
            pl.BlockSpec((tm, D_MODEL), lambda i, dest: (i, 0)),
            pl.BlockSpec((tm, LANES), lambda i, dest: (i, 0)),
            pl.BlockSpec(memory_space=pl.ANY),
            pl.BlockSpec((1, D_MODEL), lambda i, dest: (0, 0)),
        ],
        out_specs=pl.BlockSpec((tm, D_MODEL), lambda i, dest: (i, 0)),
        scratch_shapes=[pltpu.VMEM((2, tm * ROW_SPLIT, LANES), U32), pltpu.SemaphoreType.DMA(())],
    )
    return pl.pallas_call(
        functools.partial(_combine_kernel, final_norm=final_norm),
        grid_spec=grid_spec,
        out_shape=jax.ShapeDtypeStruct((t, D_MODEL), F32),
        compiler_params=_params("arbitrary"),
        name="moe_combine",
    )(dest, x1, route, ys, norm_w.reshape(1, D_MODEL).astype(F32))


def _moe(x1, h, route, counts, w_gate, w_up, w_down, layer, norm_w, final_norm):
    t = x1.shape[0]
    n_blocks = -(-(2 * t) // MOE_BLOCK) + N_EXPERTS
    n_slots = n_blocks * MOE_BLOCK
    cnt = counts[0, N_EXPERT_GROUPS:N_EXPERT_GROUPS + N_EXPERTS].astype(jnp.int32)
    padded = ((cnt + MOE_BLOCK - 1) // MOE_BLOCK) * MOE_BLOCK
    pad_end = jnp.cumsum(padded)
    pad_start = pad_end - padded
    expert = route[:, 0:2].astype(jnp.int32)
    rank = route[:, 4:6].astype(jnp.int32)
    experts = jnp.arange(N_EXPERTS, dtype=jnp.int32)
    dest = (jnp.sum(jnp.where(expert[..., None] == experts, pad_start, 0), axis=-1) + rank).reshape(-1)
    blocks = jnp.arange(n_blocks, dtype=jnp.int32)
    block_expert = jnp.minimum(jnp.sum(pad_end[None, :] <= (blocks * MOE_BLOCK)[:, None], axis=1),
                               N_EXPERTS - 1).astype(jnp.int32)
    starts_run = jnp.concatenate([jnp.ones((1,), bool), block_expert[1:] != block_expert[:-1]])
    block_run = (jnp.cumsum(starts_run) - 1).astype(jnp.int32)
    later_start = starts_run[None, :] & (blocks[None, :] > blocks[:, None])
    next_expert = jnp.where(jnp.any(later_start, axis=1),
                            block_expert[jnp.argmax(later_start, axis=1)], -1).astype(jnp.int32)
    used = (pad_end[-1:] // MOE_BLOCK).astype(jnp.int32)
    tail = blocks[n_blocks - N_EXPERTS:]
    zero_blocks = jnp.concatenate([jnp.where(padded > 0, pad_end // MOE_BLOCK - 1, -1),
                                   jnp.where(tail >= used, tail, -1)]).astype(jnp.int32)
    xs = _dispatch(h, dest, zero_blocks, n_slots)
    ys = _experts(xs, block_expert, block_run, next_expert, used, w_gate, w_up, w_down, layer)
    return _combine(x1, route, ys, dest, norm_w, final_norm)


def kernel(x, norm_mix, w_in, s5_lam_re, s5_lam_im, s5_log_dt, s5_b_re, s5_b_im, s5_c_re, s5_c_im,
           s5_d, s5_w_glu, s5_b_glu, gdn_conv_w, gdn_a_log, gdn_dt_bias, gdn_norm_w, w_out, norm_ffn,
           router_w_group, router_b_group, router_w_expert, router_b_expert,
           expert_w_gate, expert_w_up, expert_w_down, norm_final):
    bsz, seq, d = x.shape
    depth = norm_mix.shape[0]
    xt = x.astype(F32).reshape(bsz * seq, d)
    tables = _s5_tables(s5_lam_re, s5_lam_im, s5_log_dt, s5_b_re, s5_b_im, s5_c_re, s5_c_im)
    n_main = S5_WIDTH + 4 * DN_WIDTH
    w_main = w_in[:, :, :n_main].astype(BF16)
    w_out_b = w_out.astype(BF16)
    for i in range(depth):
        u, z, q, k, v, gates = _inproj(xt, norm_mix[i], w_main, w_in[i, :, n_main:], i, gdn_conv_w[i],
                                       gdn_a_log[i], gdn_dt_bias[i])
        ys = _s5_mix(u, tables, i)
        gates_t = (gates[:, :6 * DN_HEADS].reshape(-1, DN_CHUNK, 6 * DN_HEADS).transpose(0, 2, 1))
        o_f, o_b = _dn_mix(q, k, v, gates, gates_t)
        x1, h, route, counts = _post(xt, u, ys, o_f, o_b, z, s5_d[i], s5_w_glu[i], s5_b_glu[i],
                                     gdn_norm_w[i], w_out_b, i, norm_ffn[i], router_w_group[i],
                                     router_b_group[i], router_w_expert[i], router_b_expert[i])
        xt = _moe(x1, h, route, counts, expert_w_gate, expert_w_up, expert_w_down, i,
                  norm_final, i == depth - 1)
    return xt.reshape(bsz, seq, d)
```

```python
import functools
import math

import jax
import jax.numpy as jnp
import numpy as np
from jax import lax
from jax.experimental import pallas as pl
from jax.experimental.pallas import tpu as pltpu

F32 = jnp.float32
BF16 = jnp.bfloat16
HIGHEST = lax.Precision.HIGHEST

D_MODEL = 1024
S5_WIDTH = 512
S5_GROUP = 16
S5_GROUPS = 32
S5_STATE = 64
S5_MAX_RE = -1e-4
DN_HEADS = 4
DN_HEAD_DIM = 128
DN_WIDTH = 512
DN_CONV = 5
DN_CHUNK = 64
N_EXPERT_GROUPS = 4
EXPERTS_PER_GROUP = 8
N_EXPERTS = 32
D_EXPERT = 512
NORM_EPS = 1e-6

LANES = 128
SUBLANES = 8
VMEM_LIMIT = 56 * 1024 * 1024

S5_CHUNK = 16
S5_TILE = 128
ROW_TILE = 512
DN_STEP_CHUNKS = 16
MOE_BLOCK = 256
MOE_TOK_TILE = 1024


def _params(*sem):
    return pltpu.CompilerParams(dimension_semantics=sem, vmem_limit_bytes=VMEM_LIMIT)


def _silu(x):
    half = 0.5 * x
    return half + half * jnp.tanh(half)


def _sigmoid(x):
    return 0.5 + 0.5 * jnp.tanh(0.5 * x)


def _split_weight(w):
    hi = w.astype(BF16)
    lo = (w - hi.astype(F32)).astype(BF16)
    return jnp.concatenate([hi, lo], axis=1)


def _dot_split(a, w_split):
    a_hi = a.astype(BF16)
    a_lo = (a - a_hi.astype(F32)).astype(BF16)
    p = jnp.dot(a_hi, w_split, preferred_element_type=F32)
    q = jnp.dot(a_lo, w_split[:, :LANES], preferred_element_type=F32)
    return p[:, :LANES] + p[:, LANES:] + q


def _inproj_kernel(x_ref, xp_ref, xn_ref, nw_ref, w_ref, wab_ref, cw_ref, gp_ref, tri_ref,
                   u_ref, z_ref, q_ref, k_ref, v_ref, gate_ref, ext):
    i = pl.program_id(0)
    tm = x_ref.shape[0]
    pad = DN_CONV // 2

    def norm(x):
        return x * lax.rsqrt(jnp.mean(x * x, axis=-1, keepdims=True) + NORM_EPS) * nw_ref[...]

    h = norm(x_ref[...])
    h_prev = norm(jnp.where(i > 0, xp_ref[...], 0.0))
    h_next = norm(jnp.where(i < pl.num_programs(0) - 1, xn_ref[...], 0.0))
    hb = h.astype(BF16)
    for blk in range(S5_WIDTH // LANES):
        u_ref[blk] = jnp.dot(hb, w_ref[:, blk * LANES:(blk + 1) * LANES], preferred_element_type=F32)
    z_ref[...] = jnp.dot(hb, w_ref[:, S5_WIDTH + 3 * DN_WIDTH:S5_WIDTH + 4 * DN_WIDTH],
                         preferred_element_type=F32)
    h_ext = jnp.concatenate([h_prev, h, h_next], axis=0).astype(BF16)
    ext[...] = jnp.dot(h_ext, w_ref[:, S5_WIDTH:S5_WIDTH + 3 * DN_WIDTH], preferred_element_type=F32)

    outs = (q_ref, k_ref, v_ref)
    for part in range(3):
        cols = slice(part * DN_WIDTH, (part + 1) * DN_WIDTH)
        acc = ext[pl.ds(SUBLANES - pad, tm), cols] * cw_ref[0:1, cols]
        for tap in range(1, DN_CONV):
            acc = acc + ext[pl.ds(SUBLANES - pad + tap, tm), cols] * cw_ref[tap:tap + 1, cols]
        act = _silu(acc)
        if part == 2:
            v_ref[...] = act.astype(BF16)
            continue
        scale = DN_HEAD_DIM ** -0.5 if part == 0 else 1.0
        for hd in range(DN_HEADS):
            hs = slice(hd * DN_HEAD_DIM, (hd + 1) * DN_HEAD_DIM)
            xh = act[:, hs]
            inv = lax.rsqrt(jnp.sum(xh * xh, axis=-1, keepdims=True) + NORM_EPS)
            outs[part][:, hs] = (xh * inv * scale).astype(BF16)

    ab = _dot_split(h, wab_ref[...])
    lane = lax.broadcasted_iota(jnp.int32, ab.shape, 1)
    pre = ab + gp_ref[1:2, :]
    softplus = jnp.maximum(pre, 0.0) + jnp.log1p(jnp.exp(-jnp.abs(pre)))
    n_gate = 2 * DN_HEADS
    g = jnp.where(lane < n_gate, gp_ref[0:1, :] * softplus, 0.0)
    hi = g.astype(BF16).astype(F32)
    r1 = g - hi
    mid = r1.astype(BF16).astype(F32)
    lo = (r1 - mid).astype(BF16).astype(F32)
    pieces = (hi + pltpu.roll(mid, n_gate, 1) + pltpu.roll(lo, 2 * n_gate, 1)).astype(BF16)
    pref = jnp.dot(tri_ref[0], pieces, preferred_element_type=F32)
    suff = jnp.dot(tri_ref[1], pieces, preferred_element_type=F32)
    part = jnp.where(jnp.bitwise_and(lane, n_gate - 1) < DN_HEADS, pref, suff)
    gsum = part + pltpu.roll(part, LANES - n_gate, 1) + pltpu.roll(part, LANES - 2 * n_gate, 1)
    gate_ref[...] = jnp.where(lane < n_gate, g, jnp.where(
        lane < 2 * n_gate, jax.nn.sigmoid(ab), jnp.where(
            lane < 3 * n_gate, pltpu.roll(gsum, 2 * n_gate, 1), 0.0)))


def _inproj(x, norm_w, w_main, w_gates, layer, conv_w, a_log, dt_bias):
    t = x.shape[0]
    n_main = S5_WIDTH + 4 * DN_WIDTH
    w_ab = _split_weight(jnp.pad(w_gates.astype(F32), ((0, 0), (0, LANES - 4 * DN_HEADS))))
    gp = jnp.zeros((SUBLANES, LANES), F32)
    gp = gp.at[0, :2 * DN_HEADS].set(-jnp.exp(a_log.astype(F32)).reshape(-1))
    gp = gp.at[1, :2 * DN_HEADS].set(dt_bias.astype(F32).reshape(-1))
    cw = jnp.pad(conv_w.astype(F32), ((0, SUBLANES - DN_CONV), (0, 0)))
    tm = min(ROW_TILE, t)
    nb = tm // SUBLANES
    last = t // SUBLANES - 1
    step = np.arange(tm)
    same = (step[:, None] // DN_CHUNK) == (step[None, :] // DN_CHUNK)
    tri = jnp.asarray(np.stack([same & (step[:, None] >= step[None, :]),
                                same & (step[:, None] <= step[None, :])]), BF16)
    row = lambda i: (i, 0)
    const = lambda i: (0, 0)
    return pl.pallas_call(
        _inproj_kernel,
        grid=(t // tm,),
        in_specs=[
            pl.BlockSpec((tm, D_MODEL), row),
            pl.BlockSpec((SUBLANES, D_MODEL), lambda i: (jnp.maximum(i * nb - 1, 0), 0)),
            pl.BlockSpec((SUBLANES, D_MODEL), lambda i: (jnp.minimum((i + 1) * nb, last), 0)),
            pl.BlockSpec((1, D_MODEL), const),
            pl.BlockSpec((None, D_MODEL, n_main), lambda i: (layer, 0, 0)),
            pl.BlockSpec((D_MODEL, 2 * LANES), const),
            pl.BlockSpec((SUBLANES, 3 * DN_WIDTH), const),
            pl.BlockSpec((SUBLANES, LANES), const),
            pl.BlockSpec((2, tm, tm), lambda i: (0, 0, 0)),
        ],
        out_specs=[
            pl.BlockSpec((S5_WIDTH // LANES, tm, LANES), lambda i: (0, i, 0)),
            pl.BlockSpec((tm, DN_WIDTH), row),
            pl.BlockSpec((tm, DN_WIDTH), row),
            pl.BlockSpec((tm, DN_WIDTH), row),
            pl.BlockSpec((tm, DN_WIDTH), row),
            pl.BlockSpec((tm, LANES), row),
        ],
        out_shape=[
            jax.ShapeDtypeStruct((S5_WIDTH // LANES, t, LANES), F32),
            jax.ShapeDtypeStruct((t, DN_WIDTH), F32),
            jax.ShapeDtypeStruct((t, DN_WIDTH), BF16),
            jax.ShapeDtypeStruct((t, DN_WIDTH), BF16),
            jax.ShapeDtypeStruct((t, DN_WIDTH), BF16),
            jax.ShapeDtypeStruct((t, LANES), F32),
        ],
        scratch_shapes=[pltpu.VMEM((tm + 2 * SUBLANES, 3 * DN_WIDTH), F32)],
        compiler_params=_params("parallel"),
        name="inproj",
    )(x, x, x, norm_w.reshape(1, D_MODEL), w_main, w_ab, cw, gp, tri)


def _toeplitz_kernel(k_ref, o_ref):
    c_len = S5_CHUNK
    lane = lax.broadcasted_iota(jnp.int32, (S5_GROUP, LANES), 1)
    for b in range(k_ref.shape[0]):
        tiles = [k_ref[b, :, t * LANES:(t + 1) * LANES] for t in range(k_ref.shape[2] // LANES)]
        for j in range(c_len):
            first, shift = divmod((c_len - 1 - j) * S5_GROUP, LANES)
            for half in range(2):
                piece = tiles[first + half]
                if shift:
                    piece = jnp.where(lane < LANES - shift,
                                      pltpu.roll(piece, LANES - shift, 1),
                                      pltpu.roll(tiles[first + half + 1], LANES - shift, 1))
                o_ref[b, j * S5_GROUP:(j + 1) * S5_GROUP, half * LANES:(half + 1) * LANES] = piece.astype(BF16)


def _toeplitz(kpad):
    n, q, width = kpad.shape
    per_step = 8
    return pl.pallas_call(
        _toeplitz_kernel,
        grid=(n // per_step,),
        in_specs=[pl.BlockSpec((per_step, q, width), lambda i: (i, 0, 0))],
        out_specs=pl.BlockSpec((per_step, S5_CHUNK * q, 256), lambda i: (i, 0, 0)),
        out_shape=jax.ShapeDtypeStruct((n, S5_CHUNK * q, 256), BF16),
        compiler_params=_params("parallel"),
        name="s5_toeplitz",
    )(kpad)


def _s5_tables(lam_re, lam_im, log_dt, b_re, b_im, c_re, c_im):
    c_len = S5_CHUNK
    lr = jnp.minimum(lam_re.astype(F32), S5_MAX_RE)
    li = lam_im.astype(F32)
    dt = jnp.exp(log_dt.astype(F32))[..., None]
    zr, zi = lr * dt, li * dt
    e1 = jnp.exp(zr)
    ar, ai = e1 * jnp.cos(zi), e1 * jnp.sin(zi)
    den = lr * lr + li * li
    nr, ni = ar - 1.0, ai
    fr = (nr * lr + ni * li) / den
    fi = (ni * lr - nr * li) / den
    bbr = (fr[..., None] * b_re - fi[..., None] * b_im).swapaxes(-1, -2)
    bbi = (fr[..., None] * b_im + fi[..., None] * b_re).swapaxes(-1, -2)
    tau = jnp.arange(c_len + 1, dtype=F32)[:, None]
    mag = jnp.exp(tau * zr[..., None, :])
    pr = mag * jnp.cos(tau * zi[..., None, :])
    pi = mag * jnp.sin(tau * zi[..., None, :])
    prq, piq = pr[..., :, None, :], pi[..., :, None, :]
    m_r = prq * bbr[..., None, :, :] - piq * bbi[..., None, :, :]
    m_i = prq * bbi[..., None, :, :] + piq * bbr[..., None, :, :]
    kern = (jnp.einsum('ldgpn,ldgtqn->ldgqtp', c_re, m_r[..., :c_len, :, :], precision=HIGHEST)
            - jnp.einsum('ldgpn,ldgtqn->ldgqtp', c_im, m_i[..., :c_len, :, :], precision=HIGHEST))

    def per_direction(x, axis, fwd_flipped):
        f, b = x[:, 0], x[:, 1]
        f, b = (jnp.flip(f, axis), b) if fwd_flipped else (f, jnp.flip(b, axis))
        return jnp.stack([f, b], axis=1)

    padded = jnp.pad(kern, ((0, 0),) * 4 + ((c_len - 1, 0), (0, 0)))
    padded = per_direction(padded, 3, False)
    padded = jnp.pad(padded, ((0, 0),) * 4 + ((0, 1), (0, 0)))
    lead = padded.shape[:3]
    wt = _toeplitz(padded.reshape(-1, S5_GROUP, 2 * c_len * S5_GROUP)).reshape(lead + (256, 256))
    er = per_direction(m_r[..., :c_len, :, :], 2, True).reshape(wt.shape[:3] + (256, S5_STATE))
    ei = per_direction(m_i[..., :c_len, :, :], 2, True).reshape(wt.shape[:3] + (256, S5_STATE))
    we = jnp.concatenate([er, ei, ei, er], axis=-1)
    c_rt, c_it = c_re.swapaxes(-1, -2)[..., None, :], c_im.swapaxes(-1, -2)[..., None, :]
    p_rt, p_it = pr.swapaxes(-1, -2)[..., 1:, None], pi.swapaxes(-1, -2)[..., 1:, None]
    sr = per_direction(c_rt * p_rt - c_it * p_it, 3, False).reshape(wt.shape[:3] + (S5_STATE, 256))
    si = per_direction(c_rt * p_it + c_it * p_rt, 3, False).reshape(wt.shape[:3] + (S5_STATE, 256))
    ws = jnp.concatenate([sr, -si], axis=3)
    a_r, a_i = pr[..., c_len, :], pi[..., c_len, :]
    coef = jnp.stack([jnp.concatenate([a_r, a_r], -1),
                      jnp.concatenate([-a_i, a_i], -1),
                      jnp.concatenate([a_i, -a_i], -1)], axis=2)
    return wt.astype(BF16), we.astype(BF16), ws.astype(BF16), coef


def _block_transpose8(xs, lane):
    for k in (2, 1, 0):
        shift = S5_GROUP << k
        bit = jnp.bitwise_and(lax.shift_right_logical(lane, 4 + k), 1)
        new = list(xs)
        for a in range(8):
            if (a >> k) & 1:
                continue
            b = a + (1 << k)
            if 2 * shift == LANES:
                both = pltpu.roll(jnp.where(bit == 0, xs[b], xs[a]), shift, 1)
                new[a] = jnp.where(bit == 0, xs[a], both)
                new[b] = jnp.where(bit == 1, xs[b], both)
            else:
                new[a] = jnp.where(bit == 0, xs[a], pltpu.roll(xs[b], shift, 1))
                new[b] = jnp.where(bit == 1, xs[b], pltpu.roll(xs[a], LANES - shift, 1))
        xs = new
    return xs


def _s5_scan_kernel(u_ref, wt_ref, we_ref, ws_ref, coef_ref, y_ref,
                    ug_scr, yg_scr, e_scr, es_scr, s_scr, carry, *, reverse):
    n_rows = ug_scr.shape[1]
    n_blk = S5_WIDTH // LANES
    per_blk = LANES // S5_GROUP
    halves = S5_CHUNK // per_blk

    @pl.when(pl.program_id(0) == 0)
    def _():
        carry[...] = jnp.zeros_like(carry)

    lane = lax.broadcasted_iota(jnp.int32, (n_rows, LANES), 1)

    for blk in range(n_blk):
        for half in range(halves):
            xs = [u_ref[blk, pl.ds(half * per_blk + jl, n_rows, stride=S5_CHUNK), :].astype(BF16)
                  for jl in range(per_blk)]
            ys = _block_transpose8(xs, lane)
            for gl in range(per_blk):
                ug_scr[blk * per_blk + gl, :, half * LANES:(half + 1) * LANES] = ys[gl].astype(BF16)

    for g in range(S5_GROUPS):
        e = jnp.dot(ug_scr[g], we_ref[g], preferred_element_type=F32)
        e_scr[g * n_rows:(g + 1) * n_rows, :] = e[:, :LANES]
        es_scr[g * n_rows:(g + 1) * n_rows, :] = e[:, LANES:]

    c1, c2, c3 = coef_ref[0], coef_ref[1], coef_ref[2]

    def step(i, vs):
        v, vp = vs
        r = (n_rows - 1 - i) if reverse else i
        rows = pl.ds(r, S5_GROUPS, stride=n_rows)
        s_scr[rows, :] = v
        v_new = v * c1 + vp * c2 + e_scr[rows, :]
        vp_new = vp * c1 + v * c3 + es_scr[rows, :]
        return v_new, vp_new

    v, vp = lax.fori_loop(0, n_rows, step, (carry[0], carry[1]), unroll=8)
    carry[0] = v
    carry[1] = vp

    for blk in range(n_blk):
        for gl in range(per_blk):
            g = blk * per_blk + gl
            s_in = s_scr[g * n_rows:(g + 1) * n_rows, :].astype(BF16)
            yg_scr[gl] = (jnp.dot(ug_scr[g], wt_ref[g], preferred_element_type=F32)
                          + jnp.dot(s_in, ws_ref[g], preferred_element_type=F32))
        for half in range(halves):
            zs = [yg_scr[gl, :, half * LANES:(half + 1) * LANES].astype(BF16) for gl in range(per_blk)]
            ws = [w.astype(F32) for w in _block_transpose8(zs, lane)]
            for tl in range(per_blk):
                rows = pl.ds(half * per_blk + tl, n_rows, stride=S5_CHUNK)
                y_ref[blk, rows, :] = ws[tl]


def _s5_direction(u4, tables, reverse, layer):
    n_blk, t, _ = u4.shape
    n_chunks = t // S5_CHUNK
    rows = min(S5_TILE, n_chunks)
    n_tiles = n_chunks // rows
    wt, we, ws, coef = tables
    tile = (lambda i: (0, n_tiles - 1 - i, 0)) if reverse else (lambda i: (0, i, 0))
    table = lambda i: (layer, int(reverse), 0, 0, 0)
    once = pl.Buffered(1)
    in_specs = [
        pl.BlockSpec((n_blk, rows * S5_CHUNK, LANES), tile),
        pl.BlockSpec((None, None, S5_GROUPS, 256, 256), table, pipeline_mode=once),
        pl.BlockSpec((None, None, S5_GROUPS, 256, 256), table, pipeline_mode=once),
        pl.BlockSpec((None, None, S5_GROUPS, LANES, 256), table, pipeline_mode=once),
        pl.BlockSpec((None, None, 3, S5_GROUPS, LANES), table, pipeline_mode=once),
    ]
    args = [u4, wt, we, ws, coef]
    return pl.pallas_call(
        functools.partial(_s5_scan_kernel, reverse=reverse),
        grid=(n_tiles,),
        in_specs=in_specs,
        out_specs=pl.BlockSpec((n_blk, rows * S5_CHUNK, LANES), tile),
        out_shape=jax.ShapeDtypeStruct((n_blk, t, LANES), F32),
        scratch_shapes=[
            pltpu.VMEM((S5_GROUPS, rows, 256), BF16),
            pltpu.VMEM((LANES // S5_GROUP, rows, 256), F32),
            pltpu.VMEM((S5_GROUPS * rows, LANES), F32),
            pltpu.VMEM((S5_GROUPS * rows, LANES), F32),
            pltpu.VMEM((S5_GROUPS * rows, LANES), F32),
            pltpu.VMEM((2, S5_GROUPS, LANES), F32),
        ],
        compiler_params=_params("arbitrary"),
        name="s5_bwd" if reverse else "s5_fwd",
    )(*args)


def _s5_mix(u4, tables, layer):
    return _s5_direction(u4, tables, False, layer), _s5_direction(u4, tables, True, layer)


def _bmm(a, b):
    return lax.dot_general(a, b, (((2,), (1,)), ((0,), (0,))), preferred_element_type=F32)


def _dn_kernel(qf, kf, vf, gf, gtf, qb, kb_, vb, gb, gtb, of_ref, ob_ref,
               state, wq_scr, kdt_scr, at_scr, u_scr, gam_scr):
    c_len = DN_CHUNK
    n_ch = gtf.shape[0]
    n_gate = 2 * DN_HEADS
    n_chain = 2 * DN_HEADS

    @pl.when(pl.program_id(0) == 0)
    def _():
        state[...] = jnp.zeros_like(state)

    ri = lax.broadcasted_iota(jnp.int32, (c_len, c_len), 0)
    ci = lax.broadcasted_iota(jnp.int32, (c_len, c_len), 1)
    eye = jnp.where(ri == ci, 1.0, 0.0).astype(F32)
    dirs = ((qf, kf, vf, gf, gtf, ri >= ci, ri > ci, c_len - 1),
            (qb, kb_, vb, gb, gtb, ri <= ci, ri < ci, 0))

    for d, (q_ref, k_ref, v_ref, gate_ref, gate_t_ref, incl, strict, last) in enumerate(dirs):
        gates = gate_ref[...].reshape(n_ch, c_len, LANES)
        gates_t = gate_t_ref[...]
        for h in range(DN_HEADS):
            idx = d * DN_HEADS + h
            hs = slice(h * DN_HEAD_DIM, (h + 1) * DN_HEAD_DIM)
            gcol = gates[:, :, 2 * n_gate + idx:2 * n_gate + idx + 1]
            bcol = gates[:, :, n_gate + idx:n_gate + idx + 1]
            grow = gates_t[:, 2 * n_gate + idx:2 * n_gate + idx + 1, :]
            glast = grow[:, :, last:last + 1]
            qb = q_ref[:, hs].reshape(n_ch, c_len, DN_HEAD_DIM)
            kb = k_ref[:, hs].reshape(n_ch, c_len, DN_HEAD_DIM)
            qh, kh = qb.astype(F32), kb.astype(F32)
            vh = v_ref[:, hs].reshape(n_ch, c_len, DN_HEAD_DIM).astype(F32)
            qk_kk = lax.dot_general(jnp.concatenate([qb, kb], axis=1), kb,
                                    (((2,), (2,)), ((0,), (0,))), preferred_element_type=F32)
            qk, kk = qk_kk[:, :c_len], qk_kk[:, c_len:]
            decay = jnp.where(incl, jnp.exp(jnp.where(incl, gcol - grow, 0.0)), 0.0)
            a_mat = jnp.where(strict, bcol * kk * decay, 0.0)
            pw = -a_mat
            inv = eye + pw
            for _ in range(5):
                pwb = pw.astype(BF16)
                pw = _bmm(pwb, pwb)
                inv = inv + _bmm(inv.astype(BF16), pw.astype(BF16))
            egc = jnp.exp(gcol)
            rhs = jnp.concatenate([vh * bcol, kh * (bcol * egc)], axis=2).astype(BF16)
            uw = _bmm(inv.astype(BF16), rhs)
            wq = jnp.concatenate([uw[:, :, DN_HEAD_DIM:], qh * egc], axis=1).astype(BF16)
            attn = jnp.where(incl, qk * decay, 0.0).astype(BF16)
            k_dec_t = jnp.swapaxes(kh * jnp.exp(glast - gcol), 1, 2).astype(BF16)
            gamma = jnp.broadcast_to(jnp.exp(glast), (n_ch, 1, LANES))
            for c in range(n_ch):
                slot = (c if d == 0 else n_ch - 1 - c) * n_chain + idx
                wq_scr[slot] = wq[c]
                kdt_scr[slot] = k_dec_t[c]
                at_scr[slot] = attn[c]
                u_scr[slot] = uw[c, :, :DN_HEAD_DIM]
                gam_scr[slot] = gamma[c]

    for step in range(n_ch):
        grp = slice(step * n_chain, (step + 1) * n_chain)
        s = state[...]
        wq_s = _bmm(wq_scr[grp], s.astype(BF16))
        v_nb = (u_scr[grp] - wq_s[:, :c_len]).astype(BF16)
        o = wq_s[:, c_len:] + _bmm(at_scr[grp], v_nb)
        state[...] = s * gam_scr[grp] + _bmm(kdt_scr[grp], v_nb)
        for d, o_ref in enumerate((of_ref, ob_ref)):
            c = step if d == 0 else n_ch - 1 - step
            for h in range(DN_HEADS):
                o_ref[c * c_len:(c + 1) * c_len, h * DN_HEAD_DIM:(h + 1) * DN_HEAD_DIM] = (
                    o[d * DN_HEADS + h].astype(BF16))


def _dn_mix(q, k, v, gates, gates_t):
    t = q.shape[0]
    n_chunks = t // DN_CHUNK
    n_ch = min(DN_STEP_CHUNKS, n_chunks)
    n_steps = n_chunks // n_ch
    rows = n_ch * DN_CHUNK
    n_inst = n_ch * 2 * DN_HEADS
    specs = []
    for row, row3 in ((lambda i: (i, 0), lambda i: (i, 0, 0)),
                      (lambda i: (n_steps - 1 - i, 0), lambda i: (n_steps - 1 - i, 0, 0))):
        specs += [
            pl.BlockSpec((rows, DN_WIDTH), row),
            pl.BlockSpec((rows, DN_WIDTH), row),
            pl.BlockSpec((rows, DN_WIDTH), row),
            pl.BlockSpec((rows, LANES), row),
            pl.BlockSpec((n_ch, 6 * DN_HEADS, DN_CHUNK), row3),
        ]
    return pl.pallas_call(
        _dn_kernel,
        grid=(n_steps,),
        in_specs=specs,
        out_specs=[
            pl.BlockSpec((rows, DN_WIDTH), lambda i: (i, 0)),
            pl.BlockSpec((rows, DN_WIDTH), lambda i: (n_steps - 1 - i, 0)),
        ],
        out_shape=[jax.ShapeDtypeStruct((t, DN_WIDTH), BF16), jax.ShapeDtypeStruct((t, DN_WIDTH), BF16)],
        scratch_shapes=[
            pltpu.VMEM((2 * DN_HEADS, DN_HEAD_DIM, DN_HEAD_DIM), F32),
            pltpu.VMEM((n_inst, 2 * DN_CHUNK, DN_HEAD_DIM), BF16),
            pltpu.VMEM((n_inst, DN_HEAD_DIM, DN_CHUNK), BF16),
            pltpu.VMEM((n_inst, DN_CHUNK, DN_CHUNK), BF16),
            pltpu.VMEM((n_inst, DN_CHUNK, DN_HEAD_DIM), F32),
            pltpu.VMEM((n_inst, 1, LANES), F32),
        ],
        compiler_params=_params("arbitrary"),
        name="dn_mix",
    )(q, k, v, gates, gates_t, q, k, v, gates, gates_t)


def _post_kernel(x_ref, u_ref, ysf_ref, ysb_ref, of_ref, ob_ref, z_ref, d_ref, wglu_ref, bglu_ref, nw_ref,
                 wout_ref, nffn_ref, wr_ref, br_ref, before_ref, x1_ref, h_ref, route_ref, cnt_ref, base):
    i = pl.program_id(0)
    tm = x_ref.shape[0]

    @pl.when(i == 0)
    def _():
        base[...] = jnp.zeros_like(base)

    y = jnp.concatenate([ysf_ref[b] + ysb_ref[b] + d_ref[:, b * LANES:(b + 1) * LANES] * u_ref[b]
                         for b in range(S5_WIDTH // LANES)], axis=1)
    y = 0.5 * y * (1.0 + lax.erf(y * (2.0 ** -0.5)))
    gate = jnp.dot(y.astype(BF16), wglu_ref[...], preferred_element_type=F32) + bglu_ref[...]
    y_s5 = y * _sigmoid(gate)
    acc = x_ref[...] + jnp.dot(y_s5.astype(BF16), wout_ref[0:S5_WIDTH, :], preferred_element_type=F32)
    for h in range(DN_HEADS):
        hs = slice(h * DN_HEAD_DIM, (h + 1) * DN_HEAD_DIM)
        o = of_ref[:, hs].astype(F32) + ob_ref[:, hs].astype(F32)
        zh = z_ref[:, hs]
        o = o * lax.rsqrt(jnp.mean(o * o, axis=-1, keepdims=True) + NORM_EPS) * nw_ref[...]
        y_dn = o * _silu(zh)
        acc = acc + jnp.dot(y_dn.astype(BF16),
                            wout_ref[S5_WIDTH + h * DN_HEAD_DIM:S5_WIDTH + (h + 1) * DN_HEAD_DIM, :],
                            preferred_element_type=F32)
    x1_ref[...] = acc
    hn = acc * lax.rsqrt(jnp.mean(acc * acc, axis=-1, keepdims=True) + NORM_EPS) * nffn_ref[...]
    _matrix_to_rows(h_ref, hn)

    logits = _dot_split(hn, wr_ref[...]) + br_ref[...]
    lane_i = lax.broadcasted_iota(jnp.int32, logits.shape, 1)
    lane = lane_i.astype(F32)
    neg = jnp.float32(-jnp.inf)
    big = jnp.float32(LANES)
    gl = jnp.where(lane_i < N_EXPERT_GROUPS, logits, neg)
    gmax = jnp.max(gl, axis=-1, keepdims=True)
    g_sel = jnp.min(jnp.where(gl == gmax, lane, big), axis=-1, keepdims=True)
    p_group = 1.0 / jnp.sum(jnp.exp(gl - gmax), axis=-1, keepdims=True)
    lo = N_EXPERT_GROUPS + g_sel * EXPERTS_PER_GROUP
    el = jnp.where((lane >= lo) & (lane < lo + EXPERTS_PER_GROUP), logits, neg)
    top1 = jnp.max(el, axis=-1, keepdims=True)
    idx1 = jnp.min(jnp.where(el == top1, lane, big), axis=-1, keepdims=True)
    el2 = jnp.where(lane == idx1, neg, el)
    top2 = jnp.max(el2, axis=-1, keepdims=True)
    idx2 = jnp.min(jnp.where(el2 == top2, lane, big), axis=-1, keepdims=True)
    e21 = jnp.exp(top2 - top1)
    w1 = p_group / (1.0 + e21)
    w2 = w1 * e21
    oh1 = jnp.where(lane == idx1, 1.0, 0.0).astype(F32)
    oh2 = jnp.where(lane == idx2, 1.0, 0.0).astype(F32)
    ohs = oh1 + oh2
    prior = jnp.dot(before_ref[...], ohs.astype(BF16), preferred_element_type=F32) + base[0:1, :]
    rank1 = jnp.sum(oh1 * prior, axis=-1, keepdims=True)
    rank2 = jnp.sum(oh2 * prior, axis=-1, keepdims=True)
    base[0:1, :] = base[0:1, :] + jnp.sum(ohs, axis=0, keepdims=True)
    e1 = idx1 - N_EXPERT_GROUPS
    e2 = idx2 - N_EXPERT_GROUPS
    route = jnp.where(lane_i == 0, e1, jnp.where(lane_i == 1, e2, jnp.where(lane_i == 2, w1, jnp.where(
        lane_i == 3, w2, jnp.where(lane_i == 4, rank1, jnp.where(lane_i == 5, rank2, 0.0))))))
    route_ref[...] = route
    cnt_ref[...] = base[...]


def _post(x, u, ys, o_f, o_b, z, s5_d, w_glu, b_glu, dn_norm_w, w_out, layer, norm_ffn, w_rg, b_rg, w_re,
          b_re):
    t = x.shape[0]
    tm = min(ROW_TILE, t)
    wr = jnp.concatenate([w_rg, w_re.transpose(1, 0, 2).reshape(D_MODEL, N_EXPERTS)], axis=1)
    wr = _split_weight(jnp.pad(wr.astype(F32), ((0, 0), (0, LANES - N_EXPERT_GROUPS - N_EXPERTS))))
    br = jnp.pad(jnp.concatenate([b_rg, b_re.reshape(-1)]).astype(F32),
                 (0, LANES - N_EXPERT_GROUPS - N_EXPERTS)).reshape(1, LANES)
    row = lambda i: (i, 0)
    const = lambda i: (0, 0)
    return pl.pallas_call(
        _post_kernel,
        grid=(t // tm,),
        in_specs=[
            pl.BlockSpec((tm, D_MODEL), row),
            pl.BlockSpec((S5_WIDTH // LANES, tm, LANES), lambda i: (0, i, 0)),
            pl.BlockSpec((S5_WIDTH // LANES, tm, LANES), lambda i: (0, i, 0)),
            pl.BlockSpec((S5_WIDTH // LANES, tm, LANES), lambda i: (0, i, 0)),
            pl.BlockSpec((tm, DN_WIDTH), row),
            pl.BlockSpec((tm, DN_WIDTH), row),
            pl.BlockSpec((tm, DN_WIDTH), row),
            pl.BlockSpec((1, S5_WIDTH), const),
            pl.BlockSpec((S5_WIDTH, S5_WIDTH), const),
            pl.BlockSpec((1, S5_WIDTH), const),
            pl.BlockSpec((1, DN_HEAD_DIM), const),
            pl.BlockSpec((None, D_MODEL, D_MODEL), lambda i: (layer, 0, 0)),
            pl.BlockSpec((1, D_MODEL), const),
            pl.BlockSpec((D_MODEL, 2 * LANES), const),
            pl.BlockSpec((1, LANES), const),
            pl.BlockSpec((tm, tm), const),
        ],
        out_specs=[
            pl.BlockSpec((tm, D_MODEL), row),
            pl.BlockSpec((tm * ROW_SPLIT, LANES), row),
            pl.BlockSpec((tm, LANES), row),
            pl.BlockSpec((SUBLANES, LANES), const),
        ],
        out_shape=[
            jax.ShapeDtypeStruct((t, D_MODEL), F32),
            jax.ShapeDtypeStruct((t * ROW_SPLIT, LANES), U32),
            jax.ShapeDtypeStruct((t, LANES), F32),
            jax.ShapeDtypeStruct((SUBLANES, LANES), F32),
        ],
        scratch_shapes=[pltpu.VMEM((SUBLANES, LANES), F32)],
        compiler_params=_params("arbitrary"),
        name="mixer_post",
    )(x, u, ys[0], ys[1], o_f, o_b, z, s5_d.reshape(1, -1).astype(F32), w_glu.astype(BF16),
      b_glu.reshape(1, -1).astype(F32), dn_norm_w.reshape(1, -1).astype(F32), w_out,
      norm_ffn.reshape(1, -1).astype(F32), wr, br,
      jnp.asarray(np.tril(np.ones((tm, tm), np.float32), -1), BF16))


ROW_SPLIT = D_MODEL // (2 * LANES)
U32 = jnp.uint32


def _rows_to_matrix(ref, n_rows, lead=None):
    low, high = [], []
    for s in range(ROW_SPLIT):
        idx = pl.ds(s, n_rows, stride=ROW_SPLIT)
        word = ref[idx, :] if lead is None else ref[lead, idx, :]
        low.append(pltpu.bitcast(word << 16, F32))
        high.append(pltpu.bitcast(word & jnp.uint32(0xFFFF0000), F32))
    return jnp.concatenate(low + high, axis=1)


def _matrix_to_rows(ref, val):
    n_rows = val.shape[0]
    half = ROW_SPLIT * LANES
    bits = pltpu.bitcast(val.astype(BF16).astype(F32), U32)
    for s in range(ROW_SPLIT):
        low = bits[:, s * LANES:(s + 1) * LANES] >> 16
        high = bits[:, half + s * LANES:half + (s + 1) * LANES]
        ref[pl.ds(s, n_rows, stride=ROW_SPLIT), :] = high | low


def _row(ref, r):
    return ref.at[pl.ds(pl.multiple_of(r * ROW_SPLIT, ROW_SPLIT), ROW_SPLIT), :]


def _dispatch_kernel(dest_ref, zblk_ref, h_ref, xs_ref, zero_buf, sem, zsem):
    i = pl.program_id(0)
    tm = h_ref.shape[0] // ROW_SPLIT
    blk_rows = zero_buf.shape[0]

    @pl.when(i == 0)
    def _():
        zero_buf[...] = jnp.zeros_like(zero_buf)

        def zero_copy(j):
            start = pl.multiple_of(zblk_ref[j] * blk_rows, blk_rows)
            return pltpu.make_async_copy(zero_buf, xs_ref.at[pl.ds(start, blk_rows), :], zsem)

        def zstart(j, _):
            @pl.when(zblk_ref[j] >= 0)
            def _():
                zero_copy(j).start()
            return 0

        def zwait(j, _):
            @pl.when(zblk_ref[j] >= 0)
            def _():
                zero_copy(j).wait()
            return 0

        lax.fori_loop(0, zblk_ref.shape[0], zstart, 0)
        lax.fori_loop(0, zblk_ref.shape[0], zwait, 0)

    def copy(r, k):
        slot = dest_ref[2 * (i * tm + r) + k]
        return pltpu.make_async_copy(_row(h_ref, r), _row(xs_ref, slot), sem)

    def start(r, _):
        copy(r, 0).start(priority=0)
        copy(r, 1).start(priority=1)
        return 0

    def wait(r, _):
        copy(r, 0).wait()
        copy(r, 1).wait()
        return 0

    lax.fori_loop(0, tm, start, 0, unroll=8)
    lax.fori_loop(0, tm, wait, 0, unroll=8)


def _dispatch(h, dest, zero_blocks, n_slots):
    t = h.shape[0] // ROW_SPLIT
    tm = min(MOE_TOK_TILE, t)
    grid_spec = pltpu.PrefetchScalarGridSpec(
        num_scalar_prefetch=2,
        grid=(t // tm,),
        in_specs=[pl.BlockSpec((tm * ROW_SPLIT, LANES), lambda i, *_: (i, 0))],
        out_specs=pl.BlockSpec(memory_space=pl.ANY),
        scratch_shapes=[pltpu.VMEM((MOE_BLOCK * ROW_SPLIT, LANES), U32),
                        pltpu.SemaphoreType.DMA(()), pltpu.SemaphoreType.DMA(())],
    )
    return pl.pallas_call(
        _dispatch_kernel,
        grid_spec=grid_spec,
        out_shape=jax.ShapeDtypeStruct((n_slots * ROW_SPLIT, LANES), U32),
        compiler_params=_params("arbitrary"),
        name="moe_dispatch",
    )(dest, zero_blocks, h)


def _expert_kernel(be_ref, run_ref, nxt_ref, used_ref, xs_ref, wg_ref, wu_ref, wd_ref, ys_ref,
                   wg_buf, wu_buf, wd_buf, wgu_b, wd_b, wsem, *, layer):
    i = pl.program_id(0)
    blk = ys_ref.shape[0] // ROW_SPLIT

    def weight_copies(expert, slot):
        return (pltpu.make_async_copy(wg_ref.at[layer, expert], wg_buf.at[slot], wsem.at[slot]),
                pltpu.make_async_copy(wu_ref.at[layer, expert], wu_buf.at[slot], wsem.at[slot]),
                pltpu.make_async_copy(wd_ref.at[layer, expert], wd_buf.at[slot], wsem.at[slot]))

    run = run_ref[i]
    new_run = (i == 0) | (run != run_ref[jnp.maximum(i - 1, 0)])
    wslot = run % 2

    @pl.when(i == 0)
    def _():
        for c in weight_copies(be_ref[0], 0):
            c.start()

    @pl.when(new_run)
    def _():
        for c in weight_copies(be_ref[i], wslot):
            c.wait()

        @pl.when(nxt_ref[i] >= 0)
        def _():
            for c in weight_copies(nxt_ref[i], 1 - wslot):
                c.start()

        wgu_b[:, :D_EXPERT] = wg_buf[wslot].astype(BF16)
        wgu_b[:, D_EXPERT:] = wu_buf[wslot].astype(BF16)
        wd_b[...] = wd_buf[wslot].astype(BF16)

    @pl.when(i < used_ref[0])
    def _():
        xb = _rows_to_matrix(xs_ref, blk).astype(BF16)
        gu = jnp.dot(xb, wgu_b[...], preferred_element_type=F32)
        g, u = gu[:, :D_EXPERT], gu[:, D_EXPERT:]
        hid = (_silu(g) * u).astype(BF16)
        _matrix_to_rows(ys_ref, jnp.dot(hid, wd_b[...], preferred_element_type=F32))

    @pl.when(i >= used_ref[0])
    def _():
        ys_ref[...] = jnp.zeros_like(ys_ref)


def _experts(xs, block_expert, block_run, next_expert, used, w_gate, w_up, w_down, layer):
    n_slots = xs.shape[0] // ROW_SPLIT
    n_blocks = n_slots // MOE_BLOCK
    grid_spec = pltpu.PrefetchScalarGridSpec(
        num_scalar_prefetch=4,
        grid=(n_blocks,),
        in_specs=[pl.BlockSpec((MOE_BLOCK * ROW_SPLIT, LANES), lambda i, *_: (i, 0))]
        + [pl.BlockSpec(memory_space=pl.ANY)] * 3,
        out_specs=pl.BlockSpec((MOE_BLOCK * ROW_SPLIT, LANES), lambda i, *_: (i, 0)),
        scratch_shapes=[
            pltpu.VMEM((2, D_MODEL, D_EXPERT), F32),
            pltpu.VMEM((2, D_MODEL, D_EXPERT), F32),
            pltpu.VMEM((2, D_EXPERT, D_MODEL), F32),
            pltpu.VMEM((D_MODEL, 2 * D_EXPERT), BF16),
            pltpu.VMEM((D_EXPERT, D_MODEL), BF16),
            pltpu.SemaphoreType.DMA((2,)),
        ],
    )
    return pl.pallas_call(
        functools.partial(_expert_kernel, layer=layer),
        grid_spec=grid_spec,
        out_shape=jax.ShapeDtypeStruct((n_slots * ROW_SPLIT, LANES), U32),
        compiler_params=_params("arbitrary"),
        name="moe_experts",
    )(block_expert, block_run, next_expert, used, xs, w_gate, w_up, w_down)


def _combine_kernel(dest_ref, x1_ref, route_ref, ys_ref, nw_ref, out_ref, buf, sem, *, final_norm):
    i = pl.program_id(0)
    tm = x1_ref.shape[0]

    def copy(r, k):
        slot = dest_ref[2 * (i * tm + r) + k]
        dst = buf.at[k, pl.ds(pl.multiple_of(r * ROW_SPLIT, ROW_SPLIT), ROW_SPLIT), :]
        return pltpu.make_async_copy(_row(ys_ref, slot), dst, sem)

    def start(r, _):
        copy(r, 0).start(priority=0)
        copy(r, 1).start(priority=1)
        return 0

    def wait(r, _):
        copy(r, 0).wait()
        copy(r, 1).wait()
        return 0

    lax.fori_loop(0, tm, start, 0, unroll=8)
    lax.fori_loop(0, tm, wait, 0, unroll=8)
    route = route_ref[...]
    out = (x1_ref[...] + route[:, 2:3] * _rows_to_matrix(buf, tm, 0)
           + route[:, 3:4] * _rows_to_matrix(buf, tm, 1))
    if final_norm:
        out = out * lax.rsqrt(jnp.mean(out * out, axis=-1, keepdims=True) + NORM_EPS) * nw_ref[...]
    out_ref[...] = out


def _combine(x1, route, ys, dest, norm_w, final_norm):
    t = x1.shape[0]
    tm = min(MOE_TOK_TILE, t)
    grid_spec = pltpu.PrefetchScalarGridSpec(
        num_scalar_prefetch=1,
        grid=(t // tm,),
        in_specs=[
            pl.BlockSpec((tm, D_MODEL), lambda i, dest: (i, 0)),
            pl.BlockSpec((tm, LANES), lambda i, dest: (i, 0)),
            pl.BlockSpec(memory_space=pl.ANY),
            pl.BlockSpec((1, D_MODEL), lambda i, dest: (0, 0)),
        ],
        out_specs=pl.BlockSpec((tm, D_MODEL), lambda i, dest: (i, 0)),
        scratch_shapes=[pltpu.VMEM((2, tm * ROW_SPLIT, LANES), U32), pltpu.SemaphoreType.DMA(())],
    )
    return pl.pallas_call(
        functools.partial(_combine_kernel, final_norm=final_norm),
        grid_spec=grid_spec,
        out_shape=jax.ShapeDtypeStruct((t, D_MODEL), F32),
        compiler_params=_params("arbitrary"),
        name="moe_combine",
    )(dest, x1, route, ys, norm_w.reshape(1, D_MODEL).astype(F32))


def _moe(x1, h, route, counts, w_gate, w_up, w_down, layer, norm_w, final_norm):
    t = x1.shape[0]
    n_blocks = -(-(2 * t) // MOE_BLOCK) + N_EXPERTS
    n_slots = n_blocks * MOE_BLOCK
    cnt = counts[0, N_EXPERT_GROUPS:N_EXPERT_GROUPS + N_EXPERTS].astype(jnp.int32)
    padded = ((cnt + MOE_BLOCK - 1) // MOE_BLOCK) * MOE_BLOCK
    pad_end = jnp.cumsum(padded)
    pad_start = pad_end - padded
    expert = route[:, 0:2].astype(jnp.int32)
    rank = route[:, 4:6].astype(jnp.int32)
    experts = jnp.arange(N_EXPERTS, dtype=jnp.int32)
    dest = (jnp.sum(jnp.where(expert[..., None] == experts, pad_start, 0), axis=-1) + rank).reshape(-1)
    blocks = jnp.arange(n_blocks, dtype=jnp.int32)
    block_expert = jnp.minimum(jnp.sum(pad_end[None, :] <= (blocks * MOE_BLOCK)[:, None], axis=1),
                               N_EXPERTS - 1).astype(jnp.int32)
    starts_run = jnp.concatenate([jnp.ones((1,), bool), block_expert[1:] != block_expert[:-1]])
    block_run = (jnp.cumsum(starts_run) - 1).astype(jnp.int32)
    later_start = starts_run[None, :] & (blocks[None, :] > blocks[:, None])
    next_expert = jnp.where(jnp.any(later_start, axis=1),
                            block_expert[jnp.argmax(later_start, axis=1)], -1).astype(jnp.int32)
    used = (pad_end[-1:] // MOE_BLOCK).astype(jnp.int32)
    tail = blocks[n_blocks - N_EXPERTS:]
    zero_blocks = jnp.concatenate([jnp.where(padded > 0, pad_end // MOE_BLOCK - 1, -1),
                                   jnp.where(tail >= used, tail, -1)]).astype(jnp.int32)
    xs = _dispatch(h, dest, zero_blocks, n_slots)
    ys = _experts(xs, block_expert, block_run, next_expert, used, w_gate, w_up, w_down, layer)
    return _combine(x1, route, ys, dest, norm_w, final_norm)


def kernel(x, norm_mix, w_in, s5_lam_re, s5_lam_im, s5_log_dt, s5_b_re, s5_b_im, s5_c_re, s5_c_im,
           s5_d, s5_w_glu, s5_b_glu, gdn_conv_w, gdn_a_log, gdn_dt_bias, gdn_norm_w, w_out, norm_ffn,
           router_w_group, router_b_group, router_w_expert, router_b_expert,
           expert_w_gate, expert_w_up, expert_w_down, norm_final):
    bsz, seq, d = x.shape
    depth = norm_mix.shape[0]
    xt = x.astype(F32).reshape(bsz * seq, d)
    tables = _s5_tables(s5_lam_re, s5_lam_im, s5_log_dt, s5_b_re, s5_b_im, s5_c_re, s5_c_im)
    n_main = S5_WIDTH + 4 * DN_WIDTH
    w_main = w_in[:, :, :n_main].astype(BF16)
    w_out_b = w_out.astype(BF16)
    for i in range(depth):
        u, z, q, k, v, gates = _inproj(xt, norm_mix[i], w_main, w_in[i, :, n_main:], i, gdn_conv_w[i],
                                       gdn_a_log[i], gdn_dt_bias[i])
        ys = _s5_mix(u, tables, i)
        gates_t = (gates[:, :6 * DN_HEADS].reshape(-1, DN_CHUNK, 6 * DN_HEADS).transpose(0, 2, 1))
        o_f, o_b = _dn_mix(q, k, v, gates, gates_t)
        x1, h, route, counts = _post(xt, u, ys, o_f, o_b, z, s5_d[i], s5_w_glu[i], s5_b_glu[i],
                                     gdn_norm_w[i], w_out_b, i, norm_ffn[i], router_w_group[i],
                                     router_b_group[i], router_w_expert[i], router_b_expert[i])
        xt = _moe(x1, h, route, counts, expert_w_gate, expert_w_up, expert_w_down, i,
                  norm_final, i == depth - 1)
    return xt.reshape(bsz, seq, d)
```
